```python
import math
import jax, jax.numpy as jnp
from jax import lax
import numpy as np

D_MODEL = 1024
BATCH = 4
SEQ = 4096
DEPTH = 2

GRID_W = 64
CTX_LEN = 256
N_MIXERS = 2
S5_H = 16
S5_G = D_MODEL // S5_H
S5_P = 64
DT_MIN, DT_MAX = 1e-3, 1e-1
POOL_WINDOWS = (2, 4, 8, 16)
POOL_G = len(POOL_WINDOWS)
POOL_DG = D_MODEL // POOL_G
N_EXPERTS = 16
N_EXPERT_GROUPS = 4
EXPERTS_PER_GROUP = N_EXPERTS // N_EXPERT_GROUPS
TOP_K = 2
D_FF = 1024
N_S5_LAYERS = (DEPTH + 1) // 2
N_POOL_LAYERS = DEPTH // 2
DN_ALPHA = (2 * DEPTH) ** 0.25
DN_BETA = (8 * DEPTH) ** -0.25
LN_EPS = 1e-5

kernel_name = 'hybrid_s5_pool_moe_dit'


def _layer_norm(x, g, b):
    xf = x.astype(jnp.float32)
    mu = jnp.mean(xf, axis=-1, keepdims=True)
    var = jnp.mean(jnp.square(xf - mu), axis=-1, keepdims=True)
    y = (xf - mu) * lax.rsqrt(var + LN_EPS)
    return (y * g.astype(jnp.float32) + b.astype(jnp.float32)).astype(x.dtype)


def _modulation(cond, w, b):
    m = jnp.dot(jax.nn.silu(cond), w) + b
    return jnp.split(m, 6, axis=-1)


def _s5_discretize(lam_re, lam_im, log_dt, b_re, b_im):
    f32 = jnp.float32
    lr = lam_re.astype(f32)
    li = lam_im.astype(f32)
    dt = jnp.exp(log_dt.astype(f32))[:, None]
    mag = jnp.exp(lr * dt)
    ar = mag * jnp.cos(li * dt)
    ai = mag * jnp.sin(li * dt)
    den = lr * lr + li * li
    nr = ar - 1.0
    fr = (nr * lr + ai * li) / den
    fi = (ai * lr - nr * li) / den
    br_, bi_ = b_re.astype(f32), b_im.astype(f32)
    bbr = fr[..., None] * br_ - fi[..., None] * bi_
    bbi = fr[..., None] * bi_ + fi[..., None] * br_
    return ar, ai, bbr, bbi


def _complex_scan_combine(e1, e2):
    a1r, a1i, b1r, b1i = e1
    a2r, a2i, b2r, b2i = e2
    return (a2r * a1r - a2i * a1i,
            a2r * a1i + a2i * a1r,
            a2r * b1r - a2i * b1i + b2r,
            a2r * b1i + a2i * b1r + b2i)


def _s5_scan(u, ar, ai, bbr, bbi, h0, reverse):
    n = u.shape[1]
    bur = jnp.einsum('blgh,gph->blgp', u, bbr)
    bui = jnp.einsum('blgh,gph->blgp', u, bbi)
    a_shape = (1, n) + ar.shape
    elems = (jnp.broadcast_to(ar, a_shape), jnp.broadcast_to(ai, a_shape), bur, bui)
    acr, aci, xr, xi = lax.associative_scan(_complex_scan_combine, elems, reverse=reverse, axis=1)
    if h0 is not None:
        h0r = h0[0][:, None]
        h0i = h0[1][:, None]
        xr = xr + acr * h0r - aci * h0i
        xi = xi + acr * h0i + aci * h0r
    return xr, xi


def _s5_readout(xr, xi, c_re, c_im):
    return jnp.einsum('blgp,ghp->blgh', xr, c_re) - jnp.einsum('blgp,ghp->blgh', xi, c_im)


def _glu(y, w_val, w_gate, dtype):
    y = jax.nn.gelu(y).astype(dtype)
    return jnp.dot(y, w_val) * jax.nn.sigmoid(jnp.dot(y, w_gate))


def _s5_mixer(h_lat, h_ctx, lam_re, lam_im, log_dt, b_re, b_im, c_re, c_im, d_skip, w_val, w_gate, ctx_out):
    f32 = jnp.float32
    bsz, n_lat, _ = h_lat.shape
    n_ctx = h_ctx.shape[1]
    u_lat = h_lat.astype(f32).reshape(bsz, n_lat, S5_G, S5_H)
    u_ctx = h_ctx.astype(f32).reshape(h_ctx.shape[0], n_ctx, S5_G, S5_H)
    d = d_skip.astype(f32).reshape(S5_G, S5_H)
    y_lat = d * u_lat
    y_ctx = d * u_ctx if ctx_out else None
    for direction, reverse in ((0, False), (1, True)):
        ar, ai, bbr, bbi = _s5_discretize(lam_re[direction], lam_im[direction], log_dt[direction],
                                          b_re[direction], b_im[direction])
        cr = c_re[direction].astype(f32)
        ci = c_im[direction].astype(f32)
        xr_c, xi_c = _s5_scan(u_ctx, ar, ai, bbr, bbi, None, reverse)
        end = 0 if reverse else n_ctx - 1
        xr, xi = _s5_scan(u_lat, ar, ai, bbr, bbi, (xr_c[:, end], xi_c[:, end]), reverse)
        y_lat = y_lat + _s5_readout(xr, xi, cr, ci)
        if ctx_out:
            y_ctx = y_ctx + _s5_readout(xr_c, xi_c, cr, ci)
    out_lat = _glu(y_lat.reshape(bsz, n_lat, D_MODEL), w_val, w_gate, h_lat.dtype)
    out_ctx = _glu(y_ctx.reshape(bsz, n_ctx, D_MODEL), w_val, w_gate, h_ctx.dtype) if ctx_out else None
    return out_lat, out_ctx


def _box_sum(x, k, axis):
    n = x.shape[axis]
    cs = jnp.cumsum(x, axis=axis)
    cs = jnp.concatenate([jnp.zeros_like(lax.slice_in_dim(cs, 0, 1, axis=axis)), cs], axis=axis)
    t = jnp.arange(n)
    lo = jnp.clip(t - k // 2, 0, n)
    hi = jnp.clip(t + k - k // 2, 0, n)
    s = jnp.take(cs, hi, axis=axis) - jnp.take(cs, lo, axis=axis)
    shape = [1] * x.ndim
    shape[axis] = n
    return s, (hi - lo).astype(jnp.float32).reshape(shape)


def _pool_mixer(h, w_grp, scale, rows):
    bsz, n, d = h.shape
    hf = h.astype(jnp.float32)
    view = hf.reshape(bsz, rows, GRID_W, d) if rows is not None else hf
    parts = []
    for g, k in enumerate(POOL_WINDOWS):
        xg = view[..., g * POOL_DG:(g + 1) * POOL_DG]
        if rows is not None:
            s, cnt_c = _box_sum(xg, k, 2)
            s, cnt_r = _box_sum(s, k, 1)
            mean = s / (cnt_c * cnt_r)
        else:
            s, cnt = _box_sum(xg, k, 1)
            mean = s / cnt
        parts.append((mean - xg).reshape(bsz, n, POOL_DG))
    pooled = jnp.stack(parts, axis=2).astype(h.dtype)
    out = jnp.einsum('bngc,gcd->bngd', pooled, w_grp).reshape(bsz, n, d)
    return out * scale


def _moe(h, router_w, router_b, w_gate, w_up, w_down):
    f32 = jnp.float32
    shp = h.shape
    t = h.reshape(-1, shp[-1])
    logits = jnp.dot(t, router_w).astype(f32) + router_b.astype(f32)
    scores = jax.nn.softmax(logits, axis=-1)
    sg = scores.reshape(-1, N_EXPERT_GROUPS, EXPERTS_PER_GROUP)
    group_score = jnp.sum(lax.top_k(sg, TOP_K)[0], axis=-1)
    best = jnp.argmax(group_score, axis=-1)
    in_group = jnp.take_along_axis(sg, best[:, None, None], axis=1)[:, 0]
    top_w, top_i = lax.top_k(in_group, TOP_K)
    top_w = top_w / jnp.sum(top_w, axis=-1, keepdims=True)
    expert_idx = best[:, None] * EXPERTS_PER_GROUP + top_i
    combine = jnp.sum(jax.nn.one_hot(expert_idx, N_EXPERTS, dtype=f32) * top_w[..., None], axis=1)
    out = jnp.zeros(t.shape, f32)
    for e in range(N_EXPERTS):
        a = jax.nn.silu(jnp.dot(t, w_gate[e])) * jnp.dot(t, w_up[e])
        out = out + combine[:, e:e + 1] * jnp.dot(a, w_down[e]).astype(f32)
    return out.astype(h.dtype).reshape(shp)


def setup_inputs(seed: int = 0) -> dict:
    key = jax.random.key(seed)
    ks = jax.random.split(key, 26)
    f32 = jnp.float32

    def nrm(k, shape, s):
        return jax.random.normal(k, shape, f32) * s

    D, G, P, H, E, F = D_MODEL, S5_G, S5_P, S5_H, N_EXPERTS, D_FF
    n_idx = jnp.arange(P, dtype=f32)
    return {
        'x': nrm(ks[0], (BATCH, SEQ, D), 1.0),
        'c': nrm(ks[1], (BATCH, D), 1.0),
        'ctx': nrm(ks[2], (BATCH, CTX_LEN, D), 1.0),
        'c_ctx': nrm(ks[3], (D,), 1.0),
        'mod_w': nrm(ks[4], (DEPTH, D, 6 * D), 0.5 * D ** -0.5),
        'mod_b': nrm(ks[5], (DEPTH, 6 * D), 0.02),
        'ln_g': 1.0 + nrm(ks[6], (DEPTH, 2, D), 0.02),
        'ln_b': nrm(ks[7], (DEPTH, 2, D), 0.02),
        's5_lam_re': -0.5 + nrm(ks[8], (N_S5_LAYERS, 2, G, P), 0.01),
        's5_lam_im': math.pi * n_idx + nrm(ks[9], (N_S5_LAYERS, 2, G, P), 0.01),
        's5_log_dt': jax.random.uniform(ks[10], (N_S5_LAYERS, 2, G), f32, math.log(DT_MIN), math.log(DT_MAX)),
        's5_b_re': nrm(ks[11], (N_S5_LAYERS, 2, G, P, H), (2 * H) ** -0.5),
        's5_b_im': nrm(ks[12], (N_S5_LAYERS, 2, G, P, H), (2 * H) ** -0.5),
        's5_c_re': nrm(ks[13], (N_S5_LAYERS, 2, G, H, P), P ** -0.5),
        's5_c_im': nrm(ks[14], (N_S5_LAYERS, 2, G, H, P), P ** -0.5),
        's5_d': nrm(ks[15], (N_S5_LAYERS, D), 1.0),
        's5_w_val': nrm(ks[16], (N_S5_LAYERS, D, D), DN_BETA * D ** -0.5),
        's5_w_gate': nrm(ks[17], (N_S5_LAYERS, D, D), D ** -0.5),
        'pool_w': nrm(ks[18], (N_POOL_LAYERS, POOL_G, POOL_DG, POOL_DG), DN_BETA * POOL_DG ** -0.5),
        'pool_scale': 1.0 + nrm(ks[19], (N_POOL_LAYERS, D), 0.02),
        'router_w': nrm(ks[20], (D, E), D ** -0.5),
        'router_b': nrm(ks[21], (E,), 0.01),
        'moe_w_gate': nrm(ks[22], (DEPTH, E, D, F), D ** -0.5),
        'moe_w_up': nrm(ks[23], (DEPTH, E, D, F), D ** -0.5),
        'moe_w_down': nrm(ks[24], (DEPTH, E, F, D), DN_BETA * F ** -0.5),
    }


def reference(x, c, ctx, c_ctx, mod_w, mod_b, ln_g, ln_b, s5_lam_re, s5_lam_im, s5_log_dt, s5_b_re, s5_b_im,
              s5_c_re, s5_c_im, s5_d, s5_w_val, s5_w_gate, pool_w, pool_scale, router_w, router_b,
              moe_w_gate, moe_w_up, moe_w_down):
    rows = x.shape[1] // GRID_W
    cond_lat = c[:, None, :]
    cond_ctx = c_ctx[None, None, :]
    is_s5 = [i % N_MIXERS == 0 for i in range(DEPTH)]
    for i in range(DEPTH):
        j = i // N_MIXERS
        ctx_active = any(is_s5[i:])
        ctx_advance = any(is_s5[i + 1:])
        sh1, sc1, g1, sh2, sc2, g2 = _modulation(cond_lat, mod_w[i], mod_b[i])
        h = x * (1.0 + sc1) + sh1
        if ctx_active:
            csh1, csc1, cg1, csh2, csc2, cg2 = _modulation(cond_ctx, mod_w[i], mod_b[i])
            hc = ctx * (1.0 + csc1) + csh1
        if is_s5[i]:
            m, mc = _s5_mixer(h, hc, s5_lam_re[j], s5_lam_im[j], s5_log_dt[j], s5_b_re[j], s5_b_im[j],
                              s5_c_re[j], s5_c_im[j], s5_d[j], s5_w_val[j], s5_w_gate[j], ctx_advance)
        else:
            m = _pool_mixer(h, pool_w[j], pool_scale[j], rows)
            mc = _pool_mixer(hc, pool_w[j], pool_scale[j], None) if ctx_advance else None
        x = _layer_norm(DN_ALPHA * x + g1 * m, ln_g[i, 0], ln_b[i, 0])
        h = x * (1.0 + sc2) + sh2
        x = _layer_norm(DN_ALPHA * x + g2 * _moe(h, router_w, router_b, moe_w_gate[i], moe_w_up[i], moe_w_down[i]),
                        ln_g[i, 1], ln_b[i, 1])
        if ctx_advance:
            ctx = _layer_norm(DN_ALPHA * ctx + cg1 * mc, ln_g[i, 0], ln_b[i, 0])
            hc = ctx * (1.0 + csc2) + csh2
            ctx = _layer_norm(DN_ALPHA * ctx + cg2 * _moe(hc, router_w, router_b, moe_w_gate[i], moe_w_up[i],
                                                          moe_w_down[i]), ln_g[i, 1], ln_b[i, 1])
    return x
```

```python
import functools

import jax
import jax.numpy as jnp
from jax import lax
from jax.experimental import pallas as pl
from jax.experimental.pallas import tpu as pltpu

F32 = jnp.float32
BF16 = jnp.bfloat16
I32 = jnp.int32
HI = lax.Precision.HIGHEST

GRID_W = 64
S5_H = 16
S5_T = 16
POOL_WINDOWS = (2, 4, 8, 16)
N_EXPERT_GROUPS = 4
LN_EPS = 1e-5
LANES = 128
SUBLANES = 8
VMEM_LIMIT = 52 * 1024 * 1024

ROUTE_ROWS = 8


def _cparams(*sem):
    return pltpu.CompilerParams(dimension_semantics=sem, vmem_limit_bytes=VMEM_LIMIT)


def _mod_body(c_ref, w_ref, b_ref, o_ref):
    c = c_ref[...]
    s = c * jax.nn.sigmoid(c)
    o_ref[0] = jnp.dot(s, w_ref[0], precision=HI, preferred_element_type=F32) + b_ref[0]


def _modulation(cond, mod_w, mod_b):
    depth, d, n6 = mod_w.shape
    tn = min(n6, 1536)
    return pl.pallas_call(
        _mod_body,
        grid=(depth, n6 // tn),
        in_specs=[pl.BlockSpec((SUBLANES, d), lambda i, j: (0, 0)),
                  pl.BlockSpec((1, d, tn), lambda i, j: (i, 0, j)),
                  pl.BlockSpec((1, 1, tn), lambda i, j: (i, 0, j))],
        out_specs=pl.BlockSpec((1, SUBLANES, tn), lambda i, j: (i, 0, j)),
        out_shape=jax.ShapeDtypeStruct((depth, SUBLANES, n6), F32),
        compiler_params=_cparams("arbitrary", "arbitrary"),
        name="modulation",
    )(cond, mod_w, mod_b.reshape(depth, 1, n6))


def _modcast_body(x_ref, m_ref, o_ref):
    sh = m_ref[0, 0:1, :]
    sc = m_ref[0, 1:2, :]
    o_ref[0] = (x_ref[0] * (1.0 + sc) + sh).astype(o_ref.dtype)


def _modulate_cast(x, mods, shared, dtype):
    b, l, d = x.shape
    tl = min(l, 512)
    mmap = (lambda i, j: (0, 0, 0)) if shared else (lambda i, j: (i, 0, 0))
    return pl.pallas_call(
        _modcast_body,
        grid=(b, l // tl),
        in_specs=[pl.BlockSpec((1, tl, d), lambda i, j: (i, j, 0)),
                  pl.BlockSpec((1, 6, d), mmap)],
        out_specs=pl.BlockSpec((1, tl, d), lambda i, j: (i, j, 0)),
        out_shape=jax.ShapeDtypeStruct((b, l, d), dtype),
        compiler_params=_cparams("arbitrary", "arbitrary"),
        name="modulate_cast",
    )(x, mods)


def _s5_direction_terms(lam_re, lam_im, log_dt, b_re, b_im):
    lr = lam_re.astype(F32)
    li = lam_im.astype(F32)
    dt = jnp.exp(log_dt.astype(F32))[:, None]
    mag = jnp.exp(lr * dt)
    ar = mag * jnp.cos(li * dt)
    ai = mag * jnp.sin(li * dt)
    den = lr * lr + li * li
    nr = ar - 1.0
    fr = (nr * lr + ai * li) / den
    fi = (ai * lr - nr * li) / den
    br_, bi_ = b_re.astype(F32), b_im.astype(F32)
    bbr = fr[..., None] * br_ - fi[..., None] * bi_
    bbi = fr[..., None] * bi_ + fi[..., None] * br_
    k = jnp.arange(S5_T + 1, dtype=F32)[:, None, None]
    pm = jnp.exp(k * (lr * dt))
    pr = pm * jnp.cos(k * (li * dt))
    pi = pm * jnp.sin(k * (li * dt))
    return pr, pi, bbr, bbi


def _s5_weights(lam_re, lam_im, log_dt, b_re, b_im, c_re, c_im, d_skip):
    t = S5_T
    g, p = lam_re.shape[1:]
    h = b_re.shape[-1]
    terms = [_s5_direction_terms(lam_re[d], lam_im[d], log_dt[d], b_re[d], b_im[d]) for d in (0, 1)]
    toep = jnp.zeros((g, t, h, t, h), F32)
    w1_parts, w2_parts, a_parts = {}, {}, {}
    ii = jnp.arange(t)[:, None]
    jj = jnp.arange(t)[None, :]
    for d in (0, 1):
        pr, pi, bbr, bbi = terms[d]
        cr = c_re[d].astype(F32)
        ci = c_im[d].astype(F32)
        car = cr[None] * pr[:t, :, None, :] - ci[None] * pi[:t, :, None, :]
        cai = cr[None] * pi[:t, :, None, :] + ci[None] * pr[:t, :, None, :]
        kk = (jnp.einsum('tgop,gph->tgoh', car, bbr, precision=HI)
              - jnp.einsum('tgop,gph->tgoh', cai, bbi, precision=HI))
        lag = (jj - ii) if d == 0 else (ii - jj)
        blk = jnp.where((lag >= 0)[..., None, None, None], kk[jnp.clip(lag, 0, t - 1)], 0.0)
        toep = toep + jnp.transpose(blk, (2, 0, 4, 1, 3))
        er = pr[:t][::-1] if d == 0 else pr[:t]
        ei = pi[:t][::-1] if d == 0 else pi[:t]
        w1r = er[:, :, :, None] * bbr[None] - ei[:, :, :, None] * bbi[None]
        w1i = er[:, :, :, None] * bbi[None] + ei[:, :, :, None] * bbr[None]
        w1_parts[d] = [jnp.transpose(w, (1, 0, 3, 2)).reshape(g, t * h, p) for w in (w1r, w1i)]
        qr = pr[1:] if d == 0 else pr[1:][::-1]
        qi = pi[1:] if d == 0 else pi[1:][::-1]
        mr = cr[None] * qr[:, :, None, :] - ci[None] * qi[:, :, None, :]
        mi = cr[None] * qi[:, :, None, :] + ci[None] * qr[:, :, None, :]
        w2_parts[d] = [jnp.transpose(w, (1, 3, 0, 2)).reshape(g, p, t * h) for w in (mr, -mi)]
        a_parts[d] = (pr[t], pi[t])
    toep = toep.reshape(g, t * h, t * h)
    diag = jnp.tile(d_skip.astype(F32).reshape(g, h), (1, t))
    toep = toep + jnp.eye(t * h, dtype=F32)[None] * diag[:, :, None]
    w1 = jnp.concatenate([w1_parts[0][0], w1_parts[1][0], w1_parts[0][1], w1_parts[1][1]], axis=-1)
    w2 = jnp.concatenate([w2_parts[0][0], w2_parts[1][0], w2_parts[0][1], w2_parts[1][1]], axis=1)
    a_r = jnp.concatenate([a_parts[0][0], a_parts[1][0]], axis=-1)[:, None, :]
    a_i = jnp.concatenate([a_parts[0][1], a_parts[1][1]], axis=-1)[:, None, :]
    return toep.astype(BF16), w1.astype(BF16), w2.astype(BF16), a_r, a_i


def _s5_body(u_ref, tt_ref, w1_ref, w2_ref, ar_ref, ai_ref, y_ref, s_ref, xf_ref, xb_ref, *,
             gb, n_pairs, lat_lo, lat_rows):
    n_blocks = u_ref.shape[1] // SUBLANES
    half = SUBLANES // 2
    for g in range(gb):
        s_ref[g] = jnp.dot(u_ref[g], w1_ref[g], preferred_element_type=F32)

    lane = lax.broadcasted_iota(I32, (SUBLANES, LANES), 1)
    row = lax.broadcasted_iota(I32, (SUBLANES, LANES), 0)
    is_fwd = lane < LANES // 2
    top = row < half
    ars = [jnp.broadcast_to(ar_ref[g], (SUBLANES, LANES)) for g in range(gb)]
    ais = [jnp.broadcast_to(ai_ref[g], (SUBLANES, LANES)) for g in range(gb)]

    def step(k, carry):
        fo = pl.multiple_of(k * SUBLANES, SUBLANES)
        bo = pl.multiple_of((n_blocks - 1 - k) * SUBLANES, SUBLANES)
        new = []
        for g in range(gb):
            xr, xi = carry[2 * g], carry[2 * g + 1]
            vf = s_ref[g, pl.ds(fo, SUBLANES), :]
            vb = pltpu.roll(s_ref[g, pl.ds(bo, SUBLANES), :], half, 0)
            vr = jnp.where(is_fwd, vf[:, :LANES], vb[:, :LANES])
            vi = jnp.where(is_fwd, vf[:, LANES:], vb[:, LANES:])
            ar, ai = ars[g], ais[g]
            yr = ar * xr - ai * xi + vr
            yi = ar * xi + ai * xr + vi
            yrr = pltpu.roll(yr, half, 0)
            yir = pltpu.roll(yi, half, 0)
            zr = ar * yrr - ai * yir + vr
            zi = ar * yir + ai * yrr + vi
            inc = jnp.concatenate([jnp.where(top, xr, yrr), jnp.where(top, xi, yir)], axis=1)
            xf_ref[g, pl.ds(fo, SUBLANES), :] = inc
            xb_ref[g, pl.ds(bo, SUBLANES), :] = pltpu.roll(inc, half, 0)
            new.append(jnp.where(top, pltpu.roll(zr, half, 0), zr))
            new.append(jnp.where(top, pltpu.roll(zi, half, 0), zi))
        return tuple(new)

    zero = jnp.zeros((SUBLANES, LANES), F32)
    lax.fori_loop(0, n_pairs, step, tuple(zero for _ in range(2 * gb)))

    lane2 = lax.broadcasted_iota(I32, (1, 2 * LANES), 1)
    is_fwd2 = (lane2 % LANES) < LANES // 2
    for g in range(gb):
        xin = jnp.where(is_fwd2, xf_ref[g, lat_lo:lat_lo + lat_rows, :],
                        xb_ref[g, lat_lo:lat_lo + lat_rows, :]).astype(BF16)
        y_ref[g] = (jnp.dot(u_ref[g, lat_lo:lat_lo + lat_rows, :], tt_ref[g], preferred_element_type=F32)
                    + jnp.dot(xin, w2_ref[g], preferred_element_type=F32))


def _s5_mix(u, toep, w1, w2, a_r, a_i, n_ctx_chunks, n_lat_chunks, batch):
    assert batch * 2 == SUBLANES and (n_ctx_chunks + n_lat_chunks) % 2 == 0 and n_ctx_chunks % 2 == 0
    g, rows, w = u.shape
    gb = 4
    lat_rows = n_lat_chunks * batch
    body = functools.partial(_s5_body, gb=gb, n_pairs=(n_ctx_chunks + n_lat_chunks) // 2,
                             lat_lo=n_ctx_chunks * batch, lat_rows=lat_rows)
    wspec = pl.BlockSpec((gb, w, w), lambda i: (i, 0, 0))
    aspec = pl.BlockSpec((gb, 1, w // 2), lambda i: (i, 0, 0))
    return pl.pallas_call(
        body,
        grid=(g // gb,),
        in_specs=[pl.BlockSpec((gb, rows, w), lambda i: (i, 0, 0)), wspec, wspec, wspec, aspec, aspec],
        out_specs=pl.BlockSpec((gb, lat_rows, w), lambda i: (i, 0, 0)),
        out_shape=jax.ShapeDtypeStruct((g, lat_rows, w), F32),
        scratch_shapes=[pltpu.VMEM((gb, rows, w), F32) for _ in range(3)],
        compiler_params=_cparams("arbitrary"),
        name="s5_mix",
    )(u, toep, w1, w2, a_r, a_i)


def _layer_norm(r, g, b):
    mu = jnp.mean(r, axis=-1, keepdims=True)
    xc = r - mu
    var = jnp.mean(xc * xc, axis=-1, keepdims=True)
    return xc * lax.rsqrt(var + LN_EPS) * g + b


def _max2_of4(a, b, c, d):
    h1, l1 = jnp.maximum(a, b), jnp.minimum(a, b)
    h2, l2 = jnp.maximum(c, d), jnp.minimum(c, d)
    return jnp.maximum(h1, h2) + jnp.maximum(jnp.minimum(h1, h2), jnp.maximum(l1, l2))


def _argmax_first(vals):
    idx = jnp.zeros(vals[0].shape, I32)
    best = vals[0]
    for j in range(1, len(vals)):
        upd = vals[j] > best
        idx = jnp.where(upd, j, idx)
        best = jnp.where(upd, vals[j], best)
    return idx, best


def _route(logits_t, count_ref, route_ref):
    n_e, tm = logits_t.shape
    per = n_e // N_EXPERT_GROUPS
    mx = jnp.max(logits_t, axis=0, keepdims=True)
    ex = jnp.exp(logits_t - mx)
    sc = ex / jnp.sum(ex, axis=0, keepdims=True)
    rows = [sc[e:e + 1, :] for e in range(n_e)]
    gscore = [_max2_of4(*rows[per * g:per * (g + 1)]) for g in range(N_EXPERT_GROUPS)]
    best, _ = _argmax_first(gscore)
    vals = []
    for j in range(per):
        v = rows[per * (N_EXPERT_GROUPS - 1) + j]
        for g in range(N_EXPERT_GROUPS - 2, -1, -1):
            v = jnp.where(best == g, rows[per * g + j], v)
        vals.append(v)
    i1, m1 = _argmax_first(vals)
    i2, m2 = _argmax_first([jnp.where(i1 == j, -1.0, vals[j]) for j in range(per)])
    den = m1 + m2
    e0 = best * per + i1
    e1 = best * per + i2

    eidx = lax.broadcasted_iota(I32, (n_e, tm), 0)
    hit0 = eidx == e0
    hit1 = eidx == e1
    onehot = jnp.where(hit0, 1.0, jnp.where(hit1, 1.0, 0.0))
    src = lax.broadcasted_iota(I32, (tm, tm), 0)
    dst = lax.broadcasted_iota(I32, (tm, tm), 1)
    tri = jnp.where(src <= dst, 1.0, 0.0).astype(BF16)
    cum = jnp.dot(onehot.astype(BF16), tri, preferred_element_type=F32)
    excl = cum - onehot + count_ref[:, 0:1]
    rank0 = jnp.sum(jnp.where(hit0, excl, 0.0), axis=0, keepdims=True)
    rank1 = jnp.sum(jnp.where(hit1, excl, 0.0), axis=0, keepdims=True)
    count_ref[...] = count_ref[...] + jnp.sum(onehot, axis=1, keepdims=True)

    zero = jnp.zeros((1, tm), F32)
    route_ref[...] = jnp.concatenate(
        [e0.astype(F32), e1.astype(F32), m1 / den, m2 / den, rank0, rank1, zero, zero], axis=0)


def _store_rows_tiled(ref, val):
    for s in range(ref.shape[1]):
        ref[:, s, :] = val[:, s * LANES:(s + 1) * LANES]


def _load_rows_tiled(ref):
    return jnp.concatenate([ref[:, s, :] for s in range(ref.shape[1])], axis=-1)


def _post_mixer_body(m_ref, x_ref, mod_ref, lng_ref, lnb_ref, rwt_ref, rb_ref, *rest, glu, alpha):
    if glu:
        wv_ref, wg_ref, x1_ref, h_ref, route_ref, cnt_out_ref, cnt_ref = rest
        a = jax.nn.gelu(m_ref[0], approximate=True).astype(BF16)
        val = jnp.dot(a, wv_ref[...], preferred_element_type=F32)
        gate = jnp.dot(a, wg_ref[...], preferred_element_type=F32)
        m = val * jax.nn.sigmoid(gate)
    else:
        x1_ref, h_ref, route_ref, cnt_out_ref, cnt_ref = rest
        m = m_ref[0]

    @pl.when((pl.program_id(0) == 0) & (pl.program_id(1) == 0))
    def _():
        cnt_ref[...] = jnp.zeros_like(cnt_ref)

    g1 = mod_ref[0, 2:3, :]
    sh2 = mod_ref[0, 3:4, :]
    sc2 = mod_ref[0, 4:5, :]
    x1 = _layer_norm(alpha * x_ref[0] + g1 * m, lng_ref[...], lnb_ref[...])
    x1_ref[0] = x1
    h = x1 * (1.0 + sc2) + sh2
    _store_rows_tiled(h_ref, h)
    logits_t = lax.dot_general(rwt_ref[...], h, (((1,), (1,)), ((), ())),
                               precision=HI, preferred_element_type=F32) + rb_ref[...]
    _route(logits_t, cnt_ref, route_ref)
    cnt_out_ref[...] = cnt_ref[...]


def _post_mixer(m, x, mods, ln_g, ln_b, router_w, router_b, alpha, glu_w=None):
    b, l, d = x.shape
    n_e = router_w.shape[1]
    tm = min(l, 512)
    nt = l // tm
    tok = lambda i, j: (i, j, 0)
    const2 = lambda i, j: (0, 0)
    in_specs = [pl.BlockSpec((1, tm, d), tok),
                pl.BlockSpec((1, tm, d), tok),
                pl.BlockSpec((1, 6, d), lambda i, j: (i, 0, 0)),
                pl.BlockSpec((1, d), const2),
                pl.BlockSpec((1, d), const2),
                pl.BlockSpec((n_e, d), const2),
                pl.BlockSpec((n_e, 1), const2)]
    args = [m, x, mods, ln_g.reshape(1, d), ln_b.reshape(1, d), router_w.T, router_b.reshape(n_e, 1)]
    if glu_w is not None:
        in_specs += [pl.BlockSpec((d, d), const2), pl.BlockSpec((d, d), const2)]
        args += [glu_w[0].astype(BF16), glu_w[1].astype(BF16)]
    out_shape = [jax.ShapeDtypeStruct((b, l, d), F32),
                 jax.ShapeDtypeStruct((b * l, d // LANES, LANES), F32),
                 jax.ShapeDtypeStruct((ROUTE_ROWS, b * l), F32),
                 jax.ShapeDtypeStruct((n_e, LANES), F32)]
    out_specs = [pl.BlockSpec((1, tm, d), tok),
                 pl.BlockSpec((tm, d // LANES, LANES), lambda i, j: (i * nt + j, 0, 0)),
                 pl.BlockSpec((ROUTE_ROWS, tm), lambda i, j: (0, i * nt + j)),
                 pl.BlockSpec((n_e, LANES), const2)]
    return pl.pallas_call(
        functools.partial(_post_mixer_body, glu=glu_w is not None, alpha=alpha),
        grid=(b, nt),
        in_specs=in_specs,
        out_specs=out_specs,
        out_shape=out_shape,
        scratch_shapes=[pltpu.VMEM((n_e, LANES), F32)],
        compiler_params=_cparams("arbitrary", "arbitrary"),
        name="post_mixer_glu" if glu_w is not None else "post_mixer",
    )(*args)


def _dispatch_plan(route, counts, tile):
    n_e = counts.shape[0]
    cnt = counts[:, 0].astype(I32)
    tiles = (cnt + tile - 1) // tile
    tile_end = jnp.cumsum(tiles)
    row_off = (tile_end - tiles) * tile
    e0 = route[0].astype(I32)
    e1 = route[1].astype(I32)
    pos0 = row_off[e0] + route[4].astype(I32)
    pos1 = row_off[e1] + route[5].astype(I32)
    n_tiles_max = (2 * route.shape[1]) // tile + n_e
    tile_expert = jnp.minimum(
        jnp.searchsorted(tile_end, jnp.arange(n_tiles_max, dtype=I32), side='right'), n_e - 1).astype(I32)
    n_used = tile_end[-1:].astype(I32)
    pad_start = (row_off + cnt).astype(I32)
    pad_len = (tiles * tile - cnt).astype(I32)
    return pos0, pos1, tile_expert, n_used, pad_start, pad_len, n_tiles_max


def _zero_pad_rows(pad_ref, len_ref, zero_ref, hs_ref, sem, n_e, tile, wait):
    for e in range(n_e):
        ln = len_ref[e]
        bit = tile // 2
        while bit >= 1:
            @pl.when((ln & bit) != 0)
            def _(bit=bit, ln=ln, e=e):
                start = pad_ref[e] + (ln & ~(2 * bit - 1))
                cp = pltpu.make_async_copy(zero_ref.at[pl.ds(0, bit)], hs_ref.at[pl.ds(start, bit)], sem)
                if wait:
                    cp.wait()
                else:
                    cp.start()
            bit //= 2


def _zero_unused_tiles(nu_ref, zero_ref, hs_ref, sem, tile, wait):
    half = tile // 2
    n_halves = 2 * (hs_ref.shape[0] // tile - nu_ref[0])

    def body(j, carry):
        cp = pltpu.make_async_copy(zero_ref, hs_ref.at[pl.ds(nu_ref[0] * tile + j * half, half)], sem)
        if wait:
            cp.wait()
        else:
            cp.start()
        return carry

    lax.fori_loop(0, n_halves, body, 0)


def _dispatch_body(pos0_ref, pos1_ref, pad_ref, len_ref, nu_ref, h_ref, hs_ref, zero_ref, sem_z, sem, *,
                   tm, tile, n_e):
    i = pl.program_id(0)

    @pl.when(i == 0)
    def _():
        zero_ref[...] = jnp.zeros_like(zero_ref)
        _zero_pad_rows(pad_ref, len_ref, zero_ref, hs_ref, sem_z, n_e, tile, False)
        _zero_unused_tiles(nu_ref, zero_ref, hs_ref, sem_z, tile, False)

    @pl.when(i == pl.num_programs(0) - 1)
    def _():
        _zero_pad_rows(pad_ref, len_ref, zero_ref, hs_ref, sem_z, n_e, tile, True)
        _zero_unused_tiles(nu_ref, zero_ref, hs_ref, sem_z, tile, True)

    base = i * tm

    def issue(r, carry):
        pltpu.make_async_copy(h_ref.at[r], hs_ref.at[pos0_ref[base + r]], sem).start()
        pltpu.make_async_copy(h_ref.at[r], hs_ref.at[pos1_ref[base + r]], sem).start()
        return carry

    lax.fori_loop(0, tm, issue, 0, unroll=8)
    for _ in range(2):
        pltpu.make_async_copy(h_ref, hs_ref.at[pl.ds(0, tm)], sem).wait()


def _dispatch(h_rows, pos0, pos1, pad_start, pad_len, n_used, n_rows, tile):
    n, s, w = h_rows.shape
    n_e = pad_start.shape[0]
    tm = min(n, 256)
    grid_spec = pltpu.PrefetchScalarGridSpec(
        num_scalar_prefetch=5,
        grid=(n // tm,),
        in_specs=[pl.BlockSpec((tm, s, w), lambda i, *_: (i, 0, 0))],
        out_specs=pl.BlockSpec(memory_space=pl.ANY),
        scratch_shapes=[pltpu.VMEM((tile // 2, s, w), F32),
                        pltpu.SemaphoreType.DMA(()), pltpu.SemaphoreType.DMA(())],
    )
    return pl.pallas_call(
        functools.partial(_dispatch_body, tm=tm, tile=tile, n_e=n_e),
        grid_spec=grid_spec,
        out_shape=jax.ShapeDtypeStruct((n_rows, s, w), F32),
        compiler_params=_cparams("arbitrary"),
        name="moe_dispatch",
    )(pos0, pos1, pad_start, pad_len, n_used, h_rows)


def _expert_body(te_ref, nu_ref, hs_ref, wg_ref, wu_ref, wd_ref, ys_ref, wgb, wub, wdb):
    i = pl.program_id(0)
    first = jnp.logical_or(i == 0, te_ref[i] != te_ref[jnp.maximum(i - 1, 0)])

    @pl.when(jnp.logical_and(i < nu_ref[0], first))
    def _():
        wgb[...] = wg_ref[0].astype(BF16)
        wub[...] = wu_ref[0].astype(BF16)
        wdb[...] = wd_ref[0].astype(BF16)

    @pl.when(i < nu_ref[0])
    def _():
        x = _load_rows_tiled(hs_ref).astype(BF16)
        gate = jnp.dot(x, wgb[...], preferred_element_type=F32)
        up = jnp.dot(x, wub[...], preferred_element_type=F32)
        a = (gate * jax.nn.sigmoid(gate) * up).astype(BF16)
        _store_rows_tiled(ys_ref, jnp.dot(a, wdb[...], preferred_element_type=F32))

    @pl.when(i >= nu_ref[0])
    def _():
        ys_ref[...] = jnp.zeros_like(ys_ref)


def _expert_ffn(hs, tile_expert, n_used, w_gate, w_up, w_down, n_tiles, tile):
    n_rows, s, w = hs.shape
    n_e, d, f = w_gate.shape
    wmap = lambda i, te, nu: (te[i], 0, 0)
    grid_spec = pltpu.PrefetchScalarGridSpec(
        num_scalar_prefetch=2,
        grid=(n_tiles,),
        in_specs=[pl.BlockSpec((tile, s, w), lambda i, te, nu: (jnp.minimum(i, nu[0] - 1), 0, 0)),
                  pl.BlockSpec((1, d, f), wmap),
                  pl.BlockSpec((1, d, f), wmap),
                  pl.BlockSpec((1, f, d), wmap)],
        out_specs=pl.BlockSpec((tile, s, w), lambda i, te, nu: (i, 0, 0)),
        scratch_shapes=[pltpu.VMEM((d, f), BF16), pltpu.VMEM((d, f), BF16), pltpu.VMEM((f, d), BF16)],
    )
    return pl.pallas_call(
        _expert_body,
        grid_spec=grid_spec,
        out_shape=jax.ShapeDtypeStruct((n_tiles * tile, s, w), F32),
        compiler_params=_cparams("arbitrary"),
        name="moe_experts",
    )(tile_expert, n_used, hs, w_gate, w_up, w_down)


def _combine_body(pos0_ref, pos1_ref, x_ref, mod_ref, wts_ref, lng_ref, lnb_ref, ys_ref, *rest,
                  tm, alpha, next_mod):
    if next_mod:
        nmod_ref, x2_ref, h_ref, buf0, buf1, sem = rest
    else:
        x2_ref, buf0, buf1, sem = rest
    base = (pl.program_id(0) * pl.num_programs(1) + pl.program_id(1)) * tm

    def issue(r, carry):
        pltpu.make_async_copy(ys_ref.at[pos0_ref[base + r]], buf0.at[r], sem).start()
        pltpu.make_async_copy(ys_ref.at[pos1_ref[base + r]], buf1.at[r], sem).start()
        return carry

    lax.fori_loop(0, tm, issue, 0, unroll=8)
    for buf in (buf0, buf1):
        pltpu.make_async_copy(ys_ref.at[pl.ds(0, tm)], buf, sem).wait()

    w = wts_ref[...]
    moe = w[:, 0:1] * _load_rows_tiled(buf0) + w[:, 1:2] * _load_rows_tiled(buf1)
    g2 = mod_ref[0, 5:6, :]
    x2 = _layer_norm(alpha * x_ref[0] + g2 * moe, lng_ref[...], lnb_ref[...])
    x2_ref[0] = x2
    if next_mod:
        h_ref[0] = x2 * (1.0 + nmod_ref[0, 1:2, :]) + nmod_ref[0, 0:1, :]


def _combine(ys, pos0, pos1, wts, x, mods, ln_g, ln_b, alpha, next_mods=None):
    b, l, d = x.shape
    n_rows, s, w = ys.shape
    tm = min(l, 256)
    nt = l // tm
    tok = lambda i, j, *_: (i, j, 0)
    bat = lambda i, j, *_: (i, 0, 0)
    const2 = lambda i, j, *_: (0, 0)
    in_specs = [pl.BlockSpec((1, tm, d), tok),
                pl.BlockSpec((1, 6, d), bat),
                pl.BlockSpec((tm, 2), lambda i, j, *_: (i * nt + j, 0)),
                pl.BlockSpec((1, d), const2),
                pl.BlockSpec((1, d), const2),
                pl.BlockSpec(memory_space=pl.ANY)]
    args = [x, mods, wts, ln_g.reshape(1, d), ln_b.reshape(1, d), ys]
    out_shape = [jax.ShapeDtypeStruct((b, l, d), F32)]
    out_specs = [pl.BlockSpec((1, tm, d), tok)]
    if next_mods is not None:
        in_specs.append(pl.BlockSpec((1, 6, d), bat))
        args.append(next_mods)
        out_shape.append(jax.ShapeDtypeStruct((b, l, d), F32))
        out_specs.append(pl.BlockSpec((1, tm, d), tok))
    grid_spec = pltpu.PrefetchScalarGridSpec(
        num_scalar_prefetch=2,
        grid=(b, nt),
        in_specs=in_specs,
        out_specs=out_specs,
        scratch_shapes=[pltpu.VMEM((tm, s, w), F32), pltpu.VMEM((tm, s, w), F32),
                        pltpu.SemaphoreType.DMA(())],
    )
    return pl.pallas_call(
        functools.partial(_combine_body, tm=tm, alpha=alpha, next_mod=next_mods is not None),
        grid_spec=grid_spec,
        out_shape=out_shape,
        compiler_params=_cparams("arbitrary", "arbitrary"),
        name="moe_combine",
    )(pos0, pos1, *args)


def _moe(h_rows, route, counts, w_gate, w_up, w_down):
    n = h_rows.shape[0]
    tile = min(n, 512)
    pos0, pos1, tile_expert, n_used, pad_start, pad_len, n_tiles = _dispatch_plan(route, counts, tile)
    hs = _dispatch(h_rows, pos0, pos1, pad_start, pad_len, n_used, n_tiles * tile, tile)
    ys = _expert_ffn(hs, tile_expert, n_used, w_gate, w_up, w_down, n_tiles, tile)
    return ys, pos0, pos1, jnp.transpose(route[2:4])


def _pool_group(h_ref, w_ref, sc_ref, o_ref, col_ref, k, n_rows):
    n = n_rows * GRID_W
    c = h_ref.shape[2]
    blk = 4 * GRID_W
    half = k // 2
    pad = half * GRID_W
    ti = lax.broadcasted_iota(I32, (blk, blk), 0)
    si = lax.broadcasted_iota(I32, (blk, blk), 1)
    shift = GRID_W.bit_length() - 1
    same_row = (ti >> shift) == (si >> shift)
    band = jnp.where(same_row & (si - ti >= -half) & (si - ti <= half - 1), 1.0, 0.0)
    col_ref[0:pad, :] = jnp.zeros((pad, c), F32)
    col_ref[pad + n:pad + n + pad, :] = jnp.zeros((pad, c), F32)
    for b0 in range(0, n, blk):
        col_ref[pad + b0:pad + b0 + blk, :] = jnp.dot(band, h_ref[0, b0:b0 + blk, :], precision=HI,
                                                      preferred_element_type=F32)
    acc = col_ref[0:n, :]
    for j in range(1, k):
        acc = acc + col_ref[j * GRID_W:j * GRID_W + n, :]
    t = lax.broadcasted_iota(I32, (n, 1), 0)
    wc = t & (GRID_W - 1)
    wr = t >> shift
    cnt_c = jnp.minimum(wc + half - 1, GRID_W - 1) - jnp.maximum(wc - half, 0) + 1
    cnt_r = jnp.minimum(wr + half - 1, n_rows - 1) - jnp.maximum(wr - half, 0) + 1
    mean = acc / (cnt_c * cnt_r).astype(F32)
    pooled = (mean - h_ref[0]).astype(BF16)
    o_ref[0] = jnp.dot(pooled, w_ref[0].astype(BF16), preferred_element_type=F32) * sc_ref[...]


def _pool_body(h_ref, w_ref, sc_ref, o_ref, col_ref, *, n_rows):
    g = pl.program_id(1)
    for gi, k in enumerate(POOL_WINDOWS):
        @pl.when(g == gi)
        def _(k=k):
            _pool_group(h_ref, w_ref, sc_ref, o_ref, col_ref, k, n_rows)


def _pool_mix(h, w_grp, scale):
    b, n, d = h.shape
    n_g, c, _ = w_grp.shape
    n_rows = n // GRID_W
    pad = (max(POOL_WINDOWS) // 2) * GRID_W
    return pl.pallas_call(
        functools.partial(_pool_body, n_rows=n_rows),
        grid=(b, n_g),
        in_specs=[pl.BlockSpec((1, n, c), lambda i, j: (i, 0, j)),
                  pl.BlockSpec((1, c, c), lambda i, j: (j, 0, 0)),
                  pl.BlockSpec((1, c), lambda i, j: (0, j))],
        out_specs=pl.BlockSpec((1, n, c), lambda i, j: (i, 0, j)),
        out_shape=jax.ShapeDtypeStruct((b, n, d), F32),
        scratch_shapes=[pltpu.VMEM((n + 2 * pad, c), F32)],
        compiler_params=_cparams("arbitrary", "arbitrary"),
        name="pool_mix",
    )(h, w_grp, scale.reshape(1, d))


def _to_chunk_rows(h, n_groups):
    b, l, d = h.shape
    v = h.reshape(b, l // S5_T, S5_T, n_groups, S5_H)
    return jnp.transpose(v, (3, 1, 0, 2, 4)).reshape(n_groups, (l // S5_T) * b, S5_T * S5_H)


def _from_chunk_rows(y, b, l):
    n_groups = y.shape[0]
    v = y.reshape(n_groups, l // S5_T, b, S5_T, S5_H)
    return jnp.transpose(v, (2, 1, 3, 0, 4)).reshape(b, l, n_groups * S5_H)


def kernel(x, c, ctx, c_ctx, mod_w, mod_b, ln_g, ln_b, s5_lam_re, s5_lam_im, s5_log_dt, s5_b_re, s5_b_im,
           s5_c_re, s5_c_im, s5_d, s5_w_val, s5_w_gate, pool_w, pool_scale, router_w, router_b,
           moe_w_gate, moe_w_up, moe_w_down):
    b, l, d = x.shape
    depth = mod_w.shape[0]
    assert depth == 2 and b + 1 <= SUBLANES and d % LANES == 0 and GRID_W & (GRID_W - 1) == 0
    alpha = (2 * depth) ** 0.25
    n_groups = d // S5_H
    n_ctx = ctx.shape[1]

    cond = jnp.zeros((SUBLANES, d), F32).at[:b].set(c).at[b].set(c_ctx)
    mods = _modulation(cond, mod_w, mod_b).reshape(depth, SUBLANES, 6, d)

    hl = _modulate_cast(x, mods[0, :b], False, BF16)
    hc = _modulate_cast(ctx, mods[0, b:b + 1], True, BF16)
    uc = _to_chunk_rows(hc, n_groups)
    u = jnp.concatenate([uc, _to_chunk_rows(hl, n_groups), uc], axis=1)
    toep, w1, w2, a_r, a_i = _s5_weights(s5_lam_re[0], s5_lam_im[0], s5_log_dt[0], s5_b_re[0], s5_b_im[0],
                                         s5_c_re[0], s5_c_im[0], s5_d[0])
    y = _from_chunk_rows(_s5_mix(u, toep, w1, w2, a_r, a_i, n_ctx // S5_T, l // S5_T, b), b, l)
    x1, h_rows, route, counts = _post_mixer(y, x, mods[0, :b], ln_g[0, 0], ln_b[0, 0], router_w, router_b,
                                            alpha, glu_w=(s5_w_val[0], s5_w_gate[0]))
    ys, pos0, pos1, wts = _moe(h_rows, route, counts, moe_w_gate[0], moe_w_up[0], moe_w_down[0])
    x2, h = _combine(ys, pos0, pos1, wts, x1, mods[0, :b], ln_g[0, 1], ln_b[0, 1], alpha,
                     next_mods=mods[1, :b])

    m = _pool_mix(h, pool_w[0], pool_scale[0])
    x3, h_rows, route, counts = _post_mixer(m, x2, mods[1, :b], ln_g[1, 0], ln_b[1, 0], router_w, router_b,
                                            alpha)
    ys, pos0, pos1, wts = _moe(h_rows, route, counts, moe_w_gate[1], moe_w_up[1], moe_w_down[1])
    (out,) = _combine(ys, pos0, pos1, wts, x3, mods[1, :b], ln_g[1, 1], ln_b[1, 1], alpha)
    return out
```

```python
import functools

import jax
import jax.numpy as jnp
from jax import lax
from jax.experimental import pallas as pl
from jax.experimental.pallas import tpu as pltpu

F32 = jnp.float32
BF16 = jnp.bfloat16
I32 = jnp.int32
HI = lax.Precision.HIGHEST

GRID_W = 64
S5_H = 16
S5_T = 16
POOL_WINDOWS = (2, 4, 8, 16)
N_EXPERT_GROUPS = 4
LN_EPS = 1e-5
LANES = 128
SUBLANES = 8
VMEM_LIMIT = 52 * 1024 * 1024

ROUTE_ROWS = 8


def _cparams(*sem):
    return pltpu.CompilerParams(dimension_semantics=sem, vmem_limit_bytes=VMEM_LIMIT)


def _mod_body(c_ref, w_ref, b_ref, o_ref):
    c = c_ref[...]
    s = c * jax.nn.sigmoid(c)
    o_ref[0] = jnp.dot(s, w_ref[0], precision=HI, preferred_element_type=F32) + b_ref[0]


def _modulation(cond, mod_w, mod_b):
    depth, d, n6 = mod_w.shape
    tn = min(n6, 1536)
    return pl.pallas_call(
        _mod_body,
        grid=(depth, n6 // tn),
        in_specs=[pl.BlockSpec((SUBLANES, d), lambda i, j: (0, 0)),
                  pl.BlockSpec((1, d, tn), lambda i, j: (i, 0, j)),
                  pl.BlockSpec((1, 1, tn), lambda i, j: (i, 0, j))],
        out_specs=pl.BlockSpec((1, SUBLANES, tn), lambda i, j: (i, 0, j)),
        out_shape=jax.ShapeDtypeStruct((depth, SUBLANES, n6), F32),
        compiler_params=_cparams("arbitrary", "arbitrary"),
        name="modulation",
    )(cond, mod_w, mod_b.reshape(depth, 1, n6))


def _modcast_body(x_ref, m_ref, o_ref):
    sh = m_ref[0, 0:1, :]
    sc = m_ref[0, 1:2, :]
    o_ref[0] = (x_ref[0] * (1.0 + sc) + sh).astype(o_ref.dtype)


def _modulate_cast(x, mods, shared, dtype):
    b, l, d = x.shape
    tl = min(l, 512)
    mmap = (lambda i, j: (0, 0, 0)) if shared else (lambda i, j: (i, 0, 0))
    return pl.pallas_call(
        _modcast_body,
        grid=(b, l // tl),
        in_specs=[pl.BlockSpec((1, tl, d), lambda i, j: (i, j, 0)),
                  pl.BlockSpec((1, 6, d), mmap)],
        out_specs=pl.BlockSpec((1, tl, d), lambda i, j: (i, j, 0)),
        out_shape=jax.ShapeDtypeStruct((b, l, d), dtype),
        compiler_params=_cparams("arbitrary", "arbitrary"),
        name="modulate_cast",
    )(x, mods)


def _s5_direction_terms(lam_re, lam_im, log_dt, b_re, b_im):
    lr = lam_re.astype(F32)
    li = lam_im.astype(F32)
    dt = jnp.exp(log_dt.astype(F32))[:, None]
    mag = jnp.exp(lr * dt)
    ar = mag * jnp.cos(li * dt)
    ai = mag * jnp.sin(li * dt)
    den = lr * lr + li * li
    nr = ar - 1.0
    fr = (nr * lr + ai * li) / den
    fi = (ai * lr - nr * li) / den
    br_, bi_ = b_re.astype(F32), b_im.astype(F32)
    bbr = fr[..., None] * br_ - fi[..., None] * bi_
    bbi = fr[..., None] * bi_ + fi[..., None] * br_
    k = jnp.arange(S5_T + 1, dtype=F32)[:, None, None]
    pm = jnp.exp(k * (lr * dt))
    pr = pm * jnp.cos(k * (li * dt))
    pi = pm * jnp.sin(k * (li * dt))
    return pr, pi, bbr, bbi


def _s5_weights(lam_re, lam_im, log_dt, b_re, b_im, c_re, c_im, d_skip):
    t = S5_T
    g, p = lam_re.shape[1:]
    h = b_re.shape[-1]
    terms = [_s5_direction_terms(lam_re[d], lam_im[d], log_dt[d], b_re[d], b_im[d]) for d in (0, 1)]
    pw = jnp.stack([jnp.transpose(terms[d][k], (1, 0, 2)) for d in (0, 1) for k in (0, 1)], axis=1)
    bt = jnp.stack([jnp.transpose(terms[d][k], (0, 2, 1)) for d in (0, 1) for k in (2, 3)], axis=1)
    cc = jnp.stack([c[d].astype(F32) for d in (0, 1) for c in (c_re, c_im)], axis=1)
    gp = 8
    spec4 = lambda rows: pl.BlockSpec((gp, 4, rows, p), lambda i: (i, 0, 0, 0))
    wide = pl.BlockSpec((gp, t * h, t * h), lambda i: (i, 0, 0))
    toep, w1, w2t = pl.pallas_call(
        functools.partial(_s5_prep_body, gp=gp),
        grid=(g // gp,),
        in_specs=[spec4(t + 1), spec4(h), spec4(h), pl.BlockSpec((gp, 1, h), lambda i: (i, 0, 0))],
        out_specs=[wide, wide, wide],
        out_shape=[jax.ShapeDtypeStruct((g, t * h, t * h), BF16) for _ in range(3)],
        compiler_params=_cparams("arbitrary"),
        name="s5_prep",
    )(pw, bt, cc, d_skip.astype(F32).reshape(g, 1, h))
    a_r = jnp.concatenate([pw[:, 0, t], pw[:, 2, t]], axis=-1)[:, None, :]
    a_i = jnp.concatenate([pw[:, 1, t], pw[:, 3, t]], axis=-1)[:, None, :]
    return toep, w1, w2t, a_r, a_i


def _cmul(ar, ai, br, bi):
    return ar * br - ai * bi, ar * bi + ai * br


def _s5_prep_body(pw_ref, bt_ref, cc_ref, d_ref, toep_ref, w1_ref, w2t_ref, *, gp):
    t = pw_ref.shape[2] - 1
    h = bt_ref.shape[2]
    nt = (((1,), (1,)), ((), ()))
    eye = (lax.broadcasted_iota(I32, (h, h), 0) == lax.broadcasted_iota(I32, (h, h), 1))
    for g in range(gp):
        prf, pif, prb, pib = (pw_ref[g, k] for k in range(4))
        btf = (bt_ref[g, 0], bt_ref[g, 1])
        btb = (bt_ref[g, 2], bt_ref[g, 3])
        ccf = (cc_ref[g, 0], cc_ref[g, 1])
        ccb = (cc_ref[g, 2], cc_ref[g, 3])
        caf, cab = [], []
        for j in range(t):
            f_r, f_i = _cmul(*btf, prf[t - 1 - j:t - j], pif[t - 1 - j:t - j])
            b_r, b_i = _cmul(*btb, prb[j:j + 1], pib[j:j + 1])
            w1_ref[g, j * h:(j + 1) * h, :] = jnp.concatenate([f_r, b_r, f_i, b_i], axis=1).astype(BF16)
            mf_r, mf_i = _cmul(*ccf, prf[j + 1:j + 2], pif[j + 1:j + 2])
            mb_r, mb_i = _cmul(*ccb, prb[t - j:t - j + 1], pib[t - j:t - j + 1])
            w2t_ref[g, j * h:(j + 1) * h, :] = jnp.concatenate([mf_r, mb_r, -mf_i, -mb_i],
                                                               axis=1).astype(BF16)
            caf.append(_cmul(*ccf, prf[j:j + 1], pif[j:j + 1]))
            cab.append(_cmul(*ccb, prb[t - 1 - j:t - j], pib[t - 1 - j:t - j]))

        def lag_kernels(btx, ca):
            car = jnp.concatenate([c[0] for c in ca], axis=0)
            cai = jnp.concatenate([c[1] for c in ca], axis=0)
            return (lax.dot_general(btx[0], car, nt, precision=HI, preferred_element_type=F32)
                    - lax.dot_general(btx[1], cai, nt, precision=HI, preferred_element_type=F32))

        kf = lag_kernels(btf, caf)
        kb = lag_kernels(btb, cab)
        skip = jnp.where(eye, jnp.broadcast_to(d_ref[g], (h, h)), 0.0)
        mid = kb[:, (t - 1) * h:] + kf[:, :h] + skip
        kwide = jnp.concatenate([kb[:, :(t - 1) * h], mid, kf[:, h:], jnp.zeros((h, h), F32)], axis=1)
        for i in range(t):
            off = (t - 1 - i) * h
            toep_ref[g, i * h:(i + 1) * h, :] = kwide[:, off:off + t * h].astype(BF16)


def _s5_body(u_ref, tt_ref, w1_ref, w2_ref, ar_ref, ai_ref, y_ref, s_ref, xf_ref, xb_ref, *,
             gb, n_pairs, lat_lo, lat_rows):
    n_blocks = u_ref.shape[1] // SUBLANES
    half = SUBLANES // 2
    for g in range(gb):
        s_ref[g] = jnp.dot(u_ref[g], w1_ref[g], preferred_element_type=F32)

    lane = lax.broadcasted_iota(I32, (SUBLANES, LANES), 1)
    row = lax.broadcasted_iota(I32, (SUBLANES, LANES), 0)
    is_fwd = lane < LANES // 2
    top = row < half
    ars = [jnp.broadcast_to(ar_ref[g], (SUBLANES, LANES)) for g in range(gb)]
    ais = [jnp.broadcast_to(ai_ref[g], (SUBLANES, LANES)) for g in range(gb)]

    def step(k, carry):
        fo = pl.multiple_of(k * SUBLANES, SUBLANES)
        bo = pl.multiple_of((n_blocks - 1 - k) * SUBLANES, SUBLANES)
        new = []
        for g in range(gb):
            xr, xi = carry[2 * g], carry[2 * g + 1]
            vf = s_ref[g, pl.ds(fo, SUBLANES), :]
            vb = pltpu.roll(s_ref[g, pl.ds(bo, SUBLANES), :], half, 0)
            vr = jnp.where(is_fwd, vf[:, :LANES], vb[:, :LANES])
            vi = jnp.where(is_fwd, vf[:, LANES:], vb[:, LANES:])
            ar, ai = ars[g], ais[g]
            yr = ar * xr - ai * xi + vr
            yi = ar * xi + ai * xr + vi
            yrr = pltpu.roll(yr, half, 0)
            yir = pltpu.roll(yi, half, 0)
            zr = ar * yrr - ai * yir + vr
            zi = ar * yir + ai * yrr + vi
            inc = jnp.concatenate([jnp.where(top, xr, yrr), jnp.where(top, xi, yir)], axis=1)
            xf_ref[g, pl.ds(fo, SUBLANES), :] = inc
            xb_ref[g, pl.ds(bo, SUBLANES), :] = pltpu.roll(inc, half, 0)
            new.append(jnp.where(top, pltpu.roll(zr, half, 0), zr))
            new.append(jnp.where(top, pltpu.roll(zi, half, 0), zi))
        return tuple(new)

    zero = jnp.zeros((SUBLANES, LANES), F32)
    lax.fori_loop(0, n_pairs, step, tuple(zero for _ in range(2 * gb)))

    lane2 = lax.broadcasted_iota(I32, (1, 2 * LANES), 1)
    is_fwd2 = (lane2 % LANES) < LANES // 2
    for g in range(gb):
        xin = jnp.where(is_fwd2, xf_ref[g, lat_lo:lat_lo + lat_rows, :],
                        xb_ref[g, lat_lo:lat_lo + lat_rows, :]).astype(BF16)
        y_ref[g] = (jnp.dot(u_ref[g, lat_lo:lat_lo + lat_rows, :], tt_ref[g], preferred_element_type=F32)
                    + lax.dot_general(xin, w2_ref[g], (((1,), (1,)), ((), ())),
                                      preferred_element_type=F32))


def _s5_mix(u, toep, w1, w2, a_r, a_i, n_ctx_chunks, n_lat_chunks, batch):
    assert batch * 2 == SUBLANES and (n_ctx_chunks + n_lat_chunks) % 2 == 0 and n_ctx_chunks % 2 == 0
    g, rows, w = u.shape
    gb = 4
    lat_rows = n_lat_chunks * batch
    body = functools.partial(_s5_body, gb=gb, n_pairs=(n_ctx_chunks + n_lat_chunks) // 2,
                             lat_lo=n_ctx_chunks * batch, lat_rows=lat_rows)
    wspec = pl.BlockSpec((gb, w, w), lambda i: (i, 0, 0))
    aspec = pl.BlockSpec((gb, 1, w // 2), lambda i: (i, 0, 0))
    return pl.pallas_call(
        body,
        grid=(g // gb,),
        in_specs=[pl.BlockSpec((gb, rows, w), lambda i: (i, 0, 0)), wspec, wspec, wspec, aspec, aspec],
        out_specs=pl.BlockSpec((gb, lat_rows, w), lambda i: (i, 0, 0)),
        out_shape=jax.ShapeDtypeStruct((g, lat_rows, w), F32),
        scratch_shapes=[pltpu.VMEM((gb, rows, w), F32) for _ in range(3)],
        compiler_params=_cparams("arbitrary"),
        name="s5_mix",
    )(u, toep, w1, w2, a_r, a_i)


def _layer_norm(r, g, b):
    mu = jnp.mean(r, axis=-1, keepdims=True)
    xc = r - mu
    var = jnp.mean(xc * xc, axis=-1, keepdims=True)
    return xc * lax.rsqrt(var + LN_EPS) * g + b


def _max2_of4(a, b, c, d):
    h1, l1 = jnp.maximum(a, b), jnp.minimum(a, b)
    h2, l2 = jnp.maximum(c, d), jnp.minimum(c, d)
    return jnp.maximum(h1, h2) + jnp.maximum(jnp.minimum(h1, h2), jnp.maximum(l1, l2))


def _argmax_first(vals):
    idx = jnp.zeros(vals[0].shape, I32)
    best = vals[0]
    for j in range(1, len(vals)):
        upd = vals[j] > best
        idx = jnp.where(upd, j, idx)
        best = jnp.where(upd, vals[j], best)
    return idx, best


def _route(logits_t, count_ref, route_ref):
    n_e, tm = logits_t.shape
    per = n_e // N_EXPERT_GROUPS
    mx = jnp.max(logits_t, axis=0, keepdims=True)
    ex = jnp.exp(logits_t - mx)
    sc = ex / jnp.sum(ex, axis=0, keepdims=True)
    rows = [sc[e:e + 1, :] for e in range(n_e)]
    gscore = [_max2_of4(*rows[per * g:per * (g + 1)]) for g in range(N_EXPERT_GROUPS)]
    best, _ = _argmax_first(gscore)
    vals = []
    for j in range(per):
        v = rows[per * (N_EXPERT_GROUPS - 1) + j]
        for g in range(N_EXPERT_GROUPS - 2, -1, -1):
            v = jnp.where(best == g, rows[per * g + j], v)
        vals.append(v)
    i1, m1 = _argmax_first(vals)
    i2, m2 = _argmax_first([jnp.where(i1 == j, -1.0, vals[j]) for j in range(per)])
    den = m1 + m2
    e0 = best * per + i1
    e1 = best * per + i2

    eidx = lax.broadcasted_iota(I32, (n_e, tm), 0)
    hit0 = eidx == e0
    hit1 = eidx == e1
    onehot = jnp.where(hit0, 1.0, jnp.where(hit1, 1.0, 0.0))
    src = lax.broadcasted_iota(I32, (tm, tm), 0)
    dst = lax.broadcasted_iota(I32, (tm, tm), 1)
    tri = jnp.where(src <= dst, 1.0, 0.0).astype(BF16)
    cum = jnp.dot(onehot.astype(BF16), tri, preferred_element_type=F32)
    excl = cum - onehot + count_ref[:, 0:1]
    rank0 = jnp.sum(jnp.where(hit0, excl, 0.0), axis=0, keepdims=True)
    rank1 = jnp.sum(jnp.where(hit1, excl, 0.0), axis=0, keepdims=True)
    count_ref[...] = count_ref[...] + jnp.sum(onehot, axis=1, keepdims=True)

    zero = jnp.zeros((1, tm), F32)
    route_ref[...] = jnp.concatenate(
        [e0.astype(F32), e1.astype(F32), m1 / den, m2 / den, rank0, rank1, zero, zero], axis=0)


def _store_rows_tiled(ref, val):
    for s in range(ref.shape[1]):
        ref[:, s, :] = val[:, s * LANES:(s + 1) * LANES]


def _load_rows_tiled(ref):
    return jnp.concatenate([ref[:, s, :] for s in range(ref.shape[1])], axis=-1)


def _post_mixer_body(m_ref, x_ref, mod_ref, lng_ref, lnb_ref, rwt_ref, rb_ref, *rest, glu, alpha):
    if glu:
        wv_ref, wg_ref, x1_ref, h_ref, route_ref, cnt_out_ref, cnt_ref = rest
        a = jax.nn.gelu(m_ref[0], approximate=True).astype(BF16)
        val = jnp.dot(a, wv_ref[...], preferred_element_type=F32)
        gate = jnp.dot(a, wg_ref[...], preferred_element_type=F32)
        m = val * jax.nn.sigmoid(gate)
    else:
        x1_ref, h_ref, route_ref, cnt_out_ref, cnt_ref = rest
        m = m_ref[0]

    @pl.when((pl.program_id(0) == 0) & (pl.program_id(1) == 0))
    def _():
        cnt_ref[...] = jnp.zeros_like(cnt_ref)

    g1 = mod_ref[0, 2:3, :]
    sh2 = mod_ref[0, 3:4, :]
    sc2 = mod_ref[0, 4:5, :]
    x1 = _layer_norm(alpha * x_ref[0] + g1 * m, lng_ref[...], lnb_ref[...])
    x1_ref[0] = x1
    h = x1 * (1.0 + sc2) + sh2
    _store_rows_tiled(h_ref, h)
    logits_t = lax.dot_general(rwt_ref[...], h, (((1,), (1,)), ((), ())),
                               precision=HI, preferred_element_type=F32) + rb_ref[...]
    _route(logits_t, cnt_ref, route_ref)
    cnt_out_ref[...] = cnt_ref[...]


def _post_mixer(m, x, mods, ln_g, ln_b, router_w, router_b, alpha, glu_w=None):
    b, l, d = x.shape
    n_e = router_w.shape[1]
    tm = min(l, 512)
    nt = l // tm
    tok = lambda i, j: (i, j, 0)
    const2 = lambda i, j: (0, 0)
    in_specs = [pl.BlockSpec((1, tm, d), tok),
                pl.BlockSpec((1, tm, d), tok),
                pl.BlockSpec((1, 6, d), lambda i, j: (i, 0, 0)),
                pl.BlockSpec((1, d), const2),
                pl.BlockSpec((1, d), const2),
                pl.BlockSpec((n_e, d), const2),
                pl.BlockSpec((n_e, 1), const2)]
    args = [m, x, mods, ln_g.reshape(1, d), ln_b.reshape(1, d), router_w.T, router_b.reshape(n_e, 1)]
    if glu_w is not None:
        in_specs += [pl.BlockSpec((d, d), const2), pl.BlockSpec((d, d), const2)]
        args += [glu_w[0].astype(BF16), glu_w[1].astype(BF16)]
    out_shape = [jax.ShapeDtypeStruct((b, l, d), F32),
                 jax.ShapeDtypeStruct((b * l, d // LANES, LANES), F32),
                 jax.ShapeDtypeStruct((ROUTE_ROWS, b * l), F32),
                 jax.ShapeDtypeStruct((n_e, LANES), F32)]
    out_specs = [pl.BlockSpec((1, tm, d), tok),
                 pl.BlockSpec((tm, d // LANES, LANES), lambda i, j: (i * nt + j, 0, 0)),
                 pl.BlockSpec((ROUTE_ROWS, tm), lambda i, j: (0, i * nt + j)),
                 pl.BlockSpec((n_e, LANES), const2)]
    return pl.pallas_call(
        functools.partial(_post_mixer_body, glu=glu_w is not None, alpha=alpha),
        grid=(b, nt),
        in_specs=in_specs,
        out_specs=out_specs,
        out_shape=out_shape,
        scratch_shapes=[pltpu.VMEM((n_e, LANES), F32)],
        compiler_params=_cparams("arbitrary", "arbitrary"),
        name="post_mixer_glu" if glu_w is not None else "post_mixer",
    )(*args)


def _dispatch_plan(route, counts, tile):
    n_e = counts.shape[0]
    cnt = counts[:, 0].astype(I32)
    tiles = (cnt + tile - 1) // tile
    tile_end = jnp.cumsum(tiles)
    row_off = (tile_end - tiles) * tile
    eids = jnp.arange(n_e, dtype=I32)[:, None]

    def row_offset_of(e):
        return jnp.sum(jnp.where(e[None, :] == eids, row_off[:, None], 0), axis=0)

    pos0 = row_offset_of(route[0].astype(I32)) + route[4].astype(I32)
    pos1 = row_offset_of(route[1].astype(I32)) + route[5].astype(I32)
    n_tiles_max = (2 * route.shape[1]) // tile + n_e
    tile_ids = jnp.arange(n_tiles_max, dtype=I32)
    tile_expert = jnp.minimum(jnp.sum((tile_end[None, :] <= tile_ids[:, None]).astype(I32), axis=1), n_e - 1)
    n_used = tile_end[-1:].astype(I32)
    pad_start = (row_off + cnt).astype(I32)
    pad_len = (tiles * tile - cnt).astype(I32)
    return pos0, pos1, tile_expert, n_used, pad_start, pad_len, n_tiles_max


def _zero_pad_rows(pad_ref, len_ref, zero_ref, hs_ref, sem, n_e, tile, wait):
    for e in range(n_e):
        ln = len_ref[e]
        bit = tile // 2
        while bit >= 1:
            @pl.when((ln & bit) != 0)
            def _(bit=bit, ln=ln, e=e):
                start = pad_ref[e] + (ln & ~(2 * bit - 1))
                cp = pltpu.make_async_copy(zero_ref.at[pl.ds(0, bit)], hs_ref.at[pl.ds(start, bit)], sem)
                if wait:
                    cp.wait()
                else:
                    cp.start()
            bit //= 2


def _zero_unused_tiles(nu_ref, zero_ref, hs_ref, sem, tile, wait):
    half = tile // 2
    n_halves = 2 * (hs_ref.shape[0] // tile - nu_ref[0])

    def body(j, carry):
        cp = pltpu.make_async_copy(zero_ref, hs_ref.at[pl.ds(nu_ref[0] * tile + j * half, half)], sem)
        if wait:
            cp.wait()
        else:
            cp.start()
        return carry

    lax.fori_loop(0, n_halves, body, 0)


def _dispatch_body(pos0_ref, pos1_ref, pad_ref, len_ref, nu_ref, h_ref, hs_ref, zero_ref, sem_z, sem, *,
                   tm, tile, n_e):
    i = pl.program_id(0)

    @pl.when(i == 0)
    def _():
        zero_ref[...] = jnp.zeros_like(zero_ref)
        _zero_pad_rows(pad_ref, len_ref, zero_ref, hs_ref, sem_z, n_e, tile, False)
        _zero_unused_tiles(nu_ref, zero_ref, hs_ref, sem_z, tile, False)

    @pl.when(i == pl.num_programs(0) - 1)
    def _():
        _zero_pad_rows(pad_ref, len_ref, zero_ref, hs_ref, sem_z, n_e, tile, True)
        _zero_unused_tiles(nu_ref, zero_ref, hs_ref, sem_z, tile, True)

    base = i * tm

    def issue(r, carry):
        pltpu.make_async_copy(h_ref.at[r], hs_ref.at[pos0_ref[base + r]], sem).start()
        pltpu.make_async_copy(h_ref.at[r], hs_ref.at[pos1_ref[base + r]], sem).start(priority=1)
        return carry

    lax.fori_loop(0, tm, issue, 0, unroll=8)
    for _ in range(2):
        pltpu.make_async_copy(h_ref, hs_ref.at[pl.ds(0, tm)], sem).wait()


def _dispatch(h_rows, pos0, pos1, pad_start, pad_len, n_used, n_rows, tile):
    n, s, w = h_rows.shape
    n_e = pad_start.shape[0]
    tm = min(n, 256)
    grid_spec = pltpu.PrefetchScalarGridSpec(
        num_scalar_prefetch=5,
        grid=(n // tm,),
        in_specs=[pl.BlockSpec((tm, s, w), lambda i, *_: (i, 0, 0))],
        out_specs=pl.BlockSpec(memory_space=pl.ANY),
        scratch_shapes=[pltpu.VMEM((tile // 2, s, w), F32),
                        pltpu.SemaphoreType.DMA(()), pltpu.SemaphoreType.DMA(())],
    )
    return pl.pallas_call(
        functools.partial(_dispatch_body, tm=tm, tile=tile, n_e=n_e),
        grid_spec=grid_spec,
        out_shape=jax.ShapeDtypeStruct((n_rows, s, w), F32),
        compiler_params=_cparams("arbitrary"),
        name="moe_dispatch",
    )(pos0, pos1, pad_start, pad_len, n_used, h_rows)


def _expert_body(te_ref, nu_ref, hs_ref, wg_ref, wu_ref, wd_ref, ys_ref, wgb, wub, wdb):
    i = pl.program_id(0)
    first = jnp.logical_or(i == 0, te_ref[i] != te_ref[jnp.maximum(i - 1, 0)])

    @pl.when(jnp.logical_and(i < nu_ref[0], first))
    def _():
        wgb[...] = wg_ref[0].astype(BF16)
        wub[...] = wu_ref[0].astype(BF16)
        wdb[...] = wd_ref[0].astype(BF16)

    @pl.when(i < nu_ref[0])
    def _():
        x = _load_rows_tiled(hs_ref).astype(BF16)
        gate = jnp.dot(x, wgb[...], preferred_element_type=F32)
        up = jnp.dot(x, wub[...], preferred_element_type=F32)
        a = (gate * jax.nn.sigmoid(gate) * up).astype(BF16)
        _store_rows_tiled(ys_ref, jnp.dot(a, wdb[...], preferred_element_type=F32))

    @pl.when(i >= nu_ref[0])
    def _():
        ys_ref[...] = jnp.zeros_like(ys_ref)


def _expert_ffn(hs, tile_expert, n_used, w_gate, w_up, w_down, n_tiles, tile):
    n_rows, s, w = hs.shape
    n_e, d, f = w_gate.shape
    wmap = lambda i, te, nu: (te[i], 0, 0)
    grid_spec = pltpu.PrefetchScalarGridSpec(
        num_scalar_prefetch=2,
        grid=(n_tiles,),
        in_specs=[pl.BlockSpec((tile, s, w), lambda i, te, nu: (jnp.minimum(i, nu[0] - 1), 0, 0)),
                  pl.BlockSpec((1, d, f), wmap),
                  pl.BlockSpec((1, d, f), wmap),
                  pl.BlockSpec((1, f, d), wmap)],
        out_specs=pl.BlockSpec((tile, s, w), lambda i, te, nu: (i, 0, 0)),
        scratch_shapes=[pltpu.VMEM((d, f), BF16), pltpu.VMEM((d, f), BF16), pltpu.VMEM((f, d), BF16)],
    )
    return pl.pallas_call(
        _expert_body,
        grid_spec=grid_spec,
        out_shape=jax.ShapeDtypeStruct((n_tiles * tile, s, w), F32),
        compiler_params=_cparams("arbitrary"),
        name="moe_experts",
    )(tile_expert, n_used, hs, w_gate, w_up, w_down)


def _combine_body(pos0_ref, pos1_ref, x_ref, mod_ref, wts_ref, lng_ref, lnb_ref, ys_ref, *rest,
                  tm, alpha, next_mod):
    if next_mod:
        nmod_ref, x2_ref, h_ref, buf0, buf1, sem = rest
    else:
        x2_ref, buf0, buf1, sem = rest
    base = (pl.program_id(0) * pl.num_programs(1) + pl.program_id(1)) * tm

    def issue(r, carry):
        pltpu.make_async_copy(ys_ref.at[pos0_ref[base + r]], buf0.at[r], sem).start()
        pltpu.make_async_copy(ys_ref.at[pos1_ref[base + r]], buf1.at[r], sem).start(priority=1)
        return carry

    lax.fori_loop(0, tm, issue, 0, unroll=8)
    for buf in (buf0, buf1):
        pltpu.make_async_copy(ys_ref.at[pl.ds(0, tm)], buf, sem).wait()

    w = wts_ref[...]
    moe = w[:, 0:1] * _load_rows_tiled(buf0) + w[:, 1:2] * _load_rows_tiled(buf1)
    g2 = mod_ref[0, 5:6, :]
    x2 = _layer_norm(alpha * x_ref[0] + g2 * moe, lng_ref[...], lnb_ref[...])
    x2_ref[0] = x2
    if next_mod:
        h_ref[0] = x2 * (1.0 + nmod_ref[0, 1:2, :]) + nmod_ref[0, 0:1, :]


def _combine(ys, pos0, pos1, wts, x, mods, ln_g, ln_b, alpha, next_mods=None):
    b, l, d = x.shape
    n_rows, s, w = ys.shape
    tm = min(l, 256)
    nt = l // tm
    tok = lambda i, j, *_: (i, j, 0)
    bat = lambda i, j, *_: (i, 0, 0)
    const2 = lambda i, j, *_: (0, 0)
    in_specs = [pl.BlockSpec((1, tm, d), tok),
                pl.BlockSpec((1, 6, d), bat),
                pl.BlockSpec((tm, 2), lambda i, j, *_: (i * nt + j, 0)),
                pl.BlockSpec((1, d), const2),
                pl.BlockSpec((1, d), const2),
                pl.BlockSpec(memory_space=pl.ANY)]
    args = [x, mods, wts, ln_g.reshape(1, d), ln_b.reshape(1, d), ys]
    out_shape = [jax.ShapeDtypeStruct((b, l, d), F32)]
    out_specs = [pl.BlockSpec((1, tm, d), tok)]
    if next_mods is not None:
        in_specs.append(pl.BlockSpec((1, 6, d), bat))
        args.append(next_mods)
        out_shape.append(jax.ShapeDtypeStruct((b, l, d), F32))
        out_specs.append(pl.BlockSpec((1, tm, d), tok))
    grid_spec = pltpu.PrefetchScalarGridSpec(
        num_scalar_prefetch=2,
        grid=(b, nt),
        in_specs=in_specs,
        out_specs=out_specs,
        scratch_shapes=[pltpu.VMEM((tm, s, w), F32), pltpu.VMEM((tm, s, w), F32),
                        pltpu.SemaphoreType.DMA(())],
    )
    return pl.pallas_call(
        functools.partial(_combine_body, tm=tm, alpha=alpha, next_mod=next_mods is not None),
        grid_spec=grid_spec,
        out_shape=out_shape,
        compiler_params=_cparams("arbitrary", "arbitrary"),
        name="moe_combine",
    )(pos0, pos1, *args)


def _moe(h_rows, route, counts, w_gate, w_up, w_down):
    n = h_rows.shape[0]
    tile = min(n, 512)
    pos0, pos1, tile_expert, n_used, pad_start, pad_len, n_tiles = _dispatch_plan(route, counts, tile)
    hs = _dispatch(h_rows, pos0, pos1, pad_start, pad_len, n_used, n_tiles * tile, tile)
    ys = _expert_ffn(hs, tile_expert, n_used, w_gate, w_up, w_down, n_tiles, tile)
    return ys, pos0, pos1, jnp.transpose(route[2:4])


def _pool_group(h_ref, w_ref, sc_ref, o_ref, col_ref, k, n_rows):
    n = n_rows * GRID_W
    c = h_ref.shape[2]
    blk = 4 * GRID_W
    half = k // 2
    pad = half * GRID_W
    ti = lax.broadcasted_iota(I32, (blk, blk), 0)
    si = lax.broadcasted_iota(I32, (blk, blk), 1)
    shift = GRID_W.bit_length() - 1
    same_row = (ti >> shift) == (si >> shift)
    band = jnp.where(same_row & (si - ti >= -half) & (si - ti <= half - 1), 1.0, 0.0)
    col_ref[0:pad, :] = jnp.zeros((pad, c), F32)
    col_ref[pad + n:pad + n + pad, :] = jnp.zeros((pad, c), F32)
    for b0 in range(0, n, blk):
        col_ref[pad + b0:pad + b0 + blk, :] = jnp.dot(band, h_ref[0, b0:b0 + blk, :], precision=HI,
                                                      preferred_element_type=F32)
    acc = col_ref[0:n, :]
    for j in range(1, k):
        acc = acc + col_ref[j * GRID_W:j * GRID_W + n, :]
    t = lax.broadcasted_iota(I32, (n, 1), 0)
    wc = t & (GRID_W - 1)
    wr = t >> shift
    cnt_c = jnp.minimum(wc + half - 1, GRID_W - 1) - jnp.maximum(wc - half, 0) + 1
    cnt_r = jnp.minimum(wr + half - 1, n_rows - 1) - jnp.maximum(wr - half, 0) + 1
    mean = acc / (cnt_c * cnt_r).astype(F32)
    pooled = (mean - h_ref[0]).astype(BF16)
    o_ref[0] = jnp.dot(pooled, w_ref[0].astype(BF16), preferred_element_type=F32) * sc_ref[...]


def _pool_body(h_ref, w_ref, sc_ref, o_ref, col_ref, *, n_rows):
    g = pl.program_id(1)
    for gi, k in enumerate(POOL_WINDOWS):
        @pl.when(g == gi)
        def _(k=k):
            _pool_group(h_ref, w_ref, sc_ref, o_ref, col_ref, k, n_rows)


def _pool_mix(h, w_grp, scale):
    b, n, d = h.shape
    n_g, c, _ = w_grp.shape
    n_rows = n // GRID_W
    pad = (max(POOL_WINDOWS) // 2) * GRID_W
    return pl.pallas_call(
        functools.partial(_pool_body, n_rows=n_rows),
        grid=(b, n_g),
        in_specs=[pl.BlockSpec((1, n, c), lambda i, j: (i, 0, j)),
                  pl.BlockSpec((1, c, c), lambda i, j: (j, 0, 0)),
                  pl.BlockSpec((1, c), lambda i, j: (0, j))],
        out_specs=pl.BlockSpec((1, n, c), lambda i, j: (i, 0, j)),
        out_shape=jax.ShapeDtypeStruct((b, n, d), F32),
        scratch_shapes=[pltpu.VMEM((n + 2 * pad, c), F32)],
        compiler_params=_cparams("arbitrary", "arbitrary"),
        name="pool_mix",
    )(h, w_grp, scale.reshape(1, d))


def _to_chunk_rows(h, n_groups):
    b, l, d = h.shape
    v = h.reshape(b, l // S5_T, S5_T, n_groups, S5_H)
    return jnp.transpose(v, (3, 1, 0, 2, 4)).reshape(n_groups, (l // S5_T) * b, S5_T * S5_H)


def _from_chunk_rows(y, b, l):
    n_groups = y.shape[0]
    v = y.reshape(n_groups, l // S5_T, b, S5_T, S5_H)
    return jnp.transpose(v, (2, 1, 3, 0, 4)).reshape(b, l, n_groups * S5_H)


def kernel(x, c, ctx, c_ctx, mod_w, mod_b, ln_g, ln_b, s5_lam_re, s5_lam_im, s5_log_dt, s5_b_re, s5_b_im,
           s5_c_re, s5_c_im, s5_d, s5_w_val, s5_w_gate, pool_w, pool_scale, router_w, router_b,
           moe_w_gate, moe_w_up, moe_w_down):
    b, l, d = x.shape
    depth = mod_w.shape[0]
    assert depth == 2 and b + 1 <= SUBLANES and d % LANES == 0 and GRID_W & (GRID_W - 1) == 0
    alpha = (2 * depth) ** 0.25
    n_groups = d // S5_H
    n_ctx = ctx.shape[1]

    cond = jnp.zeros((SUBLANES, d), F32).at[:b].set(c).at[b].set(c_ctx)
    mods = _modulation(cond, mod_w, mod_b).reshape(depth, SUBLANES, 6, d)

    hl = _modulate_cast(x, mods[0, :b], False, BF16)
    hc = _modulate_cast(ctx, mods[0, b:b + 1], True, BF16)
    uc = _to_chunk_rows(hc, n_groups)
    u = jnp.concatenate([uc, _to_chunk_rows(hl, n_groups), uc], axis=1)
    toep, w1, w2, a_r, a_i = _s5_weights(s5_lam_re[0], s5_lam_im[0], s5_log_dt[0], s5_b_re[0], s5_b_im[0],
                                         s5_c_re[0], s5_c_im[0], s5_d[0])
    y = _from_chunk_rows(_s5_mix(u, toep, w1, w2, a_r, a_i, n_ctx // S5_T, l // S5_T, b), b, l)
    x1, h_rows, route, counts = _post_mixer(y, x, mods[0, :b], ln_g[0, 0], ln_b[0, 0], router_w, router_b,
                                            alpha, glu_w=(s5_w_val[0], s5_w_gate[0]))
    ys, pos0, pos1, wts = _moe(h_rows, route, counts, moe_w_gate[0], moe_w_up[0], moe_w_down[0])
    x2, h = _combine(ys, pos0, pos1, wts, x1, mods[0, :b], ln_g[0, 1], ln_b[0, 1], alpha,
                     next_mods=mods[1, :b])

    m = _pool_mix(h, pool_w[0], pool_scale[0])
    x3, h_rows, route, counts = _post_mixer(m, x2, mods[1, :b], ln_g[1, 0], ln_b[1, 0], router_w, router_b,
                                            alpha)
    ys, pos0, pos1, wts = _moe(h_rows, route, counts, moe_w_gate[1], moe_w_up[1], moe_w_down[1])
    (out,) = _combine(ys, pos0, pos1, wts, x3, mods[1, :b], ln_g[1, 1], ln_b[1, 1], alpha)
    return out
```

```python
import functools

import jax
import jax.numpy as jnp
from jax import lax
from jax.experimental import pallas as pl
from jax.experimental.pallas import tpu as pltpu

F32 = jnp.float32
BF16 = jnp.bfloat16
I32 = jnp.int32
HI = lax.Precision.HIGHEST

GRID_W = 64
S5_H = 16
S5_T = 16
POOL_WINDOWS = (2, 4, 8, 16)
N_EXPERT_GROUPS = 4
LN_EPS = 1e-5
LANES = 128
SUBLANES = 8
VMEM_LIMIT = 52 * 1024 * 1024

ROUTE_ROWS = 8


def _cparams(*sem):
    return pltpu.CompilerParams(dimension_semantics=sem, vmem_limit_bytes=VMEM_LIMIT)


def _mod_body(c_ref, w_ref, b_ref, o_ref):
    c = c_ref[...]
    s = c * jax.nn.sigmoid(c)
    o_ref[0] = jnp.dot(s, w_ref[0], precision=HI, preferred_element_type=F32) + b_ref[0]


def _modulation(cond, mod_w, mod_b):
    depth, d, n6 = mod_w.shape
    tn = min(n6, 1536)
    return pl.pallas_call(
        _mod_body,
        grid=(depth, n6 // tn),
        in_specs=[pl.BlockSpec((SUBLANES, d), lambda i, j: (0, 0)),
                  pl.BlockSpec((1, d, tn), lambda i, j: (i, 0, j)),
                  pl.BlockSpec((1, 1, tn), lambda i, j: (i, 0, j))],
        out_specs=pl.BlockSpec((1, SUBLANES, tn), lambda i, j: (i, 0, j)),
        out_shape=jax.ShapeDtypeStruct((depth, SUBLANES, n6), F32),
        compiler_params=_cparams("arbitrary", "arbitrary"),
        name="modulation",
    )(cond, mod_w, mod_b.reshape(depth, 1, n6))


def _s5_direction_terms(lam_re, lam_im, log_dt, b_re, b_im):
    lr = lam_re.astype(F32)
    li = lam_im.astype(F32)
    dt = jnp.exp(log_dt.astype(F32))[:, None]
    mag = jnp.exp(lr * dt)
    ar = mag * jnp.cos(li * dt)
    ai = mag * jnp.sin(li * dt)
    den = lr * lr + li * li
    nr = ar - 1.0
    fr = (nr * lr + ai * li) / den
    fi = (ai * lr - nr * li) / den
    br_, bi_ = b_re.astype(F32), b_im.astype(F32)
    bbr = fr[..., None] * br_ - fi[..., None] * bi_
    bbi = fr[..., None] * bi_ + fi[..., None] * br_
    k = jnp.arange(S5_T + 1, dtype=F32)[:, None, None]
    pm = jnp.exp(k * (lr * dt))
    pr = pm * jnp.cos(k * (li * dt))
    pi = pm * jnp.sin(k * (li * dt))
    return pr, pi, bbr, bbi


def _s5_weights(lam_re, lam_im, log_dt, b_re, b_im, c_re, c_im, d_skip):
    t = S5_T
    g, p = lam_re.shape[1:]
    h = b_re.shape[-1]
    terms = [_s5_direction_terms(lam_re[d], lam_im[d], log_dt[d], b_re[d], b_im[d]) for d in (0, 1)]
    pw = jnp.stack([jnp.transpose(terms[d][k], (1, 0, 2)) for d in (0, 1) for k in (0, 1)], axis=1)
    bt = jnp.stack([jnp.transpose(terms[d][k], (0, 2, 1)) for d in (0, 1) for k in (2, 3)], axis=1)
    cc = jnp.stack([c[d].astype(F32) for d in (0, 1) for c in (c_re, c_im)], axis=1)
    gp = 8
    spec4 = lambda rows: pl.BlockSpec((gp, 4, rows, p), lambda i: (i, 0, 0, 0))
    wide = pl.BlockSpec((gp, t * h, t * h), lambda i: (i, 0, 0))
    toep, w1, w2t = pl.pallas_call(
        functools.partial(_s5_prep_body, gp=gp),
        grid=(g // gp,),
        in_specs=[spec4(t + 1), spec4(h), spec4(h), pl.BlockSpec((gp, 1, h), lambda i: (i, 0, 0))],
        out_specs=[wide, wide, wide],
        out_shape=[jax.ShapeDtypeStruct((g, t * h, t * h), BF16) for _ in range(3)],
        compiler_params=_cparams("arbitrary"),
        name="s5_prep",
    )(pw, bt, cc, d_skip.astype(F32).reshape(g, 1, h))
    a_r = jnp.concatenate([pw[:, 0, t], pw[:, 2, t]], axis=-1)[:, None, :]
    a_i = jnp.concatenate([pw[:, 1, t], pw[:, 3, t]], axis=-1)[:, None, :]
    return toep, w1, w2t, a_r, a_i


def _cmul(ar, ai, br, bi):
    return ar * br - ai * bi, ar * bi + ai * br


def _s5_prep_body(pw_ref, bt_ref, cc_ref, d_ref, toep_ref, w1_ref, w2t_ref, *, gp):
    t = pw_ref.shape[2] - 1
    h = bt_ref.shape[2]
    nt = (((1,), (1,)), ((), ()))
    eye = (lax.broadcasted_iota(I32, (h, h), 0) == lax.broadcasted_iota(I32, (h, h), 1))
    for g in range(gp):
        prf, pif, prb, pib = (pw_ref[g, k] for k in range(4))
        btf = (bt_ref[g, 0], bt_ref[g, 1])
        btb = (bt_ref[g, 2], bt_ref[g, 3])
        ccf = (cc_ref[g, 0], cc_ref[g, 1])
        ccb = (cc_ref[g, 2], cc_ref[g, 3])
        caf, cab = [], []
        for j in range(t):
            f_r, f_i = _cmul(*btf, prf[t - 1 - j:t - j], pif[t - 1 - j:t - j])
            b_r, b_i = _cmul(*btb, prb[j:j + 1], pib[j:j + 1])
            w1_ref[g, j * h:(j + 1) * h, :] = jnp.concatenate([f_r, b_r, f_i, b_i], axis=1).astype(BF16)
            mf_r, mf_i = _cmul(*ccf, prf[j + 1:j + 2], pif[j + 1:j + 2])
            mb_r, mb_i = _cmul(*ccb, prb[t - j:t - j + 1], pib[t - j:t - j + 1])
            w2t_ref[g, j * h:(j + 1) * h, :] = jnp.concatenate([mf_r, mb_r, -mf_i, -mb_i],
                                                               axis=1).astype(BF16)
            caf.append(_cmul(*ccf, prf[j:j + 1], pif[j:j + 1]))
            cab.append(_cmul(*ccb, prb[t - 1 - j:t - j], pib[t - 1 - j:t - j]))

        def lag_kernels(btx, ca):
            car = jnp.concatenate([c[0] for c in ca], axis=0)
            cai = jnp.concatenate([c[1] for c in ca], axis=0)
            return (lax.dot_general(btx[0], car, nt, precision=HI, preferred_element_type=F32)
                    - lax.dot_general(btx[1], cai, nt, precision=HI, preferred_element_type=F32))

        kf = lag_kernels(btf, caf)
        kb = lag_kernels(btb, cab)
        skip = jnp.where(eye, jnp.broadcast_to(d_ref[g], (h, h)), 0.0)
        mid = kb[:, (t - 1) * h:] + kf[:, :h] + skip
        kwide = jnp.concatenate([kb[:, :(t - 1) * h], mid, kf[:, h:], jnp.zeros((h, h), F32)], axis=1)
        for i in range(t):
            off = (t - 1 - i) * h
            toep_ref[g, i * h:(i + 1) * h, :] = kwide[:, off:off + t * h].astype(BF16)


def _granule_transpose(v):
    lane = lax.broadcasted_iota(I32, v[0].shape, 1)
    v = list(v)
    for d in (4, 2, 1):
        low = ((lane >> 4) & d) == 0
        for j in range(SUBLANES):
            if j & d:
                continue
            a, b = v[j], v[j + d]
            v[j] = jnp.where(low, a, pltpu.roll(b, d * S5_H, 1))
            v[j + d] = jnp.where(low, pltpu.roll(a, LANES - d * S5_H, 1), b)
    return v


def _s5_body(x_ref, ctx_ref, mod_ref, cmod_ref, tt_ref, w1_ref, w2t_ref, ar_ref, ai_ref, y_ref,
             u_ref, s_ref, *, n_ctx_chunks, n_lat_chunks, gb):
    n_b = x_ref.shape[0]
    n_g = u_ref.shape[0]
    n_blocks = s_ref.shape[1] // SUBLANES
    half = SUBLANES // 2
    tok_blk = SUBLANES * S5_T
    lat_lo = n_ctx_chunks * n_b
    lat_rows = n_lat_chunks * n_b

    def chunk_rows(ref, b, tok0, shift, scale1):
        vs = [ref[b, pl.ds(tok0 + j, SUBLANES, stride=S5_T), :] * scale1 + shift for j in range(S5_T)]
        lo = _granule_transpose(vs[:SUBLANES])
        hi = _granule_transpose(vs[SUBLANES:])
        return lo, hi

    def put_rows(rows, chunk0, b):
        for q in range(n_g):
            for jh in range(2):
                u_ref[q, jh, pl.ds(chunk0 * n_b + b, SUBLANES, stride=n_b), :] = rows[jh][q]

    def u_rows(g, lo, n):
        return jnp.concatenate([u_ref[g, 0, lo:lo + n, :], u_ref[g, 1, lo:lo + n, :]], axis=1)

    for b in range(n_b):
        shift = jnp.broadcast_to(cmod_ref[0, 0:1, :], (SUBLANES, LANES))
        scale1 = 1.0 + jnp.broadcast_to(cmod_ref[0, 1:2, :], (SUBLANES, LANES))
        for cb in range(n_ctx_chunks // SUBLANES):
            rows = chunk_rows(ctx_ref, b, cb * tok_blk, shift, scale1)
            put_rows(rows, cb * SUBLANES, b)
            put_rows(rows, n_ctx_chunks + n_lat_chunks + cb * SUBLANES, b)
        shift = jnp.broadcast_to(mod_ref[b, 0:1, :], (SUBLANES, LANES))
        scale1 = 1.0 + jnp.broadcast_to(mod_ref[b, 1:2, :], (SUBLANES, LANES))

        def fill(cb, carry, b=b, shift=shift, scale1=scale1):
            put_rows(chunk_rows(x_ref, b, cb * tok_blk, shift, scale1), n_ctx_chunks + cb * SUBLANES, b)
            return carry

        lax.fori_loop(0, n_lat_chunks // SUBLANES, fill, 0)

    for g in range(n_g):
        s_ref[g] = jnp.dot(u_rows(g, 0, n_blocks * SUBLANES).astype(BF16), w1_ref[g],
                           preferred_element_type=F32)

    lane = lax.broadcasted_iota(I32, (SUBLANES, LANES), 1)
    row = lax.broadcasted_iota(I32, (SUBLANES, LANES), 0)
    is_fwd = lane < LANES // 2
    is_fwd2 = jnp.concatenate([is_fwd, is_fwd], axis=1)
    top = row < half
    zero = jnp.zeros((SUBLANES, LANES), F32)
    for g0 in range(0, n_g, gb):
        ars = [jnp.broadcast_to(ar_ref[g0 + g], (SUBLANES, LANES)) for g in range(gb)]
        ais = [jnp.broadcast_to(ai_ref[g0 + g], (SUBLANES, LANES)) for g in range(gb)]

        def step(k, carry, g0=g0, ars=ars, ais=ais):
            fo = pl.multiple_of(k * SUBLANES, SUBLANES)
            bo = pl.multiple_of((n_blocks - 1 - k) * SUBLANES, SUBLANES)
            new = []
            for g in range(gb):
                xr, xi = carry[2 * g], carry[2 * g + 1]
                vf = s_ref[g0 + g, pl.ds(fo, SUBLANES), :]
                vb_raw = s_ref[g0 + g, pl.ds(bo, SUBLANES), :]
                vb = pltpu.roll(vb_raw, half, 0)
                vr = jnp.where(is_fwd, vf[:, :LANES], vb[:, :LANES])
                vi = jnp.where(is_fwd, vf[:, LANES:], vb[:, LANES:])
                ar, ai = ars[g], ais[g]
                yr = ar * xr - ai * xi + vr
                yi = ar * xi + ai * xr + vi
                yrr = pltpu.roll(yr, half, 0)
                yir = pltpu.roll(yi, half, 0)
                zr = ar * yrr - ai * yir + vr
                zi = ar * yir + ai * yrr + vi
                inc = jnp.concatenate([jnp.where(top, xr, yrr), jnp.where(top, xi, yir)], axis=1)
                s_ref[g0 + g, pl.ds(fo, SUBLANES), :] = jnp.where(is_fwd2, inc, vf)
                s_ref[g0 + g, pl.ds(bo, SUBLANES), :] = jnp.where(is_fwd2, vb_raw, pltpu.roll(inc, half, 0))
                new.append(jnp.where(top, pltpu.roll(zr, half, 0), zr))
                new.append(jnp.where(top, pltpu.roll(zi, half, 0), zi))
            return tuple(new)

        lax.fori_loop(0, (n_ctx_chunks + n_lat_chunks) // 2, step, tuple(zero for _ in range(2 * gb)))

    for g in range(n_g):
        y = (jnp.dot(u_rows(g, lat_lo, lat_rows).astype(BF16), tt_ref[g], preferred_element_type=F32)
             + lax.dot_general(s_ref[g, lat_lo:lat_lo + lat_rows, :].astype(BF16), w2t_ref[g],
                               (((1,), (1,)), ((), ())), preferred_element_type=F32))
        for jh in range(2):
            u_ref[g, jh, lat_lo:lat_lo + lat_rows, :] = y[:, jh * LANES:(jh + 1) * LANES]

    for b in range(n_b):
        def emit(cb, carry, b=b):
            for jh in range(S5_T // SUBLANES):
                w = [u_ref[q, jh, pl.ds((n_ctx_chunks + cb * SUBLANES) * n_b + b, SUBLANES, stride=n_b), :]
                     for q in range(n_g)]
                v = _granule_transpose(w)
                for j in range(SUBLANES):
                    y_ref[b, pl.ds(cb * tok_blk + jh * SUBLANES + j, SUBLANES, stride=S5_T), :] = v[j]
            return carry

        lax.fori_loop(0, n_lat_chunks // SUBLANES, emit, 0)


def _s5_mix(x, ctx, mods, cmods, toep, w1, w2t, a_r, a_i):
    b, l, d = x.shape
    n_ctx = ctx.shape[1]
    n_g = LANES // S5_H
    n_ctx_chunks, n_lat_chunks = n_ctx // S5_T, l // S5_T
    assert b * 2 == SUBLANES and n_ctx_chunks % SUBLANES == 0 and n_lat_chunks % SUBLANES == 0
    rows = (2 * n_ctx_chunks + n_lat_chunks) * b
    w = S5_T * S5_H
    body = functools.partial(_s5_body, n_ctx_chunks=n_ctx_chunks, n_lat_chunks=n_lat_chunks, gb=4)
    lane_tile = lambda i: (0, 0, i)
    wspec = pl.BlockSpec((n_g, w, w), lambda i: (i, 0, 0))
    aspec = pl.BlockSpec((n_g, 1, w // 2), lambda i: (i, 0, 0))
    return pl.pallas_call(
        body,
        grid=(d // LANES,),
        in_specs=[pl.BlockSpec((b, l, LANES), lane_tile, pipeline_mode=pl.Buffered(1)),
                  pl.BlockSpec((b, n_ctx, LANES), lane_tile),
                  pl.BlockSpec((b, 6, LANES), lane_tile),
                  pl.BlockSpec((1, 6, LANES), lane_tile),
                  wspec, wspec, wspec, aspec, aspec],
        out_specs=pl.BlockSpec((b, l, LANES), lane_tile, pipeline_mode=pl.Buffered(1)),
        out_shape=jax.ShapeDtypeStruct((b, l, d), F32),
        scratch_shapes=[pltpu.VMEM((n_g, w // LANES, rows, LANES), F32), pltpu.VMEM((n_g, rows, w), F32)],
        compiler_params=_cparams("arbitrary"),
        name="s5_mix",
    )(x, ctx, mods, cmods, toep, w1, w2t, a_r, a_i)


def _layer_norm(r, g, b):
    mu = jnp.mean(r, axis=-1, keepdims=True)
    xc = r - mu
    var = jnp.mean(xc * xc, axis=-1, keepdims=True)
    return xc * lax.rsqrt(var + LN_EPS) * g + b


def _max2_of4(a, b, c, d):
    h1, l1 = jnp.maximum(a, b), jnp.minimum(a, b)
    h2, l2 = jnp.maximum(c, d), jnp.minimum(c, d)
    return jnp.maximum(h1, h2) + jnp.maximum(jnp.minimum(h1, h2), jnp.maximum(l1, l2))


def _argmax_first(vals):
    idx = jnp.zeros(vals[0].shape, I32)
    best = vals[0]
    for j in range(1, len(vals)):
        upd = vals[j] > best
        idx = jnp.where(upd, j, idx)
        best = jnp.where(upd, vals[j], best)
    return idx, best


def _route(logits_t, count_ref, route_ref):
    n_e, tm = logits_t.shape
    per = n_e // N_EXPERT_GROUPS
    mx = jnp.max(logits_t, axis=0, keepdims=True)
    ex = jnp.exp(logits_t - mx)
    sc = ex / jnp.sum(ex, axis=0, keepdims=True)
    rows = [sc[e:e + 1, :] for e in range(n_e)]
    gscore = [_max2_of4(*rows[per * g:per * (g + 1)]) for g in range(N_EXPERT_GROUPS)]
    best, _ = _argmax_first(gscore)
    vals = []
    for j in range(per):
        v = rows[per * (N_EXPERT_GROUPS - 1) + j]
        for g in range(N_EXPERT_GROUPS - 2, -1, -1):
            v = jnp.where(best == g, rows[per * g + j], v)
        vals.append(v)
    i1, m1 = _argmax_first(vals)
    i2, m2 = _argmax_first([jnp.where(i1 == j, -1.0, vals[j]) for j in range(per)])
    den = m1 + m2
    e0 = best * per + i1
    e1 = best * per + i2

    eidx = lax.broadcasted_iota(I32, (n_e, tm), 0)
    hit0 = eidx == e0
    hit1 = eidx == e1
    onehot = jnp.where(hit0, 1.0, jnp.where(hit1, 1.0, 0.0))
    src = lax.broadcasted_iota(I32, (tm, tm), 0)
    dst = lax.broadcasted_iota(I32, (tm, tm), 1)
    tri = jnp.where(src <= dst, 1.0, 0.0).astype(BF16)
    cum = jnp.dot(onehot.astype(BF16), tri, preferred_element_type=F32)
    excl = cum - onehot + count_ref[:, 0:1]
    rank0 = jnp.sum(jnp.where(hit0, excl, 0.0), axis=0, keepdims=True)
    rank1 = jnp.sum(jnp.where(hit1, excl, 0.0), axis=0, keepdims=True)
    count_ref[...] = count_ref[...] + jnp.sum(onehot, axis=1, keepdims=True)

    zero = jnp.zeros((1, tm), F32)
    route_ref[...] = jnp.concatenate(
        [e0.astype(F32), e1.astype(F32), m1 / den, m2 / den, rank0, rank1, zero, zero], axis=0)


def _store_rows_tiled(ref, val):
    for s in range(ref.shape[1]):
        ref[:, s, :] = val[:, s * LANES:(s + 1) * LANES]


def _load_rows_tiled(ref):
    return jnp.concatenate([ref[:, s, :] for s in range(ref.shape[1])], axis=-1)


def _post_mixer_body(m_ref, x_ref, mod_ref, lng_ref, lnb_ref, rwt_ref, rb_ref, *rest, glu, alpha):
    if glu:
        wv_ref, wg_ref, x1_ref, h_ref, route_ref, cnt_out_ref, cnt_ref = rest
        a = jax.nn.gelu(m_ref[0], approximate=True).astype(BF16)
        val = jnp.dot(a, wv_ref[...], preferred_element_type=F32)
        gate = jnp.dot(a, wg_ref[...], preferred_element_type=F32)
        m = val * jax.nn.sigmoid(gate)
    else:
        x1_ref, h_ref, route_ref, cnt_out_ref, cnt_ref = rest
        m = m_ref[0]

    @pl.when((pl.program_id(0) == 0) & (pl.program_id(1) == 0))
    def _():
        cnt_ref[...] = jnp.zeros_like(cnt_ref)

    g1 = mod_ref[0, 2:3, :]
    sh2 = mod_ref[0, 3:4, :]
    sc2 = mod_ref[0, 4:5, :]
    x1 = _layer_norm(alpha * x_ref[0] + g1 * m, lng_ref[...], lnb_ref[...])
    x1_ref[0] = x1
    h = x1 * (1.0 + sc2) + sh2
    _store_rows_tiled(h_ref, h)
    logits_t = lax.dot_general(rwt_ref[...], h, (((1,), (1,)), ((), ())),
                               precision=HI, preferred_element_type=F32) + rb_ref[...]
    _route(logits_t, cnt_ref, route_ref)
    cnt_out_ref[...] = cnt_ref[...]


def _post_mixer(m, x, mods, ln_g, ln_b, router_w, router_b, alpha, glu_w=None):
    b, l, d = x.shape
    n_e = router_w.shape[1]
    tm = min(l, 512)
    nt = l // tm
    tok = lambda i, j: (i, j, 0)
    const2 = lambda i, j: (0, 0)
    in_specs = [pl.BlockSpec((1, tm, d), tok),
                pl.BlockSpec((1, tm, d), tok),
                pl.BlockSpec((1, 6, d), lambda i, j: (i, 0, 0)),
                pl.BlockSpec((1, d), const2),
                pl.BlockSpec((1, d), const2),
                pl.BlockSpec((n_e, d), const2),
                pl.BlockSpec((n_e, 1), const2)]
    args = [m, x, mods, ln_g.reshape(1, d), ln_b.reshape(1, d), router_w.T, router_b.reshape(n_e, 1)]
    if glu_w is not None:
        in_specs += [pl.BlockSpec((d, d), const2), pl.BlockSpec((d, d), const2)]
        args += [glu_w[0].astype(BF16), glu_w[1].astype(BF16)]
    out_shape = [jax.ShapeDtypeStruct((b, l, d), F32),
                 jax.ShapeDtypeStruct((b * l, d // LANES, LANES), F32),
                 jax.ShapeDtypeStruct((ROUTE_ROWS, b * l), F32),
                 jax.ShapeDtypeStruct((n_e, LANES), F32)]
    out_specs = [pl.BlockSpec((1, tm, d), tok),
                 pl.BlockSpec((tm, d // LANES, LANES), lambda i, j: (i * nt + j, 0, 0)),
                 pl.BlockSpec((ROUTE_ROWS, tm), lambda i, j: (0, i * nt + j)),
                 pl.BlockSpec((n_e, LANES), const2)]
    return pl.pallas_call(
        functools.partial(_post_mixer_body, glu=glu_w is not None, alpha=alpha),
        grid=(b, nt),
        in_specs=in_specs,
        out_specs=out_specs,
        out_shape=out_shape,
        scratch_shapes=[pltpu.VMEM((n_e, LANES), F32)],
        compiler_params=_cparams("arbitrary", "arbitrary"),
        name="post_mixer_glu" if glu_w is not None else "post_mixer",
    )(*args)


def _dispatch_plan(route, counts, tile):
    n_e = counts.shape[0]
    cnt = counts[:, 0].astype(I32)
    tiles = (cnt + tile - 1) // tile
    tile_end = jnp.cumsum(tiles)
    row_off = (tile_end - tiles) * tile
    eids = jnp.arange(n_e, dtype=I32)[:, None]

    def row_offset_of(e):
        return jnp.sum(jnp.where(e[None, :] == eids, row_off[:, None], 0), axis=0)

    pos0 = row_offset_of(route[0].astype(I32)) + route[4].astype(I32)
    pos1 = row_offset_of(route[1].astype(I32)) + route[5].astype(I32)
    n_tiles_max = (2 * route.shape[1]) // tile + n_e
    tile_ids = jnp.arange(n_tiles_max, dtype=I32)
    tile_expert = jnp.minimum(jnp.sum((tile_end[None, :] <= tile_ids[:, None]).astype(I32), axis=1), n_e - 1)
    n_used = tile_end[-1:].astype(I32)
    pad_start = (row_off + cnt).astype(I32)
    pad_len = (tiles * tile - cnt).astype(I32)
    return pos0, pos1, tile_expert, n_used, pad_start, pad_len, n_tiles_max


def _zero_pad_rows(pad_ref, len_ref, zero_ref, hs_ref, sem, n_e, tile, wait):
    for e in range(n_e):
        ln = len_ref[e]
        bit = tile // 2
        while bit >= 1:
            @pl.when((ln & bit) != 0)
            def _(bit=bit, ln=ln, e=e):
                start = pad_ref[e] + (ln & ~(2 * bit - 1))
                cp = pltpu.make_async_copy(zero_ref.at[pl.ds(0, bit)], hs_ref.at[pl.ds(start, bit)], sem)
                if wait:
                    cp.wait()
                else:
                    cp.start()
            bit //= 2


def _zero_unused_tiles(nu_ref, zero_ref, hs_ref, sem, tile, wait):
    half = tile // 2
    n_halves = 2 * (hs_ref.shape[0] // tile - nu_ref[0])

    def body(j, carry):
        cp = pltpu.make_async_copy(zero_ref, hs_ref.at[pl.ds(nu_ref[0] * tile + j * half, half)], sem)
        if wait:
            cp.wait()
        else:
            cp.start()
        return carry

    lax.fori_loop(0, n_halves, body, 0)


def _dispatch_body(pos0_ref, pos1_ref, pad_ref, len_ref, nu_ref, h_ref, hs_ref, zero_ref, sem_z, sem, *,
                   tm, tile, n_e):
    i = pl.program_id(0)

    @pl.when(i == 0)
    def _():
        zero_ref[...] = jnp.zeros_like(zero_ref)
        _zero_pad_rows(pad_ref, len_ref, zero_ref, hs_ref, sem_z, n_e, tile, False)
        _zero_unused_tiles(nu_ref, zero_ref, hs_ref, sem_z, tile, False)

    @pl.when(i == pl.num_programs(0) - 1)
    def _():
        _zero_pad_rows(pad_ref, len_ref, zero_ref, hs_ref, sem_z, n_e, tile, True)
        _zero_unused_tiles(nu_ref, zero_ref, hs_ref, sem_z, tile, True)

    base = i * tm

    def issue(r, carry):
        pltpu.make_async_copy(h_ref.at[r], hs_ref.at[pos0_ref[base + r]], sem).start()
        pltpu.make_async_copy(h_ref.at[r], hs_ref.at[pos1_ref[base + r]], sem).start(priority=1)
        return carry

    lax.fori_loop(0, tm, issue, 0, unroll=8)
    for _ in range(2):
        pltpu.make_async_copy(h_ref, hs_ref.at[pl.ds(0, tm)], sem).wait()


def _dispatch(h_rows, pos0, pos1, pad_start, pad_len, n_used, n_rows, tile):
    n, s, w = h_rows.shape
    n_e = pad_start.shape[0]
    tm = min(n, 256)
    grid_spec = pltpu.PrefetchScalarGridSpec(
        num_scalar_prefetch=5,
        grid=(n // tm,),
        in_specs=[pl.BlockSpec((tm, s, w), lambda i, *_: (i, 0, 0))],
        out_specs=pl.BlockSpec(memory_space=pl.ANY),
        scratch_shapes=[pltpu.VMEM((tile // 2, s, w), F32),
                        pltpu.SemaphoreType.DMA(()), pltpu.SemaphoreType.DMA(())],
    )
    return pl.pallas_call(
        functools.partial(_dispatch_body, tm=tm, tile=tile, n_e=n_e),
        grid_spec=grid_spec,
        out_shape=jax.ShapeDtypeStruct((n_rows, s, w), F32),
        compiler_params=_cparams("arbitrary"),
        name="moe_dispatch",
    )(pos0, pos1, pad_start, pad_len, n_used, h_rows)


def _expert_body(te_ref, nu_ref, hs_ref, wg_ref, wu_ref, wd_ref, ys_ref, wgb, wub, wdb):
    i = pl.program_id(0)
    first = jnp.logical_or(i == 0, te_ref[i] != te_ref[jnp.maximum(i - 1, 0)])

    @pl.when(jnp.logical_and(i < nu_ref[0], first))
    def _():
        wgb[...] = wg_ref[0].astype(BF16)
        wub[...] = wu_ref[0].astype(BF16)
        wdb[...] = wd_ref[0].astype(BF16)

    @pl.when(i < nu_ref[0])
    def _():
        x = _load_rows_tiled(hs_ref).astype(BF16)
        gate = jnp.dot(x, wgb[...], preferred_element_type=F32)
        up = jnp.dot(x, wub[...], preferred_element_type=F32)
        a = (gate * jax.nn.sigmoid(gate) * up).astype(BF16)
        _store_rows_tiled(ys_ref, jnp.dot(a, wdb[...], preferred_element_type=F32))

    @pl.when(i >= nu_ref[0])
    def _():
        ys_ref[...] = jnp.zeros_like(ys_ref)


def _expert_ffn(hs, tile_expert, n_used, w_gate, w_up, w_down, n_tiles, tile):
    n_rows, s, w = hs.shape
    n_e, d, f = w_gate.shape
    wmap = lambda i, te, nu: (te[i], 0, 0)
    grid_spec = pltpu.PrefetchScalarGridSpec(
        num_scalar_prefetch=2,
        grid=(n_tiles,),
        in_specs=[pl.BlockSpec((tile, s, w), lambda i, te, nu: (jnp.maximum(jnp.minimum(i, nu[0] - 1), 0), 0, 0)),
                  pl.BlockSpec((1, d, f), wmap),
                  pl.BlockSpec((1, d, f), wmap),
                  pl.BlockSpec((1, f, d), wmap)],
        out_specs=pl.BlockSpec((tile, s, w), lambda i, te, nu: (i, 0, 0)),
        scratch_shapes=[pltpu.VMEM((d, f), BF16), pltpu.VMEM((d, f), BF16), pltpu.VMEM((f, d), BF16)],
    )
    return pl.pallas_call(
        _expert_body,
        grid_spec=grid_spec,
        out_shape=jax.ShapeDtypeStruct((n_tiles * tile, s, w), F32),
        compiler_params=_cparams("arbitrary"),
        name="moe_experts",
    )(tile_expert, n_used, hs, w_gate, w_up, w_down)


def _combine_body(pos0_ref, pos1_ref, x_ref, mod_ref, wts_ref, lng_ref, lnb_ref, ys_ref, *rest,
                  tm, alpha, next_mod):
    if next_mod:
        nmod_ref, x2_ref, h_ref, buf0, buf1, sem = rest
    else:
        x2_ref, buf0, buf1, sem = rest
    base = (pl.program_id(0) * pl.num_programs(1) + pl.program_id(1)) * tm

    def issue(r, carry):
        pltpu.make_async_copy(ys_ref.at[pos0_ref[base + r]], buf0.at[r], sem).start()
        pltpu.make_async_copy(ys_ref.at[pos1_ref[base + r]], buf1.at[r], sem).start(priority=1)
        return carry

    lax.fori_loop(0, tm, issue, 0, unroll=8)
    for buf in (buf0, buf1):
        pltpu.make_async_copy(ys_ref.at[pl.ds(0, tm)], buf, sem).wait()

    w = wts_ref[...]
    moe = w[:, 0:1] * _load_rows_tiled(buf0) + w[:, 1:2] * _load_rows_tiled(buf1)
    g2 = mod_ref[0, 5:6, :]
    x2 = _layer_norm(alpha * x_ref[0] + g2 * moe, lng_ref[...], lnb_ref[...])
    x2_ref[0] = x2
    if next_mod:
        h_ref[0] = x2 * (1.0 + nmod_ref[0, 1:2, :]) + nmod_ref[0, 0:1, :]


def _combine(ys, pos0, pos1, wts, x, mods, ln_g, ln_b, alpha, next_mods=None):
    b, l, d = x.shape
    n_rows, s, w = ys.shape
    tm = min(l, 256)
    nt = l // tm
    tok = lambda i, j, *_: (i, j, 0)
    bat = lambda i, j, *_: (i, 0, 0)
    const2 = lambda i, j, *_: (0, 0)
    in_specs = [pl.BlockSpec((1, tm, d), tok),
                pl.BlockSpec((1, 6, d), bat),
                pl.BlockSpec((tm, 2), lambda i, j, *_: (i * nt + j, 0)),
                pl.BlockSpec((1, d), const2),
                pl.BlockSpec((1, d), const2),
                pl.BlockSpec(memory_space=pl.ANY)]
    args = [x, mods, wts, ln_g.reshape(1, d), ln_b.reshape(1, d), ys]
    out_shape = [jax.ShapeDtypeStruct((b, l, d), F32)]
    out_specs = [pl.BlockSpec((1, tm, d), tok)]
    if next_mods is not None:
        in_specs.append(pl.BlockSpec((1, 6, d), bat))
        args.append(next_mods)
        out_shape.append(jax.ShapeDtypeStruct((b, l, d), F32))
        out_specs.append(pl.BlockSpec((1, tm, d), tok))
    grid_spec = pltpu.PrefetchScalarGridSpec(
        num_scalar_prefetch=2,
        grid=(b, nt),
        in_specs=in_specs,
        out_specs=out_specs,
        scratch_shapes=[pltpu.VMEM((tm, s, w), F32), pltpu.VMEM((tm, s, w), F32),
                        pltpu.SemaphoreType.DMA(())],
    )
    return pl.pallas_call(
        functools.partial(_combine_body, tm=tm, alpha=alpha, next_mod=next_mods is not None),
        grid_spec=grid_spec,
        out_shape=out_shape,
        compiler_params=_cparams("arbitrary", "arbitrary"),
        name="moe_combine",
    )(pos0, pos1, *args)


def _moe(h_rows, route, counts, w_gate, w_up, w_down):
    n = h_rows.shape[0]
    tile = min(n, 512)
    pos0, pos1, tile_expert, n_used, pad_start, pad_len, n_tiles = _dispatch_plan(route, counts, tile)
    hs = _dispatch(h_rows, pos0, pos1, pad_start, pad_len, n_used, n_tiles * tile, tile)
    ys = _expert_ffn(hs, tile_expert, n_used, w_gate, w_up, w_down, n_tiles, tile)
    return ys, pos0, pos1, jnp.transpose(route[2:4])


def _pool_group(h_ref, w_ref, sc_ref, o_ref, col_ref, k, n_rows):
    n = n_rows * GRID_W
    c = h_ref.shape[2]
    blk = 4 * GRID_W
    half = k // 2
    pad = half * GRID_W
    ti = lax.broadcasted_iota(I32, (blk, blk), 0)
    si = lax.broadcasted_iota(I32, (blk, blk), 1)
    shift = GRID_W.bit_length() - 1
    same_row = (ti >> shift) == (si >> shift)
    band = jnp.where(same_row & (si - ti >= -half) & (si - ti <= half - 1), 1.0, 0.0)
    col_ref[0:pad, :] = jnp.zeros((pad, c), F32)
    col_ref[pad + n:pad + n + pad, :] = jnp.zeros((pad, c), F32)
    for b0 in range(0, n, blk):
        col_ref[pad + b0:pad + b0 + blk, :] = jnp.dot(band, h_ref[0, b0:b0 + blk, :], precision=HI,
                                                      preferred_element_type=F32)
    acc = col_ref[0:n, :]
    for j in range(1, k):
        acc = acc + col_ref[j * GRID_W:j * GRID_W + n, :]
    t = lax.broadcasted_iota(I32, (n, 1), 0)
    wc = t & (GRID_W - 1)
    wr = t >> shift
    cnt_c = jnp.minimum(wc + half - 1, GRID_W - 1) - jnp.maximum(wc - half, 0) + 1
    cnt_r = jnp.minimum(wr + half - 1, n_rows - 1) - jnp.maximum(wr - half, 0) + 1
    mean = acc / (cnt_c * cnt_r).astype(F32)
    pooled = (mean - h_ref[0]).astype(BF16)
    o_ref[0] = jnp.dot(pooled, w_ref[0].astype(BF16), preferred_element_type=F32) * sc_ref[...]


def _pool_body(h_ref, w_ref, sc_ref, o_ref, col_ref, *, n_rows):
    g = pl.program_id(1)
    for gi, k in enumerate(POOL_WINDOWS):
        @pl.when(g == gi)
        def _(k=k):
            _pool_group(h_ref, w_ref, sc_ref, o_ref, col_ref, k, n_rows)


def _pool_mix(h, w_grp, scale):
    b, n, d = h.shape
    n_g, c, _ = w_grp.shape
    n_rows = n // GRID_W
    pad = (max(POOL_WINDOWS) // 2) * GRID_W
    return pl.pallas_call(
        functools.partial(_pool_body, n_rows=n_rows),
        grid=(b, n_g),
        in_specs=[pl.BlockSpec((1, n, c), lambda i, j: (i, 0, j)),
                  pl.BlockSpec((1, c, c), lambda i, j: (j, 0, 0)),
                  pl.BlockSpec((1, c), lambda i, j: (0, j))],
        out_specs=pl.BlockSpec((1, n, c), lambda i, j: (i, 0, j)),
        out_shape=jax.ShapeDtypeStruct((b, n, d), F32),
        scratch_shapes=[pltpu.VMEM((n + 2 * pad, c), F32)],
        compiler_params=_cparams("arbitrary", "arbitrary"),
        name="pool_mix",
    )(h, w_grp, scale.reshape(1, d))


def kernel(x, c, ctx, c_ctx, mod_w, mod_b, ln_g, ln_b, s5_lam_re, s5_lam_im, s5_log_dt, s5_b_re, s5_b_im,
           s5_c_re, s5_c_im, s5_d, s5_w_val, s5_w_gate, pool_w, pool_scale, router_w, router_b,
           moe_w_gate, moe_w_up, moe_w_down):
    b, l, d = x.shape
    depth = mod_w.shape[0]
    assert depth == 2 and b + 1 <= SUBLANES and d % LANES == 0 and GRID_W & (GRID_W - 1) == 0
    alpha = (2 * depth) ** 0.25

    cond = jnp.zeros((SUBLANES, d), F32).at[:b].set(c).at[b].set(c_ctx)
    mods = _modulation(cond, mod_w, mod_b).reshape(depth, SUBLANES, 6, d)

    toep, w1, w2t, a_r, a_i = _s5_weights(s5_lam_re[0], s5_lam_im[0], s5_log_dt[0], s5_b_re[0], s5_b_im[0],
                                          s5_c_re[0], s5_c_im[0], s5_d[0])
    y = _s5_mix(x, ctx, mods[0, :b], mods[0, b:b + 1], toep, w1, w2t, a_r, a_i)
    x1, h_rows, route, counts = _post_mixer(y, x, mods[0, :b], ln_g[0, 0], ln_b[0, 0], router_w, router_b,
                                            alpha, glu_w=(s5_w_val[0], s5_w_gate[0]))
    ys, pos0, pos1, wts = _moe(h_rows, route, counts, moe_w_gate[0], moe_w_up[0], moe_w_down[0])
    x2, h = _combine(ys, pos0, pos1, wts, x1, mods[0, :b], ln_g[0, 1], ln_b[0, 1], alpha,
                     next_mods=mods[1, :b])

    m = _pool_mix(h, pool_w[0], pool_scale[0])
    x3, h_rows, route, counts = _post_mixer(m, x2, mods[1, :b], ln_g[1, 0], ln_b[1, 0], router_w, router_b,
                                            alpha)
    ys, pos0, pos1, wts = _moe(h_rows, route, counts, moe_w_gate[1], moe_w_up[1], moe_w_down[1])
    (out,) = _combine(ys, pos0, pos1, wts, x3, mods[1, :b], ln_g[1, 1], ln_b[1, 1], alpha)
    return out
```

```python
import functools

import jax
import jax.numpy as jnp
from jax import lax
from jax.experimental import pallas as pl
from jax.experimental.pallas import tpu as pltpu

F32 = jnp.float32
BF16 = jnp.bfloat16
I32 = jnp.int32
HI = lax.Precision.HIGHEST

GRID_W = 64
S5_H = 16
S5_T = 16
POOL_WINDOWS = (2, 4, 8, 16)
N_EXPERT_GROUPS = 4
LN_EPS = 1e-5
LANES = 128
SUBLANES = 8
VMEM_LIMIT = 52 * 1024 * 1024

ROUTE_ROWS = 8


def _cparams(*sem):
    return pltpu.CompilerParams(dimension_semantics=sem, vmem_limit_bytes=VMEM_LIMIT)


def _mod_body(c_ref, w_ref, b_ref, o_ref):
    c = c_ref[...]
    s = c * jax.nn.sigmoid(c)
    o_ref[0] = jnp.dot(s, w_ref[0], precision=HI, preferred_element_type=F32) + b_ref[0]


def _modulation(cond, mod_w, mod_b):
    depth, d, n6 = mod_w.shape
    tn = min(n6, 1536)
    return pl.pallas_call(
        _mod_body,
        grid=(depth, n6 // tn),
        in_specs=[pl.BlockSpec((SUBLANES, d), lambda i, j: (0, 0)),
                  pl.BlockSpec((1, d, tn), lambda i, j: (i, 0, j)),
                  pl.BlockSpec((1, 1, tn), lambda i, j: (i, 0, j))],
        out_specs=pl.BlockSpec((1, SUBLANES, tn), lambda i, j: (i, 0, j)),
        out_shape=jax.ShapeDtypeStruct((depth, SUBLANES, n6), F32),
        compiler_params=_cparams("arbitrary", "arbitrary"),
        name="modulation",
    )(cond, mod_w, mod_b.reshape(depth, 1, n6))


def _s5_direction_terms(lam_re, lam_im, log_dt, b_re, b_im):
    lr = lam_re.astype(F32)
    li = lam_im.astype(F32)
    dt = jnp.exp(log_dt.astype(F32))[:, None]
    mag = jnp.exp(lr * dt)
    ar = mag * jnp.cos(li * dt)
    ai = mag * jnp.sin(li * dt)
    den = lr * lr + li * li
    nr = ar - 1.0
    fr = (nr * lr + ai * li) / den
    fi = (ai * lr - nr * li) / den
    br_, bi_ = b_re.astype(F32), b_im.astype(F32)
    bbr = fr[..., None] * br_ - fi[..., None] * bi_
    bbi = fr[..., None] * bi_ + fi[..., None] * br_
    k = jnp.arange(S5_T + 1, dtype=F32)[:, None, None]
    pm = jnp.exp(k * (lr * dt))
    pr = pm * jnp.cos(k * (li * dt))
    pi = pm * jnp.sin(k * (li * dt))
    return pr, pi, bbr, bbi


def _s5_weights(lam_re, lam_im, log_dt, b_re, b_im, c_re, c_im, d_skip):
    t = S5_T
    g, p = lam_re.shape[1:]
    h = b_re.shape[-1]
    terms = [_s5_direction_terms(lam_re[d], lam_im[d], log_dt[d], b_re[d], b_im[d]) for d in (0, 1)]
    pw = jnp.stack([jnp.transpose(terms[d][k], (1, 0, 2)) for d in (0, 1) for k in (0, 1)], axis=1)
    bt = jnp.stack([jnp.transpose(terms[d][k], (0, 2, 1)) for d in (0, 1) for k in (2, 3)], axis=1)
    cc = jnp.stack([c[d].astype(F32) for d in (0, 1) for c in (c_re, c_im)], axis=1)
    gp = 8
    spec4 = lambda rows: pl.BlockSpec((gp, 4, rows, p), lambda i: (i, 0, 0, 0))
    wide = pl.BlockSpec((gp, t * h, t * h), lambda i: (i, 0, 0))
    toep, w1, w2t = pl.pallas_call(
        functools.partial(_s5_prep_body, gp=gp),
        grid=(g // gp,),
        in_specs=[spec4(t + 1), spec4(h), spec4(h), pl.BlockSpec((gp, 1, h), lambda i: (i, 0, 0))],
        out_specs=[wide, wide, wide],
        out_shape=[jax.ShapeDtypeStruct((g, t * h, t * h), BF16) for _ in range(3)],
        compiler_params=_cparams("arbitrary"),
        name="s5_prep",
    )(pw, bt, cc, d_skip.astype(F32).reshape(g, 1, h))
    a_r = jnp.concatenate([pw[:, 0, t], pw[:, 2, t]], axis=-1)[:, None, :]
    a_i = jnp.concatenate([pw[:, 1, t], pw[:, 3, t]], axis=-1)[:, None, :]
    return toep, w1, w2t, a_r, a_i


def _cmul(ar, ai, br, bi):
    return ar * br - ai * bi, ar * bi + ai * br


def _s5_prep_body(pw_ref, bt_ref, cc_ref, d_ref, toep_ref, w1_ref, w2t_ref, *, gp):
    t = pw_ref.shape[2] - 1
    h = bt_ref.shape[2]
    nt = (((1,), (1,)), ((), ()))
    eye = (lax.broadcasted_iota(I32, (h, h), 0) == lax.broadcasted_iota(I32, (h, h), 1))
    for g in range(gp):
        prf, pif, prb, pib = (pw_ref[g, k] for k in range(4))
        btf = (bt_ref[g, 0], bt_ref[g, 1])
        btb = (bt_ref[g, 2], bt_ref[g, 3])
        ccf = (cc_ref[g, 0], cc_ref[g, 1])
        ccb = (cc_ref[g, 2], cc_ref[g, 3])
        caf, cab = [], []
        for j in range(t):
            f_r, f_i = _cmul(*btf, prf[t - 1 - j:t - j], pif[t - 1 - j:t - j])
            b_r, b_i = _cmul(*btb, prb[j:j + 1], pib[j:j + 1])
            w1_ref[g, j * h:(j + 1) * h, :] = jnp.concatenate([f_r, b_r, f_i, b_i], axis=1).astype(BF16)
            mf_r, mf_i = _cmul(*ccf, prf[j + 1:j + 2], pif[j + 1:j + 2])
            mb_r, mb_i = _cmul(*ccb, prb[t - j:t - j + 1], pib[t - j:t - j + 1])
            w2t_ref[g, j * h:(j + 1) * h, :] = jnp.concatenate([mf_r, mb_r, -mf_i, -mb_i],
                                                               axis=1).astype(BF16)
            caf.append(_cmul(*ccf, prf[j:j + 1], pif[j:j + 1]))
            cab.append(_cmul(*ccb, prb[t - 1 - j:t - j], pib[t - 1 - j:t - j]))

        def lag_kernels(btx, ca):
            car = jnp.concatenate([c[0] for c in ca], axis=0)
            cai = jnp.concatenate([c[1] for c in ca], axis=0)
            return (lax.dot_general(btx[0], car, nt, precision=HI, preferred_element_type=F32)
                    - lax.dot_general(btx[1], cai, nt, precision=HI, preferred_element_type=F32))

        kf = lag_kernels(btf, caf)
        kb = lag_kernels(btb, cab)
        skip = jnp.where(eye, jnp.broadcast_to(d_ref[g], (h, h)), 0.0)
        mid = kb[:, (t - 1) * h:] + kf[:, :h] + skip
        kwide = jnp.concatenate([kb[:, :(t - 1) * h], mid, kf[:, h:], jnp.zeros((h, h), F32)], axis=1)
        for i in range(t):
            off = (t - 1 - i) * h
            toep_ref[g, i * h:(i + 1) * h, :] = kwide[:, off:off + t * h].astype(BF16)


def _granule_transpose(v):
    lane = lax.broadcasted_iota(I32, v[0].shape, 1)
    v = list(v)
    for d in (4, 2, 1):
        low = ((lane >> 4) & d) == 0
        for j in range(SUBLANES):
            if j & d:
                continue
            a, b = v[j], v[j + d]
            v[j] = jnp.where(low, a, pltpu.roll(b, d * S5_H, 1))
            v[j + d] = jnp.where(low, pltpu.roll(a, LANES - d * S5_H, 1), b)
    return v


def _s5_body(x_ref, ctx_ref, mod_ref, cmod_ref, tt_ref, w1_ref, w2t_ref, ar_ref, ai_ref, y_ref,
             u_ref, s_ref, *, n_ctx_chunks, n_lat_chunks, gb):
    n_b = x_ref.shape[0]
    n_g = u_ref.shape[0]
    n_blocks = s_ref.shape[1] // SUBLANES
    half = SUBLANES // 2
    tok_blk = SUBLANES * S5_T
    lat_lo = n_ctx_chunks * n_b
    lat_rows = n_lat_chunks * n_b

    def chunk_rows(ref, b, tok0, shift, scale1):
        vs = [ref[b, pl.ds(tok0 + j, SUBLANES, stride=S5_T), :] * scale1 + shift for j in range(S5_T)]
        lo = _granule_transpose(vs[:SUBLANES])
        hi = _granule_transpose(vs[SUBLANES:])
        return lo, hi

    def put_rows(rows, chunk0, b):
        for q in range(n_g):
            for jh in range(2):
                u_ref[q, jh, pl.ds(chunk0 * n_b + b, SUBLANES, stride=n_b), :] = rows[jh][q]

    def u_rows(g, lo, n):
        return jnp.concatenate([u_ref[g, 0, lo:lo + n, :], u_ref[g, 1, lo:lo + n, :]], axis=1)

    shift = jnp.broadcast_to(cmod_ref[0, 0:1, :], (SUBLANES, LANES))
    scale1 = 1.0 + jnp.broadcast_to(cmod_ref[0, 1:2, :], (SUBLANES, LANES))
    for b in range(n_b):
        for cb in range(n_ctx_chunks // SUBLANES):
            rows = chunk_rows(ctx_ref, b, cb * tok_blk, shift, scale1)
            put_rows(rows, cb * SUBLANES, b)
            put_rows(rows, n_ctx_chunks + n_lat_chunks + cb * SUBLANES, b)

    def fill(cb, carry):
        for b in range(n_b):
            shift = jnp.broadcast_to(mod_ref[b, 0:1, :], (SUBLANES, LANES))
            scale1 = 1.0 + jnp.broadcast_to(mod_ref[b, 1:2, :], (SUBLANES, LANES))
            put_rows(chunk_rows(x_ref, b, cb * tok_blk, shift, scale1), n_ctx_chunks + cb * SUBLANES, b)
        return carry

    lax.fori_loop(0, n_lat_chunks // SUBLANES, fill, 0)

    for g in range(n_g):
        s_ref[g] = jnp.dot(u_rows(g, 0, n_blocks * SUBLANES).astype(BF16), w1_ref[g],
                           preferred_element_type=F32)

    lane = lax.broadcasted_iota(I32, (SUBLANES, LANES), 1)
    row = lax.broadcasted_iota(I32, (SUBLANES, LANES), 0)
    is_fwd = lane < LANES // 2
    is_fwd2 = jnp.concatenate([is_fwd, is_fwd], axis=1)
    top = row < half
    zero = jnp.zeros((SUBLANES, LANES), F32)
    for g0 in range(0, n_g, gb):
        ars = [jnp.broadcast_to(ar_ref[g0 + g], (SUBLANES, LANES)) for g in range(gb)]
        ais = [jnp.broadcast_to(ai_ref[g0 + g], (SUBLANES, LANES)) for g in range(gb)]

        def step(k, carry, g0=g0, ars=ars, ais=ais):
            fo = pl.multiple_of(k * SUBLANES, SUBLANES)
            bo = pl.multiple_of((n_blocks - 1 - k) * SUBLANES, SUBLANES)
            new = []
            for g in range(gb):
                xr, xi = carry[2 * g], carry[2 * g + 1]
                vf = s_ref[g0 + g, pl.ds(fo, SUBLANES), :]
                vb_raw = s_ref[g0 + g, pl.ds(bo, SUBLANES), :]
                vb = pltpu.roll(vb_raw, half, 0)
                vr = jnp.where(is_fwd, vf[:, :LANES], vb[:, :LANES])
                vi = jnp.where(is_fwd, vf[:, LANES:], vb[:, LANES:])
                ar, ai = ars[g], ais[g]
                yr = ar * xr - ai * xi + vr
                yi = ar * xi + ai * xr + vi
                yrr = pltpu.roll(yr, half, 0)
                yir = pltpu.roll(yi, half, 0)
                zr = ar * yrr - ai * yir + vr
                zi = ar * yir + ai * yrr + vi
                inc = jnp.concatenate([jnp.where(top, xr, yrr), jnp.where(top, xi, yir)], axis=1)
                s_ref[g0 + g, pl.ds(fo, SUBLANES), :] = jnp.where(is_fwd2, inc, vf)
                s_ref[g0 + g, pl.ds(bo, SUBLANES), :] = jnp.where(is_fwd2, vb_raw, pltpu.roll(inc, half, 0))
                new.append(jnp.where(top, pltpu.roll(zr, half, 0), zr))
                new.append(jnp.where(top, pltpu.roll(zi, half, 0), zi))
            return tuple(new)

        lax.fori_loop(0, (n_ctx_chunks + n_lat_chunks) // 2, step, tuple(zero for _ in range(2 * gb)))

    for g in range(n_g):
        y = (jnp.dot(u_rows(g, lat_lo, lat_rows).astype(BF16), tt_ref[g], preferred_element_type=F32)
             + lax.dot_general(s_ref[g, lat_lo:lat_lo + lat_rows, :].astype(BF16), w2t_ref[g],
                               (((1,), (1,)), ((), ())), preferred_element_type=F32))
        for jh in range(2):
            u_ref[g, jh, lat_lo:lat_lo + lat_rows, :] = y[:, jh * LANES:(jh + 1) * LANES]

    def emit(cb, carry):
        for b in range(n_b):
            for jh in range(S5_T // SUBLANES):
                w = [u_ref[q, jh, pl.ds((n_ctx_chunks + cb * SUBLANES) * n_b + b, SUBLANES, stride=n_b), :]
                     for q in range(n_g)]
                v = _granule_transpose(w)
                for j in range(SUBLANES):
                    y_ref[b, pl.ds(cb * tok_blk + jh * SUBLANES + j, SUBLANES, stride=S5_T), :] = v[j]
        return carry

    lax.fori_loop(0, n_lat_chunks // SUBLANES, emit, 0)


def _s5_mix(x, ctx, mods, cmods, toep, w1, w2t, a_r, a_i):
    b, l, d = x.shape
    n_ctx = ctx.shape[1]
    n_g = LANES // S5_H
    n_ctx_chunks, n_lat_chunks = n_ctx // S5_T, l // S5_T
    assert b * 2 == SUBLANES and n_ctx_chunks % SUBLANES == 0 and n_lat_chunks % SUBLANES == 0
    rows = (2 * n_ctx_chunks + n_lat_chunks) * b
    w = S5_T * S5_H
    body = functools.partial(_s5_body, n_ctx_chunks=n_ctx_chunks, n_lat_chunks=n_lat_chunks, gb=4)
    lane_tile = lambda i: (0, 0, i)
    wspec = pl.BlockSpec((n_g, w, w), lambda i: (i, 0, 0))
    aspec = pl.BlockSpec((n_g, 1, w // 2), lambda i: (i, 0, 0))
    return pl.pallas_call(
        body,
        grid=(d // LANES,),
        in_specs=[pl.BlockSpec((b, l, LANES), lane_tile, pipeline_mode=pl.Buffered(1)),
                  pl.BlockSpec((b, n_ctx, LANES), lane_tile),
                  pl.BlockSpec((b, 6, LANES), lane_tile),
                  pl.BlockSpec((1, 6, LANES), lane_tile),
                  wspec, wspec, wspec, aspec, aspec],
        out_specs=pl.BlockSpec((b, l, LANES), lane_tile, pipeline_mode=pl.Buffered(1)),
        out_shape=jax.ShapeDtypeStruct((b, l, d), F32),
        scratch_shapes=[pltpu.VMEM((n_g, w // LANES, rows, LANES), F32), pltpu.VMEM((n_g, rows, w), F32)],
        compiler_params=_cparams("arbitrary"),
        name="s5_mix",
    )(x, ctx, mods, cmods, toep, w1, w2t, a_r, a_i)


def _layer_norm(r, g, b):
    mu = jnp.mean(r, axis=-1, keepdims=True)
    xc = r - mu
    var = jnp.mean(xc * xc, axis=-1, keepdims=True)
    return xc * lax.rsqrt(var + LN_EPS) * g + b


def _max2_of4(a, b, c, d):
    h1, l1 = jnp.maximum(a, b), jnp.minimum(a, b)
    h2, l2 = jnp.maximum(c, d), jnp.minimum(c, d)
    return jnp.maximum(h1, h2) + jnp.maximum(jnp.minimum(h1, h2), jnp.maximum(l1, l2))


def _argmax_first(vals):
    idx = jnp.zeros(vals[0].shape, I32)
    best = vals[0]
    for j in range(1, len(vals)):
        upd = vals[j] > best
        idx = jnp.where(upd, j, idx)
        best = jnp.where(upd, vals[j], best)
    return idx, best


def _route(logits_t, count_ref, route_ref):
    n_e, tm = logits_t.shape
    per = n_e // N_EXPERT_GROUPS
    mx = jnp.max(logits_t, axis=0, keepdims=True)
    ex = jnp.exp(logits_t - mx)
    sc = ex / jnp.sum(ex, axis=0, keepdims=True)
    rows = [sc[e:e + 1, :] for e in range(n_e)]
    gscore = [_max2_of4(*rows[per * g:per * (g + 1)]) for g in range(N_EXPERT_GROUPS)]
    best, _ = _argmax_first(gscore)
    vals = []
    for j in range(per):
        v = rows[per * (N_EXPERT_GROUPS - 1) + j]
        for g in range(N_EXPERT_GROUPS - 2, -1, -1):
            v = jnp.where(best == g, rows[per * g + j], v)
        vals.append(v)
    i1, m1 = _argmax_first(vals)
    i2, m2 = _argmax_first([jnp.where(i1 == j, -1.0, vals[j]) for j in range(per)])
    den = m1 + m2
    e0 = best * per + i1
    e1 = best * per + i2

    eidx = lax.broadcasted_iota(I32, (n_e, tm), 0)
    hit0 = eidx == e0
    hit1 = eidx == e1
    onehot = jnp.where(hit0, 1.0, jnp.where(hit1, 1.0, 0.0))
    src = lax.broadcasted_iota(I32, (tm, tm), 0)
    dst = lax.broadcasted_iota(I32, (tm, tm), 1)
    tri = jnp.where(src <= dst, 1.0, 0.0).astype(BF16)
    cum = jnp.dot(onehot.astype(BF16), tri, preferred_element_type=F32)
    excl = cum - onehot + count_ref[:, 0:1]
    rank0 = jnp.sum(jnp.where(hit0, excl, 0.0), axis=0, keepdims=True)
    rank1 = jnp.sum(jnp.where(hit1, excl, 0.0), axis=0, keepdims=True)
    count_ref[...] = count_ref[...] + jnp.sum(onehot, axis=1, keepdims=True)

    zero = jnp.zeros((1, tm), F32)
    route_ref[...] = jnp.concatenate(
        [e0.astype(F32), e1.astype(F32), m1 / den, m2 / den, rank0, rank1, zero, zero], axis=0)


def _store_rows_tiled(ref, val):
    for s in range(ref.shape[1]):
        ref[:, s, :] = val[:, s * LANES:(s + 1) * LANES]


def _load_rows_tiled(ref):
    return jnp.concatenate([ref[:, s, :] for s in range(ref.shape[1])], axis=-1)


def _post_mixer_body(m_ref, x_ref, mod_ref, lng_ref, lnb_ref, rwt_ref, rb_ref, *rest, glu, alpha):
    if glu:
        wv_ref, wg_ref, x1_ref, h_ref, route_ref, cnt_out_ref, cnt_ref = rest
        a = jax.nn.gelu(m_ref[0], approximate=True).astype(BF16)
        val = jnp.dot(a, wv_ref[...], preferred_element_type=F32)
        gate = jnp.dot(a, wg_ref[...], preferred_element_type=F32)
        m = val * jax.nn.sigmoid(gate)
    else:
        x1_ref, h_ref, route_ref, cnt_out_ref, cnt_ref = rest
        m = m_ref[0]

    @pl.when((pl.program_id(0) == 0) & (pl.program_id(1) == 0))
    def _():
        cnt_ref[...] = jnp.zeros_like(cnt_ref)

    g1 = mod_ref[0, 2:3, :]
    sh2 = mod_ref[0, 3:4, :]
    sc2 = mod_ref[0, 4:5, :]
    x1 = _layer_norm(alpha * x_ref[0] + g1 * m, lng_ref[...], lnb_ref[...])
    x1_ref[0] = x1
    h = x1 * (1.0 + sc2) + sh2
    _store_rows_tiled(h_ref, h)
    logits_t = lax.dot_general(rwt_ref[...], h, (((1,), (1,)), ((), ())),
                               precision=HI, preferred_element_type=F32) + rb_ref[...]
    _route(logits_t, cnt_ref, route_ref)
    cnt_out_ref[...] = cnt_ref[...]


def _post_mixer(m, x, mods, ln_g, ln_b, router_w, router_b, alpha, glu_w=None):
    b, l, d = x.shape
    n_e = router_w.shape[1]
    tm = min(l, 512)
    nt = l // tm
    tok = lambda i, j: (i, j, 0)
    const2 = lambda i, j: (0, 0)
    in_specs = [pl.BlockSpec((1, tm, d), tok),
                pl.BlockSpec((1, tm, d), tok),
                pl.BlockSpec((1, 6, d), lambda i, j: (i, 0, 0)),
                pl.BlockSpec((1, d), const2),
                pl.BlockSpec((1, d), const2),
                pl.BlockSpec((n_e, d), const2),
                pl.BlockSpec((n_e, 1), const2)]
    args = [m, x, mods, ln_g.reshape(1, d), ln_b.reshape(1, d), router_w.T, router_b.reshape(n_e, 1)]
    if glu_w is not None:
        in_specs += [pl.BlockSpec((d, d), const2), pl.BlockSpec((d, d), const2)]
        args += [glu_w[0].astype(BF16), glu_w[1].astype(BF16)]
    out_shape = [jax.ShapeDtypeStruct((b, l, d), F32),
                 jax.ShapeDtypeStruct((b * l, d // LANES, LANES), F32),
                 jax.ShapeDtypeStruct((ROUTE_ROWS, b * l), F32),
                 jax.ShapeDtypeStruct((n_e, LANES), F32)]
    out_specs = [pl.BlockSpec((1, tm, d), tok),
                 pl.BlockSpec((tm, d // LANES, LANES), lambda i, j: (i * nt + j, 0, 0)),
                 pl.BlockSpec((ROUTE_ROWS, tm), lambda i, j: (0, i * nt + j)),
                 pl.BlockSpec((n_e, LANES), const2)]
    return pl.pallas_call(
        functools.partial(_post_mixer_body, glu=glu_w is not None, alpha=alpha),
        grid=(b, nt),
        in_specs=in_specs,
        out_specs=out_specs,
        out_shape=out_shape,
        scratch_shapes=[pltpu.VMEM((n_e, LANES), F32)],
        compiler_params=_cparams("arbitrary", "arbitrary"),
        name="post_mixer_glu" if glu_w is not None else "post_mixer",
    )(*args)


def _dispatch_plan(route, counts, tile):
    n_e = counts.shape[0]
    cnt = counts[:, 0].astype(I32)
    tiles = (cnt + tile - 1) // tile
    tile_end = jnp.cumsum(tiles)
    row_off = (tile_end - tiles) * tile
    eids = jnp.arange(n_e, dtype=I32)[:, None]

    def row_offset_of(e):
        return jnp.sum(jnp.where(e[None, :] == eids, row_off[:, None], 0), axis=0)

    pos0 = row_offset_of(route[0].astype(I32)) + route[4].astype(I32)
    pos1 = row_offset_of(route[1].astype(I32)) + route[5].astype(I32)
    n_tiles_max = (2 * route.shape[1]) // tile + n_e
    tile_ids = jnp.arange(n_tiles_max, dtype=I32)
    tile_expert = jnp.minimum(jnp.sum((tile_end[None, :] <= tile_ids[:, None]).astype(I32), axis=1), n_e - 1)
    n_used = tile_end[-1:].astype(I32)
    pad_start = (row_off + cnt).astype(I32)
    pad_len = (tiles * tile - cnt).astype(I32)
    return pos0, pos1, tile_expert, n_used, pad_start, pad_len, n_tiles_max


def _zero_pad_rows(pad_ref, len_ref, zero_ref, hs_ref, sem, n_e, tile, wait):
    for e in range(n_e):
        ln = len_ref[e]
        bit = tile // 2
        while bit >= 1:
            @pl.when((ln & bit) != 0)
            def _(bit=bit, ln=ln, e=e):
                start = pad_ref[e] + (ln & ~(2 * bit - 1))
                cp = pltpu.make_async_copy(zero_ref.at[pl.ds(0, bit)], hs_ref.at[pl.ds(start, bit)], sem)
                if wait:
                    cp.wait()
                else:
                    cp.start()
            bit //= 2


def _zero_unused_tiles(nu_ref, zero_ref, hs_ref, sem, tile, wait):
    half = tile // 2
    n_halves = 2 * (hs_ref.shape[0] // tile - nu_ref[0])

    def body(j, carry):
        cp = pltpu.make_async_copy(zero_ref, hs_ref.at[pl.ds(nu_ref[0] * tile + j * half, half)], sem)
        if wait:
            cp.wait()
        else:
            cp.start()
        return carry

    lax.fori_loop(0, n_halves, body, 0)


def _dispatch_body(pos0_ref, pos1_ref, pad_ref, len_ref, nu_ref, h_ref, hs_ref, zero_ref, sem_z, sem, *,
                   tm, tile, n_e):
    i = pl.program_id(0)

    @pl.when(i == 0)
    def _():
        zero_ref[...] = jnp.zeros_like(zero_ref)
        _zero_pad_rows(pad_ref, len_ref, zero_ref, hs_ref, sem_z, n_e, tile, False)
        _zero_unused_tiles(nu_ref, zero_ref, hs_ref, sem_z, tile, False)

    @pl.when(i == pl.num_programs(0) - 1)
    def _():
        _zero_pad_rows(pad_ref, len_ref, zero_ref, hs_ref, sem_z, n_e, tile, True)
        _zero_unused_tiles(nu_ref, zero_ref, hs_ref, sem_z, tile, True)

    base = i * tm

    def issue(r, carry):
        pltpu.make_async_copy(h_ref.at[r], hs_ref.at[pos0_ref[base + r]], sem).start()
        pltpu.make_async_copy(h_ref.at[r], hs_ref.at[pos1_ref[base + r]], sem).start(priority=1)
        return carry

    lax.fori_loop(0, tm, issue, 0, unroll=8)
    for _ in range(2):
        pltpu.make_async_copy(h_ref, hs_ref.at[pl.ds(0, tm)], sem).wait()


def _dispatch(h_rows, pos0, pos1, pad_start, pad_len, n_used, n_rows, tile):
    n, s, w = h_rows.shape
    n_e = pad_start.shape[0]
    tm = min(n, 256)
    grid_spec = pltpu.PrefetchScalarGridSpec(
        num_scalar_prefetch=5,
        grid=(n // tm,),
        in_specs=[pl.BlockSpec((tm, s, w), lambda i, *_: (i, 0, 0))],
        out_specs=pl.BlockSpec(memory_space=pl.ANY),
        scratch_shapes=[pltpu.VMEM((tile // 2, s, w), F32),
                        pltpu.SemaphoreType.DMA(()), pltpu.SemaphoreType.DMA(())],
    )
    return pl.pallas_call(
        functools.partial(_dispatch_body, tm=tm, tile=tile, n_e=n_e),
        grid_spec=grid_spec,
        out_shape=jax.ShapeDtypeStruct((n_rows, s, w), F32),
        compiler_params=_cparams("arbitrary"),
        name="moe_dispatch",
    )(pos0, pos1, pad_start, pad_len, n_used, h_rows)


def _expert_body(te_ref, nu_ref, hs_ref, wg_ref, wu_ref, wd_ref, ys_ref, wgb, wub, wdb):
    i = pl.program_id(0)
    first = jnp.logical_or(i == 0, te_ref[i] != te_ref[jnp.maximum(i - 1, 0)])

    @pl.when(jnp.logical_and(i < nu_ref[0], first))
    def _():
        wgb[...] = wg_ref[0, 0].astype(BF16)
        wub[...] = wu_ref[0, 0].astype(BF16)
        wdb[...] = wd_ref[0, 0].astype(BF16)

    @pl.when(i < nu_ref[0])
    def _():
        x = _load_rows_tiled(hs_ref).astype(BF16)
        gate = jnp.dot(x, wgb[...], preferred_element_type=F32)
        up = jnp.dot(x, wub[...], preferred_element_type=F32)
        a = (gate * jax.nn.sigmoid(gate) * up).astype(BF16)
        _store_rows_tiled(ys_ref, jnp.dot(a, wdb[...], preferred_element_type=F32))

    @pl.when(i >= nu_ref[0])
    def _():
        ys_ref[...] = jnp.zeros_like(ys_ref)


def _expert_ffn(hs, tile_expert, n_used, w_gate, w_up, w_down, layer, n_tiles, tile):
    n_rows, s, w = hs.shape
    _, n_e, d, f = w_gate.shape
    wmap = lambda i, te, nu: (layer, te[i], 0, 0)
    grid_spec = pltpu.PrefetchScalarGridSpec(
        num_scalar_prefetch=2,
        grid=(n_tiles,),
        in_specs=[pl.BlockSpec((tile, s, w), lambda i, te, nu: (jnp.maximum(jnp.minimum(i, nu[0] - 1), 0), 0, 0)),
                  pl.BlockSpec((1, 1, d, f), wmap),
                  pl.BlockSpec((1, 1, d, f), wmap),
                  pl.BlockSpec((1, 1, f, d), wmap)],
        out_specs=pl.BlockSpec((tile, s, w), lambda i, te, nu: (i, 0, 0)),
        scratch_shapes=[pltpu.VMEM((d, f), BF16), pltpu.VMEM((d, f), BF16), pltpu.VMEM((f, d), BF16)],
    )
    return pl.pallas_call(
        _expert_body,
        grid_spec=grid_spec,
        out_shape=jax.ShapeDtypeStruct((n_tiles * tile, s, w), F32),
        compiler_params=_cparams("arbitrary"),
        name="moe_experts",
    )(tile_expert, n_used, hs, w_gate, w_up, w_down)


def _combine_body(pos0_ref, pos1_ref, x_ref, mod_ref, wts_ref, lng_ref, lnb_ref, ys_ref, *rest,
                  tm, alpha, next_mod):
    if next_mod:
        nmod_ref, x2_ref, h_ref, buf0, buf1, sem = rest
    else:
        x2_ref, buf0, buf1, sem = rest
    base = (pl.program_id(0) * pl.num_programs(1) + pl.program_id(1)) * tm

    def issue(r, carry):
        pltpu.make_async_copy(ys_ref.at[pos0_ref[base + r]], buf0.at[r], sem).start()
        pltpu.make_async_copy(ys_ref.at[pos1_ref[base + r]], buf1.at[r], sem).start(priority=1)
        return carry

    lax.fori_loop(0, tm, issue, 0, unroll=8)
    for buf in (buf0, buf1):
        pltpu.make_async_copy(ys_ref.at[pl.ds(0, tm)], buf, sem).wait()

    w = wts_ref[...]
    moe = w[:, 0:1] * _load_rows_tiled(buf0) + w[:, 1:2] * _load_rows_tiled(buf1)
    g2 = mod_ref[0, 5:6, :]
    x2 = _layer_norm(alpha * x_ref[0] + g2 * moe, lng_ref[...], lnb_ref[...])
    x2_ref[0] = x2
    if next_mod:
        h_ref[0] = x2 * (1.0 + nmod_ref[0, 1:2, :]) + nmod_ref[0, 0:1, :]


def _combine(ys, pos0, pos1, wts, x, mods, ln_g, ln_b, alpha, next_mods=None):
    b, l, d = x.shape
    n_rows, s, w = ys.shape
    tm = min(l, 256)
    nt = l // tm
    tok = lambda i, j, *_: (i, j, 0)
    bat = lambda i, j, *_: (i, 0, 0)
    const2 = lambda i, j, *_: (0, 0)
    in_specs = [pl.BlockSpec((1, tm, d), tok),
                pl.BlockSpec((1, 6, d), bat),
                pl.BlockSpec((tm, 2), lambda i, j, *_: (i * nt + j, 0)),
                pl.BlockSpec((1, d), const2),
                pl.BlockSpec((1, d), const2),
                pl.BlockSpec(memory_space=pl.ANY)]
    args = [x, mods, wts, ln_g.reshape(1, d), ln_b.reshape(1, d), ys]
    out_shape = [jax.ShapeDtypeStruct((b, l, d), F32)]
    out_specs = [pl.BlockSpec((1, tm, d), tok)]
    if next_mods is not None:
        in_specs.append(pl.BlockSpec((1, 6, d), bat))
        args.append(next_mods)
        out_shape.append(jax.ShapeDtypeStruct((b, l, d), F32))
        out_specs.append(pl.BlockSpec((1, tm, d), tok))
    grid_spec = pltpu.PrefetchScalarGridSpec(
        num_scalar_prefetch=2,
        grid=(b, nt),
        in_specs=in_specs,
        out_specs=out_specs,
        scratch_shapes=[pltpu.VMEM((tm, s, w), F32), pltpu.VMEM((tm, s, w), F32),
                        pltpu.SemaphoreType.DMA(())],
    )
    return pl.pallas_call(
        functools.partial(_combine_body, tm=tm, alpha=alpha, next_mod=next_mods is not None),
        grid_spec=grid_spec,
        out_shape=out_shape,
        compiler_params=_cparams("arbitrary", "arbitrary"),
        name="moe_combine",
    )(pos0, pos1, *args)


def _moe(h_rows, route, counts, w_gate, w_up, w_down, layer):
    n = h_rows.shape[0]
    tile = min(n, 512)
    pos0, pos1, tile_expert, n_used, pad_start, pad_len, n_tiles = _dispatch_plan(route, counts, tile)
    hs = _dispatch(h_rows, pos0, pos1, pad_start, pad_len, n_used, n_tiles * tile, tile)
    ys = _expert_ffn(hs, tile_expert, n_used, w_gate, w_up, w_down, layer, n_tiles, tile)
    return ys, pos0, pos1, jnp.transpose(route[2:4])


def _pool_group(h_ref, w_ref, sc_ref, o_ref, col_ref, k, n_rows):
    n = n_rows * GRID_W
    c = h_ref.shape[2]
    blk = 4 * GRID_W
    half = k // 2
    pad = half * GRID_W
    ti = lax.broadcasted_iota(I32, (blk, blk), 0)
    si = lax.broadcasted_iota(I32, (blk, blk), 1)
    shift = GRID_W.bit_length() - 1
    same_row = (ti >> shift) == (si >> shift)
    band = jnp.where(same_row & (si - ti >= -half) & (si - ti <= half - 1), 1.0, 0.0)
    col_ref[0:pad, :] = jnp.zeros((pad, c), F32)
    col_ref[pad + n:pad + n + pad, :] = jnp.zeros((pad, c), F32)
    for b0 in range(0, n, blk):
        col_ref[pad + b0:pad + b0 + blk, :] = jnp.dot(band, h_ref[0, b0:b0 + blk, :], precision=HI,
                                                      preferred_element_type=F32)
    acc = col_ref[0:n, :]
    for j in range(1, k):
        acc = acc + col_ref[j * GRID_W:j * GRID_W + n, :]
    t = lax.broadcasted_iota(I32, (n, 1), 0)
    wc = t & (GRID_W - 1)
    wr = t >> shift
    cnt_c = jnp.minimum(wc + half - 1, GRID_W - 1) - jnp.maximum(wc - half, 0) + 1
    cnt_r = jnp.minimum(wr + half - 1, n_rows - 1) - jnp.maximum(wr - half, 0) + 1
    mean = acc / (cnt_c * cnt_r).astype(F32)
    pooled = (mean - h_ref[0]).astype(BF16)
    o_ref[0] = jnp.dot(pooled, w_ref[0].astype(BF16), preferred_element_type=F32) * sc_ref[...]


def _pool_body(h_ref, w_ref, sc_ref, o_ref, col_ref, *, n_rows):
    g = pl.program_id(1)
    for gi, k in enumerate(POOL_WINDOWS):
        @pl.when(g == gi)
        def _(k=k):
            _pool_group(h_ref, w_ref, sc_ref, o_ref, col_ref, k, n_rows)


def _pool_mix(h, w_grp, scale):
    b, n, d = h.shape
    n_g, c, _ = w_grp.shape
    n_rows = n // GRID_W
    pad = (max(POOL_WINDOWS) // 2) * GRID_W
    return pl.pallas_call(
        functools.partial(_pool_body, n_rows=n_rows),
        grid=(b, n_g),
        in_specs=[pl.BlockSpec((1, n, c), lambda i, j: (i, 0, j)),
                  pl.BlockSpec((1, c, c), lambda i, j: (j, 0, 0)),
                  pl.BlockSpec((1, c), lambda i, j: (0, j))],
        out_specs=pl.BlockSpec((1, n, c), lambda i, j: (i, 0, j)),
        out_shape=jax.ShapeDtypeStruct((b, n, d), F32),
        scratch_shapes=[pltpu.VMEM((n + 2 * pad, c), F32)],
        compiler_params=_cparams("arbitrary", "arbitrary"),
        name="pool_mix",
    )(h, w_grp, scale.reshape(1, d))


def kernel(x, c, ctx, c_ctx, mod_w, mod_b, ln_g, ln_b, s5_lam_re, s5_lam_im, s5_log_dt, s5_b_re, s5_b_im,
           s5_c_re, s5_c_im, s5_d, s5_w_val, s5_w_gate, pool_w, pool_scale, router_w, router_b,
           moe_w_gate, moe_w_up, moe_w_down):
    b, l, d = x.shape
    depth = mod_w.shape[0]
    assert depth == 2 and b + 1 <= SUBLANES and d % LANES == 0 and GRID_W & (GRID_W - 1) == 0
    alpha = (2 * depth) ** 0.25

    cond = jnp.zeros((SUBLANES, d), F32).at[:b].set(c).at[b].set(c_ctx)
    mods = _modulation(cond, mod_w, mod_b).reshape(depth, SUBLANES, 6, d)

    toep, w1, w2t, a_r, a_i = _s5_weights(s5_lam_re[0], s5_lam_im[0], s5_log_dt[0], s5_b_re[0], s5_b_im[0],
                                          s5_c_re[0], s5_c_im[0], s5_d[0])
    y = _s5_mix(x, ctx, mods[0, :b], mods[0, b:b + 1], toep, w1, w2t, a_r, a_i)
    x1, h_rows, route, counts = _post_mixer(y, x, mods[0, :b], ln_g[0, 0], ln_b[0, 0], router_w, router_b,
                                            alpha, glu_w=(s5_w_val[0], s5_w_gate[0]))
    ys, pos0, pos1, wts = _moe(h_rows, route, counts, moe_w_gate, moe_w_up, moe_w_down, 0)
    x2, h = _combine(ys, pos0, pos1, wts, x1, mods[0, :b], ln_g[0, 1], ln_b[0, 1], alpha,
                     next_mods=mods[1, :b])

    m = _pool_mix(h, pool_w[0], pool_scale[0])
    x3, h_rows, route, counts = _post_mixer(m, x2, mods[1, :b], ln_g[1, 0], ln_b[1, 0], router_w, router_b,
                                            alpha)
    ys, pos0, pos1, wts = _moe(h_rows, route, counts, moe_w_gate, moe_w_up, moe_w_down, 1)
    (out,) = _combine(ys, pos0, pos1, wts, x3, mods[1, :b], ln_g[1, 1], ln_b[1, 1], alpha)
    return out
```

```python
import functools

import jax
import jax.numpy as jnp
from jax import lax
from jax.experimental import pallas as pl
from jax.experimental.pallas import tpu as pltpu

F32 = jnp.float32
BF16 = jnp.bfloat16
I32 = jnp.int32
HI = lax.Precision.HIGHEST

GRID_W = 64
S5_H = 16
S5_T = 16
POOL_WINDOWS = (2, 4, 8, 16)
N_EXPERT_GROUPS = 4
LN_EPS = 1e-5
LANES = 128
SUBLANES = 8
VMEM_LIMIT = 52 * 1024 * 1024

ROUTE_ROWS = 8


def _cparams(*sem):
    return pltpu.CompilerParams(dimension_semantics=sem, vmem_limit_bytes=VMEM_LIMIT)


def _mod_body(c_ref, w_ref, b_ref, o_ref):
    c = c_ref[...]
    s = c * jax.nn.sigmoid(c)
    o_ref[0] = jnp.dot(s, w_ref[0], precision=HI, preferred_element_type=F32) + b_ref[0]


def _modulation(cond, mod_w, mod_b):
    depth, d, n6 = mod_w.shape
    tn = min(n6, 1536)
    return pl.pallas_call(
        _mod_body,
        grid=(depth, n6 // tn),
        in_specs=[pl.BlockSpec((SUBLANES, d), lambda i, j: (0, 0)),
                  pl.BlockSpec((1, d, tn), lambda i, j: (i, 0, j)),
                  pl.BlockSpec((1, 1, tn), lambda i, j: (i, 0, j))],
        out_specs=pl.BlockSpec((1, SUBLANES, tn), lambda i, j: (i, 0, j)),
        out_shape=jax.ShapeDtypeStruct((depth, SUBLANES, n6), F32),
        compiler_params=_cparams("arbitrary", "arbitrary"),
        name="modulation",
    )(cond, mod_w, mod_b.reshape(depth, 1, n6))


def _s5_direction_terms(lam_re, lam_im, log_dt, b_re, b_im):
    lr = lam_re.astype(F32)
    li = lam_im.astype(F32)
    dt = jnp.exp(log_dt.astype(F32))[:, None]
    mag = jnp.exp(lr * dt)
    ar = mag * jnp.cos(li * dt)
    ai = mag * jnp.sin(li * dt)
    den = lr * lr + li * li
    nr = ar - 1.0
    fr = (nr * lr + ai * li) / den
    fi = (ai * lr - nr * li) / den
    br_, bi_ = b_re.astype(F32), b_im.astype(F32)
    bbr = fr[..., None] * br_ - fi[..., None] * bi_
    bbi = fr[..., None] * bi_ + fi[..., None] * br_
    k = jnp.arange(S5_T + 1, dtype=F32)[:, None, None]
    pm = jnp.exp(k * (lr * dt))
    pr = pm * jnp.cos(k * (li * dt))
    pi = pm * jnp.sin(k * (li * dt))
    return pr, pi, bbr, bbi


def _s5_weights(lam_re, lam_im, log_dt, b_re, b_im, c_re, c_im, d_skip):
    t = S5_T
    g, p = lam_re.shape[1:]
    h = b_re.shape[-1]
    terms = [_s5_direction_terms(lam_re[d], lam_im[d], log_dt[d], b_re[d], b_im[d]) for d in (0, 1)]
    pw = jnp.stack([jnp.transpose(terms[d][k], (1, 0, 2)) for d in (0, 1) for k in (0, 1)], axis=1)
    bt = jnp.stack([jnp.transpose(terms[d][k], (0, 2, 1)) for d in (0, 1) for k in (2, 3)], axis=1)
    cc = jnp.stack([c[d].astype(F32) for d in (0, 1) for c in (c_re, c_im)], axis=1)
    gp = 8
    spec4 = lambda rows: pl.BlockSpec((gp, 4, rows, p), lambda i: (i, 0, 0, 0))
    wide = pl.BlockSpec((gp, t * h, t * h), lambda i: (i, 0, 0))
    toep, w1, w2t = pl.pallas_call(
        functools.partial(_s5_prep_body, gp=gp),
        grid=(g // gp,),
        in_specs=[spec4(t + 1), spec4(h), spec4(h), pl.BlockSpec((gp, 1, h), lambda i: (i, 0, 0))],
        out_specs=[wide, wide, wide],
        out_shape=[jax.ShapeDtypeStruct((g, t * h, t * h), BF16) for _ in range(3)],
        compiler_params=_cparams("arbitrary"),
        name="s5_prep",
    )(pw, bt, cc, d_skip.astype(F32).reshape(g, 1, h))
    a_r = jnp.concatenate([pw[:, 0, t], pw[:, 2, t]], axis=-1)[:, None, :]
    a_i = jnp.concatenate([pw[:, 1, t], pw[:, 3, t]], axis=-1)[:, None, :]
    return toep, w1, w2t, a_r, a_i


def _cmul(ar, ai, br, bi):
    return ar * br - ai * bi, ar * bi + ai * br


def _s5_prep_body(pw_ref, bt_ref, cc_ref, d_ref, toep_ref, w1_ref, w2t_ref, *, gp):
    t = pw_ref.shape[2] - 1
    h = bt_ref.shape[2]
    nt = (((1,), (1,)), ((), ()))
    eye = (lax.broadcasted_iota(I32, (h, h), 0) == lax.broadcasted_iota(I32, (h, h), 1))
    for g in range(gp):
        prf, pif, prb, pib = (pw_ref[g, k] for k in range(4))
        btf = (bt_ref[g, 0], bt_ref[g, 1])
        btb = (bt_ref[g, 2], bt_ref[g, 3])
        ccf = (cc_ref[g, 0], cc_ref[g, 1])
        ccb = (cc_ref[g, 2], cc_ref[g, 3])
        caf, cab = [], []
        for j in range(t):
            f_r, f_i = _cmul(*btf, prf[t - 1 - j:t - j], pif[t - 1 - j:t - j])
            b_r, b_i = _cmul(*btb, prb[j:j + 1], pib[j:j + 1])
            w1_ref[g, j * h:(j + 1) * h, :] = jnp.concatenate([f_r, b_r, f_i, b_i], axis=1).astype(BF16)
            mf_r, mf_i = _cmul(*ccf, prf[j + 1:j + 2], pif[j + 1:j + 2])
            mb_r, mb_i = _cmul(*ccb, prb[t - j:t - j + 1], pib[t - j:t - j + 1])
            w2t_ref[g, j * h:(j + 1) * h, :] = jnp.concatenate([mf_r, mb_r, -mf_i, -mb_i],
                                                               axis=1).astype(BF16)
            caf.append(_cmul(*ccf, prf[j:j + 1], pif[j:j + 1]))
            cab.append(_cmul(*ccb, prb[t - 1 - j:t - j], pib[t - 1 - j:t - j]))

        def lag_kernels(btx, ca):
            car = jnp.concatenate([c[0] for c in ca], axis=0)
            cai = jnp.concatenate([c[1] for c in ca], axis=0)
            return (lax.dot_general(btx[0], car, nt, precision=HI, preferred_element_type=F32)
                    - lax.dot_general(btx[1], cai, nt, precision=HI, preferred_element_type=F32))

        kf = lag_kernels(btf, caf)
        kb = lag_kernels(btb, cab)
        skip = jnp.where(eye, jnp.broadcast_to(d_ref[g], (h, h)), 0.0)
        mid = kb[:, (t - 1) * h:] + kf[:, :h] + skip
        kwide = jnp.concatenate([kb[:, :(t - 1) * h], mid, kf[:, h:], jnp.zeros((h, h), F32)], axis=1)
        for i in range(t):
            off = (t - 1 - i) * h
            toep_ref[g, i * h:(i + 1) * h, :] = kwide[:, off:off + t * h].astype(BF16)


def _granule_transpose(v):
    lane = lax.broadcasted_iota(I32, v[0].shape, 1)
    v = list(v)
    for d in (4, 2, 1):
        low = ((lane >> 4) & d) == 0
        for j in range(SUBLANES):
            if j & d:
                continue
            a, b = v[j], v[j + d]
            v[j] = jnp.where(low, a, pltpu.roll(b, d * S5_H, 1))
            v[j + d] = jnp.where(low, pltpu.roll(a, LANES - d * S5_H, 1), b)
    return v


def _s5_body(x_ref, ctx_ref, mod_ref, cmod_ref, tt_ref, w1_ref, w2t_ref, ar_ref, ai_ref, y_ref,
             u_ref, s_ref, *, n_ctx_chunks, n_lat_chunks, gb):
    n_b = x_ref.shape[0]
    n_g = u_ref.shape[0]
    n_blocks = s_ref.shape[1] // SUBLANES
    half = SUBLANES // 2
    tok_blk = SUBLANES * S5_T
    lat_lo = n_ctx_chunks * n_b
    lat_rows = n_lat_chunks * n_b

    def chunk_rows(ref, b, tok0, shift, scale1):
        vs = [ref[b, pl.ds(tok0 + j, SUBLANES, stride=S5_T), :] * scale1 + shift for j in range(S5_T)]
        lo = _granule_transpose(vs[:SUBLANES])
        hi = _granule_transpose(vs[SUBLANES:])
        return lo, hi

    def put_rows(rows, chunk0, b):
        for q in range(n_g):
            for jh in range(2):
                u_ref[q, jh, pl.ds(chunk0 * n_b + b, SUBLANES, stride=n_b), :] = rows[jh][q]

    def u_rows(g, lo, n):
        return jnp.concatenate([u_ref[g, 0, lo:lo + n, :], u_ref[g, 1, lo:lo + n, :]], axis=1)

    shift = jnp.broadcast_to(cmod_ref[0, 0:1, :], (SUBLANES, LANES))
    scale1 = 1.0 + jnp.broadcast_to(cmod_ref[0, 1:2, :], (SUBLANES, LANES))
    for b in range(n_b):
        for cb in range(n_ctx_chunks // SUBLANES):
            rows = chunk_rows(ctx_ref, b, cb * tok_blk, shift, scale1)
            put_rows(rows, cb * SUBLANES, b)
            put_rows(rows, n_ctx_chunks + n_lat_chunks + cb * SUBLANES, b)

    def fill(cb, carry):
        for b in range(n_b):
            shift = jnp.broadcast_to(mod_ref[b, 0:1, :], (SUBLANES, LANES))
            scale1 = 1.0 + jnp.broadcast_to(mod_ref[b, 1:2, :], (SUBLANES, LANES))
            put_rows(chunk_rows(x_ref, b, cb * tok_blk, shift, scale1), n_ctx_chunks + cb * SUBLANES, b)
        return carry

    lax.fori_loop(0, n_lat_chunks // SUBLANES, fill, 0)

    for g in range(n_g):
        s_ref[g] = jnp.dot(u_rows(g, 0, n_blocks * SUBLANES).astype(BF16), w1_ref[g],
                           preferred_element_type=F32)

    lane = lax.broadcasted_iota(I32, (SUBLANES, LANES), 1)
    row = lax.broadcasted_iota(I32, (SUBLANES, LANES), 0)
    is_fwd = lane < LANES // 2
    is_fwd2 = jnp.concatenate([is_fwd, is_fwd], axis=1)
    top = row < half
    zero = jnp.zeros((SUBLANES, LANES), F32)
    for g0 in range(0, n_g, gb):
        ars = [jnp.broadcast_to(ar_ref[g0 + g], (SUBLANES, LANES)) for g in range(gb)]
        ais = [jnp.broadcast_to(ai_ref[g0 + g], (SUBLANES, LANES)) for g in range(gb)]

        def step(k, carry, g0=g0, ars=ars, ais=ais):
            fo = pl.multiple_of(k * SUBLANES, SUBLANES)
            bo = pl.multiple_of((n_blocks - 1 - k) * SUBLANES, SUBLANES)
            new = []
            for g in range(gb):
                xr, xi = carry[2 * g], carry[2 * g + 1]
                vf = s_ref[g0 + g, pl.ds(fo, SUBLANES), :]
                vb_raw = s_ref[g0 + g, pl.ds(bo, SUBLANES), :]
                vb = pltpu.roll(vb_raw, half, 0)
                vr = jnp.where(is_fwd, vf[:, :LANES], vb[:, :LANES])
                vi = jnp.where(is_fwd, vf[:, LANES:], vb[:, LANES:])
                ar, ai = ars[g], ais[g]
                yr = ar * xr - ai * xi + vr
                yi = ar * xi + ai * xr + vi
                yrr = pltpu.roll(yr, half, 0)
                yir = pltpu.roll(yi, half, 0)
                zr = ar * yrr - ai * yir + vr
                zi = ar * yir + ai * yrr + vi
                inc = jnp.concatenate([jnp.where(top, xr, yrr), jnp.where(top, xi, yir)], axis=1)
                s_ref[g0 + g, pl.ds(fo, SUBLANES), :] = jnp.where(is_fwd2, inc, vf)
                s_ref[g0 + g, pl.ds(bo, SUBLANES), :] = jnp.where(is_fwd2, vb_raw, pltpu.roll(inc, half, 0))
                new.append(jnp.where(top, pltpu.roll(zr, half, 0), zr))
                new.append(jnp.where(top, pltpu.roll(zi, half, 0), zi))
            return tuple(new)

        lax.fori_loop(0, (n_ctx_chunks + n_lat_chunks) // 2, step, tuple(zero for _ in range(2 * gb)))

    for g in range(n_g):
        y = (jnp.dot(u_rows(g, lat_lo, lat_rows).astype(BF16), tt_ref[g], preferred_element_type=F32)
             + lax.dot_general(s_ref[g, lat_lo:lat_lo + lat_rows, :].astype(BF16), w2t_ref[g],
                               (((1,), (1,)), ((), ())), preferred_element_type=F32))
        for jh in range(2):
            u_ref[g, jh, lat_lo:lat_lo + lat_rows, :] = y[:, jh * LANES:(jh + 1) * LANES]

    def emit(cb, carry):
        for b in range(n_b):
            for jh in range(S5_T // SUBLANES):
                w = [u_ref[q, jh, pl.ds((n_ctx_chunks + cb * SUBLANES) * n_b + b, SUBLANES, stride=n_b), :]
                     for q in range(n_g)]
                v = _granule_transpose(w)
                for j in range(SUBLANES):
                    y_ref[b, pl.ds(cb * tok_blk + jh * SUBLANES + j, SUBLANES, stride=S5_T), :] = v[j]
        return carry

    lax.fori_loop(0, n_lat_chunks // SUBLANES, emit, 0)


def _s5_mix(x, ctx, mods, cmods, toep, w1, w2t, a_r, a_i):
    b, l, d = x.shape
    n_ctx = ctx.shape[1]
    n_g = LANES // S5_H
    n_ctx_chunks, n_lat_chunks = n_ctx // S5_T, l // S5_T
    assert b * 2 == SUBLANES and n_ctx_chunks % SUBLANES == 0 and n_lat_chunks % SUBLANES == 0
    rows = (2 * n_ctx_chunks + n_lat_chunks) * b
    w = S5_T * S5_H
    body = functools.partial(_s5_body, n_ctx_chunks=n_ctx_chunks, n_lat_chunks=n_lat_chunks, gb=4)
    lane_tile = lambda i: (0, 0, i)
    wspec = pl.BlockSpec((n_g, w, w), lambda i: (i, 0, 0))
    aspec = pl.BlockSpec((n_g, 1, w // 2), lambda i: (i, 0, 0))
    return pl.pallas_call(
        body,
        grid=(d // LANES,),
        in_specs=[pl.BlockSpec((b, l, LANES), lane_tile, pipeline_mode=pl.Buffered(1)),
                  pl.BlockSpec((b, n_ctx, LANES), lane_tile),
                  pl.BlockSpec((b, 6, LANES), lane_tile),
                  pl.BlockSpec((1, 6, LANES), lane_tile),
                  wspec, wspec, wspec, aspec, aspec],
        out_specs=pl.BlockSpec((b, l, LANES), lane_tile, pipeline_mode=pl.Buffered(1)),
        out_shape=jax.ShapeDtypeStruct((b, l, d), F32),
        scratch_shapes=[pltpu.VMEM((n_g, w // LANES, rows, LANES), F32), pltpu.VMEM((n_g, rows, w), F32)],
        compiler_params=_cparams("arbitrary"),
        name="s5_mix",
    )(x, ctx, mods, cmods, toep, w1, w2t, a_r, a_i)


def _layer_norm(r, g, b):
    mu = jnp.mean(r, axis=-1, keepdims=True)
    xc = r - mu
    var = jnp.mean(xc * xc, axis=-1, keepdims=True)
    return xc * lax.rsqrt(var + LN_EPS) * g + b


def _max2_of4(a, b, c, d):
    h1, l1 = jnp.maximum(a, b), jnp.minimum(a, b)
    h2, l2 = jnp.maximum(c, d), jnp.minimum(c, d)
    return jnp.maximum(h1, h2) + jnp.maximum(jnp.minimum(h1, h2), jnp.maximum(l1, l2))


def _argmax_first(vals):
    idx = jnp.zeros(vals[0].shape, I32)
    best = vals[0]
    for j in range(1, len(vals)):
        upd = vals[j] > best
        idx = jnp.where(upd, j, idx)
        best = jnp.where(upd, vals[j], best)
    return idx, best


def _route(logits_t, count_ref, route_ref):
    n_e, tm = logits_t.shape
    per = n_e // N_EXPERT_GROUPS
    mx = jnp.max(logits_t, axis=0, keepdims=True)
    ex = jnp.exp(logits_t - mx)
    sc = ex / jnp.sum(ex, axis=0, keepdims=True)
    rows = [sc[e:e + 1, :] for e in range(n_e)]
    gscore = [_max2_of4(*rows[per * g:per * (g + 1)]) for g in range(N_EXPERT_GROUPS)]
    best, _ = _argmax_first(gscore)
    vals = []
    for j in range(per):
        v = rows[per * (N_EXPERT_GROUPS - 1) + j]
        for g in range(N_EXPERT_GROUPS - 2, -1, -1):
            v = jnp.where(best == g, rows[per * g + j], v)
        vals.append(v)
    i1, m1 = _argmax_first(vals)
    i2, m2 = _argmax_first([jnp.where(i1 == j, -1.0, vals[j]) for j in range(per)])
    den = m1 + m2
    e0 = best * per + i1
    e1 = best * per + i2

    eidx = lax.broadcasted_iota(I32, (n_e, tm), 0)
    hit0 = eidx == e0
    hit1 = eidx == e1
    onehot = jnp.where(hit0, 1.0, jnp.where(hit1, 1.0, 0.0))
    src = lax.broadcasted_iota(I32, (tm, tm), 0)
    dst = lax.broadcasted_iota(I32, (tm, tm), 1)
    tri = jnp.where(src <= dst, 1.0, 0.0).astype(BF16)
    cum = jnp.dot(onehot.astype(BF16), tri, preferred_element_type=F32)
    excl = cum - onehot + count_ref[:, 0:1]
    rank0 = jnp.sum(jnp.where(hit0, excl, 0.0), axis=0, keepdims=True)
    rank1 = jnp.sum(jnp.where(hit1, excl, 0.0), axis=0, keepdims=True)
    count_ref[...] = count_ref[...] + jnp.sum(onehot, axis=1, keepdims=True)

    zero = jnp.zeros((1, tm), F32)
    route_ref[...] = jnp.concatenate(
        [e0.astype(F32), e1.astype(F32), m1 / den, m2 / den, rank0, rank1, zero, zero], axis=0)


def _post_mixer_body(m_ref, x_ref, mod_ref, lng_ref, lnb_ref, rwt_ref, rb_ref, *rest, glu, alpha):
    if glu:
        wv_ref, wg_ref, x1_ref, h_ref, route_ref, cnt_out_ref, cnt_ref = rest
        a = jax.nn.gelu(m_ref[0], approximate=True).astype(BF16)
        val = jnp.dot(a, wv_ref[...], preferred_element_type=F32)
        gate = jnp.dot(a, wg_ref[...], preferred_element_type=F32)
        m = val * jax.nn.sigmoid(gate)
    else:
        x1_ref, h_ref, route_ref, cnt_out_ref, cnt_ref = rest
        m = m_ref[0]

    @pl.when((pl.program_id(0) == 0) & (pl.program_id(1) == 0))
    def _():
        cnt_ref[...] = jnp.zeros_like(cnt_ref)

    g1 = mod_ref[0, 2:3, :]
    sh2 = mod_ref[0, 3:4, :]
    sc2 = mod_ref[0, 4:5, :]
    x1 = _layer_norm(alpha * x_ref[0] + g1 * m, lng_ref[...], lnb_ref[...])
    x1_ref[0] = x1
    h = x1 * (1.0 + sc2) + sh2
    h_ref[...] = h
    logits_t = lax.dot_general(rwt_ref[...], h, (((1,), (1,)), ((), ())),
                               precision=HI, preferred_element_type=F32) + rb_ref[...]
    _route(logits_t, cnt_ref, route_ref)
    cnt_out_ref[...] = cnt_ref[...]


def _post_mixer(m, x, mods, ln_g, ln_b, router_w, router_b, alpha, glu_w=None):
    b, l, d = x.shape
    n_e = router_w.shape[1]
    tm = min(l, 512)
    nt = l // tm
    tok = lambda i, j: (i, j, 0)
    const2 = lambda i, j: (0, 0)
    in_specs = [pl.BlockSpec((1, tm, d), tok),
                pl.BlockSpec((1, tm, d), tok),
                pl.BlockSpec((1, 6, d), lambda i, j: (i, 0, 0)),
                pl.BlockSpec((1, d), const2),
                pl.BlockSpec((1, d), const2),
                pl.BlockSpec((n_e, d), const2),
                pl.BlockSpec((n_e, 1), const2)]
    args = [m, x, mods, ln_g.reshape(1, d), ln_b.reshape(1, d), router_w.T, router_b.reshape(n_e, 1)]
    if glu_w is not None:
        in_specs += [pl.BlockSpec((d, d), const2), pl.BlockSpec((d, d), const2)]
        args += [glu_w[0].astype(BF16), glu_w[1].astype(BF16)]
    out_shape = [jax.ShapeDtypeStruct((b, l, d), F32),
                 jax.ShapeDtypeStruct((b * l, d), F32),
                 jax.ShapeDtypeStruct((ROUTE_ROWS, b * l), F32),
                 jax.ShapeDtypeStruct((n_e, LANES), F32)]
    out_specs = [pl.BlockSpec((1, tm, d), tok),
                 pl.BlockSpec((tm, d), lambda i, j: (i * nt + j, 0)),
                 pl.BlockSpec((ROUTE_ROWS, tm), lambda i, j: (0, i * nt + j)),
                 pl.BlockSpec((n_e, LANES), const2)]
    return pl.pallas_call(
        functools.partial(_post_mixer_body, glu=glu_w is not None, alpha=alpha),
        grid=(b, nt),
        in_specs=in_specs,
        out_specs=out_specs,
        out_shape=out_shape,
        scratch_shapes=[pltpu.VMEM((n_e, LANES), F32)],
        compiler_params=_cparams("arbitrary", "arbitrary"),
        name="post_mixer_glu" if glu_w is not None else "post_mixer",
    )(*args)


def _dispatch_plan(route, counts, tile):
    n_e = counts.shape[0]
    cnt = counts[:, 0].astype(I32)
    tiles = (cnt + tile - 1) // tile
    tile_end = jnp.cumsum(tiles)
    row_off = (tile_end - tiles) * tile
    eids = jnp.arange(n_e, dtype=I32)[:, None]

    def row_offset_of(e):
        return jnp.sum(jnp.where(e[None, :] == eids, row_off[:, None], 0), axis=0)

    pos0 = row_offset_of(route[0].astype(I32)) + route[4].astype(I32)
    pos1 = row_offset_of(route[1].astype(I32)) + route[5].astype(I32)
    n_tiles_max = (2 * route.shape[1]) // tile + n_e
    tile_ids = jnp.arange(n_tiles_max, dtype=I32)
    tile_expert = jnp.minimum(jnp.sum((tile_end[None, :] <= tile_ids[:, None]).astype(I32), axis=1), n_e - 1)
    n_used = tile_end[-1:].astype(I32)
    last_tile_row = (row_off + (tiles - 1) * tile).astype(I32)
    return pos0, pos1, tile_expert, n_used, last_tile_row, tiles.astype(I32), n_tiles_max


def _zero_tiles(last_ref, tiles_ref, nu_ref, zero_ref, hs_ref, sem, n_e, tile, wait):
    for e in range(n_e):
        @pl.when(tiles_ref[e] > 0)
        def _(e=e):
            start = pl.multiple_of(last_ref[e], tile)
            cp = pltpu.make_async_copy(zero_ref, hs_ref.at[pl.ds(start, tile)], sem)
            if wait:
                cp.wait()
            else:
                cp.start()

    def body(j, carry):
        start = pl.multiple_of(j * tile, tile)
        cp = pltpu.make_async_copy(zero_ref, hs_ref.at[pl.ds(start, tile)], sem)
        if wait:
            cp.wait()
        else:
            cp.start()
        return carry

    lax.fori_loop(nu_ref[0], hs_ref.shape[0] // tile, body, 0)


def _dispatch_body(pos0_ref, pos1_ref, last_ref, tiles_ref, nu_ref, h_ref, hs_ref, zero_ref, sem_z, sem, *,
                   tm, tile, n_e):
    i = pl.program_id(0)

    @pl.when(i == 0)
    def _():
        zero_ref[...] = jnp.zeros_like(zero_ref)
        _zero_tiles(last_ref, tiles_ref, nu_ref, zero_ref, hs_ref, sem_z, n_e, tile, False)
        _zero_tiles(last_ref, tiles_ref, nu_ref, zero_ref, hs_ref, sem_z, n_e, tile, True)

    base = i * tm

    def issue(r, carry):
        src = h_ref.at[pl.ds(r, 1)]
        pltpu.make_async_copy(src, hs_ref.at[pl.ds(pos0_ref[base + r], 1)], sem).start()
        pltpu.make_async_copy(src, hs_ref.at[pl.ds(pos1_ref[base + r], 1)], sem).start(priority=1)
        return carry

    lax.fori_loop(0, tm, issue, 0, unroll=8)
    for _ in range(2):
        pltpu.make_async_copy(h_ref, hs_ref.at[pl.ds(0, tm)], sem).wait()


def _dispatch(h_rows, pos0, pos1, last_tile_row, tiles, n_used, n_rows, tile):
    n, d = h_rows.shape
    n_e = tiles.shape[0]
    tm = min(n, 256)
    grid_spec = pltpu.PrefetchScalarGridSpec(
        num_scalar_prefetch=5,
        grid=(n // tm,),
        in_specs=[pl.BlockSpec((tm, d), lambda i, *_: (i, 0))],
        out_specs=pl.BlockSpec(memory_space=pl.ANY),
        scratch_shapes=[pltpu.VMEM((tile, d), F32),
                        pltpu.SemaphoreType.DMA(()), pltpu.SemaphoreType.DMA(())],
    )
    return pl.pallas_call(
        functools.partial(_dispatch_body, tm=tm, tile=tile, n_e=n_e),
        grid_spec=grid_spec,
        out_shape=jax.ShapeDtypeStruct((n_rows, d), F32),
        compiler_params=_cparams("arbitrary"),
        name="moe_dispatch",
    )(pos0, pos1, last_tile_row, tiles, n_used, h_rows)


def _expert_body(te_ref, nu_ref, hs_ref, wg_ref, wu_ref, wd_ref, ys_ref, wgb, wub, wdb):
    i = pl.program_id(0)
    first = jnp.logical_or(i == 0, te_ref[i] != te_ref[jnp.maximum(i - 1, 0)])

    @pl.when(jnp.logical_and(i < nu_ref[0], first))
    def _():
        wgb[...] = wg_ref[0, 0].astype(BF16)
        wub[...] = wu_ref[0, 0].astype(BF16)
        wdb[...] = wd_ref[0, 0].astype(BF16)

    @pl.when(i < nu_ref[0])
    def _():
        x = hs_ref[...].astype(BF16)
        gate = jnp.dot(x, wgb[...], preferred_element_type=F32)
        up = jnp.dot(x, wub[...], preferred_element_type=F32)
        a = (gate * jax.nn.sigmoid(gate) * up).astype(BF16)
        ys_ref[...] = jnp.dot(a, wdb[...], preferred_element_type=F32)

    @pl.when(i >= nu_ref[0])
    def _():
        ys_ref[...] = jnp.zeros_like(ys_ref)


def _expert_ffn(hs, tile_expert, n_used, w_gate, w_up, w_down, layer, n_tiles, tile):
    _, n_e, d, f = w_gate.shape
    wmap = lambda i, te, nu: (layer, te[i], 0, 0)
    grid_spec = pltpu.PrefetchScalarGridSpec(
        num_scalar_prefetch=2,
        grid=(n_tiles,),
        in_specs=[pl.BlockSpec((tile, d), lambda i, te, nu: (jnp.maximum(jnp.minimum(i, nu[0] - 1), 0), 0)),
                  pl.BlockSpec((1, 1, d, f), wmap),
                  pl.BlockSpec((1, 1, d, f), wmap),
                  pl.BlockSpec((1, 1, f, d), wmap)],
        out_specs=pl.BlockSpec((tile, d), lambda i, te, nu: (i, 0)),
        scratch_shapes=[pltpu.VMEM((d, f), BF16), pltpu.VMEM((d, f), BF16), pltpu.VMEM((f, d), BF16)],
    )
    return pl.pallas_call(
        _expert_body,
        grid_spec=grid_spec,
        out_shape=jax.ShapeDtypeStruct((n_tiles * tile, d), F32),
        compiler_params=_cparams("arbitrary"),
        name="moe_experts",
    )(tile_expert, n_used, hs, w_gate, w_up, w_down)


def _combine_body(pos0_ref, pos1_ref, x_ref, mod_ref, wts_ref, lng_ref, lnb_ref, ys_ref, *rest,
                  tm, alpha, next_mod):
    if next_mod:
        nmod_ref, x2_ref, h_ref, buf0, buf1, sem = rest
    else:
        x2_ref, buf0, buf1, sem = rest
    base = (pl.program_id(0) * pl.num_programs(1) + pl.program_id(1)) * tm

    def issue(r, carry):
        pltpu.make_async_copy(ys_ref.at[pl.ds(pos0_ref[base + r], 1)], buf0.at[pl.ds(r, 1)], sem).start()
        pltpu.make_async_copy(ys_ref.at[pl.ds(pos1_ref[base + r], 1)], buf1.at[pl.ds(r, 1)],
                              sem).start(priority=1)
        return carry

    lax.fori_loop(0, tm, issue, 0, unroll=8)
    for buf in (buf0, buf1):
        pltpu.make_async_copy(ys_ref.at[pl.ds(0, tm)], buf, sem).wait()

    w = wts_ref[...]
    moe = w[:, 0:1] * buf0[...] + w[:, 1:2] * buf1[...]
    g2 = mod_ref[0, 5:6, :]
    x2 = _layer_norm(alpha * x_ref[0] + g2 * moe, lng_ref[...], lnb_ref[...])
    x2_ref[0] = x2
    if next_mod:
        h_ref[0] = x2 * (1.0 + nmod_ref[0, 1:2, :]) + nmod_ref[0, 0:1, :]


def _combine(ys, pos0, pos1, wts, x, mods, ln_g, ln_b, alpha, next_mods=None):
    b, l, d = x.shape
    tm = min(l, 256)
    nt = l // tm
    tok = lambda i, j, *_: (i, j, 0)
    bat = lambda i, j, *_: (i, 0, 0)
    const2 = lambda i, j, *_: (0, 0)
    in_specs = [pl.BlockSpec((1, tm, d), tok),
                pl.BlockSpec((1, 6, d), bat),
                pl.BlockSpec((tm, 2), lambda i, j, *_: (i * nt + j, 0)),
                pl.BlockSpec((1, d), const2),
                pl.BlockSpec((1, d), const2),
                pl.BlockSpec(memory_space=pl.ANY)]
    args = [x, mods, wts, ln_g.reshape(1, d), ln_b.reshape(1, d), ys]
    out_shape = [jax.ShapeDtypeStruct((b, l, d), F32)]
    out_specs = [pl.BlockSpec((1, tm, d), tok)]
    if next_mods is not None:
        in_specs.append(pl.BlockSpec((1, 6, d), bat))
        args.append(next_mods)
        out_shape.append(jax.ShapeDtypeStruct((b, l, d), F32))
        out_specs.append(pl.BlockSpec((1, tm, d), tok))
    grid_spec = pltpu.PrefetchScalarGridSpec(
        num_scalar_prefetch=2,
        grid=(b, nt),
        in_specs=in_specs,
        out_specs=out_specs,
        scratch_shapes=[pltpu.VMEM((tm, d), F32), pltpu.VMEM((tm, d), F32), pltpu.SemaphoreType.DMA(())],
    )
    return pl.pallas_call(
        functools.partial(_combine_body, tm=tm, alpha=alpha, next_mod=next_mods is not None),
        grid_spec=grid_spec,
        out_shape=out_shape,
        compiler_params=_cparams("arbitrary", "arbitrary"),
        name="moe_combine",
    )(pos0, pos1, *args)


def _moe(h_rows, route, counts, w_gate, w_up, w_down, layer):
    n = h_rows.shape[0]
    tile = min(n, 512)
    pos0, pos1, tile_expert, n_used, last_tile_row, tiles, n_tiles = _dispatch_plan(route, counts, tile)
    hs = _dispatch(h_rows, pos0, pos1, last_tile_row, tiles, n_used, n_tiles * tile, tile)
    ys = _expert_ffn(hs, tile_expert, n_used, w_gate, w_up, w_down, layer, n_tiles, tile)
    return ys, pos0, pos1, jnp.transpose(route[2:4])


def _pool_group(h_ref, w_ref, sc_ref, o_ref, col_ref, k, n_rows):
    n = n_rows * GRID_W
    c = h_ref.shape[2]
    blk = 4 * GRID_W
    half = k // 2
    pad = half * GRID_W
    ti = lax.broadcasted_iota(I32, (blk, blk), 0)
    si = lax.broadcasted_iota(I32, (blk, blk), 1)
    shift = GRID_W.bit_length() - 1
    same_row = (ti >> shift) == (si >> shift)
    band = jnp.where(same_row & (si - ti >= -half) & (si - ti <= half - 1), 1.0, 0.0)
    col_ref[0:pad, :] = jnp.zeros((pad, c), F32)
    col_ref[pad + n:pad + n + pad, :] = jnp.zeros((pad, c), F32)
    for b0 in range(0, n, blk):
        col_ref[pad + b0:pad + b0 + blk, :] = jnp.dot(band, h_ref[0, b0:b0 + blk, :], precision=HI,
                                                      preferred_element_type=F32)
    acc = col_ref[0:n, :]
    for j in range(1, k):
        acc = acc + col_ref[j * GRID_W:j * GRID_W + n, :]
    t = lax.broadcasted_iota(I32, (n, 1), 0)
    wc = t & (GRID_W - 1)
    wr = t >> shift
    cnt_c = jnp.minimum(wc + half - 1, GRID_W - 1) - jnp.maximum(wc - half, 0) + 1
    cnt_r = jnp.minimum(wr + half - 1, n_rows - 1) - jnp.maximum(wr - half, 0) + 1
    mean = acc / (cnt_c * cnt_r).astype(F32)
    pooled = (mean - h_ref[0]).astype(BF16)
    o_ref[0] = jnp.dot(pooled, w_ref[0].astype(BF16), preferred_element_type=F32) * sc_ref[...]


def _pool_body(h_ref, w_ref, sc_ref, o_ref, col_ref, *, n_rows):
    g = pl.program_id(1)
    for gi, k in enumerate(POOL_WINDOWS):
        @pl.when(g == gi)
        def _(k=k):
            _pool_group(h_ref, w_ref, sc_ref, o_ref, col_ref, k, n_rows)


def _pool_mix(h, w_grp, scale):
    b, n, d = h.shape
    n_g, c, _ = w_grp.shape
    n_rows = n // GRID_W
    pad = (max(POOL_WINDOWS) // 2) * GRID_W
    return pl.pallas_call(
        functools.partial(_pool_body, n_rows=n_rows),
        grid=(b, n_g),
        in_specs=[pl.BlockSpec((1, n, c), lambda i, j: (i, 0, j)),
                  pl.BlockSpec((1, c, c), lambda i, j: (j, 0, 0)),
                  pl.BlockSpec((1, c), lambda i, j: (0, j))],
        out_specs=pl.BlockSpec((1, n, c), lambda i, j: (i, 0, j)),
        out_shape=jax.ShapeDtypeStruct((b, n, d), F32),
        scratch_shapes=[pltpu.VMEM((n + 2 * pad, c), F32)],
        compiler_params=_cparams("arbitrary", "arbitrary"),
        name="pool_mix",
    )(h, w_grp, scale.reshape(1, d))


def kernel(x, c, ctx, c_ctx, mod_w, mod_b, ln_g, ln_b, s5_lam_re, s5_lam_im, s5_log_dt, s5_b_re, s5_b_im,
           s5_c_re, s5_c_im, s5_d, s5_w_val, s5_w_gate, pool_w, pool_scale, router_w, router_b,
           moe_w_gate, moe_w_up, moe_w_down):
    b, l, d = x.shape
    depth = mod_w.shape[0]
    assert depth == 2 and b + 1 <= SUBLANES and d % LANES == 0 and GRID_W & (GRID_W - 1) == 0
    alpha = (2 * depth) ** 0.25

    cond = jnp.zeros((SUBLANES, d), F32).at[:b].set(c).at[b].set(c_ctx)
    mods = _modulation(cond, mod_w, mod_b).reshape(depth, SUBLANES, 6, d)

    toep, w1, w2t, a_r, a_i = _s5_weights(s5_lam_re[0], s5_lam_im[0], s5_log_dt[0], s5_b_re[0], s5_b_im[0],
                                          s5_c_re[0], s5_c_im[0], s5_d[0])
    y = _s5_mix(x, ctx, mods[0, :b], mods[0, b:b + 1], toep, w1, w2t, a_r, a_i)
    x1, h_rows, route, counts = _post_mixer(y, x, mods[0, :b], ln_g[0, 0], ln_b[0, 0], router_w, router_b,
                                            alpha, glu_w=(s5_w_val[0], s5_w_gate[0]))
    ys, pos0, pos1, wts = _moe(h_rows, route, counts, moe_w_gate, moe_w_up, moe_w_down, 0)
    x2, h = _combine(ys, pos0, pos1, wts, x1, mods[0, :b], ln_g[0, 1], ln_b[0, 1], alpha,
                     next_mods=mods[1, :b])

    m = _pool_mix(h, pool_w[0], pool_scale[0])
    x3, h_rows, route, counts = _post_mixer(m, x2, mods[1, :b], ln_g[1, 0], ln_b[1, 0], router_w, router_b,
                                            alpha)
    ys, pos0, pos1, wts = _moe(h_rows, route, counts, moe_w_gate, moe_w_up, moe_w_down, 1)
    (out,) = _combine(ys, pos0, pos1, wts, x3, mods[1, :b], ln_g[1, 1], ln_b[1, 1], alpha)
    return out
```

```python
import functools

import jax
import jax.numpy as jnp
from jax import lax
from jax.experimental import pallas as pl
from jax.experimental.pallas import tpu as pltpu

F32 = jnp.float32
BF16 = jnp.bfloat16
I32 = jnp.int32
HI = lax.Precision.HIGHEST

GRID_W = 64
S5_H = 16
S5_T = 16
POOL_WINDOWS = (2, 4, 8, 16)
N_EXPERT_GROUPS = 4
LN_EPS = 1e-5
LANES = 128
SUBLANES = 8
VMEM_LIMIT = 52 * 1024 * 1024

ROUTE_ROWS = 8


def _cparams(*sem):
    return pltpu.CompilerParams(dimension_semantics=sem, vmem_limit_bytes=VMEM_LIMIT)


def _mod_body(c_ref, w_ref, b_ref, o_ref):
    c = c_ref[...]
    s = c * jax.nn.sigmoid(c)
    o_ref[0] = jnp.dot(s, w_ref[0], precision=HI, preferred_element_type=F32) + b_ref[0]


def _modulation(cond, mod_w, mod_b):
    depth, d, n6 = mod_w.shape
    tn = min(n6, 1536)
    return pl.pallas_call(
        _mod_body,
        grid=(depth, n6 // tn),
        in_specs=[pl.BlockSpec((SUBLANES, d), lambda i, j: (0, 0)),
                  pl.BlockSpec((1, d, tn), lambda i, j: (i, 0, j)),
                  pl.BlockSpec((1, 1, tn), lambda i, j: (i, 0, j))],
        out_specs=pl.BlockSpec((1, SUBLANES, tn), lambda i, j: (i, 0, j)),
        out_shape=jax.ShapeDtypeStruct((depth, SUBLANES, n6), F32),
        compiler_params=_cparams("arbitrary", "arbitrary"),
        name="modulation",
    )(cond, mod_w, mod_b.reshape(depth, 1, n6))


def _s5_direction_terms(lam_re, lam_im, log_dt, b_re, b_im):
    lr = lam_re.astype(F32)
    li = lam_im.astype(F32)
    dt = jnp.exp(log_dt.astype(F32))[:, None]
    mag = jnp.exp(lr * dt)
    ar = mag * jnp.cos(li * dt)
    ai = mag * jnp.sin(li * dt)
    den = lr * lr + li * li
    nr = ar - 1.0
    fr = (nr * lr + ai * li) / den
    fi = (ai * lr - nr * li) / den
    br_, bi_ = b_re.astype(F32), b_im.astype(F32)
    bbr = fr[..., None] * br_ - fi[..., None] * bi_
    bbi = fr[..., None] * bi_ + fi[..., None] * br_
    k = jnp.arange(S5_T + 1, dtype=F32)[:, None, None]
    pm = jnp.exp(k * (lr * dt))
    pr = pm * jnp.cos(k * (li * dt))
    pi = pm * jnp.sin(k * (li * dt))
    return pr, pi, bbr, bbi


def _s5_weights(lam_re, lam_im, log_dt, b_re, b_im, c_re, c_im, d_skip):
    t = S5_T
    g, p = lam_re.shape[1:]
    h = b_re.shape[-1]
    terms = [_s5_direction_terms(lam_re[d], lam_im[d], log_dt[d], b_re[d], b_im[d]) for d in (0, 1)]
    pw = jnp.stack([jnp.transpose(terms[d][k], (1, 0, 2)) for d in (0, 1) for k in (0, 1)], axis=1)
    bt = jnp.stack([jnp.transpose(terms[d][k], (0, 2, 1)) for d in (0, 1) for k in (2, 3)], axis=1)
    cc = jnp.stack([c[d].astype(F32) for d in (0, 1) for c in (c_re, c_im)], axis=1)
    gp = 8
    spec4 = lambda rows: pl.BlockSpec((gp, 4, rows, p), lambda i: (i, 0, 0, 0))
    wide = pl.BlockSpec((gp, t * h, t * h), lambda i: (i, 0, 0))
    toep, w1, w2t = pl.pallas_call(
        functools.partial(_s5_prep_body, gp=gp),
        grid=(g // gp,),
        in_specs=[spec4(t + 1), spec4(h), spec4(h), pl.BlockSpec((gp, 1, h), lambda i: (i, 0, 0))],
        out_specs=[wide, wide, wide],
        out_shape=[jax.ShapeDtypeStruct((g, t * h, t * h), BF16) for _ in range(3)],
        compiler_params=_cparams("arbitrary"),
        name="s5_prep",
    )(pw, bt, cc, d_skip.astype(F32).reshape(g, 1, h))
    a_r = jnp.concatenate([pw[:, 0, t], pw[:, 2, t]], axis=-1)[:, None, :]
    a_i = jnp.concatenate([pw[:, 1, t], pw[:, 3, t]], axis=-1)[:, None, :]
    return toep, w1, w2t, a_r, a_i


def _cmul(ar, ai, br, bi):
    return ar * br - ai * bi, ar * bi + ai * br


def _s5_prep_body(pw_ref, bt_ref, cc_ref, d_ref, toep_ref, w1_ref, w2t_ref, *, gp):
    t = pw_ref.shape[2] - 1
    h = bt_ref.shape[2]
    nt = (((1,), (1,)), ((), ()))
    eye = (lax.broadcasted_iota(I32, (h, h), 0) == lax.broadcasted_iota(I32, (h, h), 1))
    for g in range(gp):
        prf, pif, prb, pib = (pw_ref[g, k] for k in range(4))
        btf = (bt_ref[g, 0], bt_ref[g, 1])
        btb = (bt_ref[g, 2], bt_ref[g, 3])
        ccf = (cc_ref[g, 0], cc_ref[g, 1])
        ccb = (cc_ref[g, 2], cc_ref[g, 3])
        caf, cab = [], []
        for j in range(t):
            f_r, f_i = _cmul(*btf, prf[t - 1 - j:t - j], pif[t - 1 - j:t - j])
            b_r, b_i = _cmul(*btb, prb[j:j + 1], pib[j:j + 1])
            w1_ref[g, j * h:(j + 1) * h, :] = jnp.concatenate([f_r, b_r, f_i, b_i], axis=1).astype(BF16)
            mf_r, mf_i = _cmul(*ccf, prf[j + 1:j + 2], pif[j + 1:j + 2])
            mb_r, mb_i = _cmul(*ccb, prb[t - j:t - j + 1], pib[t - j:t - j + 1])
            w2t_ref[g, j * h:(j + 1) * h, :] = jnp.concatenate([mf_r, mb_r, -mf_i, -mb_i],
                                                               axis=1).astype(BF16)
            caf.append(_cmul(*ccf, prf[j:j + 1], pif[j:j + 1]))
            cab.append(_cmul(*ccb, prb[t - 1 - j:t - j], pib[t - 1 - j:t - j]))

        def lag_kernels(btx, ca):
            car = jnp.concatenate([c[0] for c in ca], axis=0)
            cai = jnp.concatenate([c[1] for c in ca], axis=0)
            return (lax.dot_general(btx[0], car, nt, precision=HI, preferred_element_type=F32)
                    - lax.dot_general(btx[1], cai, nt, precision=HI, preferred_element_type=F32))

        kf = lag_kernels(btf, caf)
        kb = lag_kernels(btb, cab)
        skip = jnp.where(eye, jnp.broadcast_to(d_ref[g], (h, h)), 0.0)
        mid = kb[:, (t - 1) * h:] + kf[:, :h] + skip
        kwide = jnp.concatenate([kb[:, :(t - 1) * h], mid, kf[:, h:], jnp.zeros((h, h), F32)], axis=1)
        for i in range(t):
            off = (t - 1 - i) * h
            toep_ref[g, i * h:(i + 1) * h, :] = kwide[:, off:off + t * h].astype(BF16)


def _granule_transpose(v):
    n = len(v)
    gran = lax.broadcasted_iota(I32, v[0].shape, 1) >> 4
    at = [gran == q for q in range(n)]
    rot = []
    for d in range(n):
        m = v[d]
        for q in range(1, n):
            m = jnp.where(at[q], v[(q + d) % n], m)
        rot.append(pltpu.roll(m, d * S5_H, 1) if d else m)
    out = []
    for q in range(n):
        w = rot[(-q) % n]
        for j in range(1, n):
            w = jnp.where(at[j], rot[(j - q) % n], w)
        out.append(w)
    return out


def _s5_body(x_ref, ctx_ref, mod_ref, cmod_ref, tt_ref, w1_ref, w2t_ref, ar_ref, ai_ref, y_ref,
             u_ref, s_ref, *, n_ctx_chunks, n_lat_chunks, gb):
    n_b = x_ref.shape[0]
    n_g = u_ref.shape[0]
    n_blocks = s_ref.shape[1] // SUBLANES
    half = SUBLANES // 2
    tok_blk = SUBLANES * S5_T
    lat_lo = n_ctx_chunks * n_b
    lat_rows = n_lat_chunks * n_b

    def chunk_rows(ref, b, tok0, shift, scale1):
        vs = [ref[b, pl.ds(tok0 + j, SUBLANES, stride=S5_T), :] * scale1 + shift for j in range(S5_T)]
        lo = _granule_transpose(vs[:SUBLANES])
        hi = _granule_transpose(vs[SUBLANES:])
        return lo, hi

    def put_rows(rows, chunk0, b):
        for q in range(n_g):
            for jh in range(2):
                u_ref[q, jh, pl.ds(chunk0 * n_b + b, SUBLANES, stride=n_b), :] = rows[jh][q]

    def u_rows(g, lo, n):
        return jnp.concatenate([u_ref[g, 0, lo:lo + n, :], u_ref[g, 1, lo:lo + n, :]], axis=1)

    shift = jnp.broadcast_to(cmod_ref[0, 0:1, :], (SUBLANES, LANES))
    scale1 = 1.0 + jnp.broadcast_to(cmod_ref[0, 1:2, :], (SUBLANES, LANES))
    for b in range(n_b):
        for cb in range(n_ctx_chunks // SUBLANES):
            rows = chunk_rows(ctx_ref, b, cb * tok_blk, shift, scale1)
            put_rows(rows, cb * SUBLANES, b)
            put_rows(rows, n_ctx_chunks + n_lat_chunks + cb * SUBLANES, b)

    def fill(cb, carry):
        for b in range(n_b):
            shift = jnp.broadcast_to(mod_ref[b, 0:1, :], (SUBLANES, LANES))
            scale1 = 1.0 + jnp.broadcast_to(mod_ref[b, 1:2, :], (SUBLANES, LANES))
            put_rows(chunk_rows(x_ref, b, cb * tok_blk, shift, scale1), n_ctx_chunks + cb * SUBLANES, b)
        return carry

    lax.fori_loop(0, n_lat_chunks // SUBLANES, fill, 0)

    for g in range(n_g):
        s_ref[g] = jnp.dot(u_rows(g, 0, n_blocks * SUBLANES).astype(BF16), w1_ref[g],
                           preferred_element_type=F32)

    lane = lax.broadcasted_iota(I32, (SUBLANES, LANES), 1)
    row = lax.broadcasted_iota(I32, (SUBLANES, LANES), 0)
    is_fwd = lane < LANES // 2
    is_fwd2 = jnp.concatenate([is_fwd, is_fwd], axis=1)
    top = row < half
    zero = jnp.zeros((SUBLANES, LANES), F32)
    for g0 in range(0, n_g, gb):
        ars = [jnp.broadcast_to(ar_ref[g0 + g], (SUBLANES, LANES)) for g in range(gb)]
        ais = [jnp.broadcast_to(ai_ref[g0 + g], (SUBLANES, LANES)) for g in range(gb)]

        def step(k, carry, g0=g0, ars=ars, ais=ais):
            fo = pl.multiple_of(k * SUBLANES, SUBLANES)
            bo = pl.multiple_of((n_blocks - 1 - k) * SUBLANES, SUBLANES)
            new = []
            for g in range(gb):
                xr, xi = carry[2 * g], carry[2 * g + 1]
                vf = s_ref[g0 + g, pl.ds(fo, SUBLANES), :]
                vb_raw = s_ref[g0 + g, pl.ds(bo, SUBLANES), :]
                vb = pltpu.roll(vb_raw, half, 0)
                vr = jnp.where(is_fwd, vf[:, :LANES], vb[:, :LANES])
                vi = jnp.where(is_fwd, vf[:, LANES:], vb[:, LANES:])
                ar, ai = ars[g], ais[g]
                yr = ar * xr - ai * xi + vr
                yi = ar * xi + ai * xr + vi
                yrr = pltpu.roll(yr, half, 0)
                yir = pltpu.roll(yi, half, 0)
                zr = ar * yrr - ai * yir + vr
                zi = ar * yir + ai * yrr + vi
                inc = jnp.concatenate([jnp.where(top, xr, yrr), jnp.where(top, xi, yir)], axis=1)
                s_ref[g0 + g, pl.ds(fo, SUBLANES), :] = jnp.where(is_fwd2, inc, vf)
                s_ref[g0 + g, pl.ds(bo, SUBLANES), :] = jnp.where(is_fwd2, vb_raw, pltpu.roll(inc, half, 0))
                new.append(jnp.where(top, pltpu.roll(zr, half, 0), zr))
                new.append(jnp.where(top, pltpu.roll(zi, half, 0), zi))
            return tuple(new)

        lax.fori_loop(0, (n_ctx_chunks + n_lat_chunks) // 2, step, tuple(zero for _ in range(2 * gb)))

    for g in range(n_g):
        y = (jnp.dot(u_rows(g, lat_lo, lat_rows).astype(BF16), tt_ref[g], preferred_element_type=F32)
             + lax.dot_general(s_ref[g, lat_lo:lat_lo + lat_rows, :].astype(BF16), w2t_ref[g],
                               (((1,), (1,)), ((), ())), preferred_element_type=F32))
        for jh in range(2):
            u_ref[g, jh, lat_lo:lat_lo + lat_rows, :] = y[:, jh * LANES:(jh + 1) * LANES]

    def emit(cb, carry):
        for b in range(n_b):
            for jh in range(S5_T // SUBLANES):
                w = [u_ref[q, jh, pl.ds((n_ctx_chunks + cb * SUBLANES) * n_b + b, SUBLANES, stride=n_b), :]
                     for q in range(n_g)]
                v = _granule_transpose(w)
                for j in range(SUBLANES):
                    y_ref[b, pl.ds(cb * tok_blk + jh * SUBLANES + j, SUBLANES, stride=S5_T), :] = v[j]
        return carry

    lax.fori_loop(0, n_lat_chunks // SUBLANES, emit, 0)


def _s5_mix(x, ctx, mods, cmods, toep, w1, w2t, a_r, a_i):
    b, l, d = x.shape
    n_ctx = ctx.shape[1]
    n_g = LANES // S5_H
    n_ctx_chunks, n_lat_chunks = n_ctx // S5_T, l // S5_T
    assert b * 2 == SUBLANES and n_ctx_chunks % SUBLANES == 0 and n_lat_chunks % SUBLANES == 0
    rows = (2 * n_ctx_chunks + n_lat_chunks) * b
    w = S5_T * S5_H
    body = functools.partial(_s5_body, n_ctx_chunks=n_ctx_chunks, n_lat_chunks=n_lat_chunks, gb=8)
    lane_tile = lambda i: (0, 0, i)
    wspec = pl.BlockSpec((n_g, w, w), lambda i: (i, 0, 0))
    aspec = pl.BlockSpec((n_g, 1, w // 2), lambda i: (i, 0, 0))
    return pl.pallas_call(
        body,
        grid=(d // LANES,),
        in_specs=[pl.BlockSpec((b, l, LANES), lane_tile, pipeline_mode=pl.Buffered(1)),
                  pl.BlockSpec((b, n_ctx, LANES), lane_tile),
                  pl.BlockSpec((b, 6, LANES), lane_tile),
                  pl.BlockSpec((1, 6, LANES), lane_tile),
                  wspec, wspec, wspec, aspec, aspec],
        out_specs=pl.BlockSpec((b, l, LANES), lane_tile, pipeline_mode=pl.Buffered(1)),
        out_shape=jax.ShapeDtypeStruct((b, l, d), F32),
        scratch_shapes=[pltpu.VMEM((n_g, w // LANES, rows, LANES), F32), pltpu.VMEM((n_g, rows, w), F32)],
        compiler_params=_cparams("arbitrary"),
        name="s5_mix",
    )(x, ctx, mods, cmods, toep, w1, w2t, a_r, a_i)


def _layer_norm(r, g, b):
    mu = jnp.mean(r, axis=-1, keepdims=True)
    xc = r - mu
    var = jnp.mean(xc * xc, axis=-1, keepdims=True)
    return xc * lax.rsqrt(var + LN_EPS) * g + b


def _max2_of4(a, b, c, d):
    h1, l1 = jnp.maximum(a, b), jnp.minimum(a, b)
    h2, l2 = jnp.maximum(c, d), jnp.minimum(c, d)
    return jnp.maximum(h1, h2) + jnp.maximum(jnp.minimum(h1, h2), jnp.maximum(l1, l2))


def _argmax_first(vals):
    idx = jnp.zeros(vals[0].shape, I32)
    best = vals[0]
    for j in range(1, len(vals)):
        upd = vals[j] > best
        idx = jnp.where(upd, j, idx)
        best = jnp.where(upd, vals[j], best)
    return idx, best


def _route(logits_t, count_ref, route_ref):
    n_e, tm = logits_t.shape
    per = n_e // N_EXPERT_GROUPS
    mx = jnp.max(logits_t, axis=0, keepdims=True)
    ex = jnp.exp(logits_t - mx)
    sc = ex / jnp.sum(ex, axis=0, keepdims=True)
    rows = [sc[e:e + 1, :] for e in range(n_e)]
    gscore = [_max2_of4(*rows[per * g:per * (g + 1)]) for g in range(N_EXPERT_GROUPS)]
    best, _ = _argmax_first(gscore)
    vals = []
    for j in range(per):
        v = rows[per * (N_EXPERT_GROUPS - 1) + j]
        for g in range(N_EXPERT_GROUPS - 2, -1, -1):
            v = jnp.where(best == g, rows[per * g + j], v)
        vals.append(v)
    i1, m1 = _argmax_first(vals)
    i2, m2 = _argmax_first([jnp.where(i1 == j, -1.0, vals[j]) for j in range(per)])
    den = m1 + m2
    e0 = best * per + i1
    e1 = best * per + i2

    eidx = lax.broadcasted_iota(I32, (n_e, tm), 0)
    hit0 = eidx == e0
    hit1 = eidx == e1
    onehot = jnp.where(hit0, 1.0, jnp.where(hit1, 1.0, 0.0))
    src = lax.broadcasted_iota(I32, (tm, tm), 0)
    dst = lax.broadcasted_iota(I32, (tm, tm), 1)
    tri = jnp.where(src <= dst, 1.0, 0.0).astype(BF16)
    cum = jnp.dot(onehot.astype(BF16), tri, preferred_element_type=F32)
    excl = cum - onehot + count_ref[:, 0:1]
    rank0 = jnp.sum(jnp.where(hit0, excl, 0.0), axis=0, keepdims=True)
    rank1 = jnp.sum(jnp.where(hit1, excl, 0.0), axis=0, keepdims=True)
    count_ref[...] = count_ref[...] + jnp.sum(onehot, axis=1, keepdims=True)

    zero = jnp.zeros((1, tm), F32)
    route_ref[...] = jnp.concatenate(
        [e0.astype(F32), e1.astype(F32), m1 / den, m2 / den, rank0, rank1, zero, zero], axis=0)


def _post_mixer_body(m_ref, x_ref, mod_ref, lng_ref, lnb_ref, rwt_ref, rb_ref, *rest, glu, alpha):
    if glu:
        wv_ref, wg_ref, x1_ref, h_ref, route_ref, cnt_out_ref, cnt_ref = rest
        a = jax.nn.gelu(m_ref[0], approximate=True).astype(BF16)
        val = jnp.dot(a, wv_ref[...], preferred_element_type=F32)
        gate = jnp.dot(a, wg_ref[...], preferred_element_type=F32)
        m = val * jax.nn.sigmoid(gate)
    else:
        x1_ref, h_ref, route_ref, cnt_out_ref, cnt_ref = rest
        m = m_ref[0]

    @pl.when((pl.program_id(0) == 0) & (pl.program_id(1) == 0))
    def _():
        cnt_ref[...] = jnp.zeros_like(cnt_ref)

    g1 = mod_ref[0, 2:3, :]
    sh2 = mod_ref[0, 3:4, :]
    sc2 = mod_ref[0, 4:5, :]
    x1 = _layer_norm(alpha * x_ref[0] + g1 * m, lng_ref[...], lnb_ref[...])
    x1_ref[0] = x1
    h = x1 * (1.0 + sc2) + sh2
    h_ref[...] = h
    logits_t = lax.dot_general(rwt_ref[...], h, (((1,), (1,)), ((), ())),
                               precision=HI, preferred_element_type=F32) + rb_ref[...]
    _route(logits_t, cnt_ref, route_ref)
    cnt_out_ref[...] = cnt_ref[...]


def _post_mixer(m, x, mods, ln_g, ln_b, router_w, router_b, alpha, glu_w=None):
    b, l, d = x.shape
    n_e = router_w.shape[1]
    tm = min(l, 512)
    nt = l // tm
    tok = lambda i, j: (i, j, 0)
    const2 = lambda i, j: (0, 0)
    in_specs = [pl.BlockSpec((1, tm, d), tok),
                pl.BlockSpec((1, tm, d), tok),
                pl.BlockSpec((1, 6, d), lambda i, j: (i, 0, 0)),
                pl.BlockSpec((1, d), const2),
                pl.BlockSpec((1, d), const2),
                pl.BlockSpec((n_e, d), const2),
                pl.BlockSpec((n_e, 1), const2)]
    args = [m, x, mods, ln_g.reshape(1, d), ln_b.reshape(1, d), router_w.T, router_b.reshape(n_e, 1)]
    if glu_w is not None:
        in_specs += [pl.BlockSpec((d, d), const2), pl.BlockSpec((d, d), const2)]
        args += [glu_w[0].astype(BF16), glu_w[1].astype(BF16)]
    out_shape = [jax.ShapeDtypeStruct((b, l, d), F32),
                 jax.ShapeDtypeStruct((b * l, d), F32),
                 jax.ShapeDtypeStruct((ROUTE_ROWS, b * l), F32),
                 jax.ShapeDtypeStruct((n_e, LANES), F32)]
    out_specs = [pl.BlockSpec((1, tm, d), tok),
                 pl.BlockSpec((tm, d), lambda i, j: (i * nt + j, 0)),
                 pl.BlockSpec((ROUTE_ROWS, tm), lambda i, j: (0, i * nt + j)),
                 pl.BlockSpec((n_e, LANES), const2)]
    return pl.pallas_call(
        functools.partial(_post_mixer_body, glu=glu_w is not None, alpha=alpha),
        grid=(b, nt),
        in_specs=in_specs,
        out_specs=out_specs,
        out_shape=out_shape,
        scratch_shapes=[pltpu.VMEM((n_e, LANES), F32)],
        compiler_params=_cparams("arbitrary", "arbitrary"),
        name="post_mixer_glu" if glu_w is not None else "post_mixer",
    )(*args)


def _dispatch_plan(route, counts, tile):
    n_e = counts.shape[0]
    cnt = counts[:, 0].astype(I32)
    tiles = (cnt + tile - 1) // tile
    tile_end = jnp.cumsum(tiles)
    row_off = (tile_end - tiles) * tile
    eids = jnp.arange(n_e, dtype=I32)[:, None]

    def row_offset_of(e):
        return jnp.sum(jnp.where(e[None, :] == eids, row_off[:, None], 0), axis=0)

    pos0 = row_offset_of(route[0].astype(I32)) + route[4].astype(I32)
    pos1 = row_offset_of(route[1].astype(I32)) + route[5].astype(I32)
    n_tiles_max = (2 * route.shape[1]) // tile + n_e
    tile_ids = jnp.arange(n_tiles_max, dtype=I32)
    tile_expert = jnp.minimum(jnp.sum((tile_end[None, :] <= tile_ids[:, None]).astype(I32), axis=1), n_e - 1)
    n_used = tile_end[-1:].astype(I32)
    last_tile_row = (row_off + (tiles - 1) * tile).astype(I32)
    return pos0, pos1, tile_expert, n_used, last_tile_row, tiles.astype(I32), n_tiles_max


def _zero_tiles(last_ref, tiles_ref, nu_ref, zero_ref, hs_ref, sem, n_e, tile, wait):
    for e in range(n_e):
        @pl.when(tiles_ref[e] > 0)
        def _(e=e):
            start = pl.multiple_of(last_ref[e], tile)
            cp = pltpu.make_async_copy(zero_ref, hs_ref.at[pl.ds(start, tile)], sem)
            if wait:
                cp.wait()
            else:
                cp.start()

    def body(j, carry):
        start = pl.multiple_of(j * tile, tile)
        cp = pltpu.make_async_copy(zero_ref, hs_ref.at[pl.ds(start, tile)], sem)
        if wait:
            cp.wait()
        else:
            cp.start()
        return carry

    lax.fori_loop(nu_ref[0], hs_ref.shape[0] // tile, body, 0)


def _dispatch_body(pos0_ref, pos1_ref, last_ref, tiles_ref, nu_ref, h_ref, hs_ref, zero_ref, sem_z, sem, *,
                   tm, tile, n_e):
    i = pl.program_id(0)

    @pl.when(i == 0)
    def _():
        zero_ref[...] = jnp.zeros_like(zero_ref)
        _zero_tiles(last_ref, tiles_ref, nu_ref, zero_ref, hs_ref, sem_z, n_e, tile, False)
        _zero_tiles(last_ref, tiles_ref, nu_ref, zero_ref, hs_ref, sem_z, n_e, tile, True)

    base = i * tm

    def issue(r, carry):
        src = h_ref.at[pl.ds(r, 1)]
        pltpu.make_async_copy(src, hs_ref.at[pl.ds(pos0_ref[base + r], 1)], sem).start()
        pltpu.make_async_copy(src, hs_ref.at[pl.ds(pos1_ref[base + r], 1)], sem).start(priority=1)
        return carry

    lax.fori_loop(0, tm, issue, 0, unroll=8)
    for _ in range(2):
        pltpu.make_async_copy(h_ref, hs_ref.at[pl.ds(0, tm)], sem).wait()


def _dispatch(h_rows, pos0, pos1, last_tile_row, tiles, n_used, n_rows, tile):
    n, d = h_rows.shape
    n_e = tiles.shape[0]
    tm = min(n, 256)
    grid_spec = pltpu.PrefetchScalarGridSpec(
        num_scalar_prefetch=5,
        grid=(n // tm,),
        in_specs=[pl.BlockSpec((tm, d), lambda i, *_: (i, 0))],
        out_specs=pl.BlockSpec(memory_space=pl.ANY),
        scratch_shapes=[pltpu.VMEM((tile, d), F32),
                        pltpu.SemaphoreType.DMA(()), pltpu.SemaphoreType.DMA(())],
    )
    return pl.pallas_call(
        functools.partial(_dispatch_body, tm=tm, tile=tile, n_e=n_e),
        grid_spec=grid_spec,
        out_shape=jax.ShapeDtypeStruct((n_rows, d), F32),
        compiler_params=_cparams("arbitrary"),
        name="moe_dispatch",
    )(pos0, pos1, last_tile_row, tiles, n_used, h_rows)


def _expert_body(te_ref, nu_ref, hs_ref, wg_ref, wu_ref, wd_ref, ys_ref, wgb, wub, wdb):
    i = pl.program_id(0)
    first = jnp.logical_or(i == 0, te_ref[i] != te_ref[jnp.maximum(i - 1, 0)])

    @pl.when(jnp.logical_and(i < nu_ref[0], first))
    def _():
        wgb[...] = wg_ref[0, 0].astype(BF16)
        wub[...] = wu_ref[0, 0].astype(BF16)
        wdb[...] = wd_ref[0, 0].astype(BF16)

    @pl.when(i < nu_ref[0])
    def _():
        x = hs_ref[...].astype(BF16)
        gate = jnp.dot(x, wgb[...], preferred_element_type=F32)
        up = jnp.dot(x, wub[...], preferred_element_type=F32)
        a = (gate * jax.nn.sigmoid(gate) * up).astype(BF16)
        ys_ref[...] = jnp.dot(a, wdb[...], preferred_element_type=F32)

    @pl.when(i >= nu_ref[0])
    def _():
        ys_ref[...] = jnp.zeros_like(ys_ref)


def _expert_ffn(hs, tile_expert, n_used, w_gate, w_up, w_down, layer, n_tiles, tile):
    _, n_e, d, f = w_gate.shape
    wmap = lambda i, te, nu: (layer, te[i], 0, 0)
    grid_spec = pltpu.PrefetchScalarGridSpec(
        num_scalar_prefetch=2,
        grid=(n_tiles,),
        in_specs=[pl.BlockSpec((tile, d), lambda i, te, nu: (jnp.maximum(jnp.minimum(i, nu[0] - 1), 0), 0)),
                  pl.BlockSpec((1, 1, d, f), wmap),
                  pl.BlockSpec((1, 1, d, f), wmap),
                  pl.BlockSpec((1, 1, f, d), wmap)],
        out_specs=pl.BlockSpec((tile, d), lambda i, te, nu: (i, 0)),
        scratch_shapes=[pltpu.VMEM((d, f), BF16), pltpu.VMEM((d, f), BF16), pltpu.VMEM((f, d), BF16)],
    )
    return pl.pallas_call(
        _expert_body,
        grid_spec=grid_spec,
        out_shape=jax.ShapeDtypeStruct((n_tiles * tile, d), F32),
        compiler_params=_cparams("arbitrary"),
        name="moe_experts",
    )(tile_expert, n_used, hs, w_gate, w_up, w_down)


def _combine_body(pos0_ref, pos1_ref, x_ref, mod_ref, wts_ref, lng_ref, lnb_ref, ys_ref, *rest,
                  tm, alpha, next_mod):
    if next_mod:
        nmod_ref, x2_ref, h_ref, buf0, buf1, sem = rest
    else:
        x2_ref, buf0, buf1, sem = rest
    base = (pl.program_id(0) * pl.num_programs(1) + pl.program_id(1)) * tm

    def issue(r, carry):
        pltpu.make_async_copy(ys_ref.at[pl.ds(pos0_ref[base + r], 1)], buf0.at[pl.ds(r, 1)], sem).start()
        pltpu.make_async_copy(ys_ref.at[pl.ds(pos1_ref[base + r], 1)], buf1.at[pl.ds(r, 1)],
                              sem).start(priority=1)
        return carry

    lax.fori_loop(0, tm, issue, 0, unroll=8)
    for buf in (buf0, buf1):
        pltpu.make_async_copy(ys_ref.at[pl.ds(0, tm)], buf, sem).wait()

    w = wts_ref[...]
    moe = w[:, 0:1] * buf0[...] + w[:, 1:2] * buf1[...]
    g2 = mod_ref[0, 5:6, :]
    x2 = _layer_norm(alpha * x_ref[0] + g2 * moe, lng_ref[...], lnb_ref[...])
    x2_ref[0] = x2
    if next_mod:
        h_ref[0] = x2 * (1.0 + nmod_ref[0, 1:2, :]) + nmod_ref[0, 0:1, :]


def _combine(ys, pos0, pos1, wts, x, mods, ln_g, ln_b, alpha, next_mods=None):
    b, l, d = x.shape
    tm = min(l, 256)
    nt = l // tm
    tok = lambda i, j, *_: (i, j, 0)
    bat = lambda i, j, *_: (i, 0, 0)
    const2 = lambda i, j, *_: (0, 0)
    in_specs = [pl.BlockSpec((1, tm, d), tok),
                pl.BlockSpec((1, 6, d), bat),
                pl.BlockSpec((tm, 2), lambda i, j, *_: (i * nt + j, 0)),
                pl.BlockSpec((1, d), const2),
                pl.BlockSpec((1, d), const2),
                pl.BlockSpec(memory_space=pl.ANY)]
    args = [x, mods, wts, ln_g.reshape(1, d), ln_b.reshape(1, d), ys]
    out_shape = [jax.ShapeDtypeStruct((b, l, d), F32)]
    out_specs = [pl.BlockSpec((1, tm, d), tok)]
    if next_mods is not None:
        in_specs.append(pl.BlockSpec((1, 6, d), bat))
        args.append(next_mods)
        out_shape.append(jax.ShapeDtypeStruct((b, l, d), F32))
        out_specs.append(pl.BlockSpec((1, tm, d), tok))
    grid_spec = pltpu.PrefetchScalarGridSpec(
        num_scalar_prefetch=2,
        grid=(b, nt),
        in_specs=in_specs,
        out_specs=out_specs,
        scratch_shapes=[pltpu.VMEM((tm, d), F32), pltpu.VMEM((tm, d), F32), pltpu.SemaphoreType.DMA(())],
    )
    return pl.pallas_call(
        functools.partial(_combine_body, tm=tm, alpha=alpha, next_mod=next_mods is not None),
        grid_spec=grid_spec,
        out_shape=out_shape,
        compiler_params=_cparams("arbitrary", "arbitrary"),
        name="moe_combine",
    )(pos0, pos1, *args)


def _moe(h_rows, route, counts, w_gate, w_up, w_down, layer):
    n = h_rows.shape[0]
    tile = min(n, 512)
    pos0, pos1, tile_expert, n_used, last_tile_row, tiles, n_tiles = _dispatch_plan(route, counts, tile)
    hs = _dispatch(h_rows, pos0, pos1, last_tile_row, tiles, n_used, n_tiles * tile, tile)
    ys = _expert_ffn(hs, tile_expert, n_used, w_gate, w_up, w_down, layer, n_tiles, tile)
    return ys, pos0, pos1, jnp.transpose(route[2:4])


def _pool_group(h_ref, w_ref, sc_ref, o_ref, col_ref, k, n_rows):
    n = n_rows * GRID_W
    c = h_ref.shape[2]
    blk = 4 * GRID_W
    half = k // 2
    pad = half * GRID_W
    ti = lax.broadcasted_iota(I32, (blk, blk), 0)
    si = lax.broadcasted_iota(I32, (blk, blk), 1)
    shift = GRID_W.bit_length() - 1
    same_row = (ti >> shift) == (si >> shift)
    band = jnp.where(same_row & (si - ti >= -half) & (si - ti <= half - 1), 1.0, 0.0)
    col_ref[0:pad, :] = jnp.zeros((pad, c), F32)
    col_ref[pad + n:pad + n + pad, :] = jnp.zeros((pad, c), F32)
    for b0 in range(0, n, blk):
        col_ref[pad + b0:pad + b0 + blk, :] = jnp.dot(band, h_ref[0, b0:b0 + blk, :], precision=HI,
                                                      preferred_element_type=F32)
    acc = col_ref[0:n, :]
    for j in range(1, k):
        acc = acc + col_ref[j * GRID_W:j * GRID_W + n, :]
    t = lax.broadcasted_iota(I32, (n, 1), 0)
    wc = t & (GRID_W - 1)
    wr = t >> shift
    cnt_c = jnp.minimum(wc + half - 1, GRID_W - 1) - jnp.maximum(wc - half, 0) + 1
    cnt_r = jnp.minimum(wr + half - 1, n_rows - 1) - jnp.maximum(wr - half, 0) + 1
    mean = acc / (cnt_c * cnt_r).astype(F32)
    pooled = (mean - h_ref[0]).astype(BF16)
    o_ref[0] = jnp.dot(pooled, w_ref[0].astype(BF16), preferred_element_type=F32) * sc_ref[...]


def _pool_body(h_ref, w_ref, sc_ref, o_ref, col_ref, *, n_rows):
    g = pl.program_id(1)
    for gi, k in enumerate(POOL_WINDOWS):
        @pl.when(g == gi)
        def _(k=k):
            _pool_group(h_ref, w_ref, sc_ref, o_ref, col_ref, k, n_rows)


def _pool_mix(h, w_grp, scale):
    b, n, d = h.shape
    n_g, c, _ = w_grp.shape
    n_rows = n // GRID_W
    pad = (max(POOL_WINDOWS) // 2) * GRID_W
    return pl.pallas_call(
        functools.partial(_pool_body, n_rows=n_rows),
        grid=(b, n_g),
        in_specs=[pl.BlockSpec((1, n, c), lambda i, j: (i, 0, j)),
                  pl.BlockSpec((1, c, c), lambda i, j: (j, 0, 0)),
                  pl.BlockSpec((1, c), lambda i, j: (0, j))],
        out_specs=pl.BlockSpec((1, n, c), lambda i, j: (i, 0, j)),
        out_shape=jax.ShapeDtypeStruct((b, n, d), F32),
        scratch_shapes=[pltpu.VMEM((n + 2 * pad, c), F32)],
        compiler_params=_cparams("arbitrary", "arbitrary"),
        name="pool_mix",
    )(h, w_grp, scale.reshape(1, d))


def kernel(x, c, ctx, c_ctx, mod_w, mod_b, ln_g, ln_b, s5_lam_re, s5_lam_im, s5_log_dt, s5_b_re, s5_b_im,
           s5_c_re, s5_c_im, s5_d, s5_w_val, s5_w_gate, pool_w, pool_scale, router_w, router_b,
           moe_w_gate, moe_w_up, moe_w_down):
    b, l, d = x.shape
    depth = mod_w.shape[0]
    assert depth == 2 and b + 1 <= SUBLANES and d % LANES == 0 and GRID_W & (GRID_W - 1) == 0
    alpha = (2 * depth) ** 0.25

    cond = jnp.zeros((SUBLANES, d), F32).at[:b].set(c).at[b].set(c_ctx)
    mods = _modulation(cond, mod_w, mod_b).reshape(depth, SUBLANES, 6, d)

    toep, w1, w2t, a_r, a_i = _s5_weights(s5_lam_re[0], s5_lam_im[0], s5_log_dt[0], s5_b_re[0], s5_b_im[0],
                                          s5_c_re[0], s5_c_im[0], s5_d[0])
    y = _s5_mix(x, ctx, mods[0, :b], mods[0, b:b + 1], toep, w1, w2t, a_r, a_i)
    x1, h_rows, route, counts = _post_mixer(y, x, mods[0, :b], ln_g[0, 0], ln_b[0, 0], router_w, router_b,
                                            alpha, glu_w=(s5_w_val[0], s5_w_gate[0]))
    ys, pos0, pos1, wts = _moe(h_rows, route, counts, moe_w_gate, moe_w_up, moe_w_down, 0)
    x2, h = _combine(ys, pos0, pos1, wts, x1, mods[0, :b], ln_g[0, 1], ln_b[0, 1], alpha,
                     next_mods=mods[1, :b])

    m = _pool_mix(h, pool_w[0], pool_scale[0])
    x3, h_rows, route, counts = _post_mixer(m, x2, mods[1, :b], ln_g[1, 0], ln_b[1, 0], router_w, router_b,
                                            alpha)
    ys, pos0, pos1, wts = _moe(h_rows, route, counts, moe_w_gate, moe_w_up, moe_w_down, 1)
    (out,) = _combine(ys, pos0, pos1, wts, x3, mods[1, :b], ln_g[1, 1], ln_b[1, 1], alpha)
    return out
```

```python
import functools

import jax
import jax.numpy as jnp
from jax import lax
from jax.experimental import pallas as pl
from jax.experimental.pallas import tpu as pltpu

F32 = jnp.float32
BF16 = jnp.bfloat16
I32 = jnp.int32
HI = lax.Precision.HIGHEST

GRID_W = 64
S5_H = 16
S5_T = 16
POOL_WINDOWS = (2, 4, 8, 16)
N_EXPERT_GROUPS = 4
LN_EPS = 1e-5
LANES = 128
SUBLANES = 8
VMEM_LIMIT = 52 * 1024 * 1024

EXPERTS_PER_GROUP = 4
PAIRS_PER_GROUP = EXPERTS_PER_GROUP * (EXPERTS_PER_GROUP - 1) // 2
N_PAIR_CLASSES = N_EXPERT_GROUPS * PAIRS_PER_GROUP
ROUTE_ROWS = 8


def _cparams(*sem):
    return pltpu.CompilerParams(dimension_semantics=sem, vmem_limit_bytes=VMEM_LIMIT)


def _mod_body(c_ref, w_ref, b_ref, o_ref):
    c = c_ref[...]
    s = c * jax.nn.sigmoid(c)
    o_ref[0] = jnp.dot(s, w_ref[0], precision=HI, preferred_element_type=F32) + b_ref[0]


def _modulation(cond, mod_w, mod_b):
    depth, d, n6 = mod_w.shape
    tn = min(n6, 1536)
    return pl.pallas_call(
        _mod_body,
        grid=(depth, n6 // tn),
        in_specs=[pl.BlockSpec((SUBLANES, d), lambda i, j: (0, 0)),
                  pl.BlockSpec((1, d, tn), lambda i, j: (i, 0, j)),
                  pl.BlockSpec((1, 1, tn), lambda i, j: (i, 0, j))],
        out_specs=pl.BlockSpec((1, SUBLANES, tn), lambda i, j: (i, 0, j)),
        out_shape=jax.ShapeDtypeStruct((depth, SUBLANES, n6), F32),
        compiler_params=_cparams("arbitrary", "arbitrary"),
        name="modulation",
    )(cond, mod_w, mod_b.reshape(depth, 1, n6))


def _s5_direction_terms(lam_re, lam_im, log_dt, b_re, b_im):
    lr = lam_re.astype(F32)
    li = lam_im.astype(F32)
    dt = jnp.exp(log_dt.astype(F32))[:, None]
    mag = jnp.exp(lr * dt)
    ar = mag * jnp.cos(li * dt)
    ai = mag * jnp.sin(li * dt)
    den = lr * lr + li * li
    nr = ar - 1.0
    fr = (nr * lr + ai * li) / den
    fi = (ai * lr - nr * li) / den
    br_, bi_ = b_re.astype(F32), b_im.astype(F32)
    bbr = fr[..., None] * br_ - fi[..., None] * bi_
    bbi = fr[..., None] * bi_ + fi[..., None] * br_
    k = jnp.arange(S5_T + 1, dtype=F32)[:, None, None]
    pm = jnp.exp(k * (lr * dt))
    pr = pm * jnp.cos(k * (li * dt))
    pi = pm * jnp.sin(k * (li * dt))
    return pr, pi, bbr, bbi


def _s5_weights(lam_re, lam_im, log_dt, b_re, b_im, c_re, c_im, d_skip):
    t = S5_T
    g, p = lam_re.shape[1:]
    h = b_re.shape[-1]
    terms = [_s5_direction_terms(lam_re[d], lam_im[d], log_dt[d], b_re[d], b_im[d]) for d in (0, 1)]
    pw = jnp.stack([jnp.transpose(terms[d][k], (1, 0, 2)) for d in (0, 1) for k in (0, 1)], axis=1)
    bt = jnp.stack([jnp.transpose(terms[d][k], (0, 2, 1)) for d in (0, 1) for k in (2, 3)], axis=1)
    cc = jnp.stack([c[d].astype(F32) for d in (0, 1) for c in (c_re, c_im)], axis=1)
    gp = 8
    spec4 = lambda rows: pl.BlockSpec((gp, 4, rows, p), lambda i: (i, 0, 0, 0))
    wide = pl.BlockSpec((gp, t * h, t * h), lambda i: (i, 0, 0))
    toep, w1, w2t = pl.pallas_call(
        functools.partial(_s5_prep_body, gp=gp),
        grid=(g // gp,),
        in_specs=[spec4(t + 1), spec4(h), spec4(h), pl.BlockSpec((gp, 1, h), lambda i: (i, 0, 0))],
        out_specs=[wide, wide, wide],
        out_shape=[jax.ShapeDtypeStruct((g, t * h, t * h), BF16) for _ in range(3)],
        compiler_params=_cparams("arbitrary"),
        name="s5_prep",
    )(pw, bt, cc, d_skip.astype(F32).reshape(g, 1, h))
    a_r = jnp.concatenate([pw[:, 0, t], pw[:, 2, t]], axis=-1)[:, None, :]
    a_i = jnp.concatenate([pw[:, 1, t], pw[:, 3, t]], axis=-1)[:, None, :]
    return toep, w1, w2t, a_r, a_i


def _cmul(ar, ai, br, bi):
    return ar * br - ai * bi, ar * bi + ai * br


def _s5_prep_body(pw_ref, bt_ref, cc_ref, d_ref, toep_ref, w1_ref, w2t_ref, *, gp):
    t = pw_ref.shape[2] - 1
    h = bt_ref.shape[2]
    nt = (((1,), (1,)), ((), ()))
    eye = (lax.broadcasted_iota(I32, (h, h), 0) == lax.broadcasted_iota(I32, (h, h), 1))
    for g in range(gp):
        prf, pif, prb, pib = (pw_ref[g, k] for k in range(4))
        btf = (bt_ref[g, 0], bt_ref[g, 1])
        btb = (bt_ref[g, 2], bt_ref[g, 3])
        ccf = (cc_ref[g, 0], cc_ref[g, 1])
        ccb = (cc_ref[g, 2], cc_ref[g, 3])
        caf, cab = [], []
        for j in range(t):
            f_r, f_i = _cmul(*btf, prf[t - 1 - j:t - j], pif[t - 1 - j:t - j])
            b_r, b_i = _cmul(*btb, prb[j:j + 1], pib[j:j + 1])
            w1_ref[g, j * h:(j + 1) * h, :] = jnp.concatenate([f_r, b_r, f_i, b_i], axis=1).astype(BF16)
            mf_r, mf_i = _cmul(*ccf, prf[j + 1:j + 2], pif[j + 1:j + 2])
            mb_r, mb_i = _cmul(*ccb, prb[t - j:t - j + 1], pib[t - j:t - j + 1])
            w2t_ref[g, j * h:(j + 1) * h, :] = jnp.concatenate([mf_r, mb_r, -mf_i, -mb_i],
                                                               axis=1).astype(BF16)
            caf.append(_cmul(*ccf, prf[j:j + 1], pif[j:j + 1]))
            cab.append(_cmul(*ccb, prb[t - 1 - j:t - j], pib[t - 1 - j:t - j]))

        def lag_kernels(btx, ca):
            car = jnp.concatenate([c[0] for c in ca], axis=0)
            cai = jnp.concatenate([c[1] for c in ca], axis=0)
            return (lax.dot_general(btx[0], car, nt, precision=HI, preferred_element_type=F32)
                    - lax.dot_general(btx[1], cai, nt, precision=HI, preferred_element_type=F32))

        kf = lag_kernels(btf, caf)
        kb = lag_kernels(btb, cab)
        skip = jnp.where(eye, jnp.broadcast_to(d_ref[g], (h, h)), 0.0)
        mid = kb[:, (t - 1) * h:] + kf[:, :h] + skip
        kwide = jnp.concatenate([kb[:, :(t - 1) * h], mid, kf[:, h:], jnp.zeros((h, h), F32)], axis=1)
        for i in range(t):
            off = (t - 1 - i) * h
            toep_ref[g, i * h:(i + 1) * h, :] = kwide[:, off:off + t * h].astype(BF16)


def _granule_transpose(v):
    n = len(v)
    gran = lax.broadcasted_iota(I32, v[0].shape, 1) >> 4
    at = [gran == q for q in range(n)]
    rot = []
    for d in range(n):
        m = v[d]
        for q in range(1, n):
            m = jnp.where(at[q], v[(q + d) % n], m)
        rot.append(pltpu.roll(m, d * S5_H, 1) if d else m)
    out = []
    for q in range(n):
        w = rot[(-q) % n]
        for j in range(1, n):
            w = jnp.where(at[j], rot[(j - q) % n], w)
        out.append(w)
    return out


def _s5_body(x_ref, ctx_ref, mod_ref, cmod_ref, tt_ref, w1_ref, w2t_ref, ar_ref, ai_ref, y_ref,
             u_ref, s_ref, *, n_ctx_chunks, n_lat_chunks, gb):
    n_b = x_ref.shape[0]
    n_g = u_ref.shape[0]
    n_blocks = s_ref.shape[1] // SUBLANES
    half = SUBLANES // 2
    tok_blk = SUBLANES * S5_T
    lat_lo = n_ctx_chunks * n_b
    lat_rows = n_lat_chunks * n_b

    def chunk_rows(ref, b, tok0, shift, scale1):
        vs = [ref[b, pl.ds(tok0 + j, SUBLANES, stride=S5_T), :] * scale1 + shift for j in range(S5_T)]
        lo = _granule_transpose(vs[:SUBLANES])
        hi = _granule_transpose(vs[SUBLANES:])
        return lo, hi

    def put_rows(rows, chunk0, b):
        for q in range(n_g):
            for jh in range(2):
                u_ref[q, jh, pl.ds(chunk0 * n_b + b, SUBLANES, stride=n_b), :] = rows[jh][q]

    def u_rows(g, lo, n):
        return jnp.concatenate([u_ref[g, 0, lo:lo + n, :], u_ref[g, 1, lo:lo + n, :]], axis=1)

    shift = jnp.broadcast_to(cmod_ref[0, 0:1, :], (SUBLANES, LANES))
    scale1 = 1.0 + jnp.broadcast_to(cmod_ref[0, 1:2, :], (SUBLANES, LANES))
    for b in range(n_b):
        for cb in range(n_ctx_chunks // SUBLANES):
            rows = chunk_rows(ctx_ref, b, cb * tok_blk, shift, scale1)
            put_rows(rows, cb * SUBLANES, b)
            put_rows(rows, n_ctx_chunks + n_lat_chunks + cb * SUBLANES, b)

    def fill(cb, carry):
        for b in range(n_b):
            shift = jnp.broadcast_to(mod_ref[b, 0:1, :], (SUBLANES, LANES))
            scale1 = 1.0 + jnp.broadcast_to(mod_ref[b, 1:2, :], (SUBLANES, LANES))
            put_rows(chunk_rows(x_ref, b, cb * tok_blk, shift, scale1), n_ctx_chunks + cb * SUBLANES, b)
        return carry

    lax.fori_loop(0, n_lat_chunks // SUBLANES, fill, 0)

    for g in range(n_g):
        s_ref[g] = jnp.dot(u_rows(g, 0, n_blocks * SUBLANES).astype(BF16), w1_ref[g],
                           preferred_element_type=F32)

    lane = lax.broadcasted_iota(I32, (SUBLANES, LANES), 1)
    row = lax.broadcasted_iota(I32, (SUBLANES, LANES), 0)
    is_fwd = lane < LANES // 2
    is_fwd2 = jnp.concatenate([is_fwd, is_fwd], axis=1)
    top = row < half
    zero = jnp.zeros((SUBLANES, LANES), F32)
    for g0 in range(0, n_g, gb):
        ars = [jnp.broadcast_to(ar_ref[g0 + g], (SUBLANES, LANES)) for g in range(gb)]
        ais = [jnp.broadcast_to(ai_ref[g0 + g], (SUBLANES, LANES)) for g in range(gb)]

        def step(k, carry, g0=g0, ars=ars, ais=ais):
            fo = pl.multiple_of(k * SUBLANES, SUBLANES)
            bo = pl.multiple_of((n_blocks - 1 - k) * SUBLANES, SUBLANES)
            new = []
            for g in range(gb):
                xr, xi = carry[2 * g], carry[2 * g + 1]
                vf = s_ref[g0 + g, pl.ds(fo, SUBLANES), :]
                vb_raw = s_ref[g0 + g, pl.ds(bo, SUBLANES), :]
                vb = pltpu.roll(vb_raw, half, 0)
                vr = jnp.where(is_fwd, vf[:, :LANES], vb[:, :LANES])
                vi = jnp.where(is_fwd, vf[:, LANES:], vb[:, LANES:])
                ar, ai = ars[g], ais[g]
                yr = ar * xr - ai * xi + vr
                yi = ar * xi + ai * xr + vi
                yrr = pltpu.roll(yr, half, 0)
                yir = pltpu.roll(yi, half, 0)
                zr = ar * yrr - ai * yir + vr
                zi = ar * yir + ai * yrr + vi
                inc = jnp.concatenate([jnp.where(top, xr, yrr), jnp.where(top, xi, yir)], axis=1)
                s_ref[g0 + g, pl.ds(fo, SUBLANES), :] = jnp.where(is_fwd2, inc, vf)
                s_ref[g0 + g, pl.ds(bo, SUBLANES), :] = jnp.where(is_fwd2, vb_raw, pltpu.roll(inc, half, 0))
                new.append(jnp.where(top, pltpu.roll(zr, half, 0), zr))
                new.append(jnp.where(top, pltpu.roll(zi, half, 0), zi))
            return tuple(new)

        lax.fori_loop(0, (n_ctx_chunks + n_lat_chunks) // 2, step, tuple(zero for _ in range(2 * gb)))

    for g in range(n_g):
        y = (jnp.dot(u_rows(g, lat_lo, lat_rows).astype(BF16), tt_ref[g], preferred_element_type=F32)
             + lax.dot_general(s_ref[g, lat_lo:lat_lo + lat_rows, :].astype(BF16), w2t_ref[g],
                               (((1,), (1,)), ((), ())), preferred_element_type=F32))
        for jh in range(2):
            u_ref[g, jh, lat_lo:lat_lo + lat_rows, :] = y[:, jh * LANES:(jh + 1) * LANES]

    def emit(cb, carry):
        for b in range(n_b):
            for jh in range(S5_T // SUBLANES):
                w = [u_ref[q, jh, pl.ds((n_ctx_chunks + cb * SUBLANES) * n_b + b, SUBLANES, stride=n_b), :]
                     for q in range(n_g)]
                v = _granule_transpose(w)
                for j in range(SUBLANES):
                    y_ref[b, pl.ds(cb * tok_blk + jh * SUBLANES + j, SUBLANES, stride=S5_T), :] = v[j]
        return carry

    lax.fori_loop(0, n_lat_chunks // SUBLANES, emit, 0)


def _s5_mix(x, ctx, mods, cmods, toep, w1, w2t, a_r, a_i):
    b, l, d = x.shape
    n_ctx = ctx.shape[1]
    n_g = LANES // S5_H
    n_ctx_chunks, n_lat_chunks = n_ctx // S5_T, l // S5_T
    assert b * 2 == SUBLANES and n_ctx_chunks % SUBLANES == 0 and n_lat_chunks % SUBLANES == 0
    rows = (2 * n_ctx_chunks + n_lat_chunks) * b
    w = S5_T * S5_H
    body = functools.partial(_s5_body, n_ctx_chunks=n_ctx_chunks, n_lat_chunks=n_lat_chunks, gb=8)
    lane_tile = lambda i: (0, 0, i)
    wspec = pl.BlockSpec((n_g, w, w), lambda i: (i, 0, 0))
    aspec = pl.BlockSpec((n_g, 1, w // 2), lambda i: (i, 0, 0))
    return pl.pallas_call(
        body,
        grid=(d // LANES,),
        in_specs=[pl.BlockSpec((b, l, LANES), lane_tile, pipeline_mode=pl.Buffered(1)),
                  pl.BlockSpec((b, n_ctx, LANES), lane_tile),
                  pl.BlockSpec((b, 6, LANES), lane_tile),
                  pl.BlockSpec((1, 6, LANES), lane_tile),
                  wspec, wspec, wspec, aspec, aspec],
        out_specs=pl.BlockSpec((b, l, LANES), lane_tile, pipeline_mode=pl.Buffered(1)),
        out_shape=jax.ShapeDtypeStruct((b, l, d), F32),
        scratch_shapes=[pltpu.VMEM((n_g, w // LANES, rows, LANES), F32), pltpu.VMEM((n_g, rows, w), F32)],
        compiler_params=_cparams("arbitrary"),
        name="s5_mix",
    )(x, ctx, mods, cmods, toep, w1, w2t, a_r, a_i)


def _layer_norm(r, g, b):
    mu = jnp.mean(r, axis=-1, keepdims=True)
    xc = r - mu
    var = jnp.mean(xc * xc, axis=-1, keepdims=True)
    return xc * lax.rsqrt(var + LN_EPS) * g + b


def _max2_of4(a, b, c, d):
    h1, l1 = jnp.maximum(a, b), jnp.minimum(a, b)
    h2, l2 = jnp.maximum(c, d), jnp.minimum(c, d)
    return jnp.maximum(h1, h2) + jnp.maximum(jnp.minimum(h1, h2), jnp.maximum(l1, l2))


def _argmax_first(vals):
    idx = jnp.zeros(vals[0].shape, I32)
    best = vals[0]
    for j in range(1, len(vals)):
        upd = vals[j] > best
        idx = jnp.where(upd, j, idx)
        best = jnp.where(upd, vals[j], best)
    return idx, best


def _route(logits_t, count_ref, route_ref):
    n_e, tm = logits_t.shape
    per = n_e // N_EXPERT_GROUPS
    mx = jnp.max(logits_t, axis=0, keepdims=True)
    ex = jnp.exp(logits_t - mx)
    sc = ex / jnp.sum(ex, axis=0, keepdims=True)
    rows = [sc[e:e + 1, :] for e in range(n_e)]
    gscore = [_max2_of4(*rows[per * g:per * (g + 1)]) for g in range(N_EXPERT_GROUPS)]
    best, _ = _argmax_first(gscore)
    vals = []
    for j in range(per):
        v = rows[per * (N_EXPERT_GROUPS - 1) + j]
        for g in range(N_EXPERT_GROUPS - 2, -1, -1):
            v = jnp.where(best == g, rows[per * g + j], v)
        vals.append(v)
    i1, m1 = _argmax_first(vals)
    i2, m2 = _argmax_first([jnp.where(i1 == j, -1.0, vals[j]) for j in range(per)])
    den = m1 + m2
    first_lo = i1 < i2
    lo = jnp.minimum(i1, i2)
    hi = jnp.maximum(i1, i2)
    pair = jnp.where(lo == 0, 0, jnp.where(lo == 1, per - 1, 2 * per - 3)) + hi - lo - 1
    cls = best * PAIRS_PER_GROUP + pair
    w_lo = jnp.where(first_lo, m1, m2) / den
    w_hi = jnp.where(first_lo, m2, m1) / den

    n_cls = count_ref.shape[0]
    hit = lax.broadcasted_iota(I32, (n_cls, tm), 0) == cls
    onehot = jnp.where(hit, 1.0, 0.0)
    src = lax.broadcasted_iota(I32, (tm, tm), 0)
    dst = lax.broadcasted_iota(I32, (tm, tm), 1)
    tri = jnp.where(src <= dst, 1.0, 0.0).astype(BF16)
    cum = jnp.dot(onehot.astype(BF16), tri, preferred_element_type=F32)
    excl = cum - onehot + count_ref[:, 0:1]
    rank = jnp.sum(jnp.where(hit, excl, 0.0), axis=0, keepdims=True)
    count_ref[...] = count_ref[...] + jnp.sum(onehot, axis=1, keepdims=True)

    zero = jnp.zeros((1, tm), F32)
    route_ref[...] = jnp.concatenate([cls.astype(F32), rank, w_lo, w_hi, zero, zero, zero, zero], axis=0)


def _post_mixer_body(m_ref, x_ref, mod_ref, lng_ref, lnb_ref, rwt_ref, rb_ref, *rest, glu, alpha):
    if glu:
        wv_ref, wg_ref, x1_ref, h_ref, route_ref, cnt_out_ref, cnt_ref = rest
        a = jax.nn.gelu(m_ref[0], approximate=True).astype(BF16)
        val = jnp.dot(a, wv_ref[...], preferred_element_type=F32)
        gate = jnp.dot(a, wg_ref[...], preferred_element_type=F32)
        m = val * jax.nn.sigmoid(gate)
    else:
        x1_ref, h_ref, route_ref, cnt_out_ref, cnt_ref = rest
        m = m_ref[0]

    @pl.when((pl.program_id(0) == 0) & (pl.program_id(1) == 0))
    def _():
        cnt_ref[...] = jnp.zeros_like(cnt_ref)

    g1 = mod_ref[0, 2:3, :]
    sh2 = mod_ref[0, 3:4, :]
    sc2 = mod_ref[0, 4:5, :]
    x1 = _layer_norm(alpha * x_ref[0] + g1 * m, lng_ref[...], lnb_ref[...])
    x1_ref[0] = x1
    h = x1 * (1.0 + sc2) + sh2
    h_ref[...] = h
    logits_t = lax.dot_general(rwt_ref[...], h, (((1,), (1,)), ((), ())),
                               precision=HI, preferred_element_type=F32) + rb_ref[...]
    _route(logits_t, cnt_ref, route_ref)
    cnt_out_ref[...] = cnt_ref[...]


def _post_mixer(m, x, mods, ln_g, ln_b, router_w, router_b, alpha, glu_w=None):
    b, l, d = x.shape
    n_e = router_w.shape[1]
    assert n_e == N_EXPERT_GROUPS * EXPERTS_PER_GROUP
    tm = min(l, 512)
    nt = l // tm
    tok = lambda i, j: (i, j, 0)
    const2 = lambda i, j: (0, 0)
    in_specs = [pl.BlockSpec((1, tm, d), tok),
                pl.BlockSpec((1, tm, d), tok),
                pl.BlockSpec((1, 6, d), lambda i, j: (i, 0, 0)),
                pl.BlockSpec((1, d), const2),
                pl.BlockSpec((1, d), const2),
                pl.BlockSpec((n_e, d), const2),
                pl.BlockSpec((n_e, 1), const2)]
    args = [m, x, mods, ln_g.reshape(1, d), ln_b.reshape(1, d), router_w.T, router_b.reshape(n_e, 1)]
    if glu_w is not None:
        in_specs += [pl.BlockSpec((d, d), const2), pl.BlockSpec((d, d), const2)]
        args += [glu_w[0].astype(BF16), glu_w[1].astype(BF16)]
    out_shape = [jax.ShapeDtypeStruct((b, l, d), F32),
                 jax.ShapeDtypeStruct((b * l, d), F32),
                 jax.ShapeDtypeStruct((ROUTE_ROWS, b * l), F32),
                 jax.ShapeDtypeStruct((N_PAIR_CLASSES, LANES), F32)]
    out_specs = [pl.BlockSpec((1, tm, d), tok),
                 pl.BlockSpec((tm, d), lambda i, j: (i * nt + j, 0)),
                 pl.BlockSpec((ROUTE_ROWS, tm), lambda i, j: (0, i * nt + j)),
                 pl.BlockSpec((N_PAIR_CLASSES, LANES), const2)]
    return pl.pallas_call(
        functools.partial(_post_mixer_body, glu=glu_w is not None, alpha=alpha),
        grid=(b, nt),
        in_specs=in_specs,
        out_specs=out_specs,
        out_shape=out_shape,
        scratch_shapes=[pltpu.VMEM((N_PAIR_CLASSES, LANES), F32)],
        compiler_params=_cparams("arbitrary", "arbitrary"),
        name="post_mixer_glu" if glu_w is not None else "post_mixer",
    )(*args)


def _take(table, idx):
    ids = jnp.arange(table.shape[0], dtype=I32)
    return jnp.sum(jnp.where(idx[:, None] == ids[None, :], table[None, :], 0), axis=1)


def _pass_segments():
    seg_cls, seg_hi = [], []
    pairs = [(a, b) for a in range(EXPERTS_PER_GROUP) for b in range(a + 1, EXPERTS_PER_GROUP)]
    for g in range(N_EXPERT_GROUPS):
        for m in range(EXPERTS_PER_GROUP):
            for idx, (a, b) in enumerate(pairs):
                if m in (a, b):
                    seg_cls.append(g * PAIRS_PER_GROUP + idx)
                    seg_hi.append(int(m == b))
    return seg_cls, seg_hi


def _dispatch_plan(route, counts, tile):
    n_cls = counts.shape[0]
    n_tok = route.shape[1]
    cnt = counts[:, 0].astype(I32)
    tiles = (cnt + tile - 1) // tile
    tile_end = jnp.cumsum(tiles)
    tile_off = tile_end - tiles
    cids = jnp.arange(n_cls, dtype=I32)[:, None]
    cls = route[0].astype(I32)
    pos = jnp.sum(jnp.where(cls[None, :] == cids, (tile_off * tile)[:, None], 0), axis=0) + route[1].astype(I32)

    n_row_tiles = n_tok // tile + n_cls
    seg_cls, seg_hi = _pass_segments()
    seg_cls = jnp.asarray(seg_cls, I32)
    seg_hi = jnp.asarray(seg_hi, I32)
    seg_per_expert = EXPERTS_PER_GROUP - 1
    seg_tiles = _take(tiles, seg_cls)
    seg_end = jnp.cumsum(seg_tiles)
    n_used = seg_end[-1]
    p = jnp.arange(2 * n_row_tiles, dtype=I32)
    seg = jnp.minimum(jnp.sum((seg_end[None, :] <= p[:, None]).astype(I32), axis=1), seg_cls.shape[0] - 1)
    within = p - _take(seg_end - seg_tiles, seg)
    spare = p - n_used
    used = p < n_used
    pass_tile = jnp.where(used, _take(tile_off, _take(seg_cls, seg)) + within, tile_end[-1] + spare // 2).astype(I32)
    pass_hi = jnp.where(used, _take(seg_hi, seg), spare % 2).astype(I32)
    pass_expert = jnp.where(used, seg // seg_per_expert, N_EXPERT_GROUPS * EXPERTS_PER_GROUP - 1).astype(I32)
    last_tile = (tile_end - 1).astype(I32)
    return (pos.astype(I32), pass_tile, pass_hi, pass_expert, n_used.reshape(1).astype(I32),
            last_tile, tiles.astype(I32), tile_end[-1:].astype(I32), n_row_tiles)


def _zero_tiles(last_ref, tiles_ref, nu_ref, zero_ref, hs_ref, sem, n_e, tile, wait):
    for e in range(n_e):
        @pl.when(tiles_ref[e] > 0)
        def _(e=e):
            start = pl.multiple_of(last_ref[e] * tile, tile)
            cp = pltpu.make_async_copy(zero_ref, hs_ref.at[pl.ds(start, tile)], sem)
            if wait:
                cp.wait()
            else:
                cp.start()

    def body(j, carry):
        start = pl.multiple_of(j * tile, tile)
        cp = pltpu.make_async_copy(zero_ref, hs_ref.at[pl.ds(start, tile)], sem)
        if wait:
            cp.wait()
        else:
            cp.start()
        return carry

    lax.fori_loop(nu_ref[0], hs_ref.shape[0] // tile, body, 0)


def _dispatch_body(pos_ref, last_ref, tiles_ref, nu_ref, h_ref, hs_ref, zero_ref, sem_z, sem, *,
                   tm, tile, n_e):
    i = pl.program_id(0)

    @pl.when(i == 0)
    def _():
        zero_ref[...] = jnp.zeros_like(zero_ref)
        _zero_tiles(last_ref, tiles_ref, nu_ref, zero_ref, hs_ref, sem_z, n_e, tile, False)
        _zero_tiles(last_ref, tiles_ref, nu_ref, zero_ref, hs_ref, sem_z, n_e, tile, True)

    base = i * tm

    def issue(r2, carry):
        for k in range(2):
            r = 2 * r2 + k
            pltpu.make_async_copy(h_ref.at[pl.ds(r, 1)], hs_ref.at[pl.ds(pos_ref[base + r], 1)],
                                  sem).start(priority=k)
        return carry

    lax.fori_loop(0, tm // 2, issue, 0, unroll=8)
    pltpu.make_async_copy(h_ref, hs_ref.at[pl.ds(0, tm)], sem).wait()


def _dispatch(h_rows, pos, last_tile, tiles, n_used, n_rows, tile):
    n, d = h_rows.shape
    n_e = tiles.shape[0]
    tm = min(n, 1024)
    grid_spec = pltpu.PrefetchScalarGridSpec(
        num_scalar_prefetch=4,
        grid=(n // tm,),
        in_specs=[pl.BlockSpec((tm, d), lambda i, *_: (i, 0))],
        out_specs=pl.BlockSpec(memory_space=pl.ANY),
        scratch_shapes=[pltpu.VMEM((tile, d), F32),
                        pltpu.SemaphoreType.DMA(()), pltpu.SemaphoreType.DMA(())],
    )
    return pl.pallas_call(
        functools.partial(_dispatch_body, tm=tm, tile=tile, n_e=n_e),
        grid_spec=grid_spec,
        out_shape=jax.ShapeDtypeStruct((n_rows, d), F32),
        compiler_params=_cparams("arbitrary"),
        name="moe_dispatch",
    )(pos, last_tile, tiles, n_used, h_rows)


def _expert_body(pt_ref, ph_ref, te_ref, nu_ref, hs_ref, wg_ref, wu_ref, wd_ref, ys_ref, wgb, wub, wdb):
    i = pl.program_id(0)
    first = jnp.logical_or(i == 0, te_ref[i] != te_ref[jnp.maximum(i - 1, 0)])

    @pl.when(jnp.logical_and(i < nu_ref[0], first))
    def _():
        wgb[...] = wg_ref[0, 0].astype(BF16)
        wub[...] = wu_ref[0, 0].astype(BF16)
        wdb[...] = wd_ref[0, 0].astype(BF16)

    @pl.when(i < nu_ref[0])
    def _():
        x = hs_ref[...].astype(BF16)
        gate = jnp.dot(x, wgb[...], preferred_element_type=F32)
        up = jnp.dot(x, wub[...], preferred_element_type=F32)
        a = (gate * jax.nn.sigmoid(gate) * up).astype(BF16)
        ys_ref[...] = jnp.dot(a, wdb[...], preferred_element_type=F32)

    @pl.when(i >= nu_ref[0])
    def _():
        ys_ref[...] = jnp.zeros_like(ys_ref)


def _expert_ffn(hs, pass_tile, pass_hi, pass_expert, n_used, w_gate, w_up, w_down, layer, tile):
    _, n_e, d, f = w_gate.shape
    n_rows = hs.shape[0]
    wmap = lambda i, pt, ph, te, nu: (layer, te[i], 0, 0)
    grid_spec = pltpu.PrefetchScalarGridSpec(
        num_scalar_prefetch=4,
        grid=(pass_tile.shape[0],),
        in_specs=[pl.BlockSpec((tile, d), lambda i, pt, ph, te, nu: (pt[i], 0)),
                  pl.BlockSpec((1, 1, d, f), wmap),
                  pl.BlockSpec((1, 1, d, f), wmap),
                  pl.BlockSpec((1, 1, f, d), wmap)],
        out_specs=pl.BlockSpec((tile, d), lambda i, pt, ph, te, nu: (pt[i], ph[i])),
        scratch_shapes=[pltpu.VMEM((d, f), BF16), pltpu.VMEM((d, f), BF16), pltpu.VMEM((f, d), BF16)],
    )
    return pl.pallas_call(
        _expert_body,
        grid_spec=grid_spec,
        out_shape=jax.ShapeDtypeStruct((n_rows, 2 * d), F32),
        compiler_params=_cparams("arbitrary"),
        name="moe_experts",
    )(pass_tile, pass_hi, pass_expert, n_used, hs, w_gate, w_up, w_down)


def _combine_body(pos_ref, x_ref, mod_ref, wts_ref, lng_ref, lnb_ref, ys_ref, *rest,
                  tm, alpha, next_mod):
    if next_mod:
        nmod_ref, x2_ref, h_ref, buf, sem = rest
    else:
        x2_ref, buf, sem = rest
    step = pl.program_id(0) * pl.num_programs(1) + pl.program_id(1)
    n_steps = pl.num_programs(0) * pl.num_programs(1)
    d = x_ref.shape[2]

    def gather(s, slot):
        def issue(r2, carry):
            for k in range(2):
                r = 2 * r2 + k
                pltpu.make_async_copy(ys_ref.at[pl.ds(pos_ref[s * tm + r], 1)],
                                      buf.at[slot, pl.ds(r, 1)], sem.at[slot]).start(priority=k)
            return carry

        lax.fori_loop(0, tm // 2, issue, 0, unroll=8)

    @pl.when(step == 0)
    def _():
        gather(0, 0)

    @pl.when(step + 1 < n_steps)
    def _():
        gather(step + 1, (step + 1) % 2)

    slot = step % 2
    pltpu.make_async_copy(ys_ref.at[pl.ds(0, tm)], buf.at[slot], sem.at[slot]).wait()

    w = wts_ref[...]
    rows = buf[slot]
    moe = w[:, 0:1] * rows[:, :d] + w[:, 1:2] * rows[:, d:]
    g2 = mod_ref[0, 5:6, :]
    x2 = _layer_norm(alpha * x_ref[0] + g2 * moe, lng_ref[...], lnb_ref[...])
    x2_ref[0] = x2
    if next_mod:
        h_ref[0] = x2 * (1.0 + nmod_ref[0, 1:2, :]) + nmod_ref[0, 0:1, :]


def _combine(ys, pos, wts, x, mods, ln_g, ln_b, alpha, next_mods=None):
    b, l, d = x.shape
    tm = min(l, 256)
    nt = l // tm
    tok = lambda i, j, *_: (i, j, 0)
    bat = lambda i, j, *_: (i, 0, 0)
    const2 = lambda i, j, *_: (0, 0)
    in_specs = [pl.BlockSpec((1, tm, d), tok),
                pl.BlockSpec((1, 6, d), bat),
                pl.BlockSpec((tm, 2), lambda i, j, *_: (i * nt + j, 0)),
                pl.BlockSpec((1, d), const2),
                pl.BlockSpec((1, d), const2),
                pl.BlockSpec(memory_space=pl.ANY)]
    args = [x, mods, wts, ln_g.reshape(1, d), ln_b.reshape(1, d), ys]
    out_shape = [jax.ShapeDtypeStruct((b, l, d), F32)]
    out_specs = [pl.BlockSpec((1, tm, d), tok)]
    if next_mods is not None:
        in_specs.append(pl.BlockSpec((1, 6, d), bat))
        args.append(next_mods)
        out_shape.append(jax.ShapeDtypeStruct((b, l, d), F32))
        out_specs.append(pl.BlockSpec((1, tm, d), tok))
    grid_spec = pltpu.PrefetchScalarGridSpec(
        num_scalar_prefetch=1,
        grid=(b, nt),
        in_specs=in_specs,
        out_specs=out_specs,
        scratch_shapes=[pltpu.VMEM((2, tm, 2 * d), F32), pltpu.SemaphoreType.DMA((2,))],
    )
    return pl.pallas_call(
        functools.partial(_combine_body, tm=tm, alpha=alpha, next_mod=next_mods is not None),
        grid_spec=grid_spec,
        out_shape=out_shape,
        compiler_params=_cparams("arbitrary", "arbitrary"),
        name="moe_combine",
    )(pos, *args)


MOE_TILE = 256


def _moe(h_rows, route, counts, w_gate, w_up, w_down, layer):
    n = h_rows.shape[0]
    tile = min(n, MOE_TILE)
    (pos, pass_tile, pass_hi, pass_expert, n_used, last_tile, tiles, n_used_tiles,
     n_row_tiles) = _dispatch_plan(route, counts, tile)
    hs = _dispatch(h_rows, pos, last_tile, tiles, n_used_tiles, n_row_tiles * tile, tile)
    ys = _expert_ffn(hs, pass_tile, pass_hi, pass_expert, n_used, w_gate, w_up, w_down, layer, tile)
    return ys, pos, jnp.transpose(route[2:4])


def _pool_group(h_ref, w_ref, sc_ref, o_ref, col_ref, k, n_rows):
    n = n_rows * GRID_W
    c = h_ref.shape[2]
    blk = 4 * GRID_W
    half = k // 2
    pad = half * GRID_W
    ti = lax.broadcasted_iota(I32, (blk, blk), 0)
    si = lax.broadcasted_iota(I32, (blk, blk), 1)
    shift = GRID_W.bit_length() - 1
    same_row = (ti >> shift) == (si >> shift)
    band = jnp.where(same_row & (si - ti >= -half) & (si - ti <= half - 1), 1.0, 0.0)
    col_ref[0:pad, :] = jnp.zeros((pad, c), F32)
    col_ref[pad + n:pad + n + pad, :] = jnp.zeros((pad, c), F32)
    for b0 in range(0, n, blk):
        col_ref[pad + b0:pad + b0 + blk, :] = jnp.dot(band, h_ref[0, b0:b0 + blk, :], precision=HI,
                                                      preferred_element_type=F32)
    acc = col_ref[0:n, :]
    for j in range(1, k):
        acc = acc + col_ref[j * GRID_W:j * GRID_W + n, :]
    t = lax.broadcasted_iota(I32, (n, 1), 0)
    wc = t & (GRID_W - 1)
    wr = t >> shift
    cnt_c = jnp.minimum(wc + half - 1, GRID_W - 1) - jnp.maximum(wc - half, 0) + 1
    cnt_r = jnp.minimum(wr + half - 1, n_rows - 1) - jnp.maximum(wr - half, 0) + 1
    mean = acc / (cnt_c * cnt_r).astype(F32)
    pooled = (mean - h_ref[0]).astype(BF16)
    o_ref[0] = jnp.dot(pooled, w_ref[0].astype(BF16), preferred_element_type=F32) * sc_ref[...]


def _pool_body(h_ref, w_ref, sc_ref, o_ref, col_ref, *, n_rows):
    g = pl.program_id(1)
    for gi, k in enumerate(POOL_WINDOWS):
        @pl.when(g == gi)
        def _(k=k):
            _pool_group(h_ref, w_ref, sc_ref, o_ref, col_ref, k, n_rows)


def _pool_mix(h, w_grp, scale):
    b, n, d = h.shape
    n_g, c, _ = w_grp.shape
    n_rows = n // GRID_W
    pad = (max(POOL_WINDOWS) // 2) * GRID_W
    return pl.pallas_call(
        functools.partial(_pool_body, n_rows=n_rows),
        grid=(b, n_g),
        in_specs=[pl.BlockSpec((1, n, c), lambda i, j: (i, 0, j)),
                  pl.BlockSpec((1, c, c), lambda i, j: (j, 0, 0)),
                  pl.BlockSpec((1, c), lambda i, j: (0, j))],
        out_specs=pl.BlockSpec((1, n, c), lambda i, j: (i, 0, j)),
        out_shape=jax.ShapeDtypeStruct((b, n, d), F32),
        scratch_shapes=[pltpu.VMEM((n + 2 * pad, c), F32)],
        compiler_params=_cparams("arbitrary", "arbitrary"),
        name="pool_mix",
    )(h, w_grp, scale.reshape(1, d))


def kernel(x, c, ctx, c_ctx, mod_w, mod_b, ln_g, ln_b, s5_lam_re, s5_lam_im, s5_log_dt, s5_b_re, s5_b_im,
           s5_c_re, s5_c_im, s5_d, s5_w_val, s5_w_gate, pool_w, pool_scale, router_w, router_b,
           moe_w_gate, moe_w_up, moe_w_down):
    b, l, d = x.shape
    depth = mod_w.shape[0]
    assert depth == 2 and b + 1 <= SUBLANES and d % LANES == 0 and GRID_W & (GRID_W - 1) == 0
    alpha = (2 * depth) ** 0.25

    cond = jnp.zeros((SUBLANES, d), F32).at[:b].set(c).at[b].set(c_ctx)
    mods = _modulation(cond, mod_w, mod_b).reshape(depth, SUBLANES, 6, d)

    toep, w1, w2t, a_r, a_i = _s5_weights(s5_lam_re[0], s5_lam_im[0], s5_log_dt[0], s5_b_re[0], s5_b_im[0],
                                          s5_c_re[0], s5_c_im[0], s5_d[0])
    y = _s5_mix(x, ctx, mods[0, :b], mods[0, b:b + 1], toep, w1, w2t, a_r, a_i)
    x1, h_rows, route, counts = _post_mixer(y, x, mods[0, :b], ln_g[0, 0], ln_b[0, 0], router_w, router_b,
                                            alpha, glu_w=(s5_w_val[0], s5_w_gate[0]))
    ys, pos, wts = _moe(h_rows, route, counts, moe_w_gate, moe_w_up, moe_w_down, 0)
    x2, h = _combine(ys, pos, wts, x1, mods[0, :b], ln_g[0, 1], ln_b[0, 1], alpha, next_mods=mods[1, :b])

    m = _pool_mix(h, pool_w[0], pool_scale[0])
    x3, h_rows, route, counts = _post_mixer(m, x2, mods[1, :b], ln_g[1, 0], ln_b[1, 0], router_w, router_b,
                                            alpha)
    ys, pos, wts = _moe(h_rows, route, counts, moe_w_gate, moe_w_up, moe_w_down, 1)
    (out,) = _combine(ys, pos, wts, x3, mods[1, :b], ln_g[1, 1], ln_b[1, 1], alpha)
    return out
```

```python
import functools

import jax
import jax.numpy as jnp
from jax import lax
from jax.experimental import pallas as pl
from jax.experimental.pallas import tpu as pltpu

F32 = jnp.float32
BF16 = jnp.bfloat16
I32 = jnp.int32
HI = lax.Precision.HIGHEST

GRID_W = 64
S5_H = 16
S5_T = 16
POOL_WINDOWS = (2, 4, 8, 16)
N_EXPERT_GROUPS = 4
LN_EPS = 1e-5
LANES = 128
SUBLANES = 8
VMEM_LIMIT = 52 * 1024 * 1024

EXPERTS_PER_GROUP = 4
PAIRS_PER_GROUP = EXPERTS_PER_GROUP * (EXPERTS_PER_GROUP - 1) // 2
N_PAIR_CLASSES = N_EXPERT_GROUPS * PAIRS_PER_GROUP
ROUTE_ROWS = 8


def _cparams(*sem):
    return pltpu.CompilerParams(dimension_semantics=sem, vmem_limit_bytes=VMEM_LIMIT)


def _mod_body(c_ref, w_ref, b_ref, o_ref):
    c = c_ref[...]
    s = c * jax.nn.sigmoid(c)
    o_ref[0] = jnp.dot(s, w_ref[0], precision=HI, preferred_element_type=F32) + b_ref[0]


def _modulation(cond, mod_w, mod_b):
    depth, d, n6 = mod_w.shape
    tn = min(n6, 1536)
    return pl.pallas_call(
        _mod_body,
        grid=(depth, n6 // tn),
        in_specs=[pl.BlockSpec((SUBLANES, d), lambda i, j: (0, 0)),
                  pl.BlockSpec((1, d, tn), lambda i, j: (i, 0, j)),
                  pl.BlockSpec((1, 1, tn), lambda i, j: (i, 0, j))],
        out_specs=pl.BlockSpec((1, SUBLANES, tn), lambda i, j: (i, 0, j)),
        out_shape=jax.ShapeDtypeStruct((depth, SUBLANES, n6), F32),
        compiler_params=_cparams("arbitrary", "arbitrary"),
        name="modulation",
    )(cond, mod_w, mod_b.reshape(depth, 1, n6))


def _s5_direction_terms(lam_re, lam_im, log_dt, b_re, b_im):
    lr = lam_re.astype(F32)
    li = lam_im.astype(F32)
    dt = jnp.exp(log_dt.astype(F32))[:, None]
    mag = jnp.exp(lr * dt)
    ar = mag * jnp.cos(li * dt)
    ai = mag * jnp.sin(li * dt)
    den = lr * lr + li * li
    nr = ar - 1.0
    fr = (nr * lr + ai * li) / den
    fi = (ai * lr - nr * li) / den
    br_, bi_ = b_re.astype(F32), b_im.astype(F32)
    bbr = fr[..., None] * br_ - fi[..., None] * bi_
    bbi = fr[..., None] * bi_ + fi[..., None] * br_
    k = jnp.arange(S5_T + 1, dtype=F32)[:, None, None]
    pm = jnp.exp(k * (lr * dt))
    pr = pm * jnp.cos(k * (li * dt))
    pi = pm * jnp.sin(k * (li * dt))
    return pr, pi, bbr, bbi


def _s5_weights(lam_re, lam_im, log_dt, b_re, b_im, c_re, c_im, d_skip):
    t = S5_T
    g, p = lam_re.shape[1:]
    h = b_re.shape[-1]
    terms = [_s5_direction_terms(lam_re[d], lam_im[d], log_dt[d], b_re[d], b_im[d]) for d in (0, 1)]
    pw = jnp.stack([jnp.transpose(terms[d][k], (1, 0, 2)) for d in (0, 1) for k in (0, 1)], axis=1)
    bt = jnp.stack([jnp.transpose(terms[d][k], (0, 2, 1)) for d in (0, 1) for k in (2, 3)], axis=1)
    cc = jnp.stack([c[d].astype(F32) for d in (0, 1) for c in (c_re, c_im)], axis=1)
    gp = 8
    spec4 = lambda rows: pl.BlockSpec((gp, 4, rows, p), lambda i: (i, 0, 0, 0))
    wide = pl.BlockSpec((gp, t * h, t * h), lambda i: (i, 0, 0))
    toep, w1, w2t = pl.pallas_call(
        functools.partial(_s5_prep_body, gp=gp),
        grid=(g // gp,),
        in_specs=[spec4(t + 1), spec4(h), spec4(h), pl.BlockSpec((gp, 1, h), lambda i: (i, 0, 0))],
        out_specs=[wide, wide, wide],
        out_shape=[jax.ShapeDtypeStruct((g, t * h, t * h), BF16) for _ in range(3)],
        compiler_params=_cparams("arbitrary"),
        name="s5_prep",
    )(pw, bt, cc, d_skip.astype(F32).reshape(g, 1, h))
    a_r = jnp.concatenate([pw[:, 0, t], pw[:, 2, t]], axis=-1)[:, None, :]
    a_i = jnp.concatenate([pw[:, 1, t], pw[:, 3, t]], axis=-1)[:, None, :]
    return toep, w1, w2t, a_r, a_i


def _cmul(ar, ai, br, bi):
    return ar * br - ai * bi, ar * bi + ai * br


def _s5_prep_body(pw_ref, bt_ref, cc_ref, d_ref, toep_ref, w1_ref, w2t_ref, *, gp):
    t = pw_ref.shape[2] - 1
    h = bt_ref.shape[2]
    nt = (((1,), (1,)), ((), ()))
    eye = (lax.broadcasted_iota(I32, (h, h), 0) == lax.broadcasted_iota(I32, (h, h), 1))
    for g in range(gp):
        prf, pif, prb, pib = (pw_ref[g, k] for k in range(4))
        btf = (bt_ref[g, 0], bt_ref[g, 1])
        btb = (bt_ref[g, 2], bt_ref[g, 3])
        ccf = (cc_ref[g, 0], cc_ref[g, 1])
        ccb = (cc_ref[g, 2], cc_ref[g, 3])
        caf, cab = [], []
        for j in range(t):
            f_r, f_i = _cmul(*btf, prf[t - 1 - j:t - j], pif[t - 1 - j:t - j])
            b_r, b_i = _cmul(*btb, prb[j:j + 1], pib[j:j + 1])
            w1_ref[g, j * h:(j + 1) * h, :] = jnp.concatenate([f_r, b_r, f_i, b_i], axis=1).astype(BF16)
            mf_r, mf_i = _cmul(*ccf, prf[j + 1:j + 2], pif[j + 1:j + 2])
            mb_r, mb_i = _cmul(*ccb, prb[t - j:t - j + 1], pib[t - j:t - j + 1])
            w2t_ref[g, j * h:(j + 1) * h, :] = jnp.concatenate([mf_r, mb_r, -mf_i, -mb_i],
                                                               axis=1).astype(BF16)
            caf.append(_cmul(*ccf, prf[j:j + 1], pif[j:j + 1]))
            cab.append(_cmul(*ccb, prb[t - 1 - j:t - j], pib[t - 1 - j:t - j]))

        def lag_kernels(btx, ca):
            car = jnp.concatenate([c[0] for c in ca], axis=0)
            cai = jnp.concatenate([c[1] for c in ca], axis=0)
            return (lax.dot_general(btx[0], car, nt, precision=HI, preferred_element_type=F32)
                    - lax.dot_general(btx[1], cai, nt, precision=HI, preferred_element_type=F32))

        kf = lag_kernels(btf, caf)
        kb = lag_kernels(btb, cab)
        skip = jnp.where(eye, jnp.broadcast_to(d_ref[g], (h, h)), 0.0)
        mid = kb[:, (t - 1) * h:] + kf[:, :h] + skip
        kwide = jnp.concatenate([kb[:, :(t - 1) * h], mid, kf[:, h:], jnp.zeros((h, h), F32)], axis=1)
        for i in range(t):
            off = (t - 1 - i) * h
            toep_ref[g, i * h:(i + 1) * h, :] = kwide[:, off:off + t * h].astype(BF16)


def _granule_transpose(v):
    n = len(v)
    gran = lax.broadcasted_iota(I32, v[0].shape, 1) >> 4
    at = [gran == q for q in range(n)]
    rot = []
    for d in range(n):
        m = v[d]
        for q in range(1, n):
            m = jnp.where(at[q], v[(q + d) % n], m)
        rot.append(pltpu.roll(m, d * S5_H, 1) if d else m)
    out = []
    for q in range(n):
        w = rot[(-q) % n]
        for j in range(1, n):
            w = jnp.where(at[j], rot[(j - q) % n], w)
        out.append(w)
    return out


def _s5_body(x_ref, ctx_ref, mod_ref, cmod_ref, tt_ref, w1_ref, w2t_ref, ar_ref, ai_ref, y_ref,
             u_ref, s_ref, *, n_ctx_chunks, n_lat_chunks, gb):
    n_b = x_ref.shape[0]
    n_g = u_ref.shape[0]
    n_blocks = s_ref.shape[1] // SUBLANES
    half = SUBLANES // 2
    tok_blk = SUBLANES * S5_T
    lat_lo = n_ctx_chunks * n_b
    lat_rows = n_lat_chunks * n_b

    def chunk_rows(ref, b, tok0, shift, scale1):
        vs = [ref[b, pl.ds(tok0 + j, SUBLANES, stride=S5_T), :] * scale1 + shift for j in range(S5_T)]
        lo = _granule_transpose(vs[:SUBLANES])
        hi = _granule_transpose(vs[SUBLANES:])
        return lo, hi

    def put_rows(rows, chunk0, b):
        for q in range(n_g):
            for jh in range(2):
                u_ref[q, jh, pl.ds(chunk0 * n_b + b, SUBLANES, stride=n_b), :] = rows[jh][q]

    def u_rows(g, lo, n):
        return jnp.concatenate([u_ref[g, 0, lo:lo + n, :], u_ref[g, 1, lo:lo + n, :]], axis=1)

    shift = jnp.broadcast_to(cmod_ref[0, 0:1, :], (SUBLANES, LANES))
    scale1 = 1.0 + jnp.broadcast_to(cmod_ref[0, 1:2, :], (SUBLANES, LANES))
    for b in range(n_b):
        for cb in range(n_ctx_chunks // SUBLANES):
            rows = chunk_rows(ctx_ref, b, cb * tok_blk, shift, scale1)
            put_rows(rows, cb * SUBLANES, b)
            put_rows(rows, n_ctx_chunks + n_lat_chunks + cb * SUBLANES, b)

    def fill(cb, carry):
        for b in range(n_b):
            shift = jnp.broadcast_to(mod_ref[b, 0:1, :], (SUBLANES, LANES))
            scale1 = 1.0 + jnp.broadcast_to(mod_ref[b, 1:2, :], (SUBLANES, LANES))
            put_rows(chunk_rows(x_ref, b, cb * tok_blk, shift, scale1), n_ctx_chunks + cb * SUBLANES, b)
        return carry

    lax.fori_loop(0, n_lat_chunks // SUBLANES, fill, 0)

    for g in range(n_g):
        s_ref[g] = jnp.dot(u_rows(g, 0, n_blocks * SUBLANES).astype(BF16), w1_ref[g],
                           preferred_element_type=F32)

    lane = lax.broadcasted_iota(I32, (SUBLANES, LANES), 1)
    row = lax.broadcasted_iota(I32, (SUBLANES, LANES), 0)
    is_fwd = lane < LANES // 2
    is_fwd2 = jnp.concatenate([is_fwd, is_fwd], axis=1)
    top = row < half
    zero = jnp.zeros((SUBLANES, LANES), F32)
    for g0 in range(0, n_g, gb):
        ars = [jnp.broadcast_to(ar_ref[g0 + g], (SUBLANES, LANES)) for g in range(gb)]
        ais = [jnp.broadcast_to(ai_ref[g0 + g], (SUBLANES, LANES)) for g in range(gb)]

        def step(k, carry, g0=g0, ars=ars, ais=ais):
            fo = pl.multiple_of(k * SUBLANES, SUBLANES)
            bo = pl.multiple_of((n_blocks - 1 - k) * SUBLANES, SUBLANES)
            new = []
            loaded = [(s_ref[g0 + g, pl.ds(fo, SUBLANES), :], s_ref[g0 + g, pl.ds(bo, SUBLANES), :])
                      for g in range(gb)]
            stores = []
            for g in range(gb):
                xr, xi = carry[2 * g], carry[2 * g + 1]
                vf, vb_raw = loaded[g]
                vb = pltpu.roll(vb_raw, half, 0)
                vr = jnp.where(is_fwd, vf[:, :LANES], vb[:, :LANES])
                vi = jnp.where(is_fwd, vf[:, LANES:], vb[:, LANES:])
                ar, ai = ars[g], ais[g]
                yr = ar * xr - ai * xi + vr
                yi = ar * xi + ai * xr + vi
                yrr = pltpu.roll(yr, half, 0)
                yir = pltpu.roll(yi, half, 0)
                zr = ar * yrr - ai * yir + vr
                zi = ar * yir + ai * yrr + vi
                inc = jnp.concatenate([jnp.where(top, xr, yrr), jnp.where(top, xi, yir)], axis=1)
                stores.append((jnp.where(is_fwd2, inc, vf),
                               jnp.where(is_fwd2, vb_raw, pltpu.roll(inc, half, 0))))
                new.append(jnp.where(top, pltpu.roll(zr, half, 0), zr))
                new.append(jnp.where(top, pltpu.roll(zi, half, 0), zi))
            for g in range(gb):
                s_ref[g0 + g, pl.ds(fo, SUBLANES), :] = stores[g][0]
                s_ref[g0 + g, pl.ds(bo, SUBLANES), :] = stores[g][1]
            return tuple(new)

        lax.fori_loop(0, (n_ctx_chunks + n_lat_chunks) // 2, step, tuple(zero for _ in range(2 * gb)))

    for g in range(n_g):
        y = (jnp.dot(u_rows(g, lat_lo, lat_rows).astype(BF16), tt_ref[g], preferred_element_type=F32)
             + lax.dot_general(s_ref[g, lat_lo:lat_lo + lat_rows, :].astype(BF16), w2t_ref[g],
                               (((1,), (1,)), ((), ())), preferred_element_type=F32))
        for jh in range(2):
            u_ref[g, jh, lat_lo:lat_lo + lat_rows, :] = y[:, jh * LANES:(jh + 1) * LANES]

    def emit(cb, carry):
        for b in range(n_b):
            for jh in range(S5_T // SUBLANES):
                w = [u_ref[q, jh, pl.ds((n_ctx_chunks + cb * SUBLANES) * n_b + b, SUBLANES, stride=n_b), :]
                     for q in range(n_g)]
                v = _granule_transpose(w)
                for j in range(SUBLANES):
                    y_ref[b, pl.ds(cb * tok_blk + jh * SUBLANES + j, SUBLANES, stride=S5_T), :] = v[j]
        return carry

    lax.fori_loop(0, n_lat_chunks // SUBLANES, emit, 0)


def _s5_mix(x, ctx, mods, cmods, toep, w1, w2t, a_r, a_i):
    b, l, d = x.shape
    n_ctx = ctx.shape[1]
    n_g = LANES // S5_H
    n_ctx_chunks, n_lat_chunks = n_ctx // S5_T, l // S5_T
    assert b * 2 == SUBLANES and n_ctx_chunks % SUBLANES == 0 and n_lat_chunks % SUBLANES == 0
    rows = (2 * n_ctx_chunks + n_lat_chunks) * b
    w = S5_T * S5_H
    body = functools.partial(_s5_body, n_ctx_chunks=n_ctx_chunks, n_lat_chunks=n_lat_chunks, gb=8)
    lane_tile = lambda i: (0, 0, i)
    wspec = pl.BlockSpec((n_g, w, w), lambda i: (i, 0, 0))
    aspec = pl.BlockSpec((n_g, 1, w // 2), lambda i: (i, 0, 0))
    return pl.pallas_call(
        body,
        grid=(d // LANES,),
        in_specs=[pl.BlockSpec((b, l, LANES), lane_tile, pipeline_mode=pl.Buffered(1)),
                  pl.BlockSpec((b, n_ctx, LANES), lane_tile),
                  pl.BlockSpec((b, 6, LANES), lane_tile),
                  pl.BlockSpec((1, 6, LANES), lane_tile),
                  wspec, wspec, wspec, aspec, aspec],
        out_specs=pl.BlockSpec((b, l, LANES), lane_tile, pipeline_mode=pl.Buffered(1)),
        out_shape=jax.ShapeDtypeStruct((b, l, d), F32),
        scratch_shapes=[pltpu.VMEM((n_g, w // LANES, rows, LANES), F32), pltpu.VMEM((n_g, rows, w), F32)],
        compiler_params=_cparams("arbitrary"),
        name="s5_mix",
    )(x, ctx, mods, cmods, toep, w1, w2t, a_r, a_i)


def _layer_norm(r, g, b):
    mu = jnp.mean(r, axis=-1, keepdims=True)
    xc = r - mu
    var = jnp.mean(xc * xc, axis=-1, keepdims=True)
    return xc * lax.rsqrt(var + LN_EPS) * g + b


def _max2_of4(a, b, c, d):
    h1, l1 = jnp.maximum(a, b), jnp.minimum(a, b)
    h2, l2 = jnp.maximum(c, d), jnp.minimum(c, d)
    return jnp.maximum(h1, h2) + jnp.maximum(jnp.minimum(h1, h2), jnp.maximum(l1, l2))


def _argmax_first(vals):
    idx = jnp.zeros(vals[0].shape, I32)
    best = vals[0]
    for j in range(1, len(vals)):
        upd = vals[j] > best
        idx = jnp.where(upd, j, idx)
        best = jnp.where(upd, vals[j], best)
    return idx, best


def _route(logits_t, count_ref, route_ref):
    n_e, tm = logits_t.shape
    per = n_e // N_EXPERT_GROUPS
    mx = jnp.max(logits_t, axis=0, keepdims=True)
    ex = jnp.exp(logits_t - mx)
    sc = ex / jnp.sum(ex, axis=0, keepdims=True)
    rows = [sc[e:e + 1, :] for e in range(n_e)]
    gscore = [_max2_of4(*rows[per * g:per * (g + 1)]) for g in range(N_EXPERT_GROUPS)]
    best, _ = _argmax_first(gscore)
    vals = []
    for j in range(per):
        v = rows[per * (N_EXPERT_GROUPS - 1) + j]
        for g in range(N_EXPERT_GROUPS - 2, -1, -1):
            v = jnp.where(best == g, rows[per * g + j], v)
        vals.append(v)
    i1, m1 = _argmax_first(vals)
    i2, m2 = _argmax_first([jnp.where(i1 == j, -1.0, vals[j]) for j in range(per)])
    den = m1 + m2
    first_lo = i1 < i2
    lo = jnp.minimum(i1, i2)
    hi = jnp.maximum(i1, i2)
    pair = jnp.where(lo == 0, 0, jnp.where(lo == 1, per - 1, 2 * per - 3)) + hi - lo - 1
    cls = best * PAIRS_PER_GROUP + pair
    w_lo = jnp.where(first_lo, m1, m2) / den
    w_hi = jnp.where(first_lo, m2, m1) / den

    n_cls = count_ref.shape[0]
    hit = lax.broadcasted_iota(I32, (n_cls, tm), 0) == cls
    onehot = jnp.where(hit, 1.0, 0.0)
    src = lax.broadcasted_iota(I32, (tm, tm), 0)
    dst = lax.broadcasted_iota(I32, (tm, tm), 1)
    tri = jnp.where(src <= dst, 1.0, 0.0).astype(BF16)
    cum = jnp.dot(onehot.astype(BF16), tri, preferred_element_type=F32)
    excl = cum - onehot + count_ref[:, 0:1]
    rank = jnp.sum(jnp.where(hit, excl, 0.0), axis=0, keepdims=True)
    count_ref[...] = count_ref[...] + jnp.sum(onehot, axis=1, keepdims=True)

    zero = jnp.zeros((1, tm), F32)
    route_ref[...] = jnp.concatenate([cls.astype(F32), rank, w_lo, w_hi, zero, zero, zero, zero], axis=0)


def _post_mixer_body(m_ref, x_ref, mod_ref, lng_ref, lnb_ref, rwt_ref, rb_ref, *rest, glu, alpha):
    if glu:
        wv_ref, wg_ref, x1_ref, h_ref, route_ref, cnt_out_ref, cnt_ref = rest
        a = jax.nn.gelu(m_ref[0], approximate=True).astype(BF16)
        val = jnp.dot(a, wv_ref[...], preferred_element_type=F32)
        gate = jnp.dot(a, wg_ref[...], preferred_element_type=F32)
        m = val * jax.nn.sigmoid(gate)
    else:
        x1_ref, h_ref, route_ref, cnt_out_ref, cnt_ref = rest
        m = m_ref[0]

    @pl.when((pl.program_id(0) == 0) & (pl.program_id(1) == 0))
    def _():
        cnt_ref[...] = jnp.zeros_like(cnt_ref)

    g1 = mod_ref[0, 2:3, :]
    sh2 = mod_ref[0, 3:4, :]
    sc2 = mod_ref[0, 4:5, :]
    x1 = _layer_norm(alpha * x_ref[0] + g1 * m, lng_ref[...], lnb_ref[...])
    x1_ref[0] = x1
    h = x1 * (1.0 + sc2) + sh2
    h_ref[...] = h
    logits_t = lax.dot_general(rwt_ref[...], h, (((1,), (1,)), ((), ())),
                               precision=HI, preferred_element_type=F32) + rb_ref[...]
    _route(logits_t, cnt_ref, route_ref)
    cnt_out_ref[...] = cnt_ref[...]


def _post_mixer(m, x, mods, ln_g, ln_b, router_w, router_b, alpha, glu_w=None):
    b, l, d = x.shape
    n_e = router_w.shape[1]
    assert n_e == N_EXPERT_GROUPS * EXPERTS_PER_GROUP
    tm = min(l, 512)
    nt = l // tm
    tok = lambda i, j: (i, j, 0)
    const2 = lambda i, j: (0, 0)
    in_specs = [pl.BlockSpec((1, tm, d), tok),
                pl.BlockSpec((1, tm, d), tok),
                pl.BlockSpec((1, 6, d), lambda i, j: (i, 0, 0)),
                pl.BlockSpec((1, d), const2),
                pl.BlockSpec((1, d), const2),
                pl.BlockSpec((n_e, d), const2),
                pl.BlockSpec((n_e, 1), const2)]
    args = [m, x, mods, ln_g.reshape(1, d), ln_b.reshape(1, d), router_w.T, router_b.reshape(n_e, 1)]
    if glu_w is not None:
        in_specs += [pl.BlockSpec((d, d), const2), pl.BlockSpec((d, d), const2)]
        args += [glu_w[0].astype(BF16), glu_w[1].astype(BF16)]
    out_shape = [jax.ShapeDtypeStruct((b, l, d), F32),
                 jax.ShapeDtypeStruct((b * l, d), F32),
                 jax.ShapeDtypeStruct((ROUTE_ROWS, b * l), F32),
                 jax.ShapeDtypeStruct((N_PAIR_CLASSES, LANES), F32)]
    out_specs = [pl.BlockSpec((1, tm, d), tok),
                 pl.BlockSpec((tm, d), lambda i, j: (i * nt + j, 0)),
                 pl.BlockSpec((ROUTE_ROWS, tm), lambda i, j: (0, i * nt + j)),
                 pl.BlockSpec((N_PAIR_CLASSES, LANES), const2)]
    return pl.pallas_call(
        functools.partial(_post_mixer_body, glu=glu_w is not None, alpha=alpha),
        grid=(b, nt),
        in_specs=in_specs,
        out_specs=out_specs,
        out_shape=out_shape,
        scratch_shapes=[pltpu.VMEM((N_PAIR_CLASSES, LANES), F32)],
        compiler_params=_cparams("arbitrary", "arbitrary"),
        name="post_mixer_glu" if glu_w is not None else "post_mixer",
    )(*args)


def _take(table, idx):
    ids = jnp.arange(table.shape[0], dtype=I32)
    return jnp.sum(jnp.where(idx[:, None] == ids[None, :], table[None, :], 0), axis=1)


def _pass_segments():
    seg_cls, seg_hi = [], []
    pairs = [(a, b) for a in range(EXPERTS_PER_GROUP) for b in range(a + 1, EXPERTS_PER_GROUP)]
    for g in range(N_EXPERT_GROUPS):
        for m in range(EXPERTS_PER_GROUP):
            for idx, (a, b) in enumerate(pairs):
                if m in (a, b):
                    seg_cls.append(g * PAIRS_PER_GROUP + idx)
                    seg_hi.append(int(m == b))
    return seg_cls, seg_hi


def _dispatch_plan(route, counts, tile):
    n_cls = counts.shape[0]
    n_tok = route.shape[1]
    cnt = counts[:, 0].astype(I32)
    tiles = (cnt + tile - 1) // tile
    tile_end = jnp.cumsum(tiles)
    tile_off = tile_end - tiles
    cids = jnp.arange(n_cls, dtype=I32)[:, None]
    cls = route[0].astype(I32)
    pos = jnp.sum(jnp.where(cls[None, :] == cids, (tile_off * tile)[:, None], 0), axis=0) + route[1].astype(I32)

    n_row_tiles = n_tok // tile + n_cls
    seg_cls, seg_hi = _pass_segments()
    seg_cls = jnp.asarray(seg_cls, I32)
    seg_hi = jnp.asarray(seg_hi, I32)
    seg_per_expert = EXPERTS_PER_GROUP - 1
    seg_tiles = _take(tiles, seg_cls)
    seg_end = jnp.cumsum(seg_tiles)
    n_used = seg_end[-1]
    p = jnp.arange(2 * n_row_tiles, dtype=I32)
    seg = jnp.minimum(jnp.sum((seg_end[None, :] <= p[:, None]).astype(I32), axis=1), seg_cls.shape[0] - 1)
    within = p - _take(seg_end - seg_tiles, seg)
    spare = p - n_used
    used = p < n_used
    pass_tile = jnp.where(used, _take(tile_off, _take(seg_cls, seg)) + within, tile_end[-1] + spare // 2).astype(I32)
    pass_hi = jnp.where(used, _take(seg_hi, seg), spare % 2).astype(I32)
    pass_expert = jnp.where(used, seg // seg_per_expert, N_EXPERT_GROUPS * EXPERTS_PER_GROUP - 1).astype(I32)
    last_tile = (tile_end - 1).astype(I32)
    return (pos.astype(I32), pass_tile, pass_hi, pass_expert, n_used.reshape(1).astype(I32),
            last_tile, tiles.astype(I32), tile_end[-1:].astype(I32), n_row_tiles)


def _zero_tiles(last_ref, tiles_ref, nu_ref, zero_ref, hs_ref, sem, n_e, tile, wait):
    for e in range(n_e):
        @pl.when(tiles_ref[e] > 0)
        def _(e=e):
            start = pl.multiple_of(last_ref[e] * tile, tile)
            cp = pltpu.make_async_copy(zero_ref, hs_ref.at[pl.ds(start, tile)], sem)
            if wait:
                cp.wait()
            else:
                cp.start()

    def body(j, carry):
        start = pl.multiple_of(j * tile, tile)
        cp = pltpu.make_async_copy(zero_ref, hs_ref.at[pl.ds(start, tile)], sem)
        if wait:
            cp.wait()
        else:
            cp.start()
        return carry

    lax.fori_loop(nu_ref[0], hs_ref.shape[0] // tile, body, 0)


def _dispatch_body(pos_ref, last_ref, tiles_ref, nu_ref, h_ref, hs_ref, zero_ref, sem_z, sem, *,
                   tm, tile, n_e):
    i = pl.program_id(0)

    @pl.when(i == 0)
    def _():
        zero_ref[...] = jnp.zeros_like(zero_ref)
        _zero_tiles(last_ref, tiles_ref, nu_ref, zero_ref, hs_ref, sem_z, n_e, tile, False)
        _zero_tiles(last_ref, tiles_ref, nu_ref, zero_ref, hs_ref, sem_z, n_e, tile, True)

    base = i * tm

    def issue(r2, carry):
        for k in range(2):
            r = 2 * r2 + k
            pltpu.make_async_copy(h_ref.at[pl.ds(r, 1)], hs_ref.at[pl.ds(pos_ref[base + r], 1)],
                                  sem).start(priority=k)
        return carry

    lax.fori_loop(0, tm // 2, issue, 0, unroll=8)
    pltpu.make_async_copy(h_ref, hs_ref.at[pl.ds(0, tm)], sem).wait()


def _dispatch(h_rows, pos, last_tile, tiles, n_used, n_rows, tile):
    n, d = h_rows.shape
    n_e = tiles.shape[0]
    tm = min(n, 1024)
    grid_spec = pltpu.PrefetchScalarGridSpec(
        num_scalar_prefetch=4,
        grid=(n // tm,),
        in_specs=[pl.BlockSpec((tm, d), lambda i, *_: (i, 0))],
        out_specs=pl.BlockSpec(memory_space=pl.ANY),
        scratch_shapes=[pltpu.VMEM((tile, d), F32),
                        pltpu.SemaphoreType.DMA(()), pltpu.SemaphoreType.DMA(())],
    )
    return pl.pallas_call(
        functools.partial(_dispatch_body, tm=tm, tile=tile, n_e=n_e),
        grid_spec=grid_spec,
        out_shape=jax.ShapeDtypeStruct((n_rows, d), F32),
        compiler_params=_cparams("arbitrary"),
        name="moe_dispatch",
    )(pos, last_tile, tiles, n_used, h_rows)


def _expert_body(pt_ref, ph_ref, te_ref, nu_ref, hs_ref, wg_ref, wu_ref, wd_ref, ys_ref, wgb, wub, wdb):
    i = pl.program_id(0)
    first = jnp.logical_or(i == 0, te_ref[i] != te_ref[jnp.maximum(i - 1, 0)])

    @pl.when(jnp.logical_and(i < nu_ref[0], first))
    def _():
        wgb[...] = wg_ref[0, 0].astype(BF16)
        wub[...] = wu_ref[0, 0].astype(BF16)
        wdb[...] = wd_ref[0, 0].astype(BF16)

    @pl.when(i < nu_ref[0])
    def _():
        x = hs_ref[...].astype(BF16)
        gate = jnp.dot(x, wgb[...], preferred_element_type=F32)
        up = jnp.dot(x, wub[...], preferred_element_type=F32)
        a = (gate * jax.nn.sigmoid(gate) * up).astype(BF16)
        ys_ref[...] = jnp.dot(a, wdb[...], preferred_element_type=F32)

    @pl.when(i >= nu_ref[0])
    def _():
        ys_ref[...] = jnp.zeros_like(ys_ref)


def _expert_ffn(hs, pass_tile, pass_hi, pass_expert, n_used, w_gate, w_up, w_down, layer, tile):
    _, n_e, d, f = w_gate.shape
    n_rows = hs.shape[0]
    wmap = lambda i, pt, ph, te, nu: (layer, te[i], 0, 0)
    grid_spec = pltpu.PrefetchScalarGridSpec(
        num_scalar_prefetch=4,
        grid=(pass_tile.shape[0],),
        in_specs=[pl.BlockSpec((tile, d), lambda i, pt, ph, te, nu: (pt[i], 0)),
                  pl.BlockSpec((1, 1, d, f), wmap),
                  pl.BlockSpec((1, 1, d, f), wmap),
                  pl.BlockSpec((1, 1, f, d), wmap)],
        out_specs=pl.BlockSpec((tile, d), lambda i, pt, ph, te, nu: (pt[i], ph[i])),
        scratch_shapes=[pltpu.VMEM((d, f), BF16), pltpu.VMEM((d, f), BF16), pltpu.VMEM((f, d), BF16)],
    )
    return pl.pallas_call(
        _expert_body,
        grid_spec=grid_spec,
        out_shape=jax.ShapeDtypeStruct((n_rows, 2 * d), F32),
        compiler_params=_cparams("arbitrary"),
        name="moe_experts",
    )(pass_tile, pass_hi, pass_expert, n_used, hs, w_gate, w_up, w_down)


def _combine_body(pos_ref, x_ref, mod_ref, wts_ref, lng_ref, lnb_ref, ys_ref, *rest,
                  tm, alpha, next_mod):
    if next_mod:
        nmod_ref, x2_ref, h_ref, buf, sem = rest
    else:
        x2_ref, buf, sem = rest
    step = pl.program_id(0) * pl.num_programs(1) + pl.program_id(1)
    n_steps = pl.num_programs(0) * pl.num_programs(1)
    d = x_ref.shape[2]

    def gather(s, slot):
        def issue(r2, carry):
            for k in range(2):
                r = 2 * r2 + k
                pltpu.make_async_copy(ys_ref.at[pl.ds(pos_ref[s * tm + r], 1)],
                                      buf.at[slot, pl.ds(r, 1)], sem.at[slot]).start(priority=k)
            return carry

        lax.fori_loop(0, tm // 2, issue, 0, unroll=8)

    @pl.when(step == 0)
    def _():
        gather(0, 0)

    @pl.when(step + 1 < n_steps)
    def _():
        gather(step + 1, (step + 1) % 2)

    slot = step % 2
    pltpu.make_async_copy(ys_ref.at[pl.ds(0, tm)], buf.at[slot], sem.at[slot]).wait()

    w = wts_ref[...]
    rows = buf[slot]
    moe = w[:, 0:1] * rows[:, :d] + w[:, 1:2] * rows[:, d:]
    g2 = mod_ref[0, 5:6, :]
    x2 = _layer_norm(alpha * x_ref[0] + g2 * moe, lng_ref[...], lnb_ref[...])
    x2_ref[0] = x2
    if next_mod:
        h_ref[0] = x2 * (1.0 + nmod_ref[0, 1:2, :]) + nmod_ref[0, 0:1, :]


def _combine(ys, pos, wts, x, mods, ln_g, ln_b, alpha, next_mods=None):
    b, l, d = x.shape
    tm = min(l, 256)
    nt = l // tm
    tok = lambda i, j, *_: (i, j, 0)
    bat = lambda i, j, *_: (i, 0, 0)
    const2 = lambda i, j, *_: (0, 0)
    in_specs = [pl.BlockSpec((1, tm, d), tok),
                pl.BlockSpec((1, 6, d), bat),
                pl.BlockSpec((tm, 2), lambda i, j, *_: (i * nt + j, 0)),
                pl.BlockSpec((1, d), const2),
                pl.BlockSpec((1, d), const2),
                pl.BlockSpec(memory_space=pl.ANY)]
    args = [x, mods, wts, ln_g.reshape(1, d), ln_b.reshape(1, d), ys]
    out_shape = [jax.ShapeDtypeStruct((b, l, d), F32)]
    out_specs = [pl.BlockSpec((1, tm, d), tok)]
    if next_mods is not None:
        in_specs.append(pl.BlockSpec((1, 6, d), bat))
        args.append(next_mods)
        out_shape.append(jax.ShapeDtypeStruct((b, l, d), F32))
        out_specs.append(pl.BlockSpec((1, tm, d), tok))
    grid_spec = pltpu.PrefetchScalarGridSpec(
        num_scalar_prefetch=1,
        grid=(b, nt),
        in_specs=in_specs,
        out_specs=out_specs,
        scratch_shapes=[pltpu.VMEM((2, tm, 2 * d), F32), pltpu.SemaphoreType.DMA((2,))],
    )
    return pl.pallas_call(
        functools.partial(_combine_body, tm=tm, alpha=alpha, next_mod=next_mods is not None),
        grid_spec=grid_spec,
        out_shape=out_shape,
        compiler_params=_cparams("arbitrary", "arbitrary"),
        name="moe_combine",
    )(pos, *args)


MOE_TILE = 256


def _moe(h_rows, route, counts, w_gate, w_up, w_down, layer):
    n = h_rows.shape[0]
    tile = min(n, MOE_TILE)
    (pos, pass_tile, pass_hi, pass_expert, n_used, last_tile, tiles, n_used_tiles,
     n_row_tiles) = _dispatch_plan(route, counts, tile)
    hs = _dispatch(h_rows, pos, last_tile, tiles, n_used_tiles, n_row_tiles * tile, tile)
    ys = _expert_ffn(hs, pass_tile, pass_hi, pass_expert, n_used, w_gate, w_up, w_down, layer, tile)
    return ys, pos, jnp.transpose(route[2:4])


def _pool_group(h_ref, w_ref, sc_ref, o_ref, col_ref, k, n_rows):
    n = n_rows * GRID_W
    c = h_ref.shape[2]
    blk = 4 * GRID_W
    half = k // 2
    pad = half * GRID_W
    ti = lax.broadcasted_iota(I32, (blk, blk), 0)
    si = lax.broadcasted_iota(I32, (blk, blk), 1)
    shift = GRID_W.bit_length() - 1
    same_row = (ti >> shift) == (si >> shift)
    band = jnp.where(same_row & (si - ti >= -half) & (si - ti <= half - 1), 1.0, 0.0).astype(BF16)
    col_ref[0:pad, :] = jnp.zeros((pad, c), F32)
    col_ref[pad + n:pad + n + pad, :] = jnp.zeros((pad, c), F32)
    for b0 in range(0, n, blk):
        hb = h_ref[0, b0:b0 + blk, :]
        head = hb.astype(BF16)
        rest = (hb - head.astype(F32)).astype(BF16)
        col_ref[pad + b0:pad + b0 + blk, :] = (jnp.dot(band, head, preferred_element_type=F32)
                                               + jnp.dot(band, rest, preferred_element_type=F32))
    acc = col_ref[0:n, :]
    for j in range(1, k):
        acc = acc + col_ref[j * GRID_W:j * GRID_W + n, :]
    t = lax.broadcasted_iota(I32, (n, 1), 0)
    wc = t & (GRID_W - 1)
    wr = t >> shift
    cnt_c = jnp.minimum(wc + half - 1, GRID_W - 1) - jnp.maximum(wc - half, 0) + 1
    cnt_r = jnp.minimum(wr + half - 1, n_rows - 1) - jnp.maximum(wr - half, 0) + 1
    mean = acc / (cnt_c * cnt_r).astype(F32)
    pooled = (mean - h_ref[0]).astype(BF16)
    o_ref[0] = jnp.dot(pooled, w_ref[0].astype(BF16), preferred_element_type=F32) * sc_ref[...]


def _pool_body(h_ref, w_ref, sc_ref, o_ref, col_ref, *, n_rows):
    g = pl.program_id(1)
    for gi, k in enumerate(POOL_WINDOWS):
        @pl.when(g == gi)
        def _(k=k):
            _pool_group(h_ref, w_ref, sc_ref, o_ref, col_ref, k, n_rows)


def _pool_mix(h, w_grp, scale):
    b, n, d = h.shape
    n_g, c, _ = w_grp.shape
    n_rows = n // GRID_W
    pad = (max(POOL_WINDOWS) // 2) * GRID_W
    return pl.pallas_call(
        functools.partial(_pool_body, n_rows=n_rows),
        grid=(b, n_g),
        in_specs=[pl.BlockSpec((1, n, c), lambda i, j: (i, 0, j)),
                  pl.BlockSpec((1, c, c), lambda i, j: (j, 0, 0)),
                  pl.BlockSpec((1, c), lambda i, j: (0, j))],
        out_specs=pl.BlockSpec((1, n, c), lambda i, j: (i, 0, j)),
        out_shape=jax.ShapeDtypeStruct((b, n, d), F32),
        scratch_shapes=[pltpu.VMEM((n + 2 * pad, c), F32)],
        compiler_params=_cparams("arbitrary", "arbitrary"),
        name="pool_mix",
    )(h, w_grp, scale.reshape(1, d))


def kernel(x, c, ctx, c_ctx, mod_w, mod_b, ln_g, ln_b, s5_lam_re, s5_lam_im, s5_log_dt, s5_b_re, s5_b_im,
           s5_c_re, s5_c_im, s5_d, s5_w_val, s5_w_gate, pool_w, pool_scale, router_w, router_b,
           moe_w_gate, moe_w_up, moe_w_down):
    b, l, d = x.shape
    depth = mod_w.shape[0]
    assert depth == 2 and b + 1 <= SUBLANES and d % LANES == 0 and GRID_W & (GRID_W - 1) == 0
    alpha = (2 * depth) ** 0.25

    cond = jnp.zeros((SUBLANES, d), F32).at[:b].set(c).at[b].set(c_ctx)
    mods = _modulation(cond, mod_w, mod_b).reshape(depth, SUBLANES, 6, d)

    toep, w1, w2t, a_r, a_i = _s5_weights(s5_lam_re[0], s5_lam_im[0], s5_log_dt[0], s5_b_re[0], s5_b_im[0],
                                          s5_c_re[0], s5_c_im[0], s5_d[0])
    y = _s5_mix(x, ctx, mods[0, :b], mods[0, b:b + 1], toep, w1, w2t, a_r, a_i)
    x1, h_rows, route, counts = _post_mixer(y, x, mods[0, :b], ln_g[0, 0], ln_b[0, 0], router_w, router_b,
                                            alpha, glu_w=(s5_w_val[0], s5_w_gate[0]))
    ys, pos, wts = _moe(h_rows, route, counts, moe_w_gate, moe_w_up, moe_w_down, 0)
    x2, h = _combine(ys, pos, wts, x1, mods[0, :b], ln_g[0, 1], ln_b[0, 1], alpha, next_mods=mods[1, :b])

    m = _pool_mix(h, pool_w[0], pool_scale[0])
    x3, h_rows, route, counts = _post_mixer(m, x2, mods[1, :b], ln_g[1, 0], ln_b[1, 0], router_w, router_b,
                                            alpha)
    ys, pos, wts = _moe(h_rows, route, counts, moe_w_gate, moe_w_up, moe_w_down, 1)
    (out,) = _combine(ys, pos, wts, x3, mods[1, :b], ln_g[1, 1], ln_b[1, 1], alpha)
    return out
```

```python
import functools

import jax
import jax.numpy as jnp
from jax import lax
from jax.experimental import pallas as pl
from jax.experimental.pallas import tpu as pltpu

F32 = jnp.float32
BF16 = jnp.bfloat16
I32 = jnp.int32
HI = lax.Precision.HIGHEST

GRID_W = 64
S5_H = 16
S5_T = 16
POOL_WINDOWS = (2, 4, 8, 16)
N_EXPERT_GROUPS = 4
LN_EPS = 1e-5
LANES = 128
SUBLANES = 8
VMEM_LIMIT = 52 * 1024 * 1024

EXPERTS_PER_GROUP = 4
PAIRS_PER_GROUP = EXPERTS_PER_GROUP * (EXPERTS_PER_GROUP - 1) // 2
N_PAIR_CLASSES = N_EXPERT_GROUPS * PAIRS_PER_GROUP
ROUTE_ROWS = 8


def _cparams(*sem):
    return pltpu.CompilerParams(dimension_semantics=sem, vmem_limit_bytes=VMEM_LIMIT)


def _mod_body(c_ref, w_ref, b_ref, o_ref):
    c = c_ref[...]
    s = c * jax.nn.sigmoid(c)
    o_ref[0] = jnp.dot(s, w_ref[0], precision=HI, preferred_element_type=F32) + b_ref[0]


def _modulation(cond, mod_w, mod_b):
    depth, d, n6 = mod_w.shape
    tn = min(n6, 1536)
    return pl.pallas_call(
        _mod_body,
        grid=(depth, n6 // tn),
        in_specs=[pl.BlockSpec((SUBLANES, d), lambda i, j: (0, 0)),
                  pl.BlockSpec((1, d, tn), lambda i, j: (i, 0, j)),
                  pl.BlockSpec((1, 1, tn), lambda i, j: (i, 0, j))],
        out_specs=pl.BlockSpec((1, SUBLANES, tn), lambda i, j: (i, 0, j)),
        out_shape=jax.ShapeDtypeStruct((depth, SUBLANES, n6), F32),
        compiler_params=_cparams("arbitrary", "arbitrary"),
        name="modulation",
    )(cond, mod_w, mod_b.reshape(depth, 1, n6))


def _s5_direction_terms(lam_re, lam_im, log_dt, b_re, b_im):
    lr = lam_re.astype(F32)
    li = lam_im.astype(F32)
    dt = jnp.exp(log_dt.astype(F32))[:, None]
    mag = jnp.exp(lr * dt)
    ar = mag * jnp.cos(li * dt)
    ai = mag * jnp.sin(li * dt)
    den = lr * lr + li * li
    nr = ar - 1.0
    fr = (nr * lr + ai * li) / den
    fi = (ai * lr - nr * li) / den
    br_, bi_ = b_re.astype(F32), b_im.astype(F32)
    bbr = fr[..., None] * br_ - fi[..., None] * bi_
    bbi = fr[..., None] * bi_ + fi[..., None] * br_
    k = jnp.arange(S5_T + 1, dtype=F32)[:, None, None]
    pm = jnp.exp(k * (lr * dt))
    pr = pm * jnp.cos(k * (li * dt))
    pi = pm * jnp.sin(k * (li * dt))
    return pr, pi, bbr, bbi


def _s5_weights(lam_re, lam_im, log_dt, b_re, b_im, c_re, c_im, d_skip):
    t = S5_T
    g, p = lam_re.shape[1:]
    h = b_re.shape[-1]
    terms = [_s5_direction_terms(lam_re[d], lam_im[d], log_dt[d], b_re[d], b_im[d]) for d in (0, 1)]
    pw = jnp.stack([jnp.transpose(terms[d][k], (1, 0, 2)) for d in (0, 1) for k in (0, 1)], axis=1)
    bt = jnp.stack([jnp.transpose(terms[d][k], (0, 2, 1)) for d in (0, 1) for k in (2, 3)], axis=1)
    cc = jnp.stack([c[d].astype(F32) for d in (0, 1) for c in (c_re, c_im)], axis=1)
    gp = 8
    spec4 = lambda rows: pl.BlockSpec((gp, 4, rows, p), lambda i: (i, 0, 0, 0))
    wide = pl.BlockSpec((gp, t * h, t * h), lambda i: (i, 0, 0))
    toep, w1, w2t = pl.pallas_call(
        functools.partial(_s5_prep_body, gp=gp),
        grid=(g // gp,),
        in_specs=[spec4(t + 1), spec4(h), spec4(h), pl.BlockSpec((gp, 1, h), lambda i: (i, 0, 0))],
        out_specs=[wide, wide, wide],
        out_shape=[jax.ShapeDtypeStruct((g, t * h, t * h), BF16) for _ in range(3)],
        compiler_params=_cparams("arbitrary"),
        name="s5_prep",
    )(pw, bt, cc, d_skip.astype(F32).reshape(g, 1, h))
    a_r = jnp.concatenate([pw[:, 0, t], pw[:, 2, t]], axis=-1)[:, None, :]
    a_i = jnp.concatenate([pw[:, 1, t], pw[:, 3, t]], axis=-1)[:, None, :]
    return toep, w1, w2t, a_r, a_i


def _cmul(ar, ai, br, bi):
    return ar * br - ai * bi, ar * bi + ai * br


def _s5_prep_body(pw_ref, bt_ref, cc_ref, d_ref, toep_ref, w1_ref, w2t_ref, *, gp):
    t = pw_ref.shape[2] - 1
    h = bt_ref.shape[2]
    nt = (((1,), (1,)), ((), ()))
    eye = (lax.broadcasted_iota(I32, (h, h), 0) == lax.broadcasted_iota(I32, (h, h), 1))
    for g in range(gp):
        prf, pif, prb, pib = (pw_ref[g, k] for k in range(4))
        btf = (bt_ref[g, 0], bt_ref[g, 1])
        btb = (bt_ref[g, 2], bt_ref[g, 3])
        ccf = (cc_ref[g, 0], cc_ref[g, 1])
        ccb = (cc_ref[g, 2], cc_ref[g, 3])
        caf, cab = [], []
        for j in range(t):
            f_r, f_i = _cmul(*btf, prf[t - 1 - j:t - j], pif[t - 1 - j:t - j])
            b_r, b_i = _cmul(*btb, prb[j:j + 1], pib[j:j + 1])
            w1_ref[g, j * h:(j + 1) * h, :] = jnp.concatenate([f_r, b_r, f_i, b_i], axis=1).astype(BF16)
            mf_r, mf_i = _cmul(*ccf, prf[j + 1:j + 2], pif[j + 1:j + 2])
            mb_r, mb_i = _cmul(*ccb, prb[t - j:t - j + 1], pib[t - j:t - j + 1])
            w2t_ref[g, j * h:(j + 1) * h, :] = jnp.concatenate([mf_r, mb_r, -mf_i, -mb_i],
                                                               axis=1).astype(BF16)
            caf.append(_cmul(*ccf, prf[j:j + 1], pif[j:j + 1]))
            cab.append(_cmul(*ccb, prb[t - 1 - j:t - j], pib[t - 1 - j:t - j]))

        def lag_kernels(btx, ca):
            car = jnp.concatenate([c[0] for c in ca], axis=0)
            cai = jnp.concatenate([c[1] for c in ca], axis=0)
            return (lax.dot_general(btx[0], car, nt, precision=HI, preferred_element_type=F32)
                    - lax.dot_general(btx[1], cai, nt, precision=HI, preferred_element_type=F32))

        kf = lag_kernels(btf, caf)
        kb = lag_kernels(btb, cab)
        skip = jnp.where(eye, jnp.broadcast_to(d_ref[g], (h, h)), 0.0)
        mid = kb[:, (t - 1) * h:] + kf[:, :h] + skip
        kwide = jnp.concatenate([kb[:, :(t - 1) * h], mid, kf[:, h:], jnp.zeros((h, h), F32)], axis=1)
        for i in range(t):
            off = (t - 1 - i) * h
            toep_ref[g, i * h:(i + 1) * h, :] = kwide[:, off:off + t * h].astype(BF16)


def _granule_transpose(v):
    n = len(v)
    gran = lax.broadcasted_iota(I32, v[0].shape, 1) >> 4
    at = [gran == q for q in range(n)]
    rot = []
    for d in range(n):
        m = v[d]
        for q in range(1, n):
            m = jnp.where(at[q], v[(q + d) % n], m)
        rot.append(pltpu.roll(m, d * S5_H, 1) if d else m)
    out = []
    for q in range(n):
        w = rot[(-q) % n]
        for j in range(1, n):
            w = jnp.where(at[j], rot[(j - q) % n], w)
        out.append(w)
    return out


def _s5_body(x_ref, ctx_ref, mod_ref, cmod_ref, tt_ref, w1_ref, w2t_ref, ar_ref, ai_ref, y_ref,
             u_ref, s_ref, *, n_ctx_chunks, n_lat_chunks, gb):
    n_b = x_ref.shape[0]
    n_g = u_ref.shape[0]
    n_blocks = s_ref.shape[1] // SUBLANES
    half = SUBLANES // 2
    tok_blk = SUBLANES * S5_T
    lat_lo = n_ctx_chunks * n_b
    lat_rows = n_lat_chunks * n_b

    def chunk_rows(ref, b, tok0, shift, scale1):
        vs = [ref[b, pl.ds(tok0 + j, SUBLANES, stride=S5_T), :] * scale1 + shift for j in range(S5_T)]
        lo = _granule_transpose(vs[:SUBLANES])
        hi = _granule_transpose(vs[SUBLANES:])
        return lo, hi

    def put_rows(rows, chunk0, b):
        for q in range(n_g):
            for jh in range(2):
                u_ref[q, jh, pl.ds(chunk0 * n_b + b, SUBLANES, stride=n_b), :] = rows[jh][q]

    def u_rows(g, lo, n):
        return jnp.concatenate([u_ref[g, 0, lo:lo + n, :], u_ref[g, 1, lo:lo + n, :]], axis=1)

    shift = jnp.broadcast_to(cmod_ref[0, 0:1, :], (SUBLANES, LANES))
    scale1 = 1.0 + jnp.broadcast_to(cmod_ref[0, 1:2, :], (SUBLANES, LANES))
    for b in range(n_b):
        for cb in range(n_ctx_chunks // SUBLANES):
            rows = chunk_rows(ctx_ref, b, cb * tok_blk, shift, scale1)
            put_rows(rows, cb * SUBLANES, b)
            put_rows(rows, n_ctx_chunks + n_lat_chunks + cb * SUBLANES, b)

    def fill(cb, carry):
        for b in range(n_b):
            shift = jnp.broadcast_to(mod_ref[b, 0:1, :], (SUBLANES, LANES))
            scale1 = 1.0 + jnp.broadcast_to(mod_ref[b, 1:2, :], (SUBLANES, LANES))
            put_rows(chunk_rows(x_ref, b, cb * tok_blk, shift, scale1), n_ctx_chunks + cb * SUBLANES, b)
        return carry

    lax.fori_loop(0, n_lat_chunks // SUBLANES, fill, 0)

    for g in range(n_g):
        s_ref[g] = jnp.dot(u_rows(g, 0, n_blocks * SUBLANES).astype(BF16), w1_ref[g],
                           preferred_element_type=F32)

    lane = lax.broadcasted_iota(I32, (SUBLANES, LANES), 1)
    row = lax.broadcasted_iota(I32, (SUBLANES, LANES), 0)
    is_fwd = lane < LANES // 2
    is_fwd2 = jnp.concatenate([is_fwd, is_fwd], axis=1)
    top = row < half
    zero = jnp.zeros((SUBLANES, LANES), F32)
    for g0 in range(0, n_g, gb):
        ars = [jnp.broadcast_to(ar_ref[g0 + g], (SUBLANES, LANES)) for g in range(gb)]
        ais = [jnp.broadcast_to(ai_ref[g0 + g], (SUBLANES, LANES)) for g in range(gb)]

        def step(k, carry, g0=g0, ars=ars, ais=ais):
            fo = pl.multiple_of(k * SUBLANES, SUBLANES)
            bo = pl.multiple_of((n_blocks - 1 - k) * SUBLANES, SUBLANES)
            new = []
            loaded = [(s_ref[g0 + g, pl.ds(fo, SUBLANES), :], s_ref[g0 + g, pl.ds(bo, SUBLANES), :])
                      for g in range(gb)]
            stores = []
            for g in range(gb):
                xr, xi = carry[2 * g], carry[2 * g + 1]
                vf, vb_raw = loaded[g]
                vb = pltpu.roll(vb_raw, half, 0)
                vr = jnp.where(is_fwd, vf[:, :LANES], vb[:, :LANES])
                vi = jnp.where(is_fwd, vf[:, LANES:], vb[:, LANES:])
                ar, ai = ars[g], ais[g]
                yr = ar * xr - ai * xi + vr
                yi = ar * xi + ai * xr + vi
                yrr = pltpu.roll(yr, half, 0)
                yir = pltpu.roll(yi, half, 0)
                zr = ar * yrr - ai * yir + vr
                zi = ar * yir + ai * yrr + vi
                inc = jnp.concatenate([jnp.where(top, xr, yrr), jnp.where(top, xi, yir)], axis=1)
                stores.append((jnp.where(is_fwd2, inc, vf),
                               jnp.where(is_fwd2, vb_raw, pltpu.roll(inc, half, 0))))
                new.append(jnp.where(top, pltpu.roll(zr, half, 0), zr))
                new.append(jnp.where(top, pltpu.roll(zi, half, 0), zi))
            for g in range(gb):
                s_ref[g0 + g, pl.ds(fo, SUBLANES), :] = stores[g][0]
                s_ref[g0 + g, pl.ds(bo, SUBLANES), :] = stores[g][1]
            return tuple(new)

        lax.fori_loop(0, (n_ctx_chunks + n_lat_chunks) // 2, step, tuple(zero for _ in range(2 * gb)))

    for g in range(n_g):
        y = (jnp.dot(u_rows(g, lat_lo, lat_rows).astype(BF16), tt_ref[g], preferred_element_type=F32)
             + lax.dot_general(s_ref[g, lat_lo:lat_lo + lat_rows, :].astype(BF16), w2t_ref[g],
                               (((1,), (1,)), ((), ())), preferred_element_type=F32))
        for jh in range(2):
            u_ref[g, jh, lat_lo:lat_lo + lat_rows, :] = y[:, jh * LANES:(jh + 1) * LANES]

    def emit(cb, carry):
        for b in range(n_b):
            for jh in range(S5_T // SUBLANES):
                w = [u_ref[q, jh, pl.ds((n_ctx_chunks + cb * SUBLANES) * n_b + b, SUBLANES, stride=n_b), :]
                     for q in range(n_g)]
                v = _granule_transpose(w)
                for j in range(SUBLANES):
                    y_ref[b, pl.ds(cb * tok_blk + jh * SUBLANES + j, SUBLANES, stride=S5_T), :] = v[j]
        return carry

    lax.fori_loop(0, n_lat_chunks // SUBLANES, emit, 0)


def _s5_mix(x, ctx, mods, cmods, toep, w1, w2t, a_r, a_i):
    b, l, d = x.shape
    n_ctx = ctx.shape[1]
    n_g = LANES // S5_H
    n_ctx_chunks, n_lat_chunks = n_ctx // S5_T, l // S5_T
    assert b * 2 == SUBLANES and n_ctx_chunks % SUBLANES == 0 and n_lat_chunks % SUBLANES == 0
    rows = (2 * n_ctx_chunks + n_lat_chunks) * b
    w = S5_T * S5_H
    body = functools.partial(_s5_body, n_ctx_chunks=n_ctx_chunks, n_lat_chunks=n_lat_chunks, gb=8)
    lane_tile = lambda i: (0, 0, i)
    wspec = pl.BlockSpec((n_g, w, w), lambda i: (i, 0, 0))
    aspec = pl.BlockSpec((n_g, 1, w // 2), lambda i: (i, 0, 0))
    return pl.pallas_call(
        body,
        grid=(d // LANES,),
        in_specs=[pl.BlockSpec((b, l, LANES), lane_tile, pipeline_mode=pl.Buffered(1)),
                  pl.BlockSpec((b, n_ctx, LANES), lane_tile),
                  pl.BlockSpec((b, 6, LANES), lane_tile),
                  pl.BlockSpec((1, 6, LANES), lane_tile),
                  wspec, wspec, wspec, aspec, aspec],
        out_specs=pl.BlockSpec((b, l, LANES), lane_tile, pipeline_mode=pl.Buffered(1)),
        out_shape=jax.ShapeDtypeStruct((b, l, d), F32),
        scratch_shapes=[pltpu.VMEM((n_g, w // LANES, rows, LANES), F32), pltpu.VMEM((n_g, rows, w), F32)],
        compiler_params=_cparams("arbitrary"),
        name="s5_mix",
    )(x, ctx, mods, cmods, toep, w1, w2t, a_r, a_i)


def _layer_norm(r, g, b):
    mu = jnp.mean(r, axis=-1, keepdims=True)
    xc = r - mu
    var = jnp.mean(xc * xc, axis=-1, keepdims=True)
    return xc * lax.rsqrt(var + LN_EPS) * g + b


def _max2_of4(a, b, c, d):
    h1, l1 = jnp.maximum(a, b), jnp.minimum(a, b)
    h2, l2 = jnp.maximum(c, d), jnp.minimum(c, d)
    return jnp.maximum(h1, h2) + jnp.maximum(jnp.minimum(h1, h2), jnp.maximum(l1, l2))


def _argmax_first(vals):
    idx = jnp.zeros(vals[0].shape, I32)
    best = vals[0]
    for j in range(1, len(vals)):
        upd = vals[j] > best
        idx = jnp.where(upd, j, idx)
        best = jnp.where(upd, vals[j], best)
    return idx, best


def _route(logits_t, count_ref, route_ref):
    n_e, tm = logits_t.shape
    per = n_e // N_EXPERT_GROUPS
    mx = jnp.max(logits_t, axis=0, keepdims=True)
    ex = jnp.exp(logits_t - mx)
    sc = ex / jnp.sum(ex, axis=0, keepdims=True)
    rows = [sc[e:e + 1, :] for e in range(n_e)]
    gscore = [_max2_of4(*rows[per * g:per * (g + 1)]) for g in range(N_EXPERT_GROUPS)]
    best, _ = _argmax_first(gscore)
    vals = []
    for j in range(per):
        v = rows[per * (N_EXPERT_GROUPS - 1) + j]
        for g in range(N_EXPERT_GROUPS - 2, -1, -1):
            v = jnp.where(best == g, rows[per * g + j], v)
        vals.append(v)
    i1, m1 = _argmax_first(vals)
    i2, m2 = _argmax_first([jnp.where(i1 == j, -1.0, vals[j]) for j in range(per)])
    den = m1 + m2
    first_lo = i1 < i2
    lo = jnp.minimum(i1, i2)
    hi = jnp.maximum(i1, i2)
    pair = jnp.where(lo == 0, 0, jnp.where(lo == 1, per - 1, 2 * per - 3)) + hi - lo - 1
    cls = best * PAIRS_PER_GROUP + pair
    w_lo = jnp.where(first_lo, m1, m2) / den
    w_hi = jnp.where(first_lo, m2, m1) / den

    n_cls = count_ref.shape[0]
    hit = lax.broadcasted_iota(I32, (n_cls, tm), 0) == cls
    onehot = jnp.where(hit, 1.0, 0.0)
    src = lax.broadcasted_iota(I32, (tm, tm), 0)
    dst = lax.broadcasted_iota(I32, (tm, tm), 1)
    tri = jnp.where(src <= dst, 1.0, 0.0).astype(BF16)
    cum = jnp.dot(onehot.astype(BF16), tri, preferred_element_type=F32)
    excl = cum - onehot + count_ref[:, 0:1]
    rank = jnp.sum(jnp.where(hit, excl, 0.0), axis=0, keepdims=True)
    count_ref[...] = count_ref[...] + jnp.sum(onehot, axis=1, keepdims=True)

    zero = jnp.zeros((1, tm), F32)
    route_ref[...] = jnp.concatenate([cls.astype(F32), rank, w_lo, w_hi, zero, zero, zero, zero], axis=0)


def _post_mixer_body(m_ref, x_ref, mod_ref, lng_ref, lnb_ref, rwt_ref, rb_ref, *rest, glu, alpha):
    if glu:
        wv_ref, wg_ref, x1_ref, h_ref, route_ref, cnt_out_ref, cnt_ref = rest
        a = jax.nn.gelu(m_ref[0], approximate=True).astype(BF16)
        val = jnp.dot(a, wv_ref[...], preferred_element_type=F32)
        gate = jnp.dot(a, wg_ref[...], preferred_element_type=F32)
        m = val * jax.nn.sigmoid(gate)
    else:
        x1_ref, h_ref, route_ref, cnt_out_ref, cnt_ref = rest
        m = m_ref[0]

    @pl.when((pl.program_id(0) == 0) & (pl.program_id(1) == 0))
    def _():
        cnt_ref[...] = jnp.zeros_like(cnt_ref)

    g1 = mod_ref[0, 2:3, :]
    sh2 = mod_ref[0, 3:4, :]
    sc2 = mod_ref[0, 4:5, :]
    x1 = _layer_norm(alpha * x_ref[0] + g1 * m, lng_ref[...], lnb_ref[...])
    x1_ref[0] = x1
    h = x1 * (1.0 + sc2) + sh2
    h_ref[...] = h
    nt = (((1,), (1,)), ((), ()))
    rw = rwt_ref[...]
    rw_head = rw.astype(BF16)
    rw_rest = (rw - rw_head.astype(F32)).astype(BF16)
    h_head = h.astype(BF16)
    h_rest = (h - h_head.astype(F32)).astype(BF16)
    logits_t = (lax.dot_general(rw_head, h_head, nt, preferred_element_type=F32)
                + lax.dot_general(rw_head, h_rest, nt, preferred_element_type=F32)
                + lax.dot_general(rw_rest, h_head, nt, preferred_element_type=F32)) + rb_ref[...]
    _route(logits_t, cnt_ref, route_ref)
    cnt_out_ref[...] = cnt_ref[...]


def _post_mixer(m, x, mods, ln_g, ln_b, router_w, router_b, alpha, glu_w=None):
    b, l, d = x.shape
    n_e = router_w.shape[1]
    assert n_e == N_EXPERT_GROUPS * EXPERTS_PER_GROUP
    tm = min(l, 512)
    nt = l // tm
    tok = lambda i, j: (i, j, 0)
    const2 = lambda i, j: (0, 0)
    in_specs = [pl.BlockSpec((1, tm, d), tok),
                pl.BlockSpec((1, tm, d), tok),
                pl.BlockSpec((1, 6, d), lambda i, j: (i, 0, 0)),
                pl.BlockSpec((1, d), const2),
                pl.BlockSpec((1, d), const2),
                pl.BlockSpec((n_e, d), const2),
                pl.BlockSpec((n_e, 1), const2)]
    args = [m, x, mods, ln_g.reshape(1, d), ln_b.reshape(1, d), router_w.T, router_b.reshape(n_e, 1)]
    if glu_w is not None:
        in_specs += [pl.BlockSpec((d, d), const2), pl.BlockSpec((d, d), const2)]
        args += [glu_w[0].astype(BF16), glu_w[1].astype(BF16)]
    out_shape = [jax.ShapeDtypeStruct((b, l, d), F32),
                 jax.ShapeDtypeStruct((b * l, d), F32),
                 jax.ShapeDtypeStruct((ROUTE_ROWS, b * l), F32),
                 jax.ShapeDtypeStruct((N_PAIR_CLASSES, LANES), F32)]
    out_specs = [pl.BlockSpec((1, tm, d), tok),
                 pl.BlockSpec((tm, d), lambda i, j: (i * nt + j, 0)),
                 pl.BlockSpec((ROUTE_ROWS, tm), lambda i, j: (0, i * nt + j)),
                 pl.BlockSpec((N_PAIR_CLASSES, LANES), const2)]
    return pl.pallas_call(
        functools.partial(_post_mixer_body, glu=glu_w is not None, alpha=alpha),
        grid=(b, nt),
        in_specs=in_specs,
        out_specs=out_specs,
        out_shape=out_shape,
        scratch_shapes=[pltpu.VMEM((N_PAIR_CLASSES, LANES), F32)],
        compiler_params=_cparams("arbitrary", "arbitrary"),
        name="post_mixer_glu" if glu_w is not None else "post_mixer",
    )(*args)


def _take(table, idx):
    ids = jnp.arange(table.shape[0], dtype=I32)
    return jnp.sum(jnp.where(idx[:, None] == ids[None, :], table[None, :], 0), axis=1)


def _pass_segments():
    seg_cls, seg_hi = [], []
    pairs = [(a, b) for a in range(EXPERTS_PER_GROUP) for b in range(a + 1, EXPERTS_PER_GROUP)]
    for g in range(N_EXPERT_GROUPS):
        for m in range(EXPERTS_PER_GROUP):
            for idx, (a, b) in enumerate(pairs):
                if m in (a, b):
                    seg_cls.append(g * PAIRS_PER_GROUP + idx)
                    seg_hi.append(int(m == b))
    return seg_cls, seg_hi


def _dispatch_plan(route, counts, tile):
    n_cls = counts.shape[0]
    n_tok = route.shape[1]
    cnt = counts[:, 0].astype(I32)
    tiles = (cnt + tile - 1) // tile
    tile_end = jnp.cumsum(tiles)
    tile_off = tile_end - tiles
    cids = jnp.arange(n_cls, dtype=I32)[:, None]
    cls = route[0].astype(I32)
    pos = jnp.sum(jnp.where(cls[None, :] == cids, (tile_off * tile)[:, None], 0), axis=0) + route[1].astype(I32)

    n_row_tiles = n_tok // tile + n_cls
    seg_cls, seg_hi = _pass_segments()
    seg_cls = jnp.asarray(seg_cls, I32)
    seg_hi = jnp.asarray(seg_hi, I32)
    seg_per_expert = EXPERTS_PER_GROUP - 1
    seg_tiles = _take(tiles, seg_cls)
    seg_end = jnp.cumsum(seg_tiles)
    n_used = seg_end[-1]
    p = jnp.arange(2 * n_row_tiles, dtype=I32)
    seg = jnp.minimum(jnp.sum((seg_end[None, :] <= p[:, None]).astype(I32), axis=1), seg_cls.shape[0] - 1)
    within = p - _take(seg_end - seg_tiles, seg)
    spare = p - n_used
    used = p < n_used
    pass_tile = jnp.where(used, _take(tile_off, _take(seg_cls, seg)) + within, tile_end[-1] + spare // 2).astype(I32)
    pass_hi = jnp.where(used, _take(seg_hi, seg), spare % 2).astype(I32)
    pass_expert = jnp.where(used, seg // seg_per_expert, N_EXPERT_GROUPS * EXPERTS_PER_GROUP - 1).astype(I32)
    pass_rows = jnp.where(used, jnp.clip(_take(_take(cnt, seg_cls), seg) - within * tile, 0, tile), 0)
    last_tile = (tile_end - 1).astype(I32)
    return (pos.astype(I32), pass_tile, pass_hi, pass_expert, pass_rows.astype(I32),
            last_tile, tiles.astype(I32), tile_end[-1:].astype(I32), n_row_tiles)


def _zero_tiles(last_ref, tiles_ref, nu_ref, zero_ref, hs_ref, sem, n_e, tile, wait):
    for e in range(n_e):
        @pl.when(tiles_ref[e] > 0)
        def _(e=e):
            start = pl.multiple_of(last_ref[e] * tile, tile)
            cp = pltpu.make_async_copy(zero_ref, hs_ref.at[pl.ds(start, tile)], sem)
            if wait:
                cp.wait()
            else:
                cp.start()

    def body(j, carry):
        start = pl.multiple_of(j * tile, tile)
        cp = pltpu.make_async_copy(zero_ref, hs_ref.at[pl.ds(start, tile)], sem)
        if wait:
            cp.wait()
        else:
            cp.start()
        return carry

    lax.fori_loop(nu_ref[0], hs_ref.shape[0] // tile, body, 0)


def _dispatch_body(pos_ref, last_ref, tiles_ref, nu_ref, h_ref, hs_ref, zero_ref, sem_z, sem, *,
                   tm, tile, n_e):
    i = pl.program_id(0)

    @pl.when(i == 0)
    def _():
        zero_ref[...] = jnp.zeros_like(zero_ref)
        _zero_tiles(last_ref, tiles_ref, nu_ref, zero_ref, hs_ref, sem_z, n_e, tile, False)
        _zero_tiles(last_ref, tiles_ref, nu_ref, zero_ref, hs_ref, sem_z, n_e, tile, True)

    base = i * tm

    def issue(r2, carry):
        for k in range(2):
            r = 2 * r2 + k
            pltpu.make_async_copy(h_ref.at[pl.ds(r, 1)], hs_ref.at[pl.ds(pos_ref[base + r], 1)],
                                  sem).start(priority=k)
        return carry

    lax.fori_loop(0, tm // 2, issue, 0, unroll=8)
    pltpu.make_async_copy(h_ref, hs_ref.at[pl.ds(0, tm)], sem).wait()


def _dispatch(h_rows, pos, last_tile, tiles, n_used, n_rows, tile):
    n, d = h_rows.shape
    n_e = tiles.shape[0]
    tm = min(n, 1024)
    grid_spec = pltpu.PrefetchScalarGridSpec(
        num_scalar_prefetch=4,
        grid=(n // tm,),
        in_specs=[pl.BlockSpec((tm, d), lambda i, *_: (i, 0))],
        out_specs=pl.BlockSpec(memory_space=pl.ANY),
        scratch_shapes=[pltpu.VMEM((tile, d), F32),
                        pltpu.SemaphoreType.DMA(()), pltpu.SemaphoreType.DMA(())],
    )
    return pl.pallas_call(
        functools.partial(_dispatch_body, tm=tm, tile=tile, n_e=n_e),
        grid_spec=grid_spec,
        out_shape=jax.ShapeDtypeStruct((n_rows, d), F32),
        compiler_params=_cparams("arbitrary"),
        name="moe_dispatch",
    )(pos, last_tile, tiles, n_used, h_rows)


def _expert_body(pt_ref, ph_ref, te_ref, pr_ref, hs_ref, wg_ref, wu_ref, wd_ref, ys_ref, wgb, wub, wdb):
    i = pl.program_id(0)
    rows = pr_ref[i]
    half = hs_ref.shape[0] // 2
    first = jnp.logical_or(i == 0, te_ref[i] != te_ref[jnp.maximum(i - 1, 0)])

    @pl.when(jnp.logical_and(rows > 0, first))
    def _():
        wgb[...] = wg_ref[0, 0].astype(BF16)
        wub[...] = wu_ref[0, 0].astype(BF16)
        wdb[...] = wd_ref[0, 0].astype(BF16)

    def ffn(n):
        x = hs_ref[0:n, :].astype(BF16)
        gate = jnp.dot(x, wgb[...], preferred_element_type=F32)
        up = jnp.dot(x, wub[...], preferred_element_type=F32)
        a = (gate * jax.nn.sigmoid(gate) * up).astype(BF16)
        ys_ref[0:n, :] = jnp.dot(a, wdb[...], preferred_element_type=F32)

    @pl.when(rows > half)
    def _():
        ffn(2 * half)

    @pl.when(jnp.logical_and(rows > 0, rows <= half))
    def _():
        ffn(half)
        ys_ref[half:, :] = jnp.zeros((half, ys_ref.shape[1]), F32)

    @pl.when(rows == 0)
    def _():
        ys_ref[...] = jnp.zeros_like(ys_ref)


def _expert_ffn(hs, pass_tile, pass_hi, pass_expert, pass_rows, w_gate, w_up, w_down, layer, tile):
    _, n_e, d, f = w_gate.shape
    n_rows = hs.shape[0]
    wmap = lambda i, pt, ph, te, nu: (layer, te[i], 0, 0)
    grid_spec = pltpu.PrefetchScalarGridSpec(
        num_scalar_prefetch=4,
        grid=(pass_tile.shape[0],),
        in_specs=[pl.BlockSpec((tile, d), lambda i, pt, ph, te, nu: (pt[i], 0)),
                  pl.BlockSpec((1, 1, d, f), wmap),
                  pl.BlockSpec((1, 1, d, f), wmap),
                  pl.BlockSpec((1, 1, f, d), wmap)],
        out_specs=pl.BlockSpec((tile, d), lambda i, pt, ph, te, nu: (pt[i], ph[i])),
        scratch_shapes=[pltpu.VMEM((d, f), BF16), pltpu.VMEM((d, f), BF16), pltpu.VMEM((f, d), BF16)],
    )
    return pl.pallas_call(
        _expert_body,
        grid_spec=grid_spec,
        out_shape=jax.ShapeDtypeStruct((n_rows, 2 * d), F32),
        compiler_params=_cparams("arbitrary"),
        name="moe_experts",
    )(pass_tile, pass_hi, pass_expert, pass_rows, hs, w_gate, w_up, w_down)


def _combine_body(pos_ref, x_ref, mod_ref, wts_ref, lng_ref, lnb_ref, ys_ref, *rest,
                  tm, alpha, next_mod):
    if next_mod:
        nmod_ref, x2_ref, h_ref, buf, sem = rest
    else:
        x2_ref, buf, sem = rest
    step = pl.program_id(0) * pl.num_programs(1) + pl.program_id(1)
    n_steps = pl.num_programs(0) * pl.num_programs(1)
    d = x_ref.shape[2]

    def gather(s, slot):
        def issue(r2, carry):
            for k in range(2):
                r = 2 * r2 + k
                pltpu.make_async_copy(ys_ref.at[pl.ds(pos_ref[s * tm + r], 1)],
                                      buf.at[slot, pl.ds(r, 1)], sem.at[slot]).start(priority=k)
            return carry

        lax.fori_loop(0, tm // 2, issue, 0, unroll=8)

    @pl.when(step == 0)
    def _():
        gather(0, 0)

    @pl.when(step + 1 < n_steps)
    def _():
        gather(step + 1, (step + 1) % 2)

    slot = step % 2
    pltpu.make_async_copy(ys_ref.at[pl.ds(0, tm)], buf.at[slot], sem.at[slot]).wait()

    w = wts_ref[...]
    rows = buf[slot]
    moe = w[:, 0:1] * rows[:, :d] + w[:, 1:2] * rows[:, d:]
    g2 = mod_ref[0, 5:6, :]
    x2 = _layer_norm(alpha * x_ref[0] + g2 * moe, lng_ref[...], lnb_ref[...])
    x2_ref[0] = x2
    if next_mod:
        h_ref[0] = x2 * (1.0 + nmod_ref[0, 1:2, :]) + nmod_ref[0, 0:1, :]


def _combine(ys, pos, wts, x, mods, ln_g, ln_b, alpha, next_mods=None):
    b, l, d = x.shape
    tm = min(l, 512)
    nt = l // tm
    tok = lambda i, j, *_: (i, j, 0)
    bat = lambda i, j, *_: (i, 0, 0)
    const2 = lambda i, j, *_: (0, 0)
    in_specs = [pl.BlockSpec((1, tm, d), tok),
                pl.BlockSpec((1, 6, d), bat),
                pl.BlockSpec((tm, 2), lambda i, j, *_: (i * nt + j, 0)),
                pl.BlockSpec((1, d), const2),
                pl.BlockSpec((1, d), const2),
                pl.BlockSpec(memory_space=pl.ANY)]
    args = [x, mods, wts, ln_g.reshape(1, d), ln_b.reshape(1, d), ys]
    out_shape = [jax.ShapeDtypeStruct((b, l, d), F32)]
    out_specs = [pl.BlockSpec((1, tm, d), tok)]
    if next_mods is not None:
        in_specs.append(pl.BlockSpec((1, 6, d), bat))
        args.append(next_mods)
        out_shape.append(jax.ShapeDtypeStruct((b, l, d), F32))
        out_specs.append(pl.BlockSpec((1, tm, d), tok))
    grid_spec = pltpu.PrefetchScalarGridSpec(
        num_scalar_prefetch=1,
        grid=(b, nt),
        in_specs=in_specs,
        out_specs=out_specs,
        scratch_shapes=[pltpu.VMEM((2, tm, 2 * d), F32), pltpu.SemaphoreType.DMA((2,))],
    )
    return pl.pallas_call(
        functools.partial(_combine_body, tm=tm, alpha=alpha, next_mod=next_mods is not None),
        grid_spec=grid_spec,
        out_shape=out_shape,
        compiler_params=_cparams("arbitrary", "arbitrary"),
        name="moe_combine",
    )(pos, *args)


MOE_TILE = 256


def _moe(h_rows, route, counts, w_gate, w_up, w_down, layer):
    n = h_rows.shape[0]
    tile = min(n, MOE_TILE)
    (pos, pass_tile, pass_hi, pass_expert, pass_rows, last_tile, tiles, n_used_tiles,
     n_row_tiles) = _dispatch_plan(route, counts, tile)
    hs = _dispatch(h_rows, pos, last_tile, tiles, n_used_tiles, n_row_tiles * tile, tile)
    ys = _expert_ffn(hs, pass_tile, pass_hi, pass_expert, pass_rows, w_gate, w_up, w_down, layer, tile)
    return ys, pos, jnp.transpose(route[2:4])


def _pool_group(h_ref, w_ref, sc_ref, o_ref, col_ref, k, n_rows):
    n = n_rows * GRID_W
    c = h_ref.shape[2]
    blk = 4 * GRID_W
    half = k // 2
    pad = half * GRID_W
    ti = lax.broadcasted_iota(I32, (blk, blk), 0)
    si = lax.broadcasted_iota(I32, (blk, blk), 1)
    shift = GRID_W.bit_length() - 1
    same_row = (ti >> shift) == (si >> shift)
    band = jnp.where(same_row & (si - ti >= -half) & (si - ti <= half - 1), 1.0, 0.0).astype(BF16)
    col_ref[0:pad, :] = jnp.zeros((pad, c), F32)
    col_ref[pad + n:pad + n + pad, :] = jnp.zeros((pad, c), F32)
    for b0 in range(0, n, blk):
        hb = h_ref[0, b0:b0 + blk, :]
        head = hb.astype(BF16)
        rest = (hb - head.astype(F32)).astype(BF16)
        col_ref[pad + b0:pad + b0 + blk, :] = (jnp.dot(band, head, preferred_element_type=F32)
                                               + jnp.dot(band, rest, preferred_element_type=F32))
    acc = col_ref[0:n, :]
    for j in range(1, k):
        acc = acc + col_ref[j * GRID_W:j * GRID_W + n, :]
    t = lax.broadcasted_iota(I32, (n, 1), 0)
    wc = t & (GRID_W - 1)
    wr = t >> shift
    cnt_c = jnp.minimum(wc + half - 1, GRID_W - 1) - jnp.maximum(wc - half, 0) + 1
    cnt_r = jnp.minimum(wr + half - 1, n_rows - 1) - jnp.maximum(wr - half, 0) + 1
    mean = acc / (cnt_c * cnt_r).astype(F32)
    pooled = (mean - h_ref[0]).astype(BF16)
    o_ref[0] = jnp.dot(pooled, w_ref[0].astype(BF16), preferred_element_type=F32) * sc_ref[...]


def _pool_body(h_ref, w_ref, sc_ref, o_ref, col_ref, *, n_rows):
    g = pl.program_id(1)
    for gi, k in enumerate(POOL_WINDOWS):
        @pl.when(g == gi)
        def _(k=k):
            _pool_group(h_ref, w_ref, sc_ref, o_ref, col_ref, k, n_rows)


def _pool_mix(h, w_grp, scale):
    b, n, d = h.shape
    n_g, c, _ = w_grp.shape
    n_rows = n // GRID_W
    pad = (max(POOL_WINDOWS) // 2) * GRID_W
    return pl.pallas_call(
        functools.partial(_pool_body, n_rows=n_rows),
        grid=(b, n_g),
        in_specs=[pl.BlockSpec((1, n, c), lambda i, j: (i, 0, j)),
                  pl.BlockSpec((1, c, c), lambda i, j: (j, 0, 0)),
                  pl.BlockSpec((1, c), lambda i, j: (0, j))],
        out_specs=pl.BlockSpec((1, n, c), lambda i, j: (i, 0, j)),
        out_shape=jax.ShapeDtypeStruct((b, n, d), F32),
        scratch_shapes=[pltpu.VMEM((n + 2 * pad, c), F32)],
        compiler_params=_cparams("arbitrary", "arbitrary"),
        name="pool_mix",
    )(h, w_grp, scale.reshape(1, d))


def kernel(x, c, ctx, c_ctx, mod_w, mod_b, ln_g, ln_b, s5_lam_re, s5_lam_im, s5_log_dt, s5_b_re, s5_b_im,
           s5_c_re, s5_c_im, s5_d, s5_w_val, s5_w_gate, pool_w, pool_scale, router_w, router_b,
           moe_w_gate, moe_w_up, moe_w_down):
    b, l, d = x.shape
    depth = mod_w.shape[0]
    assert depth == 2 and b + 1 <= SUBLANES and d % LANES == 0 and GRID_W & (GRID_W - 1) == 0
    alpha = (2 * depth) ** 0.25

    cond = jnp.zeros((SUBLANES, d), F32).at[:b].set(c).at[b].set(c_ctx)
    mods = _modulation(cond, mod_w, mod_b).reshape(depth, SUBLANES, 6, d)

    toep, w1, w2t, a_r, a_i = _s5_weights(s5_lam_re[0], s5_lam_im[0], s5_log_dt[0], s5_b_re[0], s5_b_im[0],
                                          s5_c_re[0], s5_c_im[0], s5_d[0])
    y = _s5_mix(x, ctx, mods[0, :b], mods[0, b:b + 1], toep, w1, w2t, a_r, a_i)
    x1, h_rows, route, counts = _post_mixer(y, x, mods[0, :b], ln_g[0, 0], ln_b[0, 0], router_w, router_b,
                                            alpha, glu_w=(s5_w_val[0], s5_w_gate[0]))
    ys, pos, wts = _moe(h_rows, route, counts, moe_w_gate, moe_w_up, moe_w_down, 0)
    x2, h = _combine(ys, pos, wts, x1, mods[0, :b], ln_g[0, 1], ln_b[0, 1], alpha, next_mods=mods[1, :b])

    m = _pool_mix(h, pool_w[0], pool_scale[0])
    x3, h_rows, route, counts = _post_mixer(m, x2, mods[1, :b], ln_g[1, 0], ln_b[1, 0], router_w, router_b,
                                            alpha)
    ys, pos, wts = _moe(h_rows, route, counts, moe_w_gate, moe_w_up, moe_w_down, 1)
    (out,) = _combine(ys, pos, wts, x3, mods[1, :b], ln_g[1, 1], ln_b[1, 1], alpha)
    return out
```

```python
import functools

import jax
import jax.numpy as jnp
from jax import lax
from jax.experimental import pallas as pl
from jax.experimental.pallas import tpu as pltpu

F32 = jnp.float32
BF16 = jnp.bfloat16
I32 = jnp.int32
HI = lax.Precision.HIGHEST

GRID_W = 64
S5_H = 16
S5_T = 16
POOL_WINDOWS = (2, 4, 8, 16)
N_EXPERT_GROUPS = 4
LN_EPS = 1e-5
LANES = 128
SUBLANES = 8
VMEM_LIMIT = 52 * 1024 * 1024

EXPERTS_PER_GROUP = 4
PAIRS_PER_GROUP = EXPERTS_PER_GROUP * (EXPERTS_PER_GROUP - 1) // 2
N_PAIR_CLASSES = N_EXPERT_GROUPS * PAIRS_PER_GROUP
ROUTE_ROWS = 8


def _cparams(*sem):
    return pltpu.CompilerParams(dimension_semantics=sem, vmem_limit_bytes=VMEM_LIMIT)


def _mod_body(c_ref, w_ref, b_ref, o_ref):
    c = c_ref[...]
    s = c * jax.nn.sigmoid(c)
    o_ref[0] = jnp.dot(s, w_ref[0], precision=HI, preferred_element_type=F32) + b_ref[0]


def _modulation(cond, mod_w, mod_b):
    depth, d, n6 = mod_w.shape
    tn = min(n6, 1536)
    return pl.pallas_call(
        _mod_body,
        grid=(depth, n6 // tn),
        in_specs=[pl.BlockSpec((SUBLANES, d), lambda i, j: (0, 0)),
                  pl.BlockSpec((1, d, tn), lambda i, j: (i, 0, j)),
                  pl.BlockSpec((1, 1, tn), lambda i, j: (i, 0, j))],
        out_specs=pl.BlockSpec((1, SUBLANES, tn), lambda i, j: (i, 0, j)),
        out_shape=jax.ShapeDtypeStruct((depth, SUBLANES, n6), F32),
        compiler_params=_cparams("arbitrary", "arbitrary"),
        name="modulation",
    )(cond, mod_w, mod_b.reshape(depth, 1, n6))


def _s5_direction_terms(lam_re, lam_im, log_dt, b_re, b_im):
    lr = lam_re.astype(F32)
    li = lam_im.astype(F32)
    dt = jnp.exp(log_dt.astype(F32))[:, None]
    mag = jnp.exp(lr * dt)
    ar = mag * jnp.cos(li * dt)
    ai = mag * jnp.sin(li * dt)
    den = lr * lr + li * li
    nr = ar - 1.0
    fr = (nr * lr + ai * li) / den
    fi = (ai * lr - nr * li) / den
    br_, bi_ = b_re.astype(F32), b_im.astype(F32)
    bbr = fr[..., None] * br_ - fi[..., None] * bi_
    bbi = fr[..., None] * bi_ + fi[..., None] * br_
    k = jnp.arange(S5_T + 1, dtype=F32)[:, None, None]
    pm = jnp.exp(k * (lr * dt))
    pr = pm * jnp.cos(k * (li * dt))
    pi = pm * jnp.sin(k * (li * dt))
    return pr, pi, bbr, bbi


def _s5_weights(lam_re, lam_im, log_dt, b_re, b_im, c_re, c_im, d_skip):
    t = S5_T
    g, p = lam_re.shape[1:]
    h = b_re.shape[-1]
    terms = [_s5_direction_terms(lam_re[d], lam_im[d], log_dt[d], b_re[d], b_im[d]) for d in (0, 1)]
    pw = jnp.stack([jnp.transpose(terms[d][k], (1, 0, 2)) for d in (0, 1) for k in (0, 1)], axis=1)
    bt = jnp.stack([jnp.transpose(terms[d][k], (0, 2, 1)) for d in (0, 1) for k in (2, 3)], axis=1)
    cc = jnp.stack([c[d].astype(F32) for d in (0, 1) for c in (c_re, c_im)], axis=1)
    gp = 8
    spec4 = lambda rows: pl.BlockSpec((gp, 4, rows, p), lambda i: (i, 0, 0, 0))
    wide = pl.BlockSpec((gp, t * h, t * h), lambda i: (i, 0, 0))
    toep, w1, w2t = pl.pallas_call(
        functools.partial(_s5_prep_body, gp=gp),
        grid=(g // gp,),
        in_specs=[spec4(t + 1), spec4(h), spec4(h), pl.BlockSpec((gp, 1, h), lambda i: (i, 0, 0))],
        out_specs=[wide, wide, wide],
        out_shape=[jax.ShapeDtypeStruct((g, t * h, t * h), BF16) for _ in range(3)],
        compiler_params=_cparams("arbitrary"),
        name="s5_prep",
    )(pw, bt, cc, d_skip.astype(F32).reshape(g, 1, h))
    a_r = jnp.concatenate([pw[:, 0, t], pw[:, 2, t]], axis=-1)[:, None, :]
    a_i = jnp.concatenate([pw[:, 1, t], pw[:, 3, t]], axis=-1)[:, None, :]
    return toep, w1, w2t, a_r, a_i


def _cmul(ar, ai, br, bi):
    return ar * br - ai * bi, ar * bi + ai * br


def _s5_prep_body(pw_ref, bt_ref, cc_ref, d_ref, toep_ref, w1_ref, w2t_ref, *, gp):
    t = pw_ref.shape[2] - 1
    h = bt_ref.shape[2]
    nt = (((1,), (1,)), ((), ()))
    eye = (lax.broadcasted_iota(I32, (h, h), 0) == lax.broadcasted_iota(I32, (h, h), 1))
    for g in range(gp):
        prf, pif, prb, pib = (pw_ref[g, k] for k in range(4))
        btf = (bt_ref[g, 0], bt_ref[g, 1])
        btb = (bt_ref[g, 2], bt_ref[g, 3])
        ccf = (cc_ref[g, 0], cc_ref[g, 1])
        ccb = (cc_ref[g, 2], cc_ref[g, 3])
        caf, cab = [], []
        for j in range(t):
            f_r, f_i = _cmul(*btf, prf[t - 1 - j:t - j], pif[t - 1 - j:t - j])
            b_r, b_i = _cmul(*btb, prb[j:j + 1], pib[j:j + 1])
            w1_ref[g, j * h:(j + 1) * h, :] = jnp.concatenate([f_r, b_r, f_i, b_i], axis=1).astype(BF16)
            mf_r, mf_i = _cmul(*ccf, prf[j + 1:j + 2], pif[j + 1:j + 2])
            mb_r, mb_i = _cmul(*ccb, prb[t - j:t - j + 1], pib[t - j:t - j + 1])
            w2t_ref[g, j * h:(j + 1) * h, :] = jnp.concatenate([mf_r, mb_r, -mf_i, -mb_i],
                                                               axis=1).astype(BF16)
            caf.append(_cmul(*ccf, prf[j:j + 1], pif[j:j + 1]))
            cab.append(_cmul(*ccb, prb[t - 1 - j:t - j], pib[t - 1 - j:t - j]))

        def lag_kernels(btx, ca):
            car = jnp.concatenate([c[0] for c in ca], axis=0)
            cai = jnp.concatenate([c[1] for c in ca], axis=0)
            return (lax.dot_general(btx[0], car, nt, precision=HI, preferred_element_type=F32)
                    - lax.dot_general(btx[1], cai, nt, precision=HI, preferred_element_type=F32))

        kf = lag_kernels(btf, caf)
        kb = lag_kernels(btb, cab)
        skip = jnp.where(eye, jnp.broadcast_to(d_ref[g], (h, h)), 0.0)
        mid = kb[:, (t - 1) * h:] + kf[:, :h] + skip
        kwide = jnp.concatenate([kb[:, :(t - 1) * h], mid, kf[:, h:], jnp.zeros((h, h), F32)], axis=1)
        for i in range(t):
            off = (t - 1 - i) * h
            toep_ref[g, i * h:(i + 1) * h, :] = kwide[:, off:off + t * h].astype(BF16)


def _granule_transpose(v):
    n = len(v)
    gran = lax.broadcasted_iota(I32, v[0].shape, 1) >> 4
    at = [gran == q for q in range(n)]
    rot = []
    for d in range(n):
        m = v[d]
        for q in range(1, n):
            m = jnp.where(at[q], v[(q + d) % n], m)
        rot.append(pltpu.roll(m, d * S5_H, 1) if d else m)
    out = []
    for q in range(n):
        w = rot[(-q) % n]
        for j in range(1, n):
            w = jnp.where(at[j], rot[(j - q) % n], w)
        out.append(w)
    return out


def _s5_body(x_ref, ctx_ref, mod_ref, cmod_ref, tt_ref, w1_ref, w2t_ref, ar_ref, ai_ref, y_ref,
             u_ref, s_ref, *, n_ctx_chunks, n_lat_chunks, gb):
    n_b = x_ref.shape[0]
    n_g = u_ref.shape[0]
    n_blocks = s_ref.shape[1] // SUBLANES
    half = SUBLANES // 2
    tok_blk = SUBLANES * S5_T
    lat_lo = n_ctx_chunks * n_b
    lat_rows = n_lat_chunks * n_b

    def chunk_rows(ref, b, tok0, shift, scale1):
        vs = [ref[b, pl.ds(tok0 + j, SUBLANES, stride=S5_T), :] * scale1 + shift for j in range(S5_T)]
        lo = _granule_transpose(vs[:SUBLANES])
        hi = _granule_transpose(vs[SUBLANES:])
        return lo, hi

    def put_rows(rows, chunk0, b):
        for q in range(n_g):
            for jh in range(2):
                u_ref[q, jh, pl.ds(chunk0 * n_b + b, SUBLANES, stride=n_b), :] = rows[jh][q]

    def u_rows(g, lo, n):
        return jnp.concatenate([u_ref[g, 0, lo:lo + n, :], u_ref[g, 1, lo:lo + n, :]], axis=1)

    shift = jnp.broadcast_to(cmod_ref[0, 0:1, :], (SUBLANES, LANES))
    scale1 = 1.0 + jnp.broadcast_to(cmod_ref[0, 1:2, :], (SUBLANES, LANES))
    for b in range(n_b):
        for cb in range(n_ctx_chunks // SUBLANES):
            rows = chunk_rows(ctx_ref, b, cb * tok_blk, shift, scale1)
            put_rows(rows, cb * SUBLANES, b)
            put_rows(rows, n_ctx_chunks + n_lat_chunks + cb * SUBLANES, b)

    def fill(cb, carry):
        for b in range(n_b):
            shift = jnp.broadcast_to(mod_ref[b, 0:1, :], (SUBLANES, LANES))
            scale1 = 1.0 + jnp.broadcast_to(mod_ref[b, 1:2, :], (SUBLANES, LANES))
            put_rows(chunk_rows(x_ref, b, cb * tok_blk, shift, scale1), n_ctx_chunks + cb * SUBLANES, b)
        return carry

    lax.fori_loop(0, n_lat_chunks // SUBLANES, fill, 0)

    for g in range(n_g):
        s_ref[g] = jnp.dot(u_rows(g, 0, n_blocks * SUBLANES).astype(BF16), w1_ref[g],
                           preferred_element_type=F32)

    lane = lax.broadcasted_iota(I32, (SUBLANES, LANES), 1)
    row = lax.broadcasted_iota(I32, (SUBLANES, LANES), 0)
    is_fwd = lane < LANES // 2
    is_fwd2 = jnp.concatenate([is_fwd, is_fwd], axis=1)
    top = row < half
    zero = jnp.zeros((SUBLANES, LANES), F32)
    for g0 in range(0, n_g, gb):
        ars = [jnp.broadcast_to(ar_ref[g0 + g], (SUBLANES, LANES)) for g in range(gb)]
        ais = [jnp.broadcast_to(ai_ref[g0 + g], (SUBLANES, LANES)) for g in range(gb)]

        def step(k, carry, g0=g0, ars=ars, ais=ais):
            fo = pl.multiple_of(k * SUBLANES, SUBLANES)
            bo = pl.multiple_of((n_blocks - 1 - k) * SUBLANES, SUBLANES)
            new = []
            loaded = [(s_ref[g0 + g, pl.ds(fo, SUBLANES), :], s_ref[g0 + g, pl.ds(bo, SUBLANES), :])
                      for g in range(gb)]
            stores = []
            for g in range(gb):
                xr, xi = carry[2 * g], carry[2 * g + 1]
                vf, vb_raw = loaded[g]
                vb = pltpu.roll(vb_raw, half, 0)
                vr = jnp.where(is_fwd, vf[:, :LANES], vb[:, :LANES])
                vi = jnp.where(is_fwd, vf[:, LANES:], vb[:, LANES:])
                ar, ai = ars[g], ais[g]
                yr = ar * xr - ai * xi + vr
                yi = ar * xi + ai * xr + vi
                yrr = pltpu.roll(yr, half, 0)
                yir = pltpu.roll(yi, half, 0)
                zr = ar * yrr - ai * yir + vr
                zi = ar * yir + ai * yrr + vi
                inc = jnp.concatenate([jnp.where(top, xr, yrr), jnp.where(top, xi, yir)], axis=1)
                stores.append((jnp.where(is_fwd2, inc, vf),
                               jnp.where(is_fwd2, vb_raw, pltpu.roll(inc, half, 0))))
                new.append(jnp.where(top, pltpu.roll(zr, half, 0), zr))
                new.append(jnp.where(top, pltpu.roll(zi, half, 0), zi))
            for g in range(gb):
                s_ref[g0 + g, pl.ds(fo, SUBLANES), :] = stores[g][0]
                s_ref[g0 + g, pl.ds(bo, SUBLANES), :] = stores[g][1]
            return tuple(new)

        lax.fori_loop(0, (n_ctx_chunks + n_lat_chunks) // 2, step, tuple(zero for _ in range(2 * gb)))

    for g in range(n_g):
        y = (jnp.dot(u_rows(g, lat_lo, lat_rows).astype(BF16), tt_ref[g], preferred_element_type=F32)
             + lax.dot_general(s_ref[g, lat_lo:lat_lo + lat_rows, :].astype(BF16), w2t_ref[g],
                               (((1,), (1,)), ((), ())), preferred_element_type=F32))
        for jh in range(2):
            u_ref[g, jh, lat_lo:lat_lo + lat_rows, :] = y[:, jh * LANES:(jh + 1) * LANES]

    def emit(cb, carry):
        for b in range(n_b):
            for jh in range(S5_T // SUBLANES):
                w = [u_ref[q, jh, pl.ds((n_ctx_chunks + cb * SUBLANES) * n_b + b, SUBLANES, stride=n_b), :]
                     for q in range(n_g)]
                v = _granule_transpose(w)
                for j in range(SUBLANES):
                    y_ref[b, pl.ds(cb * tok_blk + jh * SUBLANES + j, SUBLANES, stride=S5_T), :] = v[j]
        return carry

    lax.fori_loop(0, n_lat_chunks // SUBLANES, emit, 0)


def _s5_mix(x, ctx, mods, cmods, toep, w1, w2t, a_r, a_i):
    b, l, d = x.shape
    n_ctx = ctx.shape[1]
    n_g = LANES // S5_H
    n_ctx_chunks, n_lat_chunks = n_ctx // S5_T, l // S5_T
    assert b * 2 == SUBLANES and n_ctx_chunks % SUBLANES == 0 and n_lat_chunks % SUBLANES == 0
    rows = (2 * n_ctx_chunks + n_lat_chunks) * b
    w = S5_T * S5_H
    body = functools.partial(_s5_body, n_ctx_chunks=n_ctx_chunks, n_lat_chunks=n_lat_chunks, gb=8)
    lane_tile = lambda i: (0, 0, i)
    wspec = pl.BlockSpec((n_g, w, w), lambda i: (i, 0, 0))
    aspec = pl.BlockSpec((n_g, 1, w // 2), lambda i: (i, 0, 0))
    return pl.pallas_call(
        body,
        grid=(d // LANES,),
        in_specs=[pl.BlockSpec((b, l, LANES), lane_tile, pipeline_mode=pl.Buffered(1)),
                  pl.BlockSpec((b, n_ctx, LANES), lane_tile),
                  pl.BlockSpec((b, 6, LANES), lane_tile),
                  pl.BlockSpec((1, 6, LANES), lane_tile),
                  wspec, wspec, wspec, aspec, aspec],
        out_specs=pl.BlockSpec((b, l, LANES), lane_tile, pipeline_mode=pl.Buffered(1)),
        out_shape=jax.ShapeDtypeStruct((b, l, d), F32),
        scratch_shapes=[pltpu.VMEM((n_g, w // LANES, rows, LANES), F32), pltpu.VMEM((n_g, rows, w), F32)],
        compiler_params=_cparams("arbitrary"),
        name="s5_mix",
    )(x, ctx, mods, cmods, toep, w1, w2t, a_r, a_i)


def _layer_norm(r, g, b):
    mu = jnp.mean(r, axis=-1, keepdims=True)
    xc = r - mu
    var = jnp.mean(xc * xc, axis=-1, keepdims=True)
    return xc * lax.rsqrt(var + LN_EPS) * g + b


def _max2_of4(a, b, c, d):
    h1, l1 = jnp.maximum(a, b), jnp.minimum(a, b)
    h2, l2 = jnp.maximum(c, d), jnp.minimum(c, d)
    return jnp.maximum(h1, h2) + jnp.maximum(jnp.minimum(h1, h2), jnp.maximum(l1, l2))


def _argmax_first(vals):
    idx = jnp.zeros(vals[0].shape, I32)
    best = vals[0]
    for j in range(1, len(vals)):
        upd = vals[j] > best
        idx = jnp.where(upd, j, idx)
        best = jnp.where(upd, vals[j], best)
    return idx, best


def _route(logits_t, count_ref, route_ref):
    n_e, tm = logits_t.shape
    per = n_e // N_EXPERT_GROUPS
    mx = jnp.max(logits_t, axis=0, keepdims=True)
    ex = jnp.exp(logits_t - mx)
    sc = ex / jnp.sum(ex, axis=0, keepdims=True)
    rows = [sc[e:e + 1, :] for e in range(n_e)]
    gscore = [_max2_of4(*rows[per * g:per * (g + 1)]) for g in range(N_EXPERT_GROUPS)]
    best, _ = _argmax_first(gscore)
    vals = []
    for j in range(per):
        v = rows[per * (N_EXPERT_GROUPS - 1) + j]
        for g in range(N_EXPERT_GROUPS - 2, -1, -1):
            v = jnp.where(best == g, rows[per * g + j], v)
        vals.append(v)
    i1, m1 = _argmax_first(vals)
    i2, m2 = _argmax_first([jnp.where(i1 == j, -1.0, vals[j]) for j in range(per)])
    den = m1 + m2
    first_lo = i1 < i2
    lo = jnp.minimum(i1, i2)
    hi = jnp.maximum(i1, i2)
    pair = jnp.where(lo == 0, 0, jnp.where(lo == 1, per - 1, 2 * per - 3)) + hi - lo - 1
    cls = best * PAIRS_PER_GROUP + pair
    w_lo = jnp.where(first_lo, m1, m2) / den
    w_hi = jnp.where(first_lo, m2, m1) / den

    n_cls = count_ref.shape[0]
    hit = lax.broadcasted_iota(I32, (n_cls, tm), 0) == cls
    onehot = jnp.where(hit, 1.0, 0.0)
    src = lax.broadcasted_iota(I32, (tm, tm), 0)
    dst = lax.broadcasted_iota(I32, (tm, tm), 1)
    tri = jnp.where(src <= dst, 1.0, 0.0).astype(BF16)
    cum = jnp.dot(onehot.astype(BF16), tri, preferred_element_type=F32)
    excl = cum - onehot + count_ref[:, 0:1]
    rank = jnp.sum(jnp.where(hit, excl, 0.0), axis=0, keepdims=True)
    count_ref[...] = count_ref[...] + jnp.sum(onehot, axis=1, keepdims=True)

    zero = jnp.zeros((1, tm), F32)
    route_ref[...] = jnp.concatenate([cls.astype(F32), rank, w_lo, w_hi, zero, zero, zero, zero], axis=0)


def _post_mixer_body(m_ref, x_ref, mod_ref, lng_ref, lnb_ref, rwt_ref, rb_ref, *rest, glu, alpha):
    if glu:
        wv_ref, wg_ref, x1_ref, h_ref, route_ref, cnt_out_ref, cnt_ref = rest
        a = jax.nn.gelu(m_ref[0], approximate=True).astype(BF16)
        val = jnp.dot(a, wv_ref[...], preferred_element_type=F32)
        gate = jnp.dot(a, wg_ref[...], preferred_element_type=F32)
        m = val * jax.nn.sigmoid(gate)
    else:
        x1_ref, h_ref, route_ref, cnt_out_ref, cnt_ref = rest
        m = m_ref[0]

    @pl.when((pl.program_id(0) == 0) & (pl.program_id(1) == 0))
    def _():
        cnt_ref[...] = jnp.zeros_like(cnt_ref)

    g1 = mod_ref[0, 2:3, :]
    sh2 = mod_ref[0, 3:4, :]
    sc2 = mod_ref[0, 4:5, :]
    x1 = _layer_norm(alpha * x_ref[0] + g1 * m, lng_ref[...], lnb_ref[...])
    x1_ref[0] = x1
    h = x1 * (1.0 + sc2) + sh2
    h_ref[...] = h
    nt = (((1,), (1,)), ((), ()))
    rw = rwt_ref[...]
    rw_head = rw.astype(BF16)
    rw_rest = (rw - rw_head.astype(F32)).astype(BF16)
    h_head = h.astype(BF16)
    h_rest = (h - h_head.astype(F32)).astype(BF16)
    logits_t = (lax.dot_general(rw_head, h_head, nt, preferred_element_type=F32)
                + lax.dot_general(rw_head, h_rest, nt, preferred_element_type=F32)
                + lax.dot_general(rw_rest, h_head, nt, preferred_element_type=F32)) + rb_ref[...]
    _route(logits_t, cnt_ref, route_ref)
    cnt_out_ref[...] = cnt_ref[...]


def _post_mixer(m, x, mods, ln_g, ln_b, router_w, router_b, alpha, glu_w=None):
    b, l, d = x.shape
    n_e = router_w.shape[1]
    assert n_e == N_EXPERT_GROUPS * EXPERTS_PER_GROUP
    tm = min(l, 512)
    nt = l // tm
    tok = lambda i, j: (i, j, 0)
    const2 = lambda i, j: (0, 0)
    in_specs = [pl.BlockSpec((1, tm, d), tok),
                pl.BlockSpec((1, tm, d), tok),
                pl.BlockSpec((1, 6, d), lambda i, j: (i, 0, 0)),
                pl.BlockSpec((1, d), const2),
                pl.BlockSpec((1, d), const2),
                pl.BlockSpec((n_e, d), const2),
                pl.BlockSpec((n_e, 1), const2)]
    args = [m, x, mods, ln_g.reshape(1, d), ln_b.reshape(1, d), router_w.T, router_b.reshape(n_e, 1)]
    if glu_w is not None:
        in_specs += [pl.BlockSpec((d, d), const2), pl.BlockSpec((d, d), const2)]
        args += [glu_w[0].astype(BF16), glu_w[1].astype(BF16)]
    out_shape = [jax.ShapeDtypeStruct((b, l, d), F32),
                 jax.ShapeDtypeStruct((b * l, d), F32),
                 jax.ShapeDtypeStruct((ROUTE_ROWS, b * l), F32),
                 jax.ShapeDtypeStruct((N_PAIR_CLASSES, LANES), F32)]
    out_specs = [pl.BlockSpec((1, tm, d), tok),
                 pl.BlockSpec((tm, d), lambda i, j: (i * nt + j, 0)),
                 pl.BlockSpec((ROUTE_ROWS, tm), lambda i, j: (0, i * nt + j)),
                 pl.BlockSpec((N_PAIR_CLASSES, LANES), const2)]
    return pl.pallas_call(
        functools.partial(_post_mixer_body, glu=glu_w is not None, alpha=alpha),
        grid=(b, nt),
        in_specs=in_specs,
        out_specs=out_specs,
        out_shape=out_shape,
        scratch_shapes=[pltpu.VMEM((N_PAIR_CLASSES, LANES), F32)],
        compiler_params=_cparams("arbitrary", "arbitrary"),
        name="post_mixer_glu" if glu_w is not None else "post_mixer",
    )(*args)


def _take(table, idx):
    ids = jnp.arange(table.shape[0], dtype=I32)
    return jnp.sum(jnp.where(idx[:, None] == ids[None, :], table[None, :], 0), axis=1)


def _pass_segments():
    seg_cls, seg_hi = [], []
    pairs = [(a, b) for a in range(EXPERTS_PER_GROUP) for b in range(a + 1, EXPERTS_PER_GROUP)]
    for g in range(N_EXPERT_GROUPS):
        for m in range(EXPERTS_PER_GROUP):
            for idx, (a, b) in enumerate(pairs):
                if m in (a, b):
                    seg_cls.append(g * PAIRS_PER_GROUP + idx)
                    seg_hi.append(int(m == b))
    return seg_cls, seg_hi


def _dispatch_plan(route, counts, tile):
    n_cls = counts.shape[0]
    n_tok = route.shape[1]
    cnt = counts[:, 0].astype(I32)
    tiles = (cnt + tile - 1) // tile
    tile_end = jnp.cumsum(tiles)
    tile_off = tile_end - tiles
    cids = jnp.arange(n_cls, dtype=I32)[:, None]
    cls = route[0].astype(I32)
    pos = jnp.sum(jnp.where(cls[None, :] == cids, (tile_off * tile)[:, None], 0), axis=0) + route[1].astype(I32)

    n_row_tiles = n_tok // tile + n_cls
    seg_cls, seg_hi = _pass_segments()
    seg_cls = jnp.asarray(seg_cls, I32)
    seg_hi = jnp.asarray(seg_hi, I32)
    seg_per_expert = EXPERTS_PER_GROUP - 1
    seg_tiles = _take(tiles, seg_cls)
    seg_end = jnp.cumsum(seg_tiles)
    n_used = seg_end[-1]
    p = jnp.arange(2 * n_row_tiles, dtype=I32)
    seg = jnp.minimum(jnp.sum((seg_end[None, :] <= p[:, None]).astype(I32), axis=1), seg_cls.shape[0] - 1)
    within = p - _take(seg_end - seg_tiles, seg)
    spare = p - n_used
    used = p < n_used
    pass_tile = jnp.where(used, _take(tile_off, _take(seg_cls, seg)) + within, tile_end[-1] + spare // 2).astype(I32)
    pass_hi = jnp.where(used, _take(seg_hi, seg), spare % 2).astype(I32)
    pass_expert = jnp.where(used, seg // seg_per_expert, N_EXPERT_GROUPS * EXPERTS_PER_GROUP - 1).astype(I32)
    pass_rows = jnp.where(used, jnp.clip(_take(_take(cnt, seg_cls), seg) - within * tile, 0, tile), 0)
    last_tile = (tile_end - 1).astype(I32)
    return (pos.astype(I32), pass_tile, pass_hi, pass_expert, pass_rows.astype(I32),
            last_tile, tiles.astype(I32), tile_end[-1:].astype(I32), n_row_tiles)


def _zero_tiles(last_ref, tiles_ref, nu_ref, zero_ref, hs_ref, sem, n_e, tile, wait):
    for e in range(n_e):
        @pl.when(tiles_ref[e] > 0)
        def _(e=e):
            start = pl.multiple_of(last_ref[e] * tile, tile)
            cp = pltpu.make_async_copy(zero_ref, hs_ref.at[pl.ds(start, tile)], sem)
            if wait:
                cp.wait()
            else:
                cp.start()

    def body(j, carry):
        start = pl.multiple_of(j * tile, tile)
        cp = pltpu.make_async_copy(zero_ref, hs_ref.at[pl.ds(start, tile)], sem)
        if wait:
            cp.wait()
        else:
            cp.start()
        return carry

    lax.fori_loop(nu_ref[0], hs_ref.shape[0] // tile, body, 0)


def _dispatch_body(pos_ref, last_ref, tiles_ref, nu_ref, h_ref, hs_ref, zero_ref, sem_z, sem, *,
                   tm, tile, n_e):
    i = pl.program_id(0)

    @pl.when(i == 0)
    def _():
        zero_ref[...] = jnp.zeros_like(zero_ref)
        _zero_tiles(last_ref, tiles_ref, nu_ref, zero_ref, hs_ref, sem_z, n_e, tile, False)
        _zero_tiles(last_ref, tiles_ref, nu_ref, zero_ref, hs_ref, sem_z, n_e, tile, True)

    base = i * tm

    def issue(r2, carry):
        for k in range(2):
            r = 2 * r2 + k
            pltpu.make_async_copy(h_ref.at[pl.ds(r, 1)], hs_ref.at[pl.ds(pos_ref[base + r], 1)],
                                  sem).start(priority=k)
        return carry

    lax.fori_loop(0, tm // 2, issue, 0, unroll=8)
    pltpu.make_async_copy(h_ref, hs_ref.at[pl.ds(0, tm)], sem).wait()


def _dispatch(h_rows, pos, last_tile, tiles, n_used, n_rows, tile):
    n, d = h_rows.shape
    n_e = tiles.shape[0]
    tm = min(n, 1024)
    grid_spec = pltpu.PrefetchScalarGridSpec(
        num_scalar_prefetch=4,
        grid=(n // tm,),
        in_specs=[pl.BlockSpec((tm, d), lambda i, *_: (i, 0))],
        out_specs=pl.BlockSpec(memory_space=pl.ANY),
        scratch_shapes=[pltpu.VMEM((tile, d), F32),
                        pltpu.SemaphoreType.DMA(()), pltpu.SemaphoreType.DMA(())],
    )
    return pl.pallas_call(
        functools.partial(_dispatch_body, tm=tm, tile=tile, n_e=n_e),
        grid_spec=grid_spec,
        out_shape=jax.ShapeDtypeStruct((n_rows, d), F32),
        compiler_params=_cparams("arbitrary"),
        name="moe_dispatch",
    )(pos, last_tile, tiles, n_used, h_rows)


def _expert_body(pt_ref, ph_ref, te_ref, pr_ref, hs_ref, wg_ref, wu_ref, wd_ref, ys_ref, wgb, wub, wdb):
    i = pl.program_id(0)
    rows = pr_ref[i]
    half = hs_ref.shape[0] // 2
    first = jnp.logical_or(i == 0, te_ref[i] != te_ref[jnp.maximum(i - 1, 0)])

    @pl.when(jnp.logical_and(rows > 0, first))
    def _():
        wgb[...] = wg_ref[0, 0].astype(BF16)
        wub[...] = wu_ref[0, 0].astype(BF16)
        wdb[...] = wd_ref[0, 0].astype(BF16)

    def ffn(n):
        x = hs_ref[0:n, :].astype(BF16)
        gate = jnp.dot(x, wgb[...], preferred_element_type=F32)
        up = jnp.dot(x, wub[...], preferred_element_type=F32)
        a = (gate * jax.nn.sigmoid(gate) * up).astype(BF16)
        ys_ref[0:n, :] = jnp.dot(a, wdb[...], preferred_element_type=F32)

    @pl.when(rows > half)
    def _():
        ffn(2 * half)

    @pl.when(jnp.logical_and(rows > 0, rows <= half))
    def _():
        ffn(half)
        ys_ref[half:, :] = jnp.zeros((half, ys_ref.shape[1]), F32)

    @pl.when(rows == 0)
    def _():
        ys_ref[...] = jnp.zeros_like(ys_ref)


def _expert_ffn(hs, pass_tile, pass_hi, pass_expert, pass_rows, w_gate, w_up, w_down, layer, tile):
    _, n_e, d, f = w_gate.shape
    n_rows = hs.shape[0]
    wmap = lambda i, pt, ph, te, nu: (layer, te[i], 0, 0)
    grid_spec = pltpu.PrefetchScalarGridSpec(
        num_scalar_prefetch=4,
        grid=(pass_tile.shape[0],),
        in_specs=[pl.BlockSpec((tile, d), lambda i, pt, ph, te, nu: (pt[i], 0)),
                  pl.BlockSpec((1, 1, d, f), wmap),
                  pl.BlockSpec((1, 1, d, f), wmap),
                  pl.BlockSpec((1, 1, f, d), wmap)],
        out_specs=pl.BlockSpec((tile, d), lambda i, pt, ph, te, nu: (pt[i], ph[i])),
        scratch_shapes=[pltpu.VMEM((d, f), BF16), pltpu.VMEM((d, f), BF16), pltpu.VMEM((f, d), BF16)],
    )
    return pl.pallas_call(
        _expert_body,
        grid_spec=grid_spec,
        out_shape=jax.ShapeDtypeStruct((n_rows, 2 * d), F32),
        compiler_params=_cparams("arbitrary"),
        name="moe_experts",
    )(pass_tile, pass_hi, pass_expert, pass_rows, hs, w_gate, w_up, w_down)


def _combine_body(pos_ref, x_ref, mod_ref, wts_ref, lng_ref, lnb_ref, ys_ref, *rest,
                  tm, alpha, next_mod):
    if next_mod:
        nmod_ref, x2_ref, h_ref, buf, sem = rest
    else:
        x2_ref, buf, sem = rest
    step = pl.program_id(0) * pl.num_programs(1) + pl.program_id(1)
    n_steps = pl.num_programs(0) * pl.num_programs(1)
    d = x_ref.shape[2]

    def gather(s, slot):
        def issue(r2, carry):
            for k in range(2):
                r = 2 * r2 + k
                pltpu.make_async_copy(ys_ref.at[pl.ds(pos_ref[s * tm + r], 1)],
                                      buf.at[slot, pl.ds(r, 1)], sem.at[slot]).start(priority=k)
            return carry

        lax.fori_loop(0, tm // 2, issue, 0, unroll=8)

    @pl.when(step == 0)
    def _():
        gather(0, 0)

    @pl.when(step + 1 < n_steps)
    def _():
        gather(step + 1, (step + 1) % 2)

    slot = step % 2
    pltpu.make_async_copy(ys_ref.at[pl.ds(0, tm)], buf.at[slot], sem.at[slot]).wait()

    w = wts_ref[...]
    rows = buf[slot]
    moe = w[:, 0:1] * rows[:, :d] + w[:, 1:2] * rows[:, d:]
    g2 = mod_ref[0, 5:6, :]
    x2 = _layer_norm(alpha * x_ref[0] + g2 * moe, lng_ref[...], lnb_ref[...])
    x2_ref[0] = x2
    if next_mod:
        h_ref[0] = x2 * (1.0 + nmod_ref[0, 1:2, :]) + nmod_ref[0, 0:1, :]


def _combine(ys, pos, wts, x, mods, ln_g, ln_b, alpha, next_mods=None):
    b, l, d = x.shape
    tm = min(l, 512)
    nt = l // tm
    tok = lambda i, j, *_: (i, j, 0)
    bat = lambda i, j, *_: (i, 0, 0)
    const2 = lambda i, j, *_: (0, 0)
    in_specs = [pl.BlockSpec((1, tm, d), tok),
                pl.BlockSpec((1, 6, d), bat),
                pl.BlockSpec((tm, 2), lambda i, j, *_: (i * nt + j, 0)),
                pl.BlockSpec((1, d), const2),
                pl.BlockSpec((1, d), const2),
                pl.BlockSpec(memory_space=pl.ANY)]
    args = [x, mods, wts, ln_g.reshape(1, d), ln_b.reshape(1, d), ys]
    out_shape = [jax.ShapeDtypeStruct((b, l, d), F32)]
    out_specs = [pl.BlockSpec((1, tm, d), tok)]
    if next_mods is not None:
        in_specs.append(pl.BlockSpec((1, 6, d), bat))
        args.append(next_mods)
        out_shape.append(jax.ShapeDtypeStruct((b, l, d), F32))
        out_specs.append(pl.BlockSpec((1, tm, d), tok))
    grid_spec = pltpu.PrefetchScalarGridSpec(
        num_scalar_prefetch=1,
        grid=(b, nt),
        in_specs=in_specs,
        out_specs=out_specs,
        scratch_shapes=[pltpu.VMEM((2, tm, 2 * d), F32), pltpu.SemaphoreType.DMA((2,))],
    )
    return pl.pallas_call(
        functools.partial(_combine_body, tm=tm, alpha=alpha, next_mod=next_mods is not None),
        grid_spec=grid_spec,
        out_shape=out_shape,
        compiler_params=_cparams("arbitrary", "arbitrary"),
        name="moe_combine",
    )(pos, *args)


MOE_TILE = 512


def _moe(h_rows, route, counts, w_gate, w_up, w_down, layer):
    n = h_rows.shape[0]
    tile = min(n, MOE_TILE)
    (pos, pass_tile, pass_hi, pass_expert, pass_rows, last_tile, tiles, n_used_tiles,
     n_row_tiles) = _dispatch_plan(route, counts, tile)
    hs = _dispatch(h_rows, pos, last_tile, tiles, n_used_tiles, n_row_tiles * tile, tile)
    ys = _expert_ffn(hs, pass_tile, pass_hi, pass_expert, pass_rows, w_gate, w_up, w_down, layer, tile)
    return ys, pos, jnp.transpose(route[2:4])


def _pool_group(h_ref, w_ref, sc_ref, o_ref, col_ref, k, n_rows):
    n = n_rows * GRID_W
    c = h_ref.shape[2]
    blk = 4 * GRID_W
    half = k // 2
    pad = half * GRID_W
    ti = lax.broadcasted_iota(I32, (blk, blk), 0)
    si = lax.broadcasted_iota(I32, (blk, blk), 1)
    shift = GRID_W.bit_length() - 1
    same_row = (ti >> shift) == (si >> shift)
    band = jnp.where(same_row & (si - ti >= -half) & (si - ti <= half - 1), 1.0, 0.0).astype(BF16)
    col_ref[0:pad, :] = jnp.zeros((pad, c), F32)
    col_ref[pad + n:pad + n + pad, :] = jnp.zeros((pad, c), F32)
    for b0 in range(0, n, blk):
        hb = h_ref[0, b0:b0 + blk, :]
        head = hb.astype(BF16)
        rest = (hb - head.astype(F32)).astype(BF16)
        col_ref[pad + b0:pad + b0 + blk, :] = (jnp.dot(band, head, preferred_element_type=F32)
                                               + jnp.dot(band, rest, preferred_element_type=F32))
    acc = col_ref[0:n, :]
    for j in range(1, k):
        acc = acc + col_ref[j * GRID_W:j * GRID_W + n, :]
    t = lax.broadcasted_iota(I32, (n, 1), 0)
    wc = t & (GRID_W - 1)
    wr = t >> shift
    cnt_c = jnp.minimum(wc + half - 1, GRID_W - 1) - jnp.maximum(wc - half, 0) + 1
    cnt_r = jnp.minimum(wr + half - 1, n_rows - 1) - jnp.maximum(wr - half, 0) + 1
    mean = acc / (cnt_c * cnt_r).astype(F32)
    pooled = (mean - h_ref[0]).astype(BF16)
    o_ref[0] = jnp.dot(pooled, w_ref[0].astype(BF16), preferred_element_type=F32) * sc_ref[...]


def _pool_body(h_ref, w_ref, sc_ref, o_ref, col_ref, *, n_rows):
    g = pl.program_id(1)
    for gi, k in enumerate(POOL_WINDOWS):
        @pl.when(g == gi)
        def _(k=k):
            _pool_group(h_ref, w_ref, sc_ref, o_ref, col_ref, k, n_rows)


def _pool_mix(h, w_grp, scale):
    b, n, d = h.shape
    n_g, c, _ = w_grp.shape
    n_rows = n // GRID_W
    pad = (max(POOL_WINDOWS) // 2) * GRID_W
    return pl.pallas_call(
        functools.partial(_pool_body, n_rows=n_rows),
        grid=(b, n_g),
        in_specs=[pl.BlockSpec((1, n, c), lambda i, j: (i, 0, j)),
                  pl.BlockSpec((1, c, c), lambda i, j: (j, 0, 0)),
                  pl.BlockSpec((1, c), lambda i, j: (0, j))],
        out_specs=pl.BlockSpec((1, n, c), lambda i, j: (i, 0, j)),
        out_shape=jax.ShapeDtypeStruct((b, n, d), F32),
        scratch_shapes=[pltpu.VMEM((n + 2 * pad, c), F32)],
        compiler_params=_cparams("arbitrary", "arbitrary"),
        name="pool_mix",
    )(h, w_grp, scale.reshape(1, d))


def kernel(x, c, ctx, c_ctx, mod_w, mod_b, ln_g, ln_b, s5_lam_re, s5_lam_im, s5_log_dt, s5_b_re, s5_b_im,
           s5_c_re, s5_c_im, s5_d, s5_w_val, s5_w_gate, pool_w, pool_scale, router_w, router_b,
           moe_w_gate, moe_w_up, moe_w_down):
    b, l, d = x.shape
    depth = mod_w.shape[0]
    assert depth == 2 and b + 1 <= SUBLANES and d % LANES == 0 and GRID_W & (GRID_W - 1) == 0
    alpha = (2 * depth) ** 0.25

    cond = jnp.zeros((SUBLANES, d), F32).at[:b].set(c).at[b].set(c_ctx)
    mods = _modulation(cond, mod_w, mod_b).reshape(depth, SUBLANES, 6, d)

    toep, w1, w2t, a_r, a_i = _s5_weights(s5_lam_re[0], s5_lam_im[0], s5_log_dt[0], s5_b_re[0], s5_b_im[0],
                                          s5_c_re[0], s5_c_im[0], s5_d[0])
    y = _s5_mix(x, ctx, mods[0, :b], mods[0, b:b + 1], toep, w1, w2t, a_r, a_i)
    x1, h_rows, route, counts = _post_mixer(y, x, mods[0, :b], ln_g[0, 0], ln_b[0, 0], router_w, router_b,
                                            alpha, glu_w=(s5_w_val[0], s5_w_gate[0]))
    ys, pos, wts = _moe(h_rows, route, counts, moe_w_gate, moe_w_up, moe_w_down, 0)
    x2, h = _combine(ys, pos, wts, x1, mods[0, :b], ln_g[0, 1], ln_b[0, 1], alpha, next_mods=mods[1, :b])

    m = _pool_mix(h, pool_w[0], pool_scale[0])
    x3, h_rows, route, counts = _post_mixer(m, x2, mods[1, :b], ln_g[1, 0], ln_b[1, 0], router_w, router_b,
                                            alpha)
    ys, pos, wts = _moe(h_rows, route, counts, moe_w_gate, moe_w_up, moe_w_down, 1)
    (out,) = _combine(ys, pos, wts, x3, mods[1, :b], ln_g[1, 1], ln_b[1, 1], alpha)
    return out
```

```python
import functools

import jax
import jax.numpy as jnp
from jax import lax
from jax.experimental import pallas as pl
from jax.experimental.pallas import tpu as pltpu

F32 = jnp.float32
BF16 = jnp.bfloat16
I32 = jnp.int32
HI = lax.Precision.HIGHEST

GRID_W = 64
S5_H = 16
S5_T = 16
POOL_WINDOWS = (2, 4, 8, 16)
N_EXPERT_GROUPS = 4
LN_EPS = 1e-5
LANES = 128
SUBLANES = 8
VMEM_LIMIT = 52 * 1024 * 1024

EXPERTS_PER_GROUP = 4
PAIRS_PER_GROUP = EXPERTS_PER_GROUP * (EXPERTS_PER_GROUP - 1) // 2
N_PAIR_CLASSES = N_EXPERT_GROUPS * PAIRS_PER_GROUP
ROUTE_ROWS = 8


def _cparams(*sem):
    return pltpu.CompilerParams(dimension_semantics=sem, vmem_limit_bytes=VMEM_LIMIT)


def _mod_body(c_ref, w_ref, b_ref, o_ref):
    c = c_ref[...]
    s = c * jax.nn.sigmoid(c)
    o_ref[0] = jnp.dot(s, w_ref[0], precision=HI, preferred_element_type=F32) + b_ref[0]


def _modulation(cond, mod_w, mod_b):
    depth, d, n6 = mod_w.shape
    tn = min(n6, 1536)
    return pl.pallas_call(
        _mod_body,
        grid=(depth, n6 // tn),
        in_specs=[pl.BlockSpec((SUBLANES, d), lambda i, j: (0, 0)),
                  pl.BlockSpec((1, d, tn), lambda i, j: (i, 0, j)),
                  pl.BlockSpec((1, 1, tn), lambda i, j: (i, 0, j))],
        out_specs=pl.BlockSpec((1, SUBLANES, tn), lambda i, j: (i, 0, j)),
        out_shape=jax.ShapeDtypeStruct((depth, SUBLANES, n6), F32),
        compiler_params=_cparams("arbitrary", "arbitrary"),
        name="modulation",
    )(cond, mod_w, mod_b.reshape(depth, 1, n6))


def _s5_direction_terms(lam_re, lam_im, log_dt, b_re, b_im):
    lr = lam_re.astype(F32)
    li = lam_im.astype(F32)
    dt = jnp.exp(log_dt.astype(F32))[:, None]
    mag = jnp.exp(lr * dt)
    ar = mag * jnp.cos(li * dt)
    ai = mag * jnp.sin(li * dt)
    den = lr * lr + li * li
    nr = ar - 1.0
    fr = (nr * lr + ai * li) / den
    fi = (ai * lr - nr * li) / den
    br_, bi_ = b_re.astype(F32), b_im.astype(F32)
    bbr = fr[..., None] * br_ - fi[..., None] * bi_
    bbi = fr[..., None] * bi_ + fi[..., None] * br_
    k = jnp.arange(S5_T + 1, dtype=F32)[:, None, None]
    pm = jnp.exp(k * (lr * dt))
    pr = pm * jnp.cos(k * (li * dt))
    pi = pm * jnp.sin(k * (li * dt))
    return pr, pi, bbr, bbi


def _s5_weights(lam_re, lam_im, log_dt, b_re, b_im, c_re, c_im, d_skip):
    t = S5_T
    g, p = lam_re.shape[1:]
    h = b_re.shape[-1]
    terms = [_s5_direction_terms(lam_re[d], lam_im[d], log_dt[d], b_re[d], b_im[d]) for d in (0, 1)]
    pw = jnp.stack([jnp.transpose(terms[d][k], (1, 0, 2)) for d in (0, 1) for k in (0, 1)], axis=1)
    bt = jnp.stack([jnp.transpose(terms[d][k], (0, 2, 1)) for d in (0, 1) for k in (2, 3)], axis=1)
    cc = jnp.stack([c[d].astype(F32) for d in (0, 1) for c in (c_re, c_im)], axis=1)
    gp = 8
    spec4 = lambda rows: pl.BlockSpec((gp, 4, rows, p), lambda i: (i, 0, 0, 0))
    wide = pl.BlockSpec((gp, t * h, t * h), lambda i: (i, 0, 0))
    toep, w1, w2t = pl.pallas_call(
        functools.partial(_s5_prep_body, gp=gp),
        grid=(g // gp,),
        in_specs=[spec4(t + 1), spec4(h), spec4(h), pl.BlockSpec((gp, 1, h), lambda i: (i, 0, 0))],
        out_specs=[wide, wide, wide],
        out_shape=[jax.ShapeDtypeStruct((g, t * h, t * h), BF16) for _ in range(3)],
        compiler_params=_cparams("arbitrary"),
        name="s5_prep",
    )(pw, bt, cc, d_skip.astype(F32).reshape(g, 1, h))
    a_r = jnp.concatenate([pw[:, 0, t], pw[:, 2, t]], axis=-1)[:, None, :]
    a_i = jnp.concatenate([pw[:, 1, t], pw[:, 3, t]], axis=-1)[:, None, :]
    return toep, w1, w2t, a_r, a_i


def _cmul(ar, ai, br, bi):
    return ar * br - ai * bi, ar * bi + ai * br


def _s5_prep_body(pw_ref, bt_ref, cc_ref, d_ref, toep_ref, w1_ref, w2t_ref, *, gp):
    t = pw_ref.shape[2] - 1
    h = bt_ref.shape[2]
    nt = (((1,), (1,)), ((), ()))
    eye = (lax.broadcasted_iota(I32, (h, h), 0) == lax.broadcasted_iota(I32, (h, h), 1))
    for g in range(gp):
        prf, pif, prb, pib = (pw_ref[g, k] for k in range(4))
        btf = (bt_ref[g, 0], bt_ref[g, 1])
        btb = (bt_ref[g, 2], bt_ref[g, 3])
        ccf = (cc_ref[g, 0], cc_ref[g, 1])
        ccb = (cc_ref[g, 2], cc_ref[g, 3])
        caf, cab = [], []
        for j in range(t):
            f_r, f_i = _cmul(*btf, prf[t - 1 - j:t - j], pif[t - 1 - j:t - j])
            b_r, b_i = _cmul(*btb, prb[j:j + 1], pib[j:j + 1])
            w1_ref[g, j * h:(j + 1) * h, :] = jnp.concatenate([f_r, b_r, f_i, b_i], axis=1).astype(BF16)
            mf_r, mf_i = _cmul(*ccf, prf[j + 1:j + 2], pif[j + 1:j + 2])
            mb_r, mb_i = _cmul(*ccb, prb[t - j:t - j + 1], pib[t - j:t - j + 1])
            w2t_ref[g, j * h:(j + 1) * h, :] = jnp.concatenate([mf_r, mb_r, -mf_i, -mb_i],
                                                               axis=1).astype(BF16)
            caf.append(_cmul(*ccf, prf[j:j + 1], pif[j:j + 1]))
            cab.append(_cmul(*ccb, prb[t - 1 - j:t - j], pib[t - 1 - j:t - j]))

        def lag_kernels(btx, ca):
            car = jnp.concatenate([c[0] for c in ca], axis=0)
            cai = jnp.concatenate([c[1] for c in ca], axis=0)
            return (lax.dot_general(btx[0], car, nt, precision=HI, preferred_element_type=F32)
                    - lax.dot_general(btx[1], cai, nt, precision=HI, preferred_element_type=F32))

        kf = lag_kernels(btf, caf)
        kb = lag_kernels(btb, cab)
        skip = jnp.where(eye, jnp.broadcast_to(d_ref[g], (h, h)), 0.0)
        mid = kb[:, (t - 1) * h:] + kf[:, :h] + skip
        kwide = jnp.concatenate([kb[:, :(t - 1) * h], mid, kf[:, h:], jnp.zeros((h, h), F32)], axis=1)
        for i in range(t):
            off = (t - 1 - i) * h
            toep_ref[g, i * h:(i + 1) * h, :] = kwide[:, off:off + t * h].astype(BF16)


def _granule_transpose(v):
    n = len(v)
    gran = lax.broadcasted_iota(I32, v[0].shape, 1) >> 4
    at = [gran == q for q in range(n)]
    rot = []
    for d in range(n):
        m = v[d]
        for q in range(1, n):
            m = jnp.where(at[q], v[(q + d) % n], m)
        rot.append(pltpu.roll(m, d * S5_H, 1) if d else m)
    out = []
    for q in range(n):
        w = rot[(-q) % n]
        for j in range(1, n):
            w = jnp.where(at[j], rot[(j - q) % n], w)
        out.append(w)
    return out


def _s5_body(x_ref, ctx_ref, mod_ref, cmod_ref, tt_ref, w1_ref, w2t_ref, ar_ref, ai_ref, y_ref,
             u_ref, s_ref, *, n_ctx_chunks, n_lat_chunks, gb):
    n_b = x_ref.shape[0]
    n_g = u_ref.shape[0]
    n_blocks = s_ref.shape[1] // SUBLANES
    half = SUBLANES // 2
    tok_blk = SUBLANES * S5_T
    lat_lo = n_ctx_chunks * n_b
    lat_rows = n_lat_chunks * n_b

    def chunk_rows(ref, b, tok0, shift, scale1):
        vs = [ref[b, pl.ds(tok0 + j, SUBLANES, stride=S5_T), :] * scale1 + shift for j in range(S5_T)]
        lo = _granule_transpose(vs[:SUBLANES])
        hi = _granule_transpose(vs[SUBLANES:])
        return lo, hi

    def put_rows(rows, chunk0, b):
        for q in range(n_g):
            for jh in range(2):
                u_ref[q, jh, pl.ds(chunk0 * n_b + b, SUBLANES, stride=n_b), :] = rows[jh][q]

    def u_rows(g, lo, n):
        return jnp.concatenate([u_ref[g, 0, lo:lo + n, :], u_ref[g, 1, lo:lo + n, :]], axis=1)

    shift = jnp.broadcast_to(cmod_ref[0, 0:1, :], (SUBLANES, LANES))
    scale1 = 1.0 + jnp.broadcast_to(cmod_ref[0, 1:2, :], (SUBLANES, LANES))
    for b in range(n_b):
        for cb in range(n_ctx_chunks // SUBLANES):
            rows = chunk_rows(ctx_ref, b, cb * tok_blk, shift, scale1)
            put_rows(rows, cb * SUBLANES, b)
            put_rows(rows, n_ctx_chunks + n_lat_chunks + cb * SUBLANES, b)

    def fill(cb, carry):
        for b in range(n_b):
            shift = jnp.broadcast_to(mod_ref[b, 0:1, :], (SUBLANES, LANES))
            scale1 = 1.0 + jnp.broadcast_to(mod_ref[b, 1:2, :], (SUBLANES, LANES))
            put_rows(chunk_rows(x_ref, b, cb * tok_blk, shift, scale1), n_ctx_chunks + cb * SUBLANES, b)
        return carry

    lax.fori_loop(0, n_lat_chunks // SUBLANES, fill, 0)

    for g in range(n_g):
        s_ref[g] = jnp.dot(u_rows(g, 0, n_blocks * SUBLANES).astype(BF16), w1_ref[g],
                           preferred_element_type=F32)

    lane = lax.broadcasted_iota(I32, (SUBLANES, LANES), 1)
    row = lax.broadcasted_iota(I32, (SUBLANES, LANES), 0)
    is_fwd = lane < LANES // 2
    is_fwd2 = jnp.concatenate([is_fwd, is_fwd], axis=1)
    top = row < half
    zero = jnp.zeros((SUBLANES, LANES), F32)
    for g0 in range(0, n_g, gb):
        ars = [jnp.broadcast_to(ar_ref[g0 + g], (SUBLANES, LANES)) for g in range(gb)]
        ais = [jnp.broadcast_to(ai_ref[g0 + g], (SUBLANES, LANES)) for g in range(gb)]

        def step(k, carry, g0=g0, ars=ars, ais=ais):
            fo = pl.multiple_of(k * SUBLANES, SUBLANES)
            bo = pl.multiple_of((n_blocks - 1 - k) * SUBLANES, SUBLANES)
            new = []
            loaded = [(s_ref[g0 + g, pl.ds(fo, SUBLANES), :], s_ref[g0 + g, pl.ds(bo, SUBLANES), :])
                      for g in range(gb)]
            stores = []
            for g in range(gb):
                xr, xi = carry[2 * g], carry[2 * g + 1]
                vf, vb_raw = loaded[g]
                vb = pltpu.roll(vb_raw, half, 0)
                vr = jnp.where(is_fwd, vf[:, :LANES], vb[:, :LANES])
                vi = jnp.where(is_fwd, vf[:, LANES:], vb[:, LANES:])
                ar, ai = ars[g], ais[g]
                yr = ar * xr - ai * xi + vr
                yi = ar * xi + ai * xr + vi
                yrr = pltpu.roll(yr, half, 0)
                yir = pltpu.roll(yi, half, 0)
                zr = ar * yrr - ai * yir + vr
                zi = ar * yir + ai * yrr + vi
                inc = jnp.concatenate([jnp.where(top, xr, yrr), jnp.where(top, xi, yir)], axis=1)
                stores.append((jnp.where(is_fwd2, inc, vf),
                               jnp.where(is_fwd2, vb_raw, pltpu.roll(inc, half, 0))))
                new.append(jnp.where(top, pltpu.roll(zr, half, 0), zr))
                new.append(jnp.where(top, pltpu.roll(zi, half, 0), zi))
            for g in range(gb):
                s_ref[g0 + g, pl.ds(fo, SUBLANES), :] = stores[g][0]
                s_ref[g0 + g, pl.ds(bo, SUBLANES), :] = stores[g][1]
            return tuple(new)

        lax.fori_loop(0, (n_ctx_chunks + n_lat_chunks) // 2, step, tuple(zero for _ in range(2 * gb)))

    for g in range(n_g):
        y = (jnp.dot(u_rows(g, lat_lo, lat_rows).astype(BF16), tt_ref[g], preferred_element_type=F32)
             + lax.dot_general(s_ref[g, lat_lo:lat_lo + lat_rows, :].astype(BF16), w2t_ref[g],
                               (((1,), (1,)), ((), ())), preferred_element_type=F32))
        for jh in range(2):
            u_ref[g, jh, lat_lo:lat_lo + lat_rows, :] = y[:, jh * LANES:(jh + 1) * LANES]

    def emit(cb, carry):
        for b in range(n_b):
            for jh in range(S5_T // SUBLANES):
                w = [u_ref[q, jh, pl.ds((n_ctx_chunks + cb * SUBLANES) * n_b + b, SUBLANES, stride=n_b), :]
                     for q in range(n_g)]
                v = _granule_transpose(w)
                for j in range(SUBLANES):
                    y_ref[b, pl.ds(cb * tok_blk + jh * SUBLANES + j, SUBLANES, stride=S5_T), :] = v[j]
        return carry

    lax.fori_loop(0, n_lat_chunks // SUBLANES, emit, 0)


def _s5_mix(x, ctx, mods, cmods, toep, w1, w2t, a_r, a_i):
    b, l, d = x.shape
    n_ctx = ctx.shape[1]
    n_g = LANES // S5_H
    n_ctx_chunks, n_lat_chunks = n_ctx // S5_T, l // S5_T
    assert b * 2 == SUBLANES and n_ctx_chunks % SUBLANES == 0 and n_lat_chunks % SUBLANES == 0
    rows = (2 * n_ctx_chunks + n_lat_chunks) * b
    w = S5_T * S5_H
    body = functools.partial(_s5_body, n_ctx_chunks=n_ctx_chunks, n_lat_chunks=n_lat_chunks, gb=8)
    lane_tile = lambda i: (0, 0, i)
    wspec = pl.BlockSpec((n_g, w, w), lambda i: (i, 0, 0))
    aspec = pl.BlockSpec((n_g, 1, w // 2), lambda i: (i, 0, 0))
    return pl.pallas_call(
        body,
        grid=(d // LANES,),
        in_specs=[pl.BlockSpec((b, l, LANES), lane_tile),
                  pl.BlockSpec((b, n_ctx, LANES), lane_tile),
                  pl.BlockSpec((b, 6, LANES), lane_tile),
                  pl.BlockSpec((1, 6, LANES), lane_tile),
                  wspec, wspec, wspec, aspec, aspec],
        out_specs=pl.BlockSpec((b, l, LANES), lane_tile, pipeline_mode=pl.Buffered(1)),
        out_shape=jax.ShapeDtypeStruct((b, l, d), F32),
        scratch_shapes=[pltpu.VMEM((n_g, w // LANES, rows, LANES), F32), pltpu.VMEM((n_g, rows, w), F32)],
        compiler_params=_cparams("arbitrary"),
        name="s5_mix",
    )(x, ctx, mods, cmods, toep, w1, w2t, a_r, a_i)


def _layer_norm(r, g, b):
    mu = jnp.mean(r, axis=-1, keepdims=True)
    xc = r - mu
    var = jnp.mean(xc * xc, axis=-1, keepdims=True)
    return xc * lax.rsqrt(var + LN_EPS) * g + b


def _max2_of4(a, b, c, d):
    h1, l1 = jnp.maximum(a, b), jnp.minimum(a, b)
    h2, l2 = jnp.maximum(c, d), jnp.minimum(c, d)
    return jnp.maximum(h1, h2) + jnp.maximum(jnp.minimum(h1, h2), jnp.maximum(l1, l2))


def _argmax_first(vals):
    idx = jnp.zeros(vals[0].shape, I32)
    best = vals[0]
    for j in range(1, len(vals)):
        upd = vals[j] > best
        idx = jnp.where(upd, j, idx)
        best = jnp.where(upd, vals[j], best)
    return idx, best


def _route(logits_t, count_ref, route_ref):
    n_e, tm = logits_t.shape
    per = n_e // N_EXPERT_GROUPS
    mx = jnp.max(logits_t, axis=0, keepdims=True)
    ex = jnp.exp(logits_t - mx)
    sc = ex / jnp.sum(ex, axis=0, keepdims=True)
    rows = [sc[e:e + 1, :] for e in range(n_e)]
    gscore = [_max2_of4(*rows[per * g:per * (g + 1)]) for g in range(N_EXPERT_GROUPS)]
    best, _ = _argmax_first(gscore)
    vals = []
    for j in range(per):
        v = rows[per * (N_EXPERT_GROUPS - 1) + j]
        for g in range(N_EXPERT_GROUPS - 2, -1, -1):
            v = jnp.where(best == g, rows[per * g + j], v)
        vals.append(v)
    i1, m1 = _argmax_first(vals)
    i2, m2 = _argmax_first([jnp.where(i1 == j, -1.0, vals[j]) for j in range(per)])
    den = m1 + m2
    first_lo = i1 < i2
    lo = jnp.minimum(i1, i2)
    hi = jnp.maximum(i1, i2)
    pair = jnp.where(lo == 0, 0, jnp.where(lo == 1, per - 1, 2 * per - 3)) + hi - lo - 1
    cls = best * PAIRS_PER_GROUP + pair
    w_lo = jnp.where(first_lo, m1, m2) / den
    w_hi = jnp.where(first_lo, m2, m1) / den

    n_cls = count_ref.shape[0]
    hit = lax.broadcasted_iota(I32, (n_cls, tm), 0) == cls
    onehot = jnp.where(hit, 1.0, 0.0)
    src = lax.broadcasted_iota(I32, (tm, tm), 0)
    dst = lax.broadcasted_iota(I32, (tm, tm), 1)
    tri = jnp.where(src <= dst, 1.0, 0.0).astype(BF16)
    cum = jnp.dot(onehot.astype(BF16), tri, preferred_element_type=F32)
    excl = cum - onehot + count_ref[:, 0:1]
    rank = jnp.sum(jnp.where(hit, excl, 0.0), axis=0, keepdims=True)
    count_ref[...] = count_ref[...] + jnp.sum(onehot, axis=1, keepdims=True)

    zero = jnp.zeros((1, tm), F32)
    route_ref[...] = jnp.concatenate([cls.astype(F32), rank, w_lo, w_hi, zero, zero, zero, zero], axis=0)


def _post_mixer_body(m_ref, x_ref, mod_ref, lng_ref, lnb_ref, rwt_ref, rb_ref, *rest, glu, alpha):
    if glu:
        wv_ref, wg_ref, x1_ref, h_ref, route_ref, cnt_out_ref, cnt_ref = rest
        a = jax.nn.gelu(m_ref[0], approximate=True).astype(BF16)
        val = jnp.dot(a, wv_ref[...], preferred_element_type=F32)
        gate = jnp.dot(a, wg_ref[...], preferred_element_type=F32)
        m = val * jax.nn.sigmoid(gate)
    else:
        x1_ref, h_ref, route_ref, cnt_out_ref, cnt_ref = rest
        m = m_ref[0]

    @pl.when((pl.program_id(0) == 0) & (pl.program_id(1) == 0))
    def _():
        cnt_ref[...] = jnp.zeros_like(cnt_ref)

    g1 = mod_ref[0, 2:3, :]
    sh2 = mod_ref[0, 3:4, :]
    sc2 = mod_ref[0, 4:5, :]
    x1 = _layer_norm(alpha * x_ref[0] + g1 * m, lng_ref[...], lnb_ref[...])
    x1_ref[0] = x1
    h = x1 * (1.0 + sc2) + sh2
    h_ref[...] = h
    nt = (((1,), (1,)), ((), ()))
    rw = rwt_ref[...]
    rw_head = rw.astype(BF16)
    rw_rest = (rw - rw_head.astype(F32)).astype(BF16)
    h_head = h.astype(BF16)
    h_rest = (h - h_head.astype(F32)).astype(BF16)
    logits_t = (lax.dot_general(rw_head, h_head, nt, preferred_element_type=F32)
                + lax.dot_general(rw_head, h_rest, nt, preferred_element_type=F32)
                + lax.dot_general(rw_rest, h_head, nt, preferred_element_type=F32)) + rb_ref[...]
    _route(logits_t, cnt_ref, route_ref)
    cnt_out_ref[...] = cnt_ref[...]


def _post_mixer(m, x, mods, ln_g, ln_b, router_w, router_b, alpha, glu_w=None):
    b, l, d = x.shape
    n_e = router_w.shape[1]
    assert n_e == N_EXPERT_GROUPS * EXPERTS_PER_GROUP
    tm = min(l, 512)
    nt = l // tm
    tok = lambda i, j: (i, j, 0)
    const2 = lambda i, j: (0, 0)
    in_specs = [pl.BlockSpec((1, tm, d), tok),
                pl.BlockSpec((1, tm, d), tok),
                pl.BlockSpec((1, 6, d), lambda i, j: (i, 0, 0)),
                pl.BlockSpec((1, d), const2),
                pl.BlockSpec((1, d), const2),
                pl.BlockSpec((n_e, d), const2),
                pl.BlockSpec((n_e, 1), const2)]
    args = [m, x, mods, ln_g.reshape(1, d), ln_b.reshape(1, d), router_w.T, router_b.reshape(n_e, 1)]
    if glu_w is not None:
        in_specs += [pl.BlockSpec((d, d), const2), pl.BlockSpec((d, d), const2)]
        args += [glu_w[0].astype(BF16), glu_w[1].astype(BF16)]
    out_shape = [jax.ShapeDtypeStruct((b, l, d), F32),
                 jax.ShapeDtypeStruct((b * l, d), F32),
                 jax.ShapeDtypeStruct((ROUTE_ROWS, b * l), F32),
                 jax.ShapeDtypeStruct((N_PAIR_CLASSES, LANES), F32)]
    out_specs = [pl.BlockSpec((1, tm, d), tok),
                 pl.BlockSpec((tm, d), lambda i, j: (i * nt + j, 0)),
                 pl.BlockSpec((ROUTE_ROWS, tm), lambda i, j: (0, i * nt + j)),
                 pl.BlockSpec((N_PAIR_CLASSES, LANES), const2)]
    return pl.pallas_call(
        functools.partial(_post_mixer_body, glu=glu_w is not None, alpha=alpha),
        grid=(b, nt),
        in_specs=in_specs,
        out_specs=out_specs,
        out_shape=out_shape,
        scratch_shapes=[pltpu.VMEM((N_PAIR_CLASSES, LANES), F32)],
        compiler_params=_cparams("arbitrary", "arbitrary"),
        name="post_mixer_glu" if glu_w is not None else "post_mixer",
    )(*args)


def _take(table, idx):
    ids = jnp.arange(table.shape[0], dtype=I32)
    return jnp.sum(jnp.where(idx[:, None] == ids[None, :], table[None, :], 0), axis=1)


def _pass_segments():
    seg_cls, seg_hi = [], []
    pairs = [(a, b) for a in range(EXPERTS_PER_GROUP) for b in range(a + 1, EXPERTS_PER_GROUP)]
    for g in range(N_EXPERT_GROUPS):
        for m in range(EXPERTS_PER_GROUP):
            for idx, (a, b) in enumerate(pairs):
                if m in (a, b):
                    seg_cls.append(g * PAIRS_PER_GROUP + idx)
                    seg_hi.append(int(m == b))
    return seg_cls, seg_hi


def _dispatch_plan(route, counts, tile):
    n_cls = counts.shape[0]
    n_tok = route.shape[1]
    cnt = counts[:, 0].astype(I32)
    tiles = (cnt + tile - 1) // tile
    tile_end = jnp.cumsum(tiles)
    tile_off = tile_end - tiles
    cids = jnp.arange(n_cls, dtype=I32)[:, None]
    cls = route[0].astype(I32)
    pos = jnp.sum(jnp.where(cls[None, :] == cids, (tile_off * tile)[:, None], 0), axis=0) + route[1].astype(I32)

    n_row_tiles = n_tok // tile + n_cls
    seg_cls, seg_hi = _pass_segments()
    seg_cls = jnp.asarray(seg_cls, I32)
    seg_hi = jnp.asarray(seg_hi, I32)
    seg_per_expert = EXPERTS_PER_GROUP - 1
    seg_tiles = _take(tiles, seg_cls)
    seg_end = jnp.cumsum(seg_tiles)
    n_used = seg_end[-1]
    p = jnp.arange(2 * n_row_tiles, dtype=I32)
    seg = jnp.minimum(jnp.sum((seg_end[None, :] <= p[:, None]).astype(I32), axis=1), seg_cls.shape[0] - 1)
    within = p - _take(seg_end - seg_tiles, seg)
    spare = p - n_used
    used = p < n_used
    pass_tile = jnp.where(used, _take(tile_off, _take(seg_cls, seg)) + within, tile_end[-1] + spare // 2).astype(I32)
    pass_hi = jnp.where(used, _take(seg_hi, seg), spare % 2).astype(I32)
    pass_expert = jnp.where(used, seg // seg_per_expert, N_EXPERT_GROUPS * EXPERTS_PER_GROUP - 1).astype(I32)
    pass_rows = jnp.where(used, jnp.clip(_take(_take(cnt, seg_cls), seg) - within * tile, 0, tile), 0)
    last_tile = (tile_end - 1).astype(I32)
    return (pos.astype(I32), pass_tile, pass_hi, pass_expert, pass_rows.astype(I32),
            last_tile, tiles.astype(I32), tile_end[-1:].astype(I32), n_row_tiles)


def _zero_tiles(last_ref, tiles_ref, nu_ref, zero_ref, hs_ref, sem, n_e, tile, wait):
    for e in range(n_e):
        @pl.when(tiles_ref[e] > 0)
        def _(e=e):
            start = pl.multiple_of(last_ref[e] * tile, tile)
            cp = pltpu.make_async_copy(zero_ref, hs_ref.at[pl.ds(start, tile)], sem)
            if wait:
                cp.wait()
            else:
                cp.start()

    def body(j, carry):
        start = pl.multiple_of(j * tile, tile)
        cp = pltpu.make_async_copy(zero_ref, hs_ref.at[pl.ds(start, tile)], sem)
        if wait:
            cp.wait()
        else:
            cp.start()
        return carry

    lax.fori_loop(nu_ref[0], hs_ref.shape[0] // tile, body, 0)


def _dispatch_body(pos_ref, last_ref, tiles_ref, nu_ref, h_ref, hs_ref, zero_ref, sem_z, sem, *,
                   tm, tile, n_e):
    i = pl.program_id(0)

    @pl.when(i == 0)
    def _():
        zero_ref[...] = jnp.zeros_like(zero_ref)
        _zero_tiles(last_ref, tiles_ref, nu_ref, zero_ref, hs_ref, sem_z, n_e, tile, False)
        _zero_tiles(last_ref, tiles_ref, nu_ref, zero_ref, hs_ref, sem_z, n_e, tile, True)

    base = i * tm

    def issue(r2, carry):
        for k in range(2):
            r = 2 * r2 + k
            pltpu.make_async_copy(h_ref.at[pl.ds(r, 1)], hs_ref.at[pl.ds(pos_ref[base + r], 1)],
                                  sem).start(priority=k)
        return carry

    lax.fori_loop(0, tm // 2, issue, 0, unroll=8)
    pltpu.make_async_copy(h_ref, hs_ref.at[pl.ds(0, tm)], sem).wait()


def _dispatch(h_rows, pos, last_tile, tiles, n_used, n_rows, tile):
    n, d = h_rows.shape
    n_e = tiles.shape[0]
    tm = min(n, 2048)
    grid_spec = pltpu.PrefetchScalarGridSpec(
        num_scalar_prefetch=4,
        grid=(n // tm,),
        in_specs=[pl.BlockSpec((tm, d), lambda i, *_: (i, 0))],
        out_specs=pl.BlockSpec(memory_space=pl.ANY),
        scratch_shapes=[pltpu.VMEM((tile, d), F32),
                        pltpu.SemaphoreType.DMA(()), pltpu.SemaphoreType.DMA(())],
    )
    return pl.pallas_call(
        functools.partial(_dispatch_body, tm=tm, tile=tile, n_e=n_e),
        grid_spec=grid_spec,
        out_shape=jax.ShapeDtypeStruct((n_rows, d), F32),
        compiler_params=_cparams("arbitrary"),
        name="moe_dispatch",
    )(pos, last_tile, tiles, n_used, h_rows)


def _expert_body(pt_ref, ph_ref, te_ref, pr_ref, hs_ref, wg_ref, wu_ref, wd_ref, ys_ref, wgb, wub, wdb):
    i = pl.program_id(0)
    rows = pr_ref[i]
    half = hs_ref.shape[0] // 2
    first = jnp.logical_or(i == 0, te_ref[i] != te_ref[jnp.maximum(i - 1, 0)])

    @pl.when(jnp.logical_and(rows > 0, first))
    def _():
        wgb[...] = wg_ref[0, 0].astype(BF16)
        wub[...] = wu_ref[0, 0].astype(BF16)
        wdb[...] = wd_ref[0, 0].astype(BF16)

    def ffn(n):
        x = hs_ref[0:n, :].astype(BF16)
        gate = jnp.dot(x, wgb[...], preferred_element_type=F32)
        up = jnp.dot(x, wub[...], preferred_element_type=F32)
        a = (gate * jax.nn.sigmoid(gate) * up).astype(BF16)
        ys_ref[0:n, :] = jnp.dot(a, wdb[...], preferred_element_type=F32)

    @pl.when(rows > half)
    def _():
        ffn(2 * half)

    @pl.when(jnp.logical_and(rows > 0, rows <= half))
    def _():
        ffn(half)
        ys_ref[half:, :] = jnp.zeros((half, ys_ref.shape[1]), F32)

    @pl.when(rows == 0)
    def _():
        ys_ref[...] = jnp.zeros_like(ys_ref)


def _expert_ffn(hs, pass_tile, pass_hi, pass_expert, pass_rows, w_gate, w_up, w_down, layer, tile):
    _, n_e, d, f = w_gate.shape
    n_rows = hs.shape[0]
    wmap = lambda i, pt, ph, te, nu: (layer, te[i], 0, 0)
    grid_spec = pltpu.PrefetchScalarGridSpec(
        num_scalar_prefetch=4,
        grid=(pass_tile.shape[0],),
        in_specs=[pl.BlockSpec((tile, d), lambda i, pt, ph, te, nu: (pt[i], 0)),
                  pl.BlockSpec((1, 1, d, f), wmap),
                  pl.BlockSpec((1, 1, d, f), wmap),
                  pl.BlockSpec((1, 1, f, d), wmap)],
        out_specs=pl.BlockSpec((tile, d), lambda i, pt, ph, te, nu: (pt[i], ph[i])),
        scratch_shapes=[pltpu.VMEM((d, f), BF16), pltpu.VMEM((d, f), BF16), pltpu.VMEM((f, d), BF16)],
    )
    return pl.pallas_call(
        _expert_body,
        grid_spec=grid_spec,
        out_shape=jax.ShapeDtypeStruct((n_rows, 2 * d), F32),
        compiler_params=_cparams("arbitrary"),
        name="moe_experts",
    )(pass_tile, pass_hi, pass_expert, pass_rows, hs, w_gate, w_up, w_down)


def _combine_body(pos_ref, x_ref, mod_ref, wts_ref, lng_ref, lnb_ref, ys_ref, *rest,
                  tm, alpha, next_mod):
    if next_mod:
        nmod_ref, x2_ref, h_ref, buf, sem = rest
    else:
        x2_ref, buf, sem = rest
    step = pl.program_id(0) * pl.num_programs(1) + pl.program_id(1)
    n_steps = pl.num_programs(0) * pl.num_programs(1)
    d = x_ref.shape[2]

    def gather(s, slot):
        def issue(r2, carry):
            for k in range(2):
                r = 2 * r2 + k
                pltpu.make_async_copy(ys_ref.at[pl.ds(pos_ref[s * tm + r], 1)],
                                      buf.at[slot, pl.ds(r, 1)], sem.at[slot]).start(priority=k)
            return carry

        lax.fori_loop(0, tm // 2, issue, 0, unroll=8)

    @pl.when(step == 0)
    def _():
        gather(0, 0)

    @pl.when(step + 1 < n_steps)
    def _():
        gather(step + 1, (step + 1) % 2)

    slot = step % 2
    pltpu.make_async_copy(ys_ref.at[pl.ds(0, tm)], buf.at[slot], sem.at[slot]).wait()

    w = wts_ref[...]
    rows = buf[slot]
    moe = w[:, 0:1] * rows[:, :d] + w[:, 1:2] * rows[:, d:]
    g2 = mod_ref[0, 5:6, :]
    x2 = _layer_norm(alpha * x_ref[0] + g2 * moe, lng_ref[...], lnb_ref[...])
    x2_ref[0] = x2
    if next_mod:
        h_ref[0] = x2 * (1.0 + nmod_ref[0, 1:2, :]) + nmod_ref[0, 0:1, :]


def _combine(ys, pos, wts, x, mods, ln_g, ln_b, alpha, next_mods=None):
    b, l, d = x.shape
    tm = min(l, 512)
    nt = l // tm
    tok = lambda i, j, *_: (i, j, 0)
    bat = lambda i, j, *_: (i, 0, 0)
    const2 = lambda i, j, *_: (0, 0)
    in_specs = [pl.BlockSpec((1, tm, d), tok),
                pl.BlockSpec((1, 6, d), bat),
                pl.BlockSpec((tm, 2), lambda i, j, *_: (i * nt + j, 0)),
                pl.BlockSpec((1, d), const2),
                pl.BlockSpec((1, d), const2),
                pl.BlockSpec(memory_space=pl.ANY)]
    args = [x, mods, wts, ln_g.reshape(1, d), ln_b.reshape(1, d), ys]
    out_shape = [jax.ShapeDtypeStruct((b, l, d), F32)]
    out_specs = [pl.BlockSpec((1, tm, d), tok)]
    if next_mods is not None:
        in_specs.append(pl.BlockSpec((1, 6, d), bat))
        args.append(next_mods)
        out_shape.append(jax.ShapeDtypeStruct((b, l, d), F32))
        out_specs.append(pl.BlockSpec((1, tm, d), tok))
    grid_spec = pltpu.PrefetchScalarGridSpec(
        num_scalar_prefetch=1,
        grid=(b, nt),
        in_specs=in_specs,
        out_specs=out_specs,
        scratch_shapes=[pltpu.VMEM((2, tm, 2 * d), F32), pltpu.SemaphoreType.DMA((2,))],
    )
    return pl.pallas_call(
        functools.partial(_combine_body, tm=tm, alpha=alpha, next_mod=next_mods is not None),
        grid_spec=grid_spec,
        out_shape=out_shape,
        compiler_params=_cparams("arbitrary", "arbitrary"),
        name="moe_combine",
    )(pos, *args)


MOE_TILE = 256


def _moe(h_rows, route, counts, w_gate, w_up, w_down, layer):
    n = h_rows.shape[0]
    tile = min(n, MOE_TILE)
    (pos, pass_tile, pass_hi, pass_expert, pass_rows, last_tile, tiles, n_used_tiles,
     n_row_tiles) = _dispatch_plan(route, counts, tile)
    hs = _dispatch(h_rows, pos, last_tile, tiles, n_used_tiles, n_row_tiles * tile, tile)
    ys = _expert_ffn(hs, pass_tile, pass_hi, pass_expert, pass_rows, w_gate, w_up, w_down, layer, tile)
    return ys, pos, jnp.transpose(route[2:4])


def _pool_group(h_ref, w_ref, sc_ref, o_ref, col_ref, k, n_rows):
    n = n_rows * GRID_W
    c = h_ref.shape[2]
    blk = 4 * GRID_W
    half = k // 2
    pad = half * GRID_W
    ti = lax.broadcasted_iota(I32, (blk, blk), 0)
    si = lax.broadcasted_iota(I32, (blk, blk), 1)
    shift = GRID_W.bit_length() - 1
    same_row = (ti >> shift) == (si >> shift)
    band = jnp.where(same_row & (si - ti >= -half) & (si - ti <= half - 1), 1.0, 0.0).astype(BF16)
    col_ref[0:pad, :] = jnp.zeros((pad, c), F32)
    col_ref[pad + n:pad + n + pad, :] = jnp.zeros((pad, c), F32)
    for b0 in range(0, n, blk):
        hb = h_ref[0, b0:b0 + blk, :]
        head = hb.astype(BF16)
        rest = (hb - head.astype(F32)).astype(BF16)
        col_ref[pad + b0:pad + b0 + blk, :] = (jnp.dot(band, head, preferred_element_type=F32)
                                               + jnp.dot(band, rest, preferred_element_type=F32))
    acc = col_ref[0:n, :]
    for j in range(1, k):
        acc = acc + col_ref[j * GRID_W:j * GRID_W + n, :]
    t = lax.broadcasted_iota(I32, (n, 1), 0)
    wc = t & (GRID_W - 1)
    wr = t >> shift
    cnt_c = jnp.minimum(wc + half - 1, GRID_W - 1) - jnp.maximum(wc - half, 0) + 1
    cnt_r = jnp.minimum(wr + half - 1, n_rows - 1) - jnp.maximum(wr - half, 0) + 1
    mean = acc / (cnt_c * cnt_r).astype(F32)
    pooled = (mean - h_ref[0]).astype(BF16)
    o_ref[0] = jnp.dot(pooled, w_ref[0].astype(BF16), preferred_element_type=F32) * sc_ref[...]


def _pool_body(h_ref, w_ref, sc_ref, o_ref, col_ref, *, n_rows):
    g = pl.program_id(1)
    for gi, k in enumerate(POOL_WINDOWS):
        @pl.when(g == gi)
        def _(k=k):
            _pool_group(h_ref, w_ref, sc_ref, o_ref, col_ref, k, n_rows)


def _pool_mix(h, w_grp, scale):
    b, n, d = h.shape
    n_g, c, _ = w_grp.shape
    n_rows = n // GRID_W
    pad = (max(POOL_WINDOWS) // 2) * GRID_W
    return pl.pallas_call(
        functools.partial(_pool_body, n_rows=n_rows),
        grid=(b, n_g),
        in_specs=[pl.BlockSpec((1, n, c), lambda i, j: (i, 0, j)),
                  pl.BlockSpec((1, c, c), lambda i, j: (j, 0, 0)),
                  pl.BlockSpec((1, c), lambda i, j: (0, j))],
        out_specs=pl.BlockSpec((1, n, c), lambda i, j: (i, 0, j)),
        out_shape=jax.ShapeDtypeStruct((b, n, d), F32),
        scratch_shapes=[pltpu.VMEM((n + 2 * pad, c), F32)],
        compiler_params=_cparams("arbitrary", "arbitrary"),
        name="pool_mix",
    )(h, w_grp, scale.reshape(1, d))


def kernel(x, c, ctx, c_ctx, mod_w, mod_b, ln_g, ln_b, s5_lam_re, s5_lam_im, s5_log_dt, s5_b_re, s5_b_im,
           s5_c_re, s5_c_im, s5_d, s5_w_val, s5_w_gate, pool_w, pool_scale, router_w, router_b,
           moe_w_gate, moe_w_up, moe_w_down):
    b, l, d = x.shape
    depth = mod_w.shape[0]
    assert depth == 2 and b + 1 <= SUBLANES and d % LANES == 0 and GRID_W & (GRID_W - 1) == 0
    alpha = (2 * depth) ** 0.25

    cond = jnp.zeros((SUBLANES, d), F32).at[:b].set(c).at[b].set(c_ctx)
    mods = _modulation(cond, mod_w, mod_b).reshape(depth, SUBLANES, 6, d)

    toep, w1, w2t, a_r, a_i = _s5_weights(s5_lam_re[0], s5_lam_im[0], s5_log_dt[0], s5_b_re[0], s5_b_im[0],
                                          s5_c_re[0], s5_c_im[0], s5_d[0])
    y = _s5_mix(x, ctx, mods[0, :b], mods[0, b:b + 1], toep, w1, w2t, a_r, a_i)
    x1, h_rows, route, counts = _post_mixer(y, x, mods[0, :b], ln_g[0, 0], ln_b[0, 0], router_w, router_b,
                                            alpha, glu_w=(s5_w_val[0], s5_w_gate[0]))
    ys, pos, wts = _moe(h_rows, route, counts, moe_w_gate, moe_w_up, moe_w_down, 0)
    x2, h = _combine(ys, pos, wts, x1, mods[0, :b], ln_g[0, 1], ln_b[0, 1], alpha, next_mods=mods[1, :b])

    m = _pool_mix(h, pool_w[0], pool_scale[0])
    x3, h_rows, route, counts = _post_mixer(m, x2, mods[1, :b], ln_g[1, 0], ln_b[1, 0], router_w, router_b,
                                            alpha)
    ys, pos, wts = _moe(h_rows, route, counts, moe_w_gate, moe_w_up, moe_w_down, 1)
    (out,) = _combine(ys, pos, wts, x3, mods[1, :b], ln_g[1, 1], ln_b[1, 1], alpha)
    return out
```

```python
import functools

import jax
import jax.numpy as jnp
from jax import lax
from jax.experimental import pallas as pl
from jax.experimental.pallas import tpu as pltpu

F32 = jnp.float32
BF16 = jnp.bfloat16
I32 = jnp.int32
HI = lax.Precision.HIGHEST

GRID_W = 64
S5_H = 16
S5_T = 16
POOL_WINDOWS = (2, 4, 8, 16)
N_EXPERT_GROUPS = 4
LN_EPS = 1e-5
LANES = 128
SUBLANES = 8
VMEM_LIMIT = 52 * 1024 * 1024

EXPERTS_PER_GROUP = 4
PAIRS_PER_GROUP = EXPERTS_PER_GROUP * (EXPERTS_PER_GROUP - 1) // 2
N_PAIR_CLASSES = N_EXPERT_GROUPS * PAIRS_PER_GROUP
ROUTE_ROWS = 8


def _cparams(*sem):
    return pltpu.CompilerParams(dimension_semantics=sem, vmem_limit_bytes=VMEM_LIMIT)


def _mod_body(c_ref, w_ref, b_ref, o_ref):
    c = c_ref[...]
    s = c * jax.nn.sigmoid(c)
    o_ref[0] = jnp.dot(s, w_ref[0], precision=HI, preferred_element_type=F32) + b_ref[0]


def _modulation(cond, mod_w, mod_b):
    depth, d, n6 = mod_w.shape
    tn = min(n6, 1536)
    return pl.pallas_call(
        _mod_body,
        grid=(depth, n6 // tn),
        in_specs=[pl.BlockSpec((SUBLANES, d), lambda i, j: (0, 0)),
                  pl.BlockSpec((1, d, tn), lambda i, j: (i, 0, j)),
                  pl.BlockSpec((1, 1, tn), lambda i, j: (i, 0, j))],
        out_specs=pl.BlockSpec((1, SUBLANES, tn), lambda i, j: (i, 0, j)),
        out_shape=jax.ShapeDtypeStruct((depth, SUBLANES, n6), F32),
        compiler_params=_cparams("arbitrary", "arbitrary"),
        name="modulation",
    )(cond, mod_w, mod_b.reshape(depth, 1, n6))


def _s5_direction_terms(lam_re, lam_im, log_dt, b_re, b_im):
    lr = lam_re.astype(F32)
    li = lam_im.astype(F32)
    dt = jnp.exp(log_dt.astype(F32))[:, None]
    mag = jnp.exp(lr * dt)
    ar = mag * jnp.cos(li * dt)
    ai = mag * jnp.sin(li * dt)
    den = lr * lr + li * li
    nr = ar - 1.0
    fr = (nr * lr + ai * li) / den
    fi = (ai * lr - nr * li) / den
    br_, bi_ = b_re.astype(F32), b_im.astype(F32)
    bbr = fr[..., None] * br_ - fi[..., None] * bi_
    bbi = fr[..., None] * bi_ + fi[..., None] * br_
    k = jnp.arange(S5_T + 1, dtype=F32)[:, None, None]
    pm = jnp.exp(k * (lr * dt))
    pr = pm * jnp.cos(k * (li * dt))
    pi = pm * jnp.sin(k * (li * dt))
    return pr, pi, bbr, bbi


def _s5_weights(lam_re, lam_im, log_dt, b_re, b_im, c_re, c_im, d_skip):
    t = S5_T
    g, p = lam_re.shape[1:]
    h = b_re.shape[-1]
    terms = [_s5_direction_terms(lam_re[d], lam_im[d], log_dt[d], b_re[d], b_im[d]) for d in (0, 1)]
    pw = jnp.stack([jnp.transpose(terms[d][k], (1, 0, 2)) for d in (0, 1) for k in (0, 1)], axis=1)
    bt = jnp.stack([jnp.transpose(terms[d][k], (0, 2, 1)) for d in (0, 1) for k in (2, 3)], axis=1)
    cc = jnp.stack([c[d].astype(F32) for d in (0, 1) for c in (c_re, c_im)], axis=1)
    gp = 8
    spec4 = lambda rows: pl.BlockSpec((gp, 4, rows, p), lambda i: (i, 0, 0, 0))
    wide = pl.BlockSpec((gp, t * h, t * h), lambda i: (i, 0, 0))
    toep, w1, w2t = pl.pallas_call(
        functools.partial(_s5_prep_body, gp=gp),
        grid=(g // gp,),
        in_specs=[spec4(t + 1), spec4(h), spec4(h), pl.BlockSpec((gp, 1, h), lambda i: (i, 0, 0))],
        out_specs=[wide, wide, wide],
        out_shape=[jax.ShapeDtypeStruct((g, t * h, t * h), BF16) for _ in range(3)],
        compiler_params=_cparams("arbitrary"),
        name="s5_prep",
    )(pw, bt, cc, d_skip.astype(F32).reshape(g, 1, h))
    a_r = jnp.concatenate([pw[:, 0, t], pw[:, 2, t]], axis=-1)[:, None, :]
    a_i = jnp.concatenate([pw[:, 1, t], pw[:, 3, t]], axis=-1)[:, None, :]
    return toep, w1, w2t, a_r, a_i


def _cmul(ar, ai, br, bi):
    return ar * br - ai * bi, ar * bi + ai * br


def _s5_prep_body(pw_ref, bt_ref, cc_ref, d_ref, toep_ref, w1_ref, w2t_ref, *, gp):
    t = pw_ref.shape[2] - 1
    h = bt_ref.shape[2]
    nt = (((1,), (1,)), ((), ()))
    eye = (lax.broadcasted_iota(I32, (h, h), 0) == lax.broadcasted_iota(I32, (h, h), 1))
    for g in range(gp):
        prf, pif, prb, pib = (pw_ref[g, k] for k in range(4))
        btf = (bt_ref[g, 0], bt_ref[g, 1])
        btb = (bt_ref[g, 2], bt_ref[g, 3])
        ccf = (cc_ref[g, 0], cc_ref[g, 1])
        ccb = (cc_ref[g, 2], cc_ref[g, 3])
        caf, cab = [], []
        for j in range(t):
            f_r, f_i = _cmul(*btf, prf[t - 1 - j:t - j], pif[t - 1 - j:t - j])
            b_r, b_i = _cmul(*btb, prb[j:j + 1], pib[j:j + 1])
            w1_ref[g, j * h:(j + 1) * h, :] = jnp.concatenate([f_r, b_r, f_i, b_i], axis=1).astype(BF16)
            mf_r, mf_i = _cmul(*ccf, prf[j + 1:j + 2], pif[j + 1:j + 2])
            mb_r, mb_i = _cmul(*ccb, prb[t - j:t - j + 1], pib[t - j:t - j + 1])
            w2t_ref[g, j * h:(j + 1) * h, :] = jnp.concatenate([mf_r, mb_r, -mf_i, -mb_i],
                                                               axis=1).astype(BF16)
            caf.append(_cmul(*ccf, prf[j:j + 1], pif[j:j + 1]))
            cab.append(_cmul(*ccb, prb[t - 1 - j:t - j], pib[t - 1 - j:t - j]))

        def lag_kernels(btx, ca):
            car = jnp.concatenate([c[0] for c in ca], axis=0)
            cai = jnp.concatenate([c[1] for c in ca], axis=0)
            return (lax.dot_general(btx[0], car, nt, precision=HI, preferred_element_type=F32)
                    - lax.dot_general(btx[1], cai, nt, precision=HI, preferred_element_type=F32))

        kf = lag_kernels(btf, caf)
        kb = lag_kernels(btb, cab)
        skip = jnp.where(eye, jnp.broadcast_to(d_ref[g], (h, h)), 0.0)
        mid = kb[:, (t - 1) * h:] + kf[:, :h] + skip
        kwide = jnp.concatenate([kb[:, :(t - 1) * h], mid, kf[:, h:], jnp.zeros((h, h), F32)], axis=1)
        for i in range(t):
            off = (t - 1 - i) * h
            toep_ref[g, i * h:(i + 1) * h, :] = kwide[:, off:off + t * h].astype(BF16)


def _granule_transpose(v):
    n = len(v)
    gran = lax.broadcasted_iota(I32, v[0].shape, 1) >> 4
    at = [gran == q for q in range(n)]
    rot = []
    for d in range(n):
        m = v[d]
        for q in range(1, n):
            m = jnp.where(at[q], v[(q + d) % n], m)
        rot.append(pltpu.roll(m, d * S5_H, 1) if d else m)
    out = []
    for q in range(n):
        w = rot[(-q) % n]
        for j in range(1, n):
            w = jnp.where(at[j], rot[(j - q) % n], w)
        out.append(w)
    return out


def _s5_body(x_ref, ctx_ref, mod_ref, cmod_ref, tt_ref, w1_ref, w2t_ref, ar_ref, ai_ref, y_ref,
             u_ref, s_ref, *, n_ctx_chunks, n_lat_chunks, gb):
    n_b = x_ref.shape[0]
    n_g = u_ref.shape[0]
    n_blocks = s_ref.shape[1] // SUBLANES
    half = SUBLANES // 2
    tok_blk = SUBLANES * S5_T
    lat_lo = n_ctx_chunks * n_b
    lat_rows = n_lat_chunks * n_b

    def chunk_rows(ref, b, tok0, shift, scale1):
        vs = [ref[b, pl.ds(tok0 + j, SUBLANES, stride=S5_T), :] * scale1 + shift for j in range(S5_T)]
        lo = _granule_transpose(vs[:SUBLANES])
        hi = _granule_transpose(vs[SUBLANES:])
        return lo, hi

    def put_rows(rows, chunk0, b):
        for q in range(n_g):
            for jh in range(2):
                u_ref[q, jh, pl.ds(chunk0 * n_b + b, SUBLANES, stride=n_b), :] = rows[jh][q]

    def u_rows(g, lo, n):
        return jnp.concatenate([u_ref[g, 0, lo:lo + n, :], u_ref[g, 1, lo:lo + n, :]], axis=1)

    shift = jnp.broadcast_to(cmod_ref[0, 0:1, :], (SUBLANES, LANES))
    scale1 = 1.0 + jnp.broadcast_to(cmod_ref[0, 1:2, :], (SUBLANES, LANES))
    for b in range(n_b):
        for cb in range(n_ctx_chunks // SUBLANES):
            rows = chunk_rows(ctx_ref, b, cb * tok_blk, shift, scale1)
            put_rows(rows, cb * SUBLANES, b)
            put_rows(rows, n_ctx_chunks + n_lat_chunks + cb * SUBLANES, b)

    def fill(cb, carry):
        for b in range(n_b):
            shift = jnp.broadcast_to(mod_ref[b, 0:1, :], (SUBLANES, LANES))
            scale1 = 1.0 + jnp.broadcast_to(mod_ref[b, 1:2, :], (SUBLANES, LANES))
            put_rows(chunk_rows(x_ref, b, cb * tok_blk, shift, scale1), n_ctx_chunks + cb * SUBLANES, b)
        return carry

    lax.fori_loop(0, n_lat_chunks // SUBLANES, fill, 0)

    for g in range(n_g):
        s_ref[g] = jnp.dot(u_rows(g, 0, n_blocks * SUBLANES).astype(BF16), w1_ref[g],
                           preferred_element_type=F32)

    lane = lax.broadcasted_iota(I32, (SUBLANES, LANES), 1)
    row = lax.broadcasted_iota(I32, (SUBLANES, LANES), 0)
    is_fwd = lane < LANES // 2
    is_fwd2 = jnp.concatenate([is_fwd, is_fwd], axis=1)
    top = row < half
    zero = jnp.zeros((SUBLANES, LANES), F32)
    for g0 in range(0, n_g, gb):
        ars = [jnp.broadcast_to(ar_ref[g0 + g], (SUBLANES, LANES)) for g in range(gb)]
        ais = [jnp.broadcast_to(ai_ref[g0 + g], (SUBLANES, LANES)) for g in range(gb)]

        def step(k, carry, g0=g0, ars=ars, ais=ais):
            fo = pl.multiple_of(k * SUBLANES, SUBLANES)
            bo = pl.multiple_of((n_blocks - 1 - k) * SUBLANES, SUBLANES)
            new = []
            loaded = [(s_ref[g0 + g, pl.ds(fo, SUBLANES), :], s_ref[g0 + g, pl.ds(bo, SUBLANES), :])
                      for g in range(gb)]
            stores = []
            for g in range(gb):
                xr, xi = carry[2 * g], carry[2 * g + 1]
                vf, vb_raw = loaded[g]
                vb = pltpu.roll(vb_raw, half, 0)
                vr = jnp.where(is_fwd, vf[:, :LANES], vb[:, :LANES])
                vi = jnp.where(is_fwd, vf[:, LANES:], vb[:, LANES:])
                ar, ai = ars[g], ais[g]
                yr = ar * xr - ai * xi + vr
                yi = ar * xi + ai * xr + vi
                yrr = pltpu.roll(yr, half, 0)
                yir = pltpu.roll(yi, half, 0)
                zr = ar * yrr - ai * yir + vr
                zi = ar * yir + ai * yrr + vi
                inc = jnp.concatenate([jnp.where(top, xr, yrr), jnp.where(top, xi, yir)], axis=1)
                stores.append((jnp.where(is_fwd2, inc, vf),
                               jnp.where(is_fwd2, vb_raw, pltpu.roll(inc, half, 0))))
                new.append(jnp.where(top, pltpu.roll(zr, half, 0), zr))
                new.append(jnp.where(top, pltpu.roll(zi, half, 0), zi))
            for g in range(gb):
                s_ref[g0 + g, pl.ds(fo, SUBLANES), :] = stores[g][0]
                s_ref[g0 + g, pl.ds(bo, SUBLANES), :] = stores[g][1]
            return tuple(new)

        lax.fori_loop(0, (n_ctx_chunks + n_lat_chunks) // 2, step, tuple(zero for _ in range(2 * gb)))

    for g in range(n_g):
        y = (jnp.dot(u_rows(g, lat_lo, lat_rows).astype(BF16), tt_ref[g], preferred_element_type=F32)
             + lax.dot_general(s_ref[g, lat_lo:lat_lo + lat_rows, :].astype(BF16), w2t_ref[g],
                               (((1,), (1,)), ((), ())), preferred_element_type=F32))
        for jh in range(2):
            u_ref[g, jh, lat_lo:lat_lo + lat_rows, :] = y[:, jh * LANES:(jh + 1) * LANES]

    def emit(cb, carry):
        for b in range(n_b):
            for jh in range(S5_T // SUBLANES):
                w = [u_ref[q, jh, pl.ds((n_ctx_chunks + cb * SUBLANES) * n_b + b, SUBLANES, stride=n_b), :]
                     for q in range(n_g)]
                v = _granule_transpose(w)
                for j in range(SUBLANES):
                    y_ref[b, pl.ds(cb * tok_blk + jh * SUBLANES + j, SUBLANES, stride=S5_T), :] = v[j]
        return carry

    lax.fori_loop(0, n_lat_chunks // SUBLANES, emit, 0)


def _s5_mix(x, ctx, mods, cmods, toep, w1, w2t, a_r, a_i):
    b, l, d = x.shape
    n_ctx = ctx.shape[1]
    n_g = LANES // S5_H
    n_ctx_chunks, n_lat_chunks = n_ctx // S5_T, l // S5_T
    assert b * 2 == SUBLANES and n_ctx_chunks % SUBLANES == 0 and n_lat_chunks % SUBLANES == 0
    rows = (2 * n_ctx_chunks + n_lat_chunks) * b
    w = S5_T * S5_H
    body = functools.partial(_s5_body, n_ctx_chunks=n_ctx_chunks, n_lat_chunks=n_lat_chunks, gb=8)
    lane_tile = lambda i: (0, 0, i)
    wspec = pl.BlockSpec((n_g, w, w), lambda i: (i, 0, 0))
    aspec = pl.BlockSpec((n_g, 1, w // 2), lambda i: (i, 0, 0))
    return pl.pallas_call(
        body,
        grid=(d // LANES,),
        in_specs=[pl.BlockSpec((b, l, LANES), lane_tile),
                  pl.BlockSpec((b, n_ctx, LANES), lane_tile),
                  pl.BlockSpec((b, 6, LANES), lane_tile),
                  pl.BlockSpec((1, 6, LANES), lane_tile),
                  wspec, wspec, wspec, aspec, aspec],
        out_specs=pl.BlockSpec((b, l, LANES), lane_tile, pipeline_mode=pl.Buffered(1)),
        out_shape=jax.ShapeDtypeStruct((b, l, d), F32),
        scratch_shapes=[pltpu.VMEM((n_g, w // LANES, rows, LANES), F32), pltpu.VMEM((n_g, rows, w), F32)],
        compiler_params=_cparams("arbitrary"),
        name="s5_mix",
    )(x, ctx, mods, cmods, toep, w1, w2t, a_r, a_i)


def _layer_norm(r, g, b):
    mu = jnp.mean(r, axis=-1, keepdims=True)
    xc = r - mu
    var = jnp.mean(xc * xc, axis=-1, keepdims=True)
    return xc * lax.rsqrt(var + LN_EPS) * g + b


def _max2_of4(a, b, c, d):
    h1, l1 = jnp.maximum(a, b), jnp.minimum(a, b)
    h2, l2 = jnp.maximum(c, d), jnp.minimum(c, d)
    return jnp.maximum(h1, h2) + jnp.maximum(jnp.minimum(h1, h2), jnp.maximum(l1, l2))


def _argmax_first(vals):
    idx = jnp.zeros(vals[0].shape, I32)
    best = vals[0]
    for j in range(1, len(vals)):
        upd = vals[j] > best
        idx = jnp.where(upd, j, idx)
        best = jnp.where(upd, vals[j], best)
    return idx, best


def _route(logits_t, count_ref, route_ref):
    n_e, tm = logits_t.shape
    per = n_e // N_EXPERT_GROUPS
    mx = jnp.max(logits_t, axis=0, keepdims=True)
    ex = jnp.exp(logits_t - mx)
    sc = ex / jnp.sum(ex, axis=0, keepdims=True)
    rows = [sc[e:e + 1, :] for e in range(n_e)]
    gscore = [_max2_of4(*rows[per * g:per * (g + 1)]) for g in range(N_EXPERT_GROUPS)]
    best, _ = _argmax_first(gscore)
    vals = []
    for j in range(per):
        v = rows[per * (N_EXPERT_GROUPS - 1) + j]
        for g in range(N_EXPERT_GROUPS - 2, -1, -1):
            v = jnp.where(best == g, rows[per * g + j], v)
        vals.append(v)
    i1, m1 = _argmax_first(vals)
    i2, m2 = _argmax_first([jnp.where(i1 == j, -1.0, vals[j]) for j in range(per)])
    den = m1 + m2
    first_lo = i1 < i2
    lo = jnp.minimum(i1, i2)
    hi = jnp.maximum(i1, i2)
    pair = jnp.where(lo == 0, 0, jnp.where(lo == 1, per - 1, 2 * per - 3)) + hi - lo - 1
    cls = best * PAIRS_PER_GROUP + pair
    w_lo = jnp.where(first_lo, m1, m2) / den
    w_hi = jnp.where(first_lo, m2, m1) / den

    n_cls = count_ref.shape[0]
    hit = lax.broadcasted_iota(I32, (n_cls, tm), 0) == cls
    onehot = jnp.where(hit, 1.0, 0.0)
    src = lax.broadcasted_iota(I32, (tm, tm), 0)
    dst = lax.broadcasted_iota(I32, (tm, tm), 1)
    tri = jnp.where(src <= dst, 1.0, 0.0).astype(BF16)
    cum = jnp.dot(onehot.astype(BF16), tri, preferred_element_type=F32)
    excl = cum - onehot + count_ref[:, 0:1]
    rank = jnp.sum(jnp.where(hit, excl, 0.0), axis=0, keepdims=True)
    count_ref[...] = count_ref[...] + jnp.sum(onehot, axis=1, keepdims=True)

    zero = jnp.zeros((1, tm), F32)
    route_ref[...] = jnp.concatenate([cls.astype(F32), rank, w_lo, w_hi, zero, zero, zero, zero], axis=0)


def _post_mixer_body(m_ref, x_ref, mod_ref, lng_ref, lnb_ref, rwt_ref, rb_ref, *rest, glu, alpha):
    if glu:
        wv_ref, wg_ref, x1_ref, h_ref, route_ref, cnt_out_ref, cnt_ref = rest
        a = jax.nn.gelu(m_ref[0], approximate=True).astype(BF16)
        val = jnp.dot(a, wv_ref[...], preferred_element_type=F32)
        gate = jnp.dot(a, wg_ref[...], preferred_element_type=F32)
        m = val * jax.nn.sigmoid(gate)
    else:
        x1_ref, h_ref, route_ref, cnt_out_ref, cnt_ref = rest
        m = m_ref[0]

    @pl.when((pl.program_id(0) == 0) & (pl.program_id(1) == 0))
    def _():
        cnt_ref[...] = jnp.zeros_like(cnt_ref)

    g1 = mod_ref[0, 2:3, :]
    sh2 = mod_ref[0, 3:4, :]
    sc2 = mod_ref[0, 4:5, :]
    x1 = _layer_norm(alpha * x_ref[0] + g1 * m, lng_ref[...], lnb_ref[...])
    x1_ref[0] = x1
    h = x1 * (1.0 + sc2) + sh2
    h_ref[...] = h
    nt = (((1,), (1,)), ((), ()))
    rw = rwt_ref[...]
    rw_head = rw.astype(BF16)
    rw_rest = (rw - rw_head.astype(F32)).astype(BF16)
    h_head = h.astype(BF16)
    h_rest = (h - h_head.astype(F32)).astype(BF16)
    logits_t = (lax.dot_general(rw_head, h_head, nt, preferred_element_type=F32)
                + lax.dot_general(rw_head, h_rest, nt, preferred_element_type=F32)
                + lax.dot_general(rw_rest, h_head, nt, preferred_element_type=F32)) + rb_ref[...]
    _route(logits_t, cnt_ref, route_ref)
    cnt_out_ref[...] = cnt_ref[...]


def _post_mixer(m, x, mods, ln_g, ln_b, router_w, router_b, alpha, glu_w=None):
    b, l, d = x.shape
    n_e = router_w.shape[1]
    assert n_e == N_EXPERT_GROUPS * EXPERTS_PER_GROUP
    tm = min(l, 512)
    nt = l // tm
    tok = lambda i, j: (i, j, 0)
    const2 = lambda i, j: (0, 0)
    in_specs = [pl.BlockSpec((1, tm, d), tok),
                pl.BlockSpec((1, tm, d), tok),
                pl.BlockSpec((1, 6, d), lambda i, j: (i, 0, 0)),
                pl.BlockSpec((1, d), const2),
                pl.BlockSpec((1, d), const2),
                pl.BlockSpec((n_e, d), const2),
                pl.BlockSpec((n_e, 1), const2)]
    args = [m, x, mods, ln_g.reshape(1, d), ln_b.reshape(1, d), router_w.T, router_b.reshape(n_e, 1)]
    if glu_w is not None:
        in_specs += [pl.BlockSpec((d, d), const2), pl.BlockSpec((d, d), const2)]
        args += [glu_w[0].astype(BF16), glu_w[1].astype(BF16)]
    out_shape = [jax.ShapeDtypeStruct((b, l, d), F32),
                 jax.ShapeDtypeStruct((b * l, d), F32),
                 jax.ShapeDtypeStruct((ROUTE_ROWS, b * l), F32),
                 jax.ShapeDtypeStruct((N_PAIR_CLASSES, LANES), F32)]
    out_specs = [pl.BlockSpec((1, tm, d), tok),
                 pl.BlockSpec((tm, d), lambda i, j: (i * nt + j, 0)),
                 pl.BlockSpec((ROUTE_ROWS, tm), lambda i, j: (0, i * nt + j)),
                 pl.BlockSpec((N_PAIR_CLASSES, LANES), const2)]
    return pl.pallas_call(
        functools.partial(_post_mixer_body, glu=glu_w is not None, alpha=alpha),
        grid=(b, nt),
        in_specs=in_specs,
        out_specs=out_specs,
        out_shape=out_shape,
        scratch_shapes=[pltpu.VMEM((N_PAIR_CLASSES, LANES), F32)],
        compiler_params=_cparams("arbitrary", "arbitrary"),
        name="post_mixer_glu" if glu_w is not None else "post_mixer",
    )(*args)


def _take(table, idx):
    ids = jnp.arange(table.shape[0], dtype=I32)
    return jnp.sum(jnp.where(idx[:, None] == ids[None, :], table[None, :], 0), axis=1)


def _pass_segments():
    seg_cls, seg_hi = [], []
    pairs = [(a, b) for a in range(EXPERTS_PER_GROUP) for b in range(a + 1, EXPERTS_PER_GROUP)]
    for g in range(N_EXPERT_GROUPS):
        for m in range(EXPERTS_PER_GROUP):
            for idx, (a, b) in enumerate(pairs):
                if m in (a, b):
                    seg_cls.append(g * PAIRS_PER_GROUP + idx)
                    seg_hi.append(int(m == b))
    return seg_cls, seg_hi


def _dispatch_plan(route, counts, tile):
    n_cls = counts.shape[0]
    n_tok = route.shape[1]
    cnt = counts[:, 0].astype(I32)
    tiles = (cnt + tile - 1) // tile
    tile_end = jnp.cumsum(tiles)
    tile_off = tile_end - tiles
    cids = jnp.arange(n_cls, dtype=I32)[:, None]
    cls = route[0].astype(I32)
    pos = jnp.sum(jnp.where(cls[None, :] == cids, (tile_off * tile)[:, None], 0), axis=0) + route[1].astype(I32)

    n_row_tiles = n_tok // tile + n_cls
    seg_cls, seg_hi = _pass_segments()
    seg_cls = jnp.asarray(seg_cls, I32)
    seg_hi = jnp.asarray(seg_hi, I32)
    seg_per_expert = EXPERTS_PER_GROUP - 1
    seg_tiles = _take(tiles, seg_cls)
    seg_end = jnp.cumsum(seg_tiles)
    n_used = seg_end[-1]
    p = jnp.arange(2 * n_row_tiles, dtype=I32)
    seg = jnp.minimum(jnp.sum((seg_end[None, :] <= p[:, None]).astype(I32), axis=1), seg_cls.shape[0] - 1)
    within = p - _take(seg_end - seg_tiles, seg)
    spare = p - n_used
    used = p < n_used
    pass_tile = jnp.where(used, _take(tile_off, _take(seg_cls, seg)) + within, tile_end[-1] + spare // 2).astype(I32)
    pass_hi = jnp.where(used, _take(seg_hi, seg), spare % 2).astype(I32)
    pass_expert = jnp.where(used, seg // seg_per_expert, N_EXPERT_GROUPS * EXPERTS_PER_GROUP - 1).astype(I32)
    pass_rows = jnp.where(used, jnp.clip(_take(_take(cnt, seg_cls), seg) - within * tile, 0, tile), 0)
    last_tile = (tile_end - 1).astype(I32)
    return (pos.astype(I32), pass_tile, pass_hi, pass_expert, pass_rows.astype(I32),
            last_tile, tiles.astype(I32), tile_end[-1:].astype(I32), n_row_tiles)


def _zero_tiles(last_ref, tiles_ref, nu_ref, zero_ref, hs_ref, sem, n_e, tile, wait):
    t8 = tile // SUBLANES
    for e in range(n_e):
        @pl.when(tiles_ref[e] > 0)
        def _(e=e):
            cp = pltpu.make_async_copy(zero_ref, hs_ref.at[pl.ds(last_ref[e] * t8, t8)], sem)
            if wait:
                cp.wait()
            else:
                cp.start()

    def body(j, carry):
        cp = pltpu.make_async_copy(zero_ref, hs_ref.at[pl.ds(j * t8, t8)], sem)
        if wait:
            cp.wait()
        else:
            cp.start()
        return carry

    lax.fori_loop(nu_ref[0], hs_ref.shape[0] // t8, body, 0)


def _dispatch_body(pos_ref, last_ref, tiles_ref, nu_ref, h_ref, hs_ref, zero_ref, sem_z, sem, *,
                   tm, tile, n_e):
    i = pl.program_id(0)

    @pl.when(i == 0)
    def _():
        zero_ref[...] = jnp.zeros_like(zero_ref)
        _zero_tiles(last_ref, tiles_ref, nu_ref, zero_ref, hs_ref, sem_z, n_e, tile, False)
        _zero_tiles(last_ref, tiles_ref, nu_ref, zero_ref, hs_ref, sem_z, n_e, tile, True)

    base = i * tm

    def issue(blk, carry):
        for u in range(SUBLANES):
            p = pos_ref[base + blk * SUBLANES + u]
            pltpu.make_async_copy(h_ref.at[blk, pl.ds(u, 1)],
                                  hs_ref.at[p >> 3, pl.ds(p & (SUBLANES - 1), 1)], sem).start(priority=u % 2)
        return carry

    lax.fori_loop(0, tm // SUBLANES, issue, 0, unroll=2)
    pltpu.make_async_copy(h_ref, hs_ref.at[pl.ds(0, tm // SUBLANES)], sem).wait()


def _dispatch(h_rows, pos, last_tile, tiles, n_used, n_rows, tile):
    n, d = h_rows.shape
    n_e = tiles.shape[0]
    tm = min(n, 2048)
    grid_spec = pltpu.PrefetchScalarGridSpec(
        num_scalar_prefetch=4,
        grid=(n // tm,),
        in_specs=[pl.BlockSpec((tm // SUBLANES, SUBLANES, d), lambda i, *_: (i, 0, 0))],
        out_specs=pl.BlockSpec(memory_space=pl.ANY),
        scratch_shapes=[pltpu.VMEM((tile // SUBLANES, SUBLANES, d), F32),
                        pltpu.SemaphoreType.DMA(()), pltpu.SemaphoreType.DMA(())],
    )
    hs = pl.pallas_call(
        functools.partial(_dispatch_body, tm=tm, tile=tile, n_e=n_e),
        grid_spec=grid_spec,
        out_shape=jax.ShapeDtypeStruct((n_rows // SUBLANES, SUBLANES, d), F32),
        compiler_params=_cparams("arbitrary"),
        name="moe_dispatch",
    )(pos, last_tile, tiles, n_used, h_rows.reshape(n // SUBLANES, SUBLANES, d))
    return hs.reshape(n_rows, d)


def _expert_body(pt_ref, ph_ref, te_ref, pr_ref, nx_ref, sl_ref, hs_ref, wg_ref, wu_ref, wd_ref, ys_ref,
                 wgf, wuf, wdf, wgb, wub, wdb, sem, *, layer):
    i = pl.program_id(0)
    rows = pr_ref[i]
    half = hs_ref.shape[0] // 2
    first = jnp.logical_or(i == 0, te_ref[i] != te_ref[jnp.maximum(i - 1, 0)])

    def weight_copies(expert, slot):
        return [pltpu.make_async_copy(w_ref.at[layer, expert], buf.at[slot], sem.at[slot])
                for w_ref, buf in ((wg_ref, wgf), (wu_ref, wuf), (wd_ref, wdf))]

    @pl.when(jnp.logical_and(rows > 0, first))
    def _():
        slot = sl_ref[i]

        @pl.when(i == 0)
        def _():
            for cp in weight_copies(te_ref[i], slot):
                cp.start()

        for cp in weight_copies(te_ref[i], slot):
            cp.wait()

        @pl.when(nx_ref[i] >= 0)
        def _():
            for cp in weight_copies(nx_ref[i], 1 - slot):
                cp.start()

        wgb[...] = wgf[slot].astype(BF16)
        wub[...] = wuf[slot].astype(BF16)
        wdb[...] = wdf[slot].astype(BF16)

    def ffn(n):
        x = hs_ref[0:n, :].astype(BF16)
        gate = jnp.dot(x, wgb[...], preferred_element_type=F32)
        up = jnp.dot(x, wub[...], preferred_element_type=F32)
        a = (gate * jax.nn.sigmoid(gate) * up).astype(BF16)
        ys_ref[0:n, :] = jnp.dot(a, wdb[...], preferred_element_type=F32)

    @pl.when(rows > half)
    def _():
        ffn(2 * half)

    @pl.when(jnp.logical_and(rows > 0, rows <= half))
    def _():
        ffn(half)
        ys_ref[half:, :] = jnp.zeros((half, ys_ref.shape[1]), F32)

    @pl.when(rows == 0)
    def _():
        ys_ref[...] = jnp.zeros_like(ys_ref)


def _expert_ffn(hs, pass_tile, pass_hi, pass_expert, pass_rows, w_gate, w_up, w_down, layer, tile):
    _, n_e, d, f = w_gate.shape
    n_rows = hs.shape[0]
    eids = jnp.arange(n_e, dtype=I32)
    n_pass_e = jnp.sum(jnp.where((pass_expert[None, :] == eids[:, None]) & (pass_rows[None, :] > 0), 1, 0), axis=1)
    has = n_pass_e > 0
    later = (eids[None, :] > eids[:, None]) & has[None, :]
    next_used = jnp.min(jnp.where(later, eids[None, :], n_e), axis=1)
    next_used = jnp.where(next_used == n_e, -1, next_used).astype(I32)
    slot_e = ((jnp.cumsum(has.astype(I32)) - 1) % 2).astype(I32)
    pass_next = _take(next_used, pass_expert)
    pass_slot = jnp.maximum(_take(slot_e, pass_expert), 0)
    any_spec = pl.BlockSpec(memory_space=pl.ANY)
    grid_spec = pltpu.PrefetchScalarGridSpec(
        num_scalar_prefetch=6,
        grid=(pass_tile.shape[0],),
        in_specs=[pl.BlockSpec((tile, d), lambda i, pt, *_: (pt[i], 0)), any_spec, any_spec, any_spec],
        out_specs=pl.BlockSpec((tile, d), lambda i, pt, ph, *_: (pt[i], ph[i])),
        scratch_shapes=[pltpu.VMEM((2, d, f), F32), pltpu.VMEM((2, d, f), F32), pltpu.VMEM((2, f, d), F32),
                        pltpu.VMEM((d, f), BF16), pltpu.VMEM((d, f), BF16), pltpu.VMEM((f, d), BF16),
                        pltpu.SemaphoreType.DMA((2,))],
    )
    return pl.pallas_call(
        functools.partial(_expert_body, layer=layer),
        grid_spec=grid_spec,
        out_shape=jax.ShapeDtypeStruct((n_rows, 2 * d), F32),
        compiler_params=_cparams("arbitrary"),
        name="moe_experts",
    )(pass_tile, pass_hi, pass_expert, pass_rows, pass_next, pass_slot, hs, w_gate, w_up, w_down)


def _combine_body(pos_ref, x_ref, mod_ref, wts_ref, lng_ref, lnb_ref, ys_ref, *rest,
                  tm, alpha, next_mod):
    if next_mod:
        nmod_ref, x2_ref, h_ref, buf, sem = rest
    else:
        x2_ref, buf, sem = rest
    step = pl.program_id(0) * pl.num_programs(1) + pl.program_id(1)
    n_steps = pl.num_programs(0) * pl.num_programs(1)
    d = x_ref.shape[2]

    def gather(s, slot):
        def issue(blk, carry):
            for u in range(SUBLANES):
                p = pos_ref[s * tm + blk * SUBLANES + u]
                pltpu.make_async_copy(ys_ref.at[p >> 3, pl.ds(p & (SUBLANES - 1), 1)],
                                      buf.at[slot, blk, pl.ds(u, 1)], sem.at[slot]).start(priority=u % 2)
            return carry

        lax.fori_loop(0, tm // SUBLANES, issue, 0, unroll=2)

    @pl.when(step == 0)
    def _():
        gather(0, 0)

    @pl.when(step + 1 < n_steps)
    def _():
        gather(step + 1, (step + 1) % 2)

    slot = step % 2
    pltpu.make_async_copy(ys_ref.at[pl.ds(0, tm // SUBLANES)], buf.at[slot], sem.at[slot]).wait()

    w = wts_ref[...]
    rows = buf[slot].reshape(tm, 2 * d)
    moe = w[:, 0:1] * rows[:, :d] + w[:, 1:2] * rows[:, d:]
    g2 = mod_ref[0, 5:6, :]
    x2 = _layer_norm(alpha * x_ref[0] + g2 * moe, lng_ref[...], lnb_ref[...])
    x2_ref[0] = x2
    if next_mod:
        h_ref[0] = x2 * (1.0 + nmod_ref[0, 1:2, :]) + nmod_ref[0, 0:1, :]


def _combine(ys, pos, wts, x, mods, ln_g, ln_b, alpha, next_mods=None):
    b, l, d = x.shape
    tm = min(l, 512)
    nt = l // tm
    tok = lambda i, j, *_: (i, j, 0)
    bat = lambda i, j, *_: (i, 0, 0)
    const2 = lambda i, j, *_: (0, 0)
    in_specs = [pl.BlockSpec((1, tm, d), tok),
                pl.BlockSpec((1, 6, d), bat),
                pl.BlockSpec((tm, 2), lambda i, j, *_: (i * nt + j, 0)),
                pl.BlockSpec((1, d), const2),
                pl.BlockSpec((1, d), const2),
                pl.BlockSpec(memory_space=pl.ANY)]
    ys_tiles = ys.reshape(ys.shape[0] // SUBLANES, SUBLANES, ys.shape[1])
    args = [x, mods, wts, ln_g.reshape(1, d), ln_b.reshape(1, d), ys_tiles]
    out_shape = [jax.ShapeDtypeStruct((b, l, d), F32)]
    out_specs = [pl.BlockSpec((1, tm, d), tok)]
    if next_mods is not None:
        in_specs.append(pl.BlockSpec((1, 6, d), bat))
        args.append(next_mods)
        out_shape.append(jax.ShapeDtypeStruct((b, l, d), F32))
        out_specs.append(pl.BlockSpec((1, tm, d), tok))
    grid_spec = pltpu.PrefetchScalarGridSpec(
        num_scalar_prefetch=1,
        grid=(b, nt),
        in_specs=in_specs,
        out_specs=out_specs,
        scratch_shapes=[pltpu.VMEM((2, tm // SUBLANES, SUBLANES, 2 * d), F32),
                        pltpu.SemaphoreType.DMA((2,))],
    )
    return pl.pallas_call(
        functools.partial(_combine_body, tm=tm, alpha=alpha, next_mod=next_mods is not None),
        grid_spec=grid_spec,
        out_shape=out_shape,
        compiler_params=_cparams("arbitrary", "arbitrary"),
        name="moe_combine",
    )(pos, *args)


MOE_TILE = 256


def _moe(h_rows, route, counts, w_gate, w_up, w_down, layer):
    n = h_rows.shape[0]
    tile = min(n, MOE_TILE)
    (pos, pass_tile, pass_hi, pass_expert, pass_rows, last_tile, tiles, n_used_tiles,
     n_row_tiles) = _dispatch_plan(route, counts, tile)
    hs = _dispatch(h_rows, pos, last_tile, tiles, n_used_tiles, n_row_tiles * tile, tile)
    ys = _expert_ffn(hs, pass_tile, pass_hi, pass_expert, pass_rows, w_gate, w_up, w_down, layer, tile)
    return ys, pos, jnp.transpose(route[2:4])


def _pool_group(h_ref, w_ref, sc_ref, o_ref, col_ref, k, n_rows):
    n = n_rows * GRID_W
    c = h_ref.shape[2]
    blk = 4 * GRID_W
    half = k // 2
    pad = half * GRID_W
    ti = lax.broadcasted_iota(I32, (blk, blk), 0)
    si = lax.broadcasted_iota(I32, (blk, blk), 1)
    shift = GRID_W.bit_length() - 1
    same_row = (ti >> shift) == (si >> shift)
    band = jnp.where(same_row & (si - ti >= -half) & (si - ti <= half - 1), 1.0, 0.0).astype(BF16)
    col_ref[0:pad, :] = jnp.zeros((pad, c), F32)
    col_ref[pad + n:pad + n + pad, :] = jnp.zeros((pad, c), F32)
    for b0 in range(0, n, blk):
        hb = h_ref[0, b0:b0 + blk, :]
        head = hb.astype(BF16)
        rest = (hb - head.astype(F32)).astype(BF16)
        col_ref[pad + b0:pad + b0 + blk, :] = (jnp.dot(band, head, preferred_element_type=F32)
                                               + jnp.dot(band, rest, preferred_element_type=F32))
    acc = col_ref[0:n, :]
    for j in range(1, k):
        acc = acc + col_ref[j * GRID_W:j * GRID_W + n, :]
    t = lax.broadcasted_iota(I32, (n, 1), 0)
    wc = t & (GRID_W - 1)
    wr = t >> shift
    cnt_c = jnp.minimum(wc + half - 1, GRID_W - 1) - jnp.maximum(wc - half, 0) + 1
    cnt_r = jnp.minimum(wr + half - 1, n_rows - 1) - jnp.maximum(wr - half, 0) + 1
    mean = acc / (cnt_c * cnt_r).astype(F32)
    pooled = (mean - h_ref[0]).astype(BF16)
    o_ref[0] = jnp.dot(pooled, w_ref[0].astype(BF16), preferred_element_type=F32) * sc_ref[...]


def _pool_body(h_ref, w_ref, sc_ref, o_ref, col_ref, *, n_rows):
    g = pl.program_id(1)
    for gi, k in enumerate(POOL_WINDOWS):
        @pl.when(g == gi)
        def _(k=k):
            _pool_group(h_ref, w_ref, sc_ref, o_ref, col_ref, k, n_rows)


def _pool_mix(h, w_grp, scale):
    b, n, d = h.shape
    n_g, c, _ = w_grp.shape
    n_rows = n // GRID_W
    pad = (max(POOL_WINDOWS) // 2) * GRID_W
    return pl.pallas_call(
        functools.partial(_pool_body, n_rows=n_rows),
        grid=(b, n_g),
        in_specs=[pl.BlockSpec((1, n, c), lambda i, j: (i, 0, j)),
                  pl.BlockSpec((1, c, c), lambda i, j: (j, 0, 0)),
                  pl.BlockSpec((1, c), lambda i, j: (0, j))],
        out_specs=pl.BlockSpec((1, n, c), lambda i, j: (i, 0, j)),
        out_shape=jax.ShapeDtypeStruct((b, n, d), F32),
        scratch_shapes=[pltpu.VMEM((n + 2 * pad, c), F32)],
        compiler_params=_cparams("arbitrary", "arbitrary"),
        name="pool_mix",
    )(h, w_grp, scale.reshape(1, d))


def kernel(x, c, ctx, c_ctx, mod_w, mod_b, ln_g, ln_b, s5_lam_re, s5_lam_im, s5_log_dt, s5_b_re, s5_b_im,
           s5_c_re, s5_c_im, s5_d, s5_w_val, s5_w_gate, pool_w, pool_scale, router_w, router_b,
           moe_w_gate, moe_w_up, moe_w_down):
    b, l, d = x.shape
    depth = mod_w.shape[0]
    assert depth == 2 and b + 1 <= SUBLANES and d % LANES == 0 and GRID_W & (GRID_W - 1) == 0
    alpha = (2 * depth) ** 0.25

    cond = jnp.zeros((SUBLANES, d), F32).at[:b].set(c).at[b].set(c_ctx)
    mods = _modulation(cond, mod_w, mod_b).reshape(depth, SUBLANES, 6, d)

    toep, w1, w2t, a_r, a_i = _s5_weights(s5_lam_re[0], s5_lam_im[0], s5_log_dt[0], s5_b_re[0], s5_b_im[0],
                                          s5_c_re[0], s5_c_im[0], s5_d[0])
    y = _s5_mix(x, ctx, mods[0, :b], mods[0, b:b + 1], toep, w1, w2t, a_r, a_i)
    x1, h_rows, route, counts = _post_mixer(y, x, mods[0, :b], ln_g[0, 0], ln_b[0, 0], router_w, router_b,
                                            alpha, glu_w=(s5_w_val[0], s5_w_gate[0]))
    ys, pos, wts = _moe(h_rows, route, counts, moe_w_gate, moe_w_up, moe_w_down, 0)
    x2, h = _combine(ys, pos, wts, x1, mods[0, :b], ln_g[0, 1], ln_b[0, 1], alpha, next_mods=mods[1, :b])

    m = _pool_mix(h, pool_w[0], pool_scale[0])
    x3, h_rows, route, counts = _post_mixer(m, x2, mods[1, :b], ln_g[1, 0], ln_b[1, 0], router_w, router_b,
                                            alpha)
    ys, pos, wts = _moe(h_rows, route, counts, moe_w_gate, moe_w_up, moe_w_down, 1)
    (out,) = _combine(ys, pos, wts, x3, mods[1, :b], ln_g[1, 1], ln_b[1, 1], alpha)
    return out
```

```python
import functools

import jax
import jax.numpy as jnp
from jax import lax
from jax.experimental import pallas as pl
from jax.experimental.pallas import tpu as pltpu

F32 = jnp.float32
BF16 = jnp.bfloat16
I32 = jnp.int32
HI = lax.Precision.HIGHEST

GRID_W = 64
S5_H = 16
S5_T = 16
POOL_WINDOWS = (2, 4, 8, 16)
N_EXPERT_GROUPS = 4
LN_EPS = 1e-5
LANES = 128
SUBLANES = 8
VMEM_LIMIT = 52 * 1024 * 1024

EXPERTS_PER_GROUP = 4
PAIRS_PER_GROUP = EXPERTS_PER_GROUP * (EXPERTS_PER_GROUP - 1) // 2
N_PAIR_CLASSES = N_EXPERT_GROUPS * PAIRS_PER_GROUP
ROUTE_ROWS = 8


def _cparams(*sem):
    return pltpu.CompilerParams(dimension_semantics=sem, vmem_limit_bytes=VMEM_LIMIT)


def _mod_body(c_ref, w_ref, b_ref, o_ref):
    c = c_ref[...]
    s = c * jax.nn.sigmoid(c)
    o_ref[0] = jnp.dot(s, w_ref[0], precision=HI, preferred_element_type=F32) + b_ref[0]


def _modulation(cond, mod_w, mod_b):
    depth, d, n6 = mod_w.shape
    tn = min(n6, 1536)
    return pl.pallas_call(
        _mod_body,
        grid=(depth, n6 // tn),
        in_specs=[pl.BlockSpec((SUBLANES, d), lambda i, j: (0, 0)),
                  pl.BlockSpec((1, d, tn), lambda i, j: (i, 0, j)),
                  pl.BlockSpec((1, 1, tn), lambda i, j: (i, 0, j))],
        out_specs=pl.BlockSpec((1, SUBLANES, tn), lambda i, j: (i, 0, j)),
        out_shape=jax.ShapeDtypeStruct((depth, SUBLANES, n6), F32),
        compiler_params=_cparams("arbitrary", "arbitrary"),
        name="modulation",
    )(cond, mod_w, mod_b.reshape(depth, 1, n6))


def _s5_direction_terms(lam_re, lam_im, log_dt, b_re, b_im):
    lr = lam_re.astype(F32)
    li = lam_im.astype(F32)
    dt = jnp.exp(log_dt.astype(F32))[:, None]
    mag = jnp.exp(lr * dt)
    ar = mag * jnp.cos(li * dt)
    ai = mag * jnp.sin(li * dt)
    den = lr * lr + li * li
    nr = ar - 1.0
    fr = (nr * lr + ai * li) / den
    fi = (ai * lr - nr * li) / den
    br_, bi_ = b_re.astype(F32), b_im.astype(F32)
    bbr = fr[..., None] * br_ - fi[..., None] * bi_
    bbi = fr[..., None] * bi_ + fi[..., None] * br_
    k = jnp.arange(S5_T + 1, dtype=F32)[:, None, None]
    pm = jnp.exp(k * (lr * dt))
    pr = pm * jnp.cos(k * (li * dt))
    pi = pm * jnp.sin(k * (li * dt))
    return pr, pi, bbr, bbi


def _s5_weights(lam_re, lam_im, log_dt, b_re, b_im, c_re, c_im, d_skip):
    t = S5_T
    g, p = lam_re.shape[1:]
    h = b_re.shape[-1]
    terms = [_s5_direction_terms(lam_re[d], lam_im[d], log_dt[d], b_re[d], b_im[d]) for d in (0, 1)]
    pw = jnp.stack([jnp.transpose(terms[d][k], (1, 0, 2)) for d in (0, 1) for k in (0, 1)], axis=1)
    bt = jnp.stack([jnp.transpose(terms[d][k], (0, 2, 1)) for d in (0, 1) for k in (2, 3)], axis=1)
    cc = jnp.stack([c[d].astype(F32) for d in (0, 1) for c in (c_re, c_im)], axis=1)
    gp = 8
    spec4 = lambda rows: pl.BlockSpec((gp, 4, rows, p), lambda i: (i, 0, 0, 0))
    wide = pl.BlockSpec((gp, t * h, t * h), lambda i: (i, 0, 0))
    toep, w1, w2t = pl.pallas_call(
        functools.partial(_s5_prep_body, gp=gp),
        grid=(g // gp,),
        in_specs=[spec4(t + 1), spec4(h), spec4(h), pl.BlockSpec((gp, 1, h), lambda i: (i, 0, 0))],
        out_specs=[wide, wide, wide],
        out_shape=[jax.ShapeDtypeStruct((g, t * h, t * h), BF16) for _ in range(3)],
        compiler_params=_cparams("arbitrary"),
        name="s5_prep",
    )(pw, bt, cc, d_skip.astype(F32).reshape(g, 1, h))
    a_r = jnp.concatenate([pw[:, 0, t], pw[:, 2, t]], axis=-1)[:, None, :]
    a_i = jnp.concatenate([pw[:, 1, t], pw[:, 3, t]], axis=-1)[:, None, :]
    return toep, w1, w2t, a_r, a_i


def _cmul(ar, ai, br, bi):
    return ar * br - ai * bi, ar * bi + ai * br


def _s5_prep_body(pw_ref, bt_ref, cc_ref, d_ref, toep_ref, w1_ref, w2t_ref, *, gp):
    t = pw_ref.shape[2] - 1
    h = bt_ref.shape[2]
    nt = (((1,), (1,)), ((), ()))
    eye = (lax.broadcasted_iota(I32, (h, h), 0) == lax.broadcasted_iota(I32, (h, h), 1))
    for g in range(gp):
        prf, pif, prb, pib = (pw_ref[g, k] for k in range(4))
        btf = (bt_ref[g, 0], bt_ref[g, 1])
        btb = (bt_ref[g, 2], bt_ref[g, 3])
        ccf = (cc_ref[g, 0], cc_ref[g, 1])
        ccb = (cc_ref[g, 2], cc_ref[g, 3])
        caf, cab = [], []
        for j in range(t):
            f_r, f_i = _cmul(*btf, prf[t - 1 - j:t - j], pif[t - 1 - j:t - j])
            b_r, b_i = _cmul(*btb, prb[j:j + 1], pib[j:j + 1])
            w1_ref[g, j * h:(j + 1) * h, :] = jnp.concatenate([f_r, b_r, f_i, b_i], axis=1).astype(BF16)
            mf_r, mf_i = _cmul(*ccf, prf[j + 1:j + 2], pif[j + 1:j + 2])
            mb_r, mb_i = _cmul(*ccb, prb[t - j:t - j + 1], pib[t - j:t - j + 1])
            w2t_ref[g, j * h:(j + 1) * h, :] = jnp.concatenate([mf_r, mb_r, -mf_i, -mb_i],
                                                               axis=1).astype(BF16)
            caf.append(_cmul(*ccf, prf[j:j + 1], pif[j:j + 1]))
            cab.append(_cmul(*ccb, prb[t - 1 - j:t - j], pib[t - 1 - j:t - j]))

        def lag_kernels(btx, ca):
            car = jnp.concatenate([c[0] for c in ca], axis=0)
            cai = jnp.concatenate([c[1] for c in ca], axis=0)
            return (lax.dot_general(btx[0], car, nt, precision=HI, preferred_element_type=F32)
                    - lax.dot_general(btx[1], cai, nt, precision=HI, preferred_element_type=F32))

        kf = lag_kernels(btf, caf)
        kb = lag_kernels(btb, cab)
        skip = jnp.where(eye, jnp.broadcast_to(d_ref[g], (h, h)), 0.0)
        mid = kb[:, (t - 1) * h:] + kf[:, :h] + skip
        kwide = jnp.concatenate([kb[:, :(t - 1) * h], mid, kf[:, h:], jnp.zeros((h, h), F32)], axis=1)
        for i in range(t):
            off = (t - 1 - i) * h
            toep_ref[g, i * h:(i + 1) * h, :] = kwide[:, off:off + t * h].astype(BF16)


def _granule_transpose(v):
    n = len(v)
    gran = lax.broadcasted_iota(I32, v[0].shape, 1) >> 4
    at = [gran == q for q in range(n)]
    rot = []
    for d in range(n):
        m = v[d]
        for q in range(1, n):
            m = jnp.where(at[q], v[(q + d) % n], m)
        rot.append(pltpu.roll(m, d * S5_H, 1) if d else m)
    out = []
    for q in range(n):
        w = rot[(-q) % n]
        for j in range(1, n):
            w = jnp.where(at[j], rot[(j - q) % n], w)
        out.append(w)
    return out


def _s5_body(x_ref, ctx_ref, mod_ref, cmod_ref, tt_ref, w1_ref, w2t_ref, ar_ref, ai_ref, y_ref,
             u_ref, s_ref, *, n_ctx_chunks, n_lat_chunks, gb):
    n_b = x_ref.shape[0]
    n_g = u_ref.shape[0]
    n_blocks = s_ref.shape[1] // SUBLANES
    half = SUBLANES // 2
    tok_blk = SUBLANES * S5_T
    lat_lo = n_ctx_chunks * n_b
    lat_rows = n_lat_chunks * n_b

    def chunk_rows(ref, b, tok0, shift, scale1):
        vs = [ref[b, pl.ds(tok0 + j, SUBLANES, stride=S5_T), :] * scale1 + shift for j in range(S5_T)]
        lo = _granule_transpose(vs[:SUBLANES])
        hi = _granule_transpose(vs[SUBLANES:])
        return lo, hi

    def put_rows(rows, chunk0, b):
        for q in range(n_g):
            for jh in range(2):
                u_ref[q, jh, pl.ds(chunk0 * n_b + b, SUBLANES, stride=n_b), :] = rows[jh][q]

    def u_rows(g, lo, n):
        return jnp.concatenate([u_ref[g, 0, lo:lo + n, :], u_ref[g, 1, lo:lo + n, :]], axis=1)

    shift = jnp.broadcast_to(cmod_ref[0, 0:1, :], (SUBLANES, LANES))
    scale1 = 1.0 + jnp.broadcast_to(cmod_ref[0, 1:2, :], (SUBLANES, LANES))
    for b in range(n_b):
        for cb in range(n_ctx_chunks // SUBLANES):
            rows = chunk_rows(ctx_ref, b, cb * tok_blk, shift, scale1)
            put_rows(rows, cb * SUBLANES, b)
            put_rows(rows, n_ctx_chunks + n_lat_chunks + cb * SUBLANES, b)

    def fill(cb, carry):
        for b in range(n_b):
            shift = jnp.broadcast_to(mod_ref[b, 0:1, :], (SUBLANES, LANES))
            scale1 = 1.0 + jnp.broadcast_to(mod_ref[b, 1:2, :], (SUBLANES, LANES))
            put_rows(chunk_rows(x_ref, b, cb * tok_blk, shift, scale1), n_ctx_chunks + cb * SUBLANES, b)
        return carry

    lax.fori_loop(0, n_lat_chunks // SUBLANES, fill, 0)

    for g in range(n_g):
        s_ref[g] = jnp.dot(u_rows(g, 0, n_blocks * SUBLANES).astype(BF16), w1_ref[g],
                           preferred_element_type=F32)

    lane = lax.broadcasted_iota(I32, (SUBLANES, LANES), 1)
    row = lax.broadcasted_iota(I32, (SUBLANES, LANES), 0)
    is_fwd = lane < LANES // 2
    is_fwd2 = jnp.concatenate([is_fwd, is_fwd], axis=1)
    top = row < half
    zero = jnp.zeros((SUBLANES, LANES), F32)
    for g0 in range(0, n_g, gb):
        ars = [jnp.broadcast_to(ar_ref[g0 + g], (SUBLANES, LANES)) for g in range(gb)]
        ais = [jnp.broadcast_to(ai_ref[g0 + g], (SUBLANES, LANES)) for g in range(gb)]

        def step(k, carry, g0=g0, ars=ars, ais=ais):
            fo = pl.multiple_of(k * SUBLANES, SUBLANES)
            bo = pl.multiple_of((n_blocks - 1 - k) * SUBLANES, SUBLANES)
            new = []
            loaded = [(s_ref[g0 + g, pl.ds(fo, SUBLANES), :], s_ref[g0 + g, pl.ds(bo, SUBLANES), :])
                      for g in range(gb)]
            stores = []
            for g in range(gb):
                xr, xi = carry[2 * g], carry[2 * g + 1]
                vf, vb_raw = loaded[g]
                vb = pltpu.roll(vb_raw, half, 0)
                vr = jnp.where(is_fwd, vf[:, :LANES], vb[:, :LANES])
                vi = jnp.where(is_fwd, vf[:, LANES:], vb[:, LANES:])
                ar, ai = ars[g], ais[g]
                yr = ar * xr - ai * xi + vr
                yi = ar * xi + ai * xr + vi
                yrr = pltpu.roll(yr, half, 0)
                yir = pltpu.roll(yi, half, 0)
                zr = ar * yrr - ai * yir + vr
                zi = ar * yir + ai * yrr + vi
                inc = jnp.concatenate([jnp.where(top, xr, yrr), jnp.where(top, xi, yir)], axis=1)
                stores.append((jnp.where(is_fwd2, inc, vf),
                               jnp.where(is_fwd2, vb_raw, pltpu.roll(inc, half, 0))))
                new.append(jnp.where(top, pltpu.roll(zr, half, 0), zr))
                new.append(jnp.where(top, pltpu.roll(zi, half, 0), zi))
            for g in range(gb):
                s_ref[g0 + g, pl.ds(fo, SUBLANES), :] = stores[g][0]
                s_ref[g0 + g, pl.ds(bo, SUBLANES), :] = stores[g][1]
            return tuple(new)

        lax.fori_loop(0, (n_ctx_chunks + n_lat_chunks) // 2, step, tuple(zero for _ in range(2 * gb)))

    for g in range(n_g):
        y = (jnp.dot(u_rows(g, lat_lo, lat_rows).astype(BF16), tt_ref[g], preferred_element_type=F32)
             + lax.dot_general(s_ref[g, lat_lo:lat_lo + lat_rows, :].astype(BF16), w2t_ref[g],
                               (((1,), (1,)), ((), ())), preferred_element_type=F32))
        for jh in range(2):
            u_ref[g, jh, lat_lo:lat_lo + lat_rows, :] = y[:, jh * LANES:(jh + 1) * LANES]

    def emit(cb, carry):
        for b in range(n_b):
            for jh in range(S5_T // SUBLANES):
                w = [u_ref[q, jh, pl.ds((n_ctx_chunks + cb * SUBLANES) * n_b + b, SUBLANES, stride=n_b), :]
                     for q in range(n_g)]
                v = _granule_transpose(w)
                for j in range(SUBLANES):
                    y_ref[b, pl.ds(cb * tok_blk + jh * SUBLANES + j, SUBLANES, stride=S5_T), :] = v[j]
        return carry

    lax.fori_loop(0, n_lat_chunks // SUBLANES, emit, 0)


def _s5_mix(x, ctx, mods, cmods, toep, w1, w2t, a_r, a_i):
    b, l, d = x.shape
    n_ctx = ctx.shape[1]
    n_g = LANES // S5_H
    n_ctx_chunks, n_lat_chunks = n_ctx // S5_T, l // S5_T
    assert b * 2 == SUBLANES and n_ctx_chunks % SUBLANES == 0 and n_lat_chunks % SUBLANES == 0
    rows = (2 * n_ctx_chunks + n_lat_chunks) * b
    w = S5_T * S5_H
    body = functools.partial(_s5_body, n_ctx_chunks=n_ctx_chunks, n_lat_chunks=n_lat_chunks, gb=8)
    lane_tile = lambda i: (0, 0, i)
    wspec = pl.BlockSpec((n_g, w, w), lambda i: (i, 0, 0))
    aspec = pl.BlockSpec((n_g, 1, w // 2), lambda i: (i, 0, 0))
    return pl.pallas_call(
        body,
        grid=(d // LANES,),
        in_specs=[pl.BlockSpec((b, l, LANES), lane_tile),
                  pl.BlockSpec((b, n_ctx, LANES), lane_tile),
                  pl.BlockSpec((b, 6, LANES), lane_tile),
                  pl.BlockSpec((1, 6, LANES), lane_tile),
                  wspec, wspec, wspec, aspec, aspec],
        out_specs=pl.BlockSpec((b, l, LANES), lane_tile, pipeline_mode=pl.Buffered(1)),
        out_shape=jax.ShapeDtypeStruct((b, l, d), F32),
        scratch_shapes=[pltpu.VMEM((n_g, w // LANES, rows, LANES), F32), pltpu.VMEM((n_g, rows, w), F32)],
        compiler_params=_cparams("arbitrary"),
        name="s5_mix",
    )(x, ctx, mods, cmods, toep, w1, w2t, a_r, a_i)


def _layer_norm(r, g, b):
    mu = jnp.mean(r, axis=-1, keepdims=True)
    xc = r - mu
    var = jnp.mean(xc * xc, axis=-1, keepdims=True)
    return xc * lax.rsqrt(var + LN_EPS) * g + b


def _max2_of4(a, b, c, d):
    h1, l1 = jnp.maximum(a, b), jnp.minimum(a, b)
    h2, l2 = jnp.maximum(c, d), jnp.minimum(c, d)
    return jnp.maximum(h1, h2) + jnp.maximum(jnp.minimum(h1, h2), jnp.maximum(l1, l2))


def _argmax_first(vals):
    idx = jnp.zeros(vals[0].shape, I32)
    best = vals[0]
    for j in range(1, len(vals)):
        upd = vals[j] > best
        idx = jnp.where(upd, j, idx)
        best = jnp.where(upd, vals[j], best)
    return idx, best


def _route(logits_t, count_ref, route_ref):
    n_e, tm = logits_t.shape
    per = n_e // N_EXPERT_GROUPS
    mx = jnp.max(logits_t, axis=0, keepdims=True)
    ex = jnp.exp(logits_t - mx)
    sc = ex / jnp.sum(ex, axis=0, keepdims=True)
    rows = [sc[e:e + 1, :] for e in range(n_e)]
    gscore = [_max2_of4(*rows[per * g:per * (g + 1)]) for g in range(N_EXPERT_GROUPS)]
    best, _ = _argmax_first(gscore)
    vals = []
    for j in range(per):
        v = rows[per * (N_EXPERT_GROUPS - 1) + j]
        for g in range(N_EXPERT_GROUPS - 2, -1, -1):
            v = jnp.where(best == g, rows[per * g + j], v)
        vals.append(v)
    i1, m1 = _argmax_first(vals)
    i2, m2 = _argmax_first([jnp.where(i1 == j, -1.0, vals[j]) for j in range(per)])
    den = m1 + m2
    first_lo = i1 < i2
    lo = jnp.minimum(i1, i2)
    hi = jnp.maximum(i1, i2)
    pair = jnp.where(lo == 0, 0, jnp.where(lo == 1, per - 1, 2 * per - 3)) + hi - lo - 1
    cls = best * PAIRS_PER_GROUP + pair
    w_lo = jnp.where(first_lo, m1, m2) / den
    w_hi = jnp.where(first_lo, m2, m1) / den

    n_cls = count_ref.shape[0]
    hit = lax.broadcasted_iota(I32, (n_cls, tm), 0) == cls
    onehot = jnp.where(hit, 1.0, 0.0)
    src = lax.broadcasted_iota(I32, (tm, tm), 0)
    dst = lax.broadcasted_iota(I32, (tm, tm), 1)
    tri = jnp.where(src <= dst, 1.0, 0.0).astype(BF16)
    cum = jnp.dot(onehot.astype(BF16), tri, preferred_element_type=F32)
    excl = cum - onehot + count_ref[:, 0:1]
    rank = jnp.sum(jnp.where(hit, excl, 0.0), axis=0, keepdims=True)
    count_ref[...] = count_ref[...] + jnp.sum(onehot, axis=1, keepdims=True)

    zero = jnp.zeros((1, tm), F32)
    route_ref[...] = jnp.concatenate([cls.astype(F32), rank, w_lo, w_hi, zero, zero, zero, zero], axis=0)


def _post_mixer_body(m_ref, x_ref, mod_ref, lng_ref, lnb_ref, rwt_ref, rb_ref, *rest, glu, alpha):
    if glu:
        wv_ref, wg_ref, x1_ref, h_ref, route_ref, cnt_out_ref, cnt_ref = rest
        a = jax.nn.gelu(m_ref[0], approximate=True).astype(BF16)
        val = jnp.dot(a, wv_ref[...], preferred_element_type=F32)
        gate = jnp.dot(a, wg_ref[...], preferred_element_type=F32)
        m = val * jax.nn.sigmoid(gate)
    else:
        x1_ref, h_ref, route_ref, cnt_out_ref, cnt_ref = rest
        m = m_ref[0]

    @pl.when((pl.program_id(0) == 0) & (pl.program_id(1) == 0))
    def _():
        cnt_ref[...] = jnp.zeros_like(cnt_ref)

    g1 = mod_ref[0, 2:3, :]
    sh2 = mod_ref[0, 3:4, :]
    sc2 = mod_ref[0, 4:5, :]
    x1 = _layer_norm(alpha * x_ref[0] + g1 * m, lng_ref[...], lnb_ref[...])
    x1_ref[0] = x1
    h = x1 * (1.0 + sc2) + sh2
    h_ref[...] = h
    nt = (((1,), (1,)), ((), ()))
    rw = rwt_ref[...]
    rw_head = rw.astype(BF16)
    rw_rest = (rw - rw_head.astype(F32)).astype(BF16)
    h_head = h.astype(BF16)
    h_rest = (h - h_head.astype(F32)).astype(BF16)
    logits_t = (lax.dot_general(rw_head, h_head, nt, preferred_element_type=F32)
                + lax.dot_general(rw_head, h_rest, nt, preferred_element_type=F32)
                + lax.dot_general(rw_rest, h_head, nt, preferred_element_type=F32)) + rb_ref[...]
    _route(logits_t, cnt_ref, route_ref)
    cnt_out_ref[...] = cnt_ref[...]


def _post_mixer(m, x, mods, ln_g, ln_b, router_w, router_b, alpha, glu_w=None):
    b, l, d = x.shape
    n_e = router_w.shape[1]
    assert n_e == N_EXPERT_GROUPS * EXPERTS_PER_GROUP
    tm = min(l, 512)
    nt = l // tm
    tok = lambda i, j: (i, j, 0)
    const2 = lambda i, j: (0, 0)
    in_specs = [pl.BlockSpec((1, tm, d), tok),
                pl.BlockSpec((1, tm, d), tok),
                pl.BlockSpec((1, 6, d), lambda i, j: (i, 0, 0)),
                pl.BlockSpec((1, d), const2),
                pl.BlockSpec((1, d), const2),
                pl.BlockSpec((n_e, d), const2),
                pl.BlockSpec((n_e, 1), const2)]
    args = [m, x, mods, ln_g.reshape(1, d), ln_b.reshape(1, d), router_w.T, router_b.reshape(n_e, 1)]
    if glu_w is not None:
        in_specs += [pl.BlockSpec((d, d), const2), pl.BlockSpec((d, d), const2)]
        args += [glu_w[0].astype(BF16), glu_w[1].astype(BF16)]
    out_shape = [jax.ShapeDtypeStruct((b, l, d), F32),
                 jax.ShapeDtypeStruct((b * l, d), F32),
                 jax.ShapeDtypeStruct((ROUTE_ROWS, b * l), F32),
                 jax.ShapeDtypeStruct((N_PAIR_CLASSES, LANES), F32)]
    out_specs = [pl.BlockSpec((1, tm, d), tok),
                 pl.BlockSpec((tm, d), lambda i, j: (i * nt + j, 0)),
                 pl.BlockSpec((ROUTE_ROWS, tm), lambda i, j: (0, i * nt + j)),
                 pl.BlockSpec((N_PAIR_CLASSES, LANES), const2)]
    return pl.pallas_call(
        functools.partial(_post_mixer_body, glu=glu_w is not None, alpha=alpha),
        grid=(b, nt),
        in_specs=in_specs,
        out_specs=out_specs,
        out_shape=out_shape,
        scratch_shapes=[pltpu.VMEM((N_PAIR_CLASSES, LANES), F32)],
        compiler_params=_cparams("arbitrary", "arbitrary"),
        name="post_mixer_glu" if glu_w is not None else "post_mixer",
    )(*args)


def _take(table, idx):
    ids = jnp.arange(table.shape[0], dtype=I32)
    return jnp.sum(jnp.where(idx[:, None] == ids[None, :], table[None, :], 0), axis=1)


def _pass_segments():
    seg_cls, seg_hi = [], []
    pairs = [(a, b) for a in range(EXPERTS_PER_GROUP) for b in range(a + 1, EXPERTS_PER_GROUP)]
    for g in range(N_EXPERT_GROUPS):
        for m in range(EXPERTS_PER_GROUP):
            for idx, (a, b) in enumerate(pairs):
                if m in (a, b):
                    seg_cls.append(g * PAIRS_PER_GROUP + idx)
                    seg_hi.append(int(m == b))
    return seg_cls, seg_hi


def _dispatch_plan(route, counts, tile):
    n_cls = counts.shape[0]
    n_tok = route.shape[1]
    cnt = counts[:, 0].astype(I32)
    tiles = (cnt + tile - 1) // tile
    tile_end = jnp.cumsum(tiles)
    tile_off = tile_end - tiles
    cids = jnp.arange(n_cls, dtype=I32)[:, None]
    cls = route[0].astype(I32)
    pos = jnp.sum(jnp.where(cls[None, :] == cids, (tile_off * tile)[:, None], 0), axis=0) + route[1].astype(I32)

    n_row_tiles = n_tok // tile + n_cls
    seg_cls, seg_hi = _pass_segments()
    seg_cls = jnp.asarray(seg_cls, I32)
    seg_hi = jnp.asarray(seg_hi, I32)
    seg_per_expert = EXPERTS_PER_GROUP - 1
    seg_tiles = _take(tiles, seg_cls)
    seg_end = jnp.cumsum(seg_tiles)
    n_used = seg_end[-1]
    p = jnp.arange(2 * n_row_tiles, dtype=I32)
    seg = jnp.minimum(jnp.sum((seg_end[None, :] <= p[:, None]).astype(I32), axis=1), seg_cls.shape[0] - 1)
    within = p - _take(seg_end - seg_tiles, seg)
    spare = p - n_used
    used = p < n_used
    pass_tile = jnp.where(used, _take(tile_off, _take(seg_cls, seg)) + within, tile_end[-1] + spare // 2).astype(I32)
    pass_hi = jnp.where(used, _take(seg_hi, seg), spare % 2).astype(I32)
    pass_expert = jnp.where(used, seg // seg_per_expert, N_EXPERT_GROUPS * EXPERTS_PER_GROUP - 1).astype(I32)
    pass_rows = jnp.where(used, jnp.clip(_take(_take(cnt, seg_cls), seg) - within * tile, 0, tile), 0)
    last_tile = (tile_end - 1).astype(I32)
    return (pos.astype(I32), pass_tile, pass_hi, pass_expert, pass_rows.astype(I32),
            last_tile, tiles.astype(I32), tile_end[-1:].astype(I32), n_row_tiles)


def _zero_tiles(last_ref, tiles_ref, nu_ref, zero_ref, hs_ref, sem, n_e, tile, wait):
    t8 = tile // SUBLANES
    for e in range(n_e):
        @pl.when(tiles_ref[e] > 0)
        def _(e=e):
            cp = pltpu.make_async_copy(zero_ref, hs_ref.at[pl.ds(last_ref[e] * t8, t8)], sem)
            if wait:
                cp.wait()
            else:
                cp.start()

    def body(j, carry):
        cp = pltpu.make_async_copy(zero_ref, hs_ref.at[pl.ds(j * t8, t8)], sem)
        if wait:
            cp.wait()
        else:
            cp.start()
        return carry

    lax.fori_loop(nu_ref[0], hs_ref.shape[0] // t8, body, 0)


def _dispatch_body(pos_ref, last_ref, tiles_ref, nu_ref, h_ref, hs_ref, zero_ref, sem_z, sem, *,
                   tm, tile, n_e):
    i = pl.program_id(0)

    @pl.when(i == 0)
    def _():
        zero_ref[...] = jnp.zeros_like(zero_ref)
        _zero_tiles(last_ref, tiles_ref, nu_ref, zero_ref, hs_ref, sem_z, n_e, tile, False)
        _zero_tiles(last_ref, tiles_ref, nu_ref, zero_ref, hs_ref, sem_z, n_e, tile, True)

    base = i * tm

    def issue(blk, carry):
        for u in range(SUBLANES):
            p = pos_ref[base + blk * SUBLANES + u]
            pltpu.make_async_copy(h_ref.at[blk, pl.ds(u, 1)],
                                  hs_ref.at[p >> 3, pl.ds(p & (SUBLANES - 1), 1)], sem).start(priority=u % 2)
        return carry

    lax.fori_loop(0, tm // SUBLANES, issue, 0, unroll=2)
    pltpu.make_async_copy(h_ref, hs_ref.at[pl.ds(0, tm // SUBLANES)], sem).wait()


def _dispatch(h_rows, pos, last_tile, tiles, n_used, n_rows, tile):
    n, d = h_rows.shape
    n_e = tiles.shape[0]
    tm = min(n, 2048)
    grid_spec = pltpu.PrefetchScalarGridSpec(
        num_scalar_prefetch=4,
        grid=(n // tm,),
        in_specs=[pl.BlockSpec((tm // SUBLANES, SUBLANES, d), lambda i, *_: (i, 0, 0))],
        out_specs=pl.BlockSpec(memory_space=pl.ANY),
        scratch_shapes=[pltpu.VMEM((tile // SUBLANES, SUBLANES, d), F32),
                        pltpu.SemaphoreType.DMA(()), pltpu.SemaphoreType.DMA(())],
    )
    hs = pl.pallas_call(
        functools.partial(_dispatch_body, tm=tm, tile=tile, n_e=n_e),
        grid_spec=grid_spec,
        out_shape=jax.ShapeDtypeStruct((n_rows // SUBLANES, SUBLANES, d), F32),
        compiler_params=_cparams("arbitrary"),
        name="moe_dispatch",
    )(pos, last_tile, tiles, n_used, h_rows.reshape(n // SUBLANES, SUBLANES, d))
    return hs.reshape(n_rows, d)


def _expert_body(pt_ref, ph_ref, te_ref, pr_ref, nx_ref, sl_ref, hs_ref, wg_ref, wu_ref, wd_ref, ys_ref,
                 wgf, wuf, wdf, wgb, wub, wdb, sem, *, layer):
    i = pl.program_id(0)
    rows = pr_ref[i]
    first = jnp.logical_or(i == 0, te_ref[i] != te_ref[jnp.maximum(i - 1, 0)])

    def weight_copies(expert, slot):
        return [pltpu.make_async_copy(w_ref.at[layer, expert], buf.at[slot], sem.at[slot])
                for w_ref, buf in ((wg_ref, wgf), (wu_ref, wuf), (wd_ref, wdf))]

    @pl.when(jnp.logical_and(rows > 0, first))
    def _():
        slot = sl_ref[i]

        @pl.when(i == 0)
        def _():
            for cp in weight_copies(te_ref[i], slot):
                cp.start()

        for cp in weight_copies(te_ref[i], slot):
            cp.wait()

        @pl.when(nx_ref[i] >= 0)
        def _():
            for cp in weight_copies(nx_ref[i], 1 - slot):
                cp.start()

        wgb[...] = wgf[slot].astype(BF16)
        wub[...] = wuf[slot].astype(BF16)
        wdb[...] = wdf[slot].astype(BF16)

    def ffn(n):
        x = hs_ref[0:n, :].astype(BF16)
        gate = jnp.dot(x, wgb[...], preferred_element_type=F32)
        up = jnp.dot(x, wub[...], preferred_element_type=F32)
        a = (gate * jax.nn.sigmoid(gate) * up).astype(BF16)
        ys_ref[0:n, :] = jnp.dot(a, wdb[...], preferred_element_type=F32)

    tile = hs_ref.shape[0]
    sizes = (tile, tile // 2, tile // 4)
    for k, n in enumerate(sizes):
        lo = sizes[k + 1] if k + 1 < len(sizes) else 0

        @pl.when(jnp.logical_and(rows > lo, rows <= n))
        def _(n=n):
            ffn(n)
            if n < tile:
                ys_ref[n:, :] = jnp.zeros((tile - n, ys_ref.shape[1]), F32)

    @pl.when(rows == 0)
    def _():
        ys_ref[...] = jnp.zeros_like(ys_ref)


def _expert_ffn(hs, pass_tile, pass_hi, pass_expert, pass_rows, w_gate, w_up, w_down, layer, tile):
    _, n_e, d, f = w_gate.shape
    n_rows = hs.shape[0]
    eids = jnp.arange(n_e, dtype=I32)
    n_pass_e = jnp.sum(jnp.where((pass_expert[None, :] == eids[:, None]) & (pass_rows[None, :] > 0), 1, 0), axis=1)
    has = n_pass_e > 0
    later = (eids[None, :] > eids[:, None]) & has[None, :]
    next_used = jnp.min(jnp.where(later, eids[None, :], n_e), axis=1)
    next_used = jnp.where(next_used == n_e, -1, next_used).astype(I32)
    slot_e = ((jnp.cumsum(has.astype(I32)) - 1) % 2).astype(I32)
    pass_next = _take(next_used, pass_expert)
    pass_slot = jnp.maximum(_take(slot_e, pass_expert), 0)
    any_spec = pl.BlockSpec(memory_space=pl.ANY)
    grid_spec = pltpu.PrefetchScalarGridSpec(
        num_scalar_prefetch=6,
        grid=(pass_tile.shape[0],),
        in_specs=[pl.BlockSpec((tile, d), lambda i, pt, *_: (pt[i], 0)), any_spec, any_spec, any_spec],
        out_specs=pl.BlockSpec((tile, d), lambda i, pt, ph, *_: (pt[i], ph[i])),
        scratch_shapes=[pltpu.VMEM((2, d, f), F32), pltpu.VMEM((2, d, f), F32), pltpu.VMEM((2, f, d), F32),
                        pltpu.VMEM((d, f), BF16), pltpu.VMEM((d, f), BF16), pltpu.VMEM((f, d), BF16),
                        pltpu.SemaphoreType.DMA((2,))],
    )
    return pl.pallas_call(
        functools.partial(_expert_body, layer=layer),
        grid_spec=grid_spec,
        out_shape=jax.ShapeDtypeStruct((n_rows, 2 * d), F32),
        compiler_params=_cparams("arbitrary"),
        name="moe_experts",
    )(pass_tile, pass_hi, pass_expert, pass_rows, pass_next, pass_slot, hs, w_gate, w_up, w_down)


def _combine_body(pos_ref, x_ref, mod_ref, wts_ref, lng_ref, lnb_ref, ys_ref, *rest,
                  tm, alpha, next_mod):
    if next_mod:
        nmod_ref, x2_ref, h_ref, buf, sem = rest
    else:
        x2_ref, buf, sem = rest
    step = pl.program_id(0) * pl.num_programs(1) + pl.program_id(1)
    n_steps = pl.num_programs(0) * pl.num_programs(1)
    d = x_ref.shape[2]

    def gather(s, slot):
        def issue(blk, carry):
            for u in range(SUBLANES):
                p = pos_ref[s * tm + blk * SUBLANES + u]
                pltpu.make_async_copy(ys_ref.at[p >> 3, pl.ds(p & (SUBLANES - 1), 1)],
                                      buf.at[slot, blk, pl.ds(u, 1)], sem.at[slot]).start(priority=u % 2)
            return carry

        lax.fori_loop(0, tm // SUBLANES, issue, 0, unroll=2)

    @pl.when(step == 0)
    def _():
        gather(0, 0)

    @pl.when(step + 1 < n_steps)
    def _():
        gather(step + 1, (step + 1) % 2)

    slot = step % 2
    pltpu.make_async_copy(ys_ref.at[pl.ds(0, tm // SUBLANES)], buf.at[slot], sem.at[slot]).wait()

    w = wts_ref[...]
    rows = buf[slot].reshape(tm, 2 * d)
    moe = w[:, 0:1] * rows[:, :d] + w[:, 1:2] * rows[:, d:]
    g2 = mod_ref[0, 5:6, :]
    x2 = _layer_norm(alpha * x_ref[0] + g2 * moe, lng_ref[...], lnb_ref[...])
    x2_ref[0] = x2
    if next_mod:
        h_ref[0] = x2 * (1.0 + nmod_ref[0, 1:2, :]) + nmod_ref[0, 0:1, :]


def _combine(ys, pos, wts, x, mods, ln_g, ln_b, alpha, next_mods=None):
    b, l, d = x.shape
    tm = min(l, 512)
    nt = l // tm
    tok = lambda i, j, *_: (i, j, 0)
    bat = lambda i, j, *_: (i, 0, 0)
    const2 = lambda i, j, *_: (0, 0)
    in_specs = [pl.BlockSpec((1, tm, d), tok),
                pl.BlockSpec((1, 6, d), bat),
                pl.BlockSpec((tm, 2), lambda i, j, *_: (i * nt + j, 0)),
                pl.BlockSpec((1, d), const2),
                pl.BlockSpec((1, d), const2),
                pl.BlockSpec(memory_space=pl.ANY)]
    ys_tiles = ys.reshape(ys.shape[0] // SUBLANES, SUBLANES, ys.shape[1])
    args = [x, mods, wts, ln_g.reshape(1, d), ln_b.reshape(1, d), ys_tiles]
    out_shape = [jax.ShapeDtypeStruct((b, l, d), F32)]
    out_specs = [pl.BlockSpec((1, tm, d), tok)]
    if next_mods is not None:
        in_specs.append(pl.BlockSpec((1, 6, d), bat))
        args.append(next_mods)
        out_shape.append(jax.ShapeDtypeStruct((b, l, d), F32))
        out_specs.append(pl.BlockSpec((1, tm, d), tok))
    grid_spec = pltpu.PrefetchScalarGridSpec(
        num_scalar_prefetch=1,
        grid=(b, nt),
        in_specs=in_specs,
        out_specs=out_specs,
        scratch_shapes=[pltpu.VMEM((2, tm // SUBLANES, SUBLANES, 2 * d), F32),
                        pltpu.SemaphoreType.DMA((2,))],
    )
    return pl.pallas_call(
        functools.partial(_combine_body, tm=tm, alpha=alpha, next_mod=next_mods is not None),
        grid_spec=grid_spec,
        out_shape=out_shape,
        compiler_params=_cparams("arbitrary", "arbitrary"),
        name="moe_combine",
    )(pos, *args)


MOE_TILE = 512


def _moe(h_rows, route, counts, w_gate, w_up, w_down, layer):
    n = h_rows.shape[0]
    tile = min(n, MOE_TILE)
    (pos, pass_tile, pass_hi, pass_expert, pass_rows, last_tile, tiles, n_used_tiles,
     n_row_tiles) = _dispatch_plan(route, counts, tile)
    hs = _dispatch(h_rows, pos, last_tile, tiles, n_used_tiles, n_row_tiles * tile, tile)
    ys = _expert_ffn(hs, pass_tile, pass_hi, pass_expert, pass_rows, w_gate, w_up, w_down, layer, tile)
    return ys, pos, jnp.transpose(route[2:4])


def _pool_group(h_ref, w_ref, sc_ref, o_ref, col_ref, k, n_rows):
    n = n_rows * GRID_W
    c = h_ref.shape[2]
    blk = 4 * GRID_W
    half = k // 2
    pad = half * GRID_W
    ti = lax.broadcasted_iota(I32, (blk, blk), 0)
    si = lax.broadcasted_iota(I32, (blk, blk), 1)
    shift = GRID_W.bit_length() - 1
    same_row = (ti >> shift) == (si >> shift)
    band = jnp.where(same_row & (si - ti >= -half) & (si - ti <= half - 1), 1.0, 0.0).astype(BF16)
    col_ref[0:pad, :] = jnp.zeros((pad, c), F32)
    col_ref[pad + n:pad + n + pad, :] = jnp.zeros((pad, c), F32)
    for b0 in range(0, n, blk):
        hb = h_ref[0, b0:b0 + blk, :]
        head = hb.astype(BF16)
        rest = (hb - head.astype(F32)).astype(BF16)
        col_ref[pad + b0:pad + b0 + blk, :] = (jnp.dot(band, head, preferred_element_type=F32)
                                               + jnp.dot(band, rest, preferred_element_type=F32))
    acc = col_ref[0:n, :]
    for j in range(1, k):
        acc = acc + col_ref[j * GRID_W:j * GRID_W + n, :]
    t = lax.broadcasted_iota(I32, (n, 1), 0)
    wc = t & (GRID_W - 1)
    wr = t >> shift
    cnt_c = jnp.minimum(wc + half - 1, GRID_W - 1) - jnp.maximum(wc - half, 0) + 1
    cnt_r = jnp.minimum(wr + half - 1, n_rows - 1) - jnp.maximum(wr - half, 0) + 1
    mean = acc / (cnt_c * cnt_r).astype(F32)
    pooled = (mean - h_ref[0]).astype(BF16)
    o_ref[0] = jnp.dot(pooled, w_ref[0].astype(BF16), preferred_element_type=F32) * sc_ref[...]


def _pool_body(h_ref, w_ref, sc_ref, o_ref, col_ref, *, n_rows):
    g = pl.program_id(1)
    for gi, k in enumerate(POOL_WINDOWS):
        @pl.when(g == gi)
        def _(k=k):
            _pool_group(h_ref, w_ref, sc_ref, o_ref, col_ref, k, n_rows)


def _pool_mix(h, w_grp, scale):
    b, n, d = h.shape
    n_g, c, _ = w_grp.shape
    n_rows = n // GRID_W
    pad = (max(POOL_WINDOWS) // 2) * GRID_W
    return pl.pallas_call(
        functools.partial(_pool_body, n_rows=n_rows),
        grid=(b, n_g),
        in_specs=[pl.BlockSpec((1, n, c), lambda i, j: (i, 0, j)),
                  pl.BlockSpec((1, c, c), lambda i, j: (j, 0, 0)),
                  pl.BlockSpec((1, c), lambda i, j: (0, j))],
        out_specs=pl.BlockSpec((1, n, c), lambda i, j: (i, 0, j)),
        out_shape=jax.ShapeDtypeStruct((b, n, d), F32),
        scratch_shapes=[pltpu.VMEM((n + 2 * pad, c), F32)],
        compiler_params=_cparams("arbitrary", "arbitrary"),
        name="pool_mix",
    )(h, w_grp, scale.reshape(1, d))


def kernel(x, c, ctx, c_ctx, mod_w, mod_b, ln_g, ln_b, s5_lam_re, s5_lam_im, s5_log_dt, s5_b_re, s5_b_im,
           s5_c_re, s5_c_im, s5_d, s5_w_val, s5_w_gate, pool_w, pool_scale, router_w, router_b,
           moe_w_gate, moe_w_up, moe_w_down):
    b, l, d = x.shape
    depth = mod_w.shape[0]
    assert depth == 2 and b + 1 <= SUBLANES and d % LANES == 0 and GRID_W & (GRID_W - 1) == 0
    alpha = (2 * depth) ** 0.25

    cond = jnp.zeros((SUBLANES, d), F32).at[:b].set(c).at[b].set(c_ctx)
    mods = _modulation(cond, mod_w, mod_b).reshape(depth, SUBLANES, 6, d)

    toep, w1, w2t, a_r, a_i = _s5_weights(s5_lam_re[0], s5_lam_im[0], s5_log_dt[0], s5_b_re[0], s5_b_im[0],
                                          s5_c_re[0], s5_c_im[0], s5_d[0])
    y = _s5_mix(x, ctx, mods[0, :b], mods[0, b:b + 1], toep, w1, w2t, a_r, a_i)
    x1, h_rows, route, counts = _post_mixer(y, x, mods[0, :b], ln_g[0, 0], ln_b[0, 0], router_w, router_b,
                                            alpha, glu_w=(s5_w_val[0], s5_w_gate[0]))
    ys, pos, wts = _moe(h_rows, route, counts, moe_w_gate, moe_w_up, moe_w_down, 0)
    x2, h = _combine(ys, pos, wts, x1, mods[0, :b], ln_g[0, 1], ln_b[0, 1], alpha, next_mods=mods[1, :b])

    m = _pool_mix(h, pool_w[0], pool_scale[0])
    x3, h_rows, route, counts = _post_mixer(m, x2, mods[1, :b], ln_g[1, 0], ln_b[1, 0], router_w, router_b,
                                            alpha)
    ys, pos, wts = _moe(h_rows, route, counts, moe_w_gate, moe_w_up, moe_w_down, 1)
    (out,) = _combine(ys, pos, wts, x3, mods[1, :b], ln_g[1, 1], ln_b[1, 1], alpha)
    return out
```

```python
import functools

import jax
import jax.numpy as jnp
from jax import lax
from jax.experimental import pallas as pl
from jax.experimental.pallas import tpu as pltpu

F32 = jnp.float32
BF16 = jnp.bfloat16
I32 = jnp.int32

GRID_W = 64
S5_H = 16
S5_T = 16
POOL_WINDOWS = (2, 4, 8, 16)
N_EXPERT_GROUPS = 4
LN_EPS = 1e-5
LANES = 128
SUBLANES = 8
VMEM_LIMIT = 52 * 1024 * 1024

EXPERTS_PER_GROUP = 4
PAIRS_PER_GROUP = EXPERTS_PER_GROUP * (EXPERTS_PER_GROUP - 1) // 2
N_PAIR_CLASSES = N_EXPERT_GROUPS * PAIRS_PER_GROUP
ROUTE_ROWS = 8


def _cparams(*sem):
    return pltpu.CompilerParams(dimension_semantics=sem, vmem_limit_bytes=VMEM_LIMIT)


def _dot3(a, b, dims):
    a_head = a.astype(BF16)
    a_rest = (a - a_head.astype(F32)).astype(BF16)
    b_head = b.astype(BF16)
    b_rest = (b - b_head.astype(F32)).astype(BF16)
    dot = functools.partial(lax.dot_general, dimension_numbers=dims, preferred_element_type=F32)
    return dot(a_head, b_head) + dot(a_head, b_rest) + dot(a_rest, b_head)


MATMUL_NN = (((1,), (0,)), ((), ()))
MATMUL_NT = (((1,), (1,)), ((), ()))


def _mod_body(c_ref, w_ref, b_ref, o_ref):
    c = c_ref[...]
    s = c * jax.nn.sigmoid(c)
    o_ref[0] = _dot3(s, w_ref[0], MATMUL_NN) + b_ref[0]


def _modulation(cond, mod_w, mod_b):
    depth, d, n6 = mod_w.shape
    tn = min(n6, 1536)
    return pl.pallas_call(
        _mod_body,
        grid=(depth, n6 // tn),
        in_specs=[pl.BlockSpec((SUBLANES, d), lambda i, j: (0, 0)),
                  pl.BlockSpec((1, d, tn), lambda i, j: (i, 0, j)),
                  pl.BlockSpec((1, 1, tn), lambda i, j: (i, 0, j))],
        out_specs=pl.BlockSpec((1, SUBLANES, tn), lambda i, j: (i, 0, j)),
        out_shape=jax.ShapeDtypeStruct((depth, SUBLANES, n6), F32),
        compiler_params=_cparams("arbitrary", "arbitrary"),
        name="modulation",
    )(cond, mod_w, mod_b.reshape(depth, 1, n6))


def _s5_direction_terms(lam_re, lam_im, log_dt, b_re, b_im):
    lr = lam_re.astype(F32)
    li = lam_im.astype(F32)
    dt = jnp.exp(log_dt.astype(F32))[:, None]
    mag = jnp.exp(lr * dt)
    ar = mag * jnp.cos(li * dt)
    ai = mag * jnp.sin(li * dt)
    den = lr * lr + li * li
    nr = ar - 1.0
    fr = (nr * lr + ai * li) / den
    fi = (ai * lr - nr * li) / den
    br_, bi_ = b_re.astype(F32), b_im.astype(F32)
    bbr = fr[..., None] * br_ - fi[..., None] * bi_
    bbi = fr[..., None] * bi_ + fi[..., None] * br_
    k = jnp.arange(S5_T + 1, dtype=F32)[:, None, None]
    pm = jnp.exp(k * (lr * dt))
    pr = pm * jnp.cos(k * (li * dt))
    pi = pm * jnp.sin(k * (li * dt))
    return pr, pi, bbr, bbi


def _s5_weights(lam_re, lam_im, log_dt, b_re, b_im, c_re, c_im, d_skip):
    t = S5_T
    g, p = lam_re.shape[1:]
    h = b_re.shape[-1]
    terms = [_s5_direction_terms(lam_re[d], lam_im[d], log_dt[d], b_re[d], b_im[d]) for d in (0, 1)]
    pw = jnp.stack([jnp.transpose(terms[d][k], (1, 0, 2)) for d in (0, 1) for k in (0, 1)], axis=1)
    bt = jnp.stack([jnp.transpose(terms[d][k], (0, 2, 1)) for d in (0, 1) for k in (2, 3)], axis=1)
    cc = jnp.stack([c[d].astype(F32) for d in (0, 1) for c in (c_re, c_im)], axis=1)
    gp = 8
    spec4 = lambda rows: pl.BlockSpec((gp, 4, rows, p), lambda i: (i, 0, 0, 0))
    wide = pl.BlockSpec((gp, t * h, t * h), lambda i: (i, 0, 0))
    toep, w1, w2t = pl.pallas_call(
        functools.partial(_s5_prep_body, gp=gp),
        grid=(g // gp,),
        in_specs=[spec4(t + 1), spec4(h), spec4(h), pl.BlockSpec((gp, 1, h), lambda i: (i, 0, 0))],
        out_specs=[wide, wide, wide],
        out_shape=[jax.ShapeDtypeStruct((g, t * h, t * h), BF16) for _ in range(3)],
        compiler_params=_cparams("arbitrary"),
        name="s5_prep",
    )(pw, bt, cc, d_skip.astype(F32).reshape(g, 1, h))
    a_r = jnp.concatenate([pw[:, 0, t], pw[:, 2, t]], axis=-1)[:, None, :]
    a_i = jnp.concatenate([pw[:, 1, t], pw[:, 3, t]], axis=-1)[:, None, :]
    return toep, w1, w2t, a_r, a_i


def _cmul(ar, ai, br, bi):
    return ar * br - ai * bi, ar * bi + ai * br


def _s5_prep_body(pw_ref, bt_ref, cc_ref, d_ref, toep_ref, w1_ref, w2t_ref, *, gp):
    t = pw_ref.shape[2] - 1
    h = bt_ref.shape[2]
    eye =(lax.broadcasted_iota(I32, (h, h), 0) == lax.broadcasted_iota(I32, (h, h), 1))
    for g in range(gp):
        prf, pif, prb, pib = (pw_ref[g, k] for k in range(4))
        btf = (bt_ref[g, 0], bt_ref[g, 1])
        btb = (bt_ref[g, 2], bt_ref[g, 3])
        ccf = (cc_ref[g, 0], cc_ref[g, 1])
        ccb = (cc_ref[g, 2], cc_ref[g, 3])
        caf, cab = [], []
        for j in range(t):
            f_r, f_i = _cmul(*btf, prf[t - 1 - j:t - j], pif[t - 1 - j:t - j])
            b_r, b_i = _cmul(*btb, prb[j:j + 1], pib[j:j + 1])
            w1_ref[g, j * h:(j + 1) * h, :] = jnp.concatenate([f_r, b_r, f_i, b_i], axis=1).astype(BF16)
            mf_r, mf_i = _cmul(*ccf, prf[j + 1:j + 2], pif[j + 1:j + 2])
            mb_r, mb_i = _cmul(*ccb, prb[t - j:t - j + 1], pib[t - j:t - j + 1])
            w2t_ref[g, j * h:(j + 1) * h, :] = jnp.concatenate([mf_r, mb_r, -mf_i, -mb_i],
                                                               axis=1).astype(BF16)
            caf.append(_cmul(*ccf, prf[j:j + 1], pif[j:j + 1]))
            cab.append(_cmul(*ccb, prb[t - 1 - j:t - j], pib[t - 1 - j:t - j]))

        def lag_kernels(btx, ca):
            car = jnp.concatenate([c[0] for c in ca], axis=0)
            cai = jnp.concatenate([c[1] for c in ca], axis=0)
            return _dot3(btx[0], car, MATMUL_NT) - _dot3(btx[1], cai, MATMUL_NT)

        kf = lag_kernels(btf, caf)
        kb = lag_kernels(btb, cab)
        skip = jnp.where(eye, jnp.broadcast_to(d_ref[g], (h, h)), 0.0)
        mid = kb[:, (t - 1) * h:] + kf[:, :h] + skip
        kwide = jnp.concatenate([kb[:, :(t - 1) * h], mid, kf[:, h:], jnp.zeros((h, h), F32)], axis=1)
        for i in range(t):
            off = (t - 1 - i) * h
            toep_ref[g, i * h:(i + 1) * h, :] = kwide[:, off:off + t * h].astype(BF16)


def _granule_transpose(v):
    n = len(v)
    gran = lax.broadcasted_iota(I32, v[0].shape, 1) >> 4
    at = [gran == q for q in range(n)]
    rot = []
    for d in range(n):
        m = v[d]
        for q in range(1, n):
            m = jnp.where(at[q], v[(q + d) % n], m)
        rot.append(pltpu.roll(m, d * S5_H, 1) if d else m)
    out = []
    for q in range(n):
        w = rot[(-q) % n]
        for j in range(1, n):
            w = jnp.where(at[j], rot[(j - q) % n], w)
        out.append(w)
    return out


def _s5_body(x_ref, ctx_ref, mod_ref, cmod_ref, tt_ref, w1_ref, w2t_ref, ar_ref, ai_ref, y_ref,
             u_ref, s_ref, *, n_ctx_chunks, n_lat_chunks, gb):
    n_b = x_ref.shape[0]
    n_g = u_ref.shape[0]
    n_blocks = s_ref.shape[1] // SUBLANES
    half = SUBLANES // 2
    tok_blk = SUBLANES * S5_T
    lat_lo = n_ctx_chunks * n_b
    lat_rows = n_lat_chunks * n_b

    def chunk_rows(ref, b, tok0, shift, scale1):
        vs = [ref[b, pl.ds(tok0 + j, SUBLANES, stride=S5_T), :] * scale1 + shift for j in range(S5_T)]
        lo = _granule_transpose(vs[:SUBLANES])
        hi = _granule_transpose(vs[SUBLANES:])
        return lo, hi

    def put_rows(rows, chunk0, b):
        for q in range(n_g):
            for jh in range(2):
                u_ref[q, jh, pl.ds(chunk0 * n_b + b, SUBLANES, stride=n_b), :] = rows[jh][q]

    def u_rows(g, lo, n):
        return jnp.concatenate([u_ref[g, 0, lo:lo + n, :], u_ref[g, 1, lo:lo + n, :]], axis=1)

    shift = jnp.broadcast_to(cmod_ref[0, 0:1, :], (SUBLANES, LANES))
    scale1 = 1.0 + jnp.broadcast_to(cmod_ref[0, 1:2, :], (SUBLANES, LANES))
    for b in range(n_b):
        for cb in range(n_ctx_chunks // SUBLANES):
            rows = chunk_rows(ctx_ref, b, cb * tok_blk, shift, scale1)
            put_rows(rows, cb * SUBLANES, b)
            put_rows(rows, n_ctx_chunks + n_lat_chunks + cb * SUBLANES, b)

    def fill(cb, carry):
        for b in range(n_b):
            shift = jnp.broadcast_to(mod_ref[b, 0:1, :], (SUBLANES, LANES))
            scale1 = 1.0 + jnp.broadcast_to(mod_ref[b, 1:2, :], (SUBLANES, LANES))
            put_rows(chunk_rows(x_ref, b, cb * tok_blk, shift, scale1), n_ctx_chunks + cb * SUBLANES, b)
        return carry

    lax.fori_loop(0, n_lat_chunks // SUBLANES, fill, 0)

    for g in range(n_g):
        s_ref[g] = jnp.dot(u_rows(g, 0, n_blocks * SUBLANES).astype(BF16), w1_ref[g],
                           preferred_element_type=F32)

    lane = lax.broadcasted_iota(I32, (SUBLANES, LANES), 1)
    row = lax.broadcasted_iota(I32, (SUBLANES, LANES), 0)
    is_fwd = lane < LANES // 2
    is_fwd2 = jnp.concatenate([is_fwd, is_fwd], axis=1)
    top = row < half
    zero = jnp.zeros((SUBLANES, LANES), F32)
    for g0 in range(0, n_g, gb):
        ars = [jnp.broadcast_to(ar_ref[g0 + g], (SUBLANES, LANES)) for g in range(gb)]
        ais = [jnp.broadcast_to(ai_ref[g0 + g], (SUBLANES, LANES)) for g in range(gb)]

        def step(k, carry, g0=g0, ars=ars, ais=ais):
            fo = pl.multiple_of(k * SUBLANES, SUBLANES)
            bo = pl.multiple_of((n_blocks - 1 - k) * SUBLANES, SUBLANES)
            new = []
            loaded = [(s_ref[g0 + g, pl.ds(fo, SUBLANES), :], s_ref[g0 + g, pl.ds(bo, SUBLANES), :])
                      for g in range(gb)]
            stores = []
            for g in range(gb):
                xr, xi = carry[2 * g], carry[2 * g + 1]
                vf, vb_raw = loaded[g]
                vb = pltpu.roll(vb_raw, half, 0)
                vr = jnp.where(is_fwd, vf[:, :LANES], vb[:, :LANES])
                vi = jnp.where(is_fwd, vf[:, LANES:], vb[:, LANES:])
                ar, ai = ars[g], ais[g]
                yr = ar * xr - ai * xi + vr
                yi = ar * xi + ai * xr + vi
                yrr = pltpu.roll(yr, half, 0)
                yir = pltpu.roll(yi, half, 0)
                zr = ar * yrr - ai * yir + vr
                zi = ar * yir + ai * yrr + vi
                inc = jnp.concatenate([jnp.where(top, xr, yrr), jnp.where(top, xi, yir)], axis=1)
                stores.append((jnp.where(is_fwd2, inc, vf),
                               jnp.where(is_fwd2, vb_raw, pltpu.roll(inc, half, 0))))
                new.append(jnp.where(top, pltpu.roll(zr, half, 0), zr))
                new.append(jnp.where(top, pltpu.roll(zi, half, 0), zi))
            for g in range(gb):
                s_ref[g0 + g, pl.ds(fo, SUBLANES), :] = stores[g][0]
                s_ref[g0 + g, pl.ds(bo, SUBLANES), :] = stores[g][1]
            return tuple(new)

        lax.fori_loop(0, (n_ctx_chunks + n_lat_chunks) // 2, step, tuple(zero for _ in range(2 * gb)))

    for g in range(n_g):
        y = (jnp.dot(u_rows(g, lat_lo, lat_rows).astype(BF16), tt_ref[g], preferred_element_type=F32)
             + lax.dot_general(s_ref[g, lat_lo:lat_lo + lat_rows, :].astype(BF16), w2t_ref[g],
                               (((1,), (1,)), ((), ())), preferred_element_type=F32))
        for jh in range(2):
            u_ref[g, jh, lat_lo:lat_lo + lat_rows, :] = y[:, jh * LANES:(jh + 1) * LANES]

    def emit(cb, carry):
        for b in range(n_b):
            for jh in range(S5_T // SUBLANES):
                w = [u_ref[q, jh, pl.ds((n_ctx_chunks + cb * SUBLANES) * n_b + b, SUBLANES, stride=n_b), :]
                     for q in range(n_g)]
                v = _granule_transpose(w)
                for j in range(SUBLANES):
                    y_ref[b, pl.ds(cb * tok_blk + jh * SUBLANES + j, SUBLANES, stride=S5_T), :] = v[j]
        return carry

    lax.fori_loop(0, n_lat_chunks // SUBLANES, emit, 0)


def _s5_mix(x, ctx, mods, cmods, toep, w1, w2t, a_r, a_i):
    b, l, d = x.shape
    n_ctx = ctx.shape[1]
    n_g = LANES // S5_H
    n_ctx_chunks, n_lat_chunks = n_ctx // S5_T, l // S5_T
    assert b * 2 == SUBLANES and n_ctx_chunks % SUBLANES == 0 and n_lat_chunks % SUBLANES == 0
    rows = (2 * n_ctx_chunks + n_lat_chunks) * b
    w = S5_T * S5_H
    body = functools.partial(_s5_body, n_ctx_chunks=n_ctx_chunks, n_lat_chunks=n_lat_chunks, gb=8)
    lane_tile = lambda i: (0, 0, i)
    wspec = pl.BlockSpec((n_g, w, w), lambda i: (i, 0, 0))
    aspec = pl.BlockSpec((n_g, 1, w // 2), lambda i: (i, 0, 0))
    return pl.pallas_call(
        body,
        grid=(d // LANES,),
        in_specs=[pl.BlockSpec((b, l, LANES), lane_tile),
                  pl.BlockSpec((b, n_ctx, LANES), lane_tile),
                  pl.BlockSpec((b, 6, LANES), lane_tile),
                  pl.BlockSpec((1, 6, LANES), lane_tile),
                  wspec, wspec, wspec, aspec, aspec],
        out_specs=pl.BlockSpec((b, l, LANES), lane_tile, pipeline_mode=pl.Buffered(1)),
        out_shape=jax.ShapeDtypeStruct((b, l, d), F32),
        scratch_shapes=[pltpu.VMEM((n_g, w // LANES, rows, LANES), F32), pltpu.VMEM((n_g, rows, w), F32)],
        compiler_params=_cparams("arbitrary"),
        name="s5_mix",
    )(x, ctx, mods, cmods, toep, w1, w2t, a_r, a_i)


def _layer_norm(r, g, b):
    mu = jnp.mean(r, axis=-1, keepdims=True)
    xc = r - mu
    var = jnp.mean(xc * xc, axis=-1, keepdims=True)
    return xc * lax.rsqrt(var + LN_EPS) * g + b


def _max2_of4(a, b, c, d):
    h1, l1 = jnp.maximum(a, b), jnp.minimum(a, b)
    h2, l2 = jnp.maximum(c, d), jnp.minimum(c, d)
    return jnp.maximum(h1, h2) + jnp.maximum(jnp.minimum(h1, h2), jnp.maximum(l1, l2))


def _argmax_first(vals):
    idx = jnp.zeros(vals[0].shape, I32)
    best = vals[0]
    for j in range(1, len(vals)):
        upd = vals[j] > best
        idx = jnp.where(upd, j, idx)
        best = jnp.where(upd, vals[j], best)
    return idx, best


def _route(logits_t, count_ref, route_ref):
    n_e, tm = logits_t.shape
    per = n_e // N_EXPERT_GROUPS
    mx = jnp.max(logits_t, axis=0, keepdims=True)
    ex = jnp.exp(logits_t - mx)
    sc = ex / jnp.sum(ex, axis=0, keepdims=True)
    rows = [sc[e:e + 1, :] for e in range(n_e)]
    gscore = [_max2_of4(*rows[per * g:per * (g + 1)]) for g in range(N_EXPERT_GROUPS)]
    best, _ = _argmax_first(gscore)
    vals = []
    for j in range(per):
        v = rows[per * (N_EXPERT_GROUPS - 1) + j]
        for g in range(N_EXPERT_GROUPS - 2, -1, -1):
            v = jnp.where(best == g, rows[per * g + j], v)
        vals.append(v)
    i1, m1 = _argmax_first(vals)
    i2, m2 = _argmax_first([jnp.where(i1 == j, -1.0, vals[j]) for j in range(per)])
    den = m1 + m2
    first_lo = i1 < i2
    lo = jnp.minimum(i1, i2)
    hi = jnp.maximum(i1, i2)
    pair = jnp.where(lo == 0, 0, jnp.where(lo == 1, per - 1, 2 * per - 3)) + hi - lo - 1
    cls = best * PAIRS_PER_GROUP + pair
    w_lo = jnp.where(first_lo, m1, m2) / den
    w_hi = jnp.where(first_lo, m2, m1) / den

    n_cls = count_ref.shape[0]
    hit = lax.broadcasted_iota(I32, (n_cls, tm), 0) == cls
    onehot = jnp.where(hit, 1.0, 0.0)
    src = lax.broadcasted_iota(I32, (tm, tm), 0)
    dst = lax.broadcasted_iota(I32, (tm, tm), 1)
    tri = jnp.where(src <= dst, 1.0, 0.0).astype(BF16)
    cum = jnp.dot(onehot.astype(BF16), tri, preferred_element_type=F32)
    excl = cum - onehot + count_ref[:, 0:1]
    rank = jnp.sum(jnp.where(hit, excl, 0.0), axis=0, keepdims=True)
    count_ref[...] = count_ref[...] + jnp.sum(onehot, axis=1, keepdims=True)

    zero = jnp.zeros((1, tm), F32)
    route_ref[...] = jnp.concatenate([cls.astype(F32), rank, w_lo, w_hi, zero, zero, zero, zero], axis=0)


def _post_mixer_body(m_ref, x_ref, mod_ref, lng_ref, lnb_ref, rwt_ref, rb_ref, *rest, glu, alpha):
    if glu:
        wv_ref, wg_ref, x1_ref, h_ref, route_ref, cnt_out_ref, cnt_ref, wvb, wgb = rest
    else:
        x1_ref, h_ref, route_ref, cnt_out_ref, cnt_ref = rest

    @pl.when((pl.program_id(0) == 0) & (pl.program_id(1) == 0))
    def _():
        cnt_ref[...] = jnp.zeros_like(cnt_ref)
        if glu:
            wvb[...] = wv_ref[...].astype(BF16)
            wgb[...] = wg_ref[...].astype(BF16)

    if glu:
        a = jax.nn.gelu(m_ref[0], approximate=True).astype(BF16)
        val = jnp.dot(a, wvb[...], preferred_element_type=F32)
        gate = jnp.dot(a, wgb[...], preferred_element_type=F32)
        m = val * jax.nn.sigmoid(gate)
    else:
        m = m_ref[0]

    g1 = mod_ref[0, 2:3, :]
    sh2 = mod_ref[0, 3:4, :]
    sc2 = mod_ref[0, 4:5, :]
    x1 = _layer_norm(alpha * x_ref[0] + g1 * m, lng_ref[...], lnb_ref[...])
    x1_ref[0] = x1
    h = x1 * (1.0 + sc2) + sh2
    h_ref[...] = h
    logits_t = _dot3(rwt_ref[...], h, MATMUL_NT) + rb_ref[...]
    _route(logits_t, cnt_ref, route_ref)
    cnt_out_ref[...] = cnt_ref[...]


def _post_mixer(m, x, mods, ln_g, ln_b, router_w, router_b, alpha, glu_w=None):
    b, l, d = x.shape
    n_e = router_w.shape[1]
    assert n_e == N_EXPERT_GROUPS * EXPERTS_PER_GROUP
    tm = min(l, 512)
    nt = l // tm
    tok = lambda i, j: (i, j, 0)
    const2 = lambda i, j: (0, 0)
    in_specs = [pl.BlockSpec((1, tm, d), tok),
                pl.BlockSpec((1, tm, d), tok),
                pl.BlockSpec((1, 6, d), lambda i, j: (i, 0, 0)),
                pl.BlockSpec((1, d), const2),
                pl.BlockSpec((1, d), const2),
                pl.BlockSpec((n_e, d), const2),
                pl.BlockSpec((n_e, 1), const2)]
    args = [m, x, mods, ln_g.reshape(1, d), ln_b.reshape(1, d), router_w.T, router_b.reshape(n_e, 1)]
    scratch = [pltpu.VMEM((N_PAIR_CLASSES, LANES), F32)]
    if glu_w is not None:
        resident = pl.BlockSpec((d, d), const2, pipeline_mode=pl.Buffered(1))
        in_specs += [resident, resident]
        args += [glu_w[0], glu_w[1]]
        scratch += [pltpu.VMEM((d, d), BF16), pltpu.VMEM((d, d), BF16)]
    out_shape = [jax.ShapeDtypeStruct((b, l, d), F32),
                 jax.ShapeDtypeStruct((b * l, d), F32),
                 jax.ShapeDtypeStruct((ROUTE_ROWS, b * l), F32),
                 jax.ShapeDtypeStruct((N_PAIR_CLASSES, LANES), F32)]
    out_specs = [pl.BlockSpec((1, tm, d), tok),
                 pl.BlockSpec((tm, d), lambda i, j: (i * nt + j, 0)),
                 pl.BlockSpec((ROUTE_ROWS, tm), lambda i, j: (0, i * nt + j)),
                 pl.BlockSpec((N_PAIR_CLASSES, LANES), const2)]
    return pl.pallas_call(
        functools.partial(_post_mixer_body, glu=glu_w is not None, alpha=alpha),
        grid=(b, nt),
        in_specs=in_specs,
        out_specs=out_specs,
        out_shape=out_shape,
        scratch_shapes=scratch,
        compiler_params=_cparams("arbitrary", "arbitrary"),
        name="post_mixer_glu" if glu_w is not None else "post_mixer",
    )(*args)


def _take(table, idx):
    ids = jnp.arange(table.shape[0], dtype=I32)
    return jnp.sum(jnp.where(idx[:, None] == ids[None, :], table[None, :], 0), axis=1)


def _pass_segments():
    seg_cls, seg_hi = [], []
    pairs = [(a, b) for a in range(EXPERTS_PER_GROUP) for b in range(a + 1, EXPERTS_PER_GROUP)]
    for g in range(N_EXPERT_GROUPS):
        for m in range(EXPERTS_PER_GROUP):
            for idx, (a, b) in enumerate(pairs):
                if m in (a, b):
                    seg_cls.append(g * PAIRS_PER_GROUP + idx)
                    seg_hi.append(int(m == b))
    return seg_cls, seg_hi


def _dispatch_plan(route, counts, tile):
    n_cls = counts.shape[0]
    n_tok = route.shape[1]
    cnt = counts[:, 0].astype(I32)
    tiles = (cnt + tile - 1) // tile
    tile_end = jnp.cumsum(tiles)
    tile_off = tile_end - tiles
    cids = jnp.arange(n_cls, dtype=I32)[:, None]
    cls = route[0].astype(I32)
    pos = jnp.sum(jnp.where(cls[None, :] == cids, (tile_off * tile)[:, None], 0), axis=0) + route[1].astype(I32)

    n_row_tiles = n_tok // tile + n_cls
    seg_cls, seg_hi = _pass_segments()
    seg_cls = jnp.asarray(seg_cls, I32)
    seg_hi = jnp.asarray(seg_hi, I32)
    seg_per_expert = EXPERTS_PER_GROUP - 1
    seg_tiles = _take(tiles, seg_cls)
    seg_end = jnp.cumsum(seg_tiles)
    n_used = seg_end[-1]
    p = jnp.arange(2 * n_row_tiles, dtype=I32)
    seg = jnp.minimum(jnp.sum((seg_end[None, :] <= p[:, None]).astype(I32), axis=1), seg_cls.shape[0] - 1)
    within = p - _take(seg_end - seg_tiles, seg)
    spare = p - n_used
    used = p < n_used
    pass_tile = jnp.where(used, _take(tile_off, _take(seg_cls, seg)) + within, tile_end[-1] + spare // 2).astype(I32)
    pass_hi = jnp.where(used, _take(seg_hi, seg), spare % 2).astype(I32)
    pass_expert = jnp.where(used, seg // seg_per_expert, N_EXPERT_GROUPS * EXPERTS_PER_GROUP - 1).astype(I32)
    pass_rows = jnp.where(used, jnp.clip(_take(_take(cnt, seg_cls), seg) - within * tile, 0, tile), 0)
    last_tile = (tile_end - 1).astype(I32)
    return (pos.astype(I32), pass_tile, pass_hi, pass_expert, pass_rows.astype(I32),
            last_tile, tiles.astype(I32), tile_end[-1:].astype(I32), n_row_tiles)


def _zero_tiles(last_ref, tiles_ref, nu_ref, zero_ref, hs_ref, sem, n_e, tile, wait):
    t8 = tile // SUBLANES
    for e in range(n_e):
        @pl.when(tiles_ref[e] > 0)
        def _(e=e):
            cp = pltpu.make_async_copy(zero_ref, hs_ref.at[pl.ds(last_ref[e] * t8, t8)], sem)
            if wait:
                cp.wait()
            else:
                cp.start()

    def body(j, carry):
        cp = pltpu.make_async_copy(zero_ref, hs_ref.at[pl.ds(j * t8, t8)], sem)
        if wait:
            cp.wait()
        else:
            cp.start()
        return carry

    lax.fori_loop(nu_ref[0], hs_ref.shape[0] // t8, body, 0)


def _dispatch_body(pos_ref, last_ref, tiles_ref, nu_ref, h_ref, hs_ref, zero_ref, sem_z, sem, *,
                   tm, tile, n_e):
    i = pl.program_id(0)

    @pl.when(i == 0)
    def _():
        zero_ref[...] = jnp.zeros_like(zero_ref)
        _zero_tiles(last_ref, tiles_ref, nu_ref, zero_ref, hs_ref, sem_z, n_e, tile, False)
        _zero_tiles(last_ref, tiles_ref, nu_ref, zero_ref, hs_ref, sem_z, n_e, tile, True)

    base = i * tm

    def issue(blk, carry):
        for u in range(SUBLANES):
            p = pos_ref[base + blk * SUBLANES + u]
            pltpu.make_async_copy(h_ref.at[blk, pl.ds(u, 1)],
                                  hs_ref.at[p >> 3, pl.ds(p & (SUBLANES - 1), 1)], sem).start(priority=u % 2)
        return carry

    lax.fori_loop(0, tm // SUBLANES, issue, 0, unroll=2)
    pltpu.make_async_copy(h_ref, hs_ref.at[pl.ds(0, tm // SUBLANES)], sem).wait()


def _dispatch(h_rows, pos, last_tile, tiles, n_used, n_rows, tile):
    n, d = h_rows.shape
    n_e = tiles.shape[0]
    tm = min(n, 2048)
    grid_spec = pltpu.PrefetchScalarGridSpec(
        num_scalar_prefetch=4,
        grid=(n // tm,),
        in_specs=[pl.BlockSpec((tm // SUBLANES, SUBLANES, d), lambda i, *_: (i, 0, 0))],
        out_specs=pl.BlockSpec(memory_space=pl.ANY),
        scratch_shapes=[pltpu.VMEM((tile // SUBLANES, SUBLANES, d), F32),
                        pltpu.SemaphoreType.DMA(()), pltpu.SemaphoreType.DMA(())],
    )
    hs = pl.pallas_call(
        functools.partial(_dispatch_body, tm=tm, tile=tile, n_e=n_e),
        grid_spec=grid_spec,
        out_shape=jax.ShapeDtypeStruct((n_rows // SUBLANES, SUBLANES, d), F32),
        compiler_params=_cparams("arbitrary"),
        name="moe_dispatch",
    )(pos, last_tile, tiles, n_used, h_rows.reshape(n // SUBLANES, SUBLANES, d))
    return hs.reshape(n_rows, d)


def _expert_body(pt_ref, ph_ref, te_ref, pr_ref, nx_ref, sl_ref, hs_ref, wg_ref, wu_ref, wd_ref, ys_ref,
                 wgf, wuf, wdf, wgb, wub, wdb, sem, *, layer):
    i = pl.program_id(0)
    rows = pr_ref[i]
    first = jnp.logical_or(i == 0, te_ref[i] != te_ref[jnp.maximum(i - 1, 0)])

    def weight_copies(expert, slot):
        return [pltpu.make_async_copy(w_ref.at[layer, expert], buf.at[slot], sem.at[slot])
                for w_ref, buf in ((wg_ref, wgf), (wu_ref, wuf), (wd_ref, wdf))]

    @pl.when(jnp.logical_and(rows > 0, first))
    def _():
        slot = sl_ref[i]

        @pl.when(i == 0)
        def _():
            for cp in weight_copies(te_ref[i], slot):
                cp.start()

        for cp in weight_copies(te_ref[i], slot):
            cp.wait()

        @pl.when(nx_ref[i] >= 0)
        def _():
            for cp in weight_copies(nx_ref[i], 1 - slot):
                cp.start()

        wgb[...] = wgf[slot].astype(BF16)
        wub[...] = wuf[slot].astype(BF16)
        wdb[...] = wdf[slot].astype(BF16)

    def ffn(n):
        x = hs_ref[0:n, :].astype(BF16)
        gate = jnp.dot(x, wgb[...], preferred_element_type=F32)
        up = jnp.dot(x, wub[...], preferred_element_type=F32)
        a = (gate * jax.nn.sigmoid(gate) * up).astype(BF16)
        ys_ref[0:n, :] = jnp.dot(a, wdb[...], preferred_element_type=F32)

    tile = hs_ref.shape[0]
    sizes = (tile, tile // 2, tile // 4)
    for k, n in enumerate(sizes):
        lo = sizes[k + 1] if k + 1 < len(sizes) else 0

        @pl.when(jnp.logical_and(rows > lo, rows <= n))
        def _(n=n):
            ffn(n)
            if n < tile:
                ys_ref[n:, :] = jnp.zeros((tile - n, ys_ref.shape[1]), F32)

    @pl.when(rows == 0)
    def _():
        ys_ref[...] = jnp.zeros_like(ys_ref)


def _expert_ffn(hs, pass_tile, pass_hi, pass_expert, pass_rows, w_gate, w_up, w_down, layer, tile):
    _, n_e, d, f = w_gate.shape
    n_rows = hs.shape[0]
    eids = jnp.arange(n_e, dtype=I32)
    n_pass_e = jnp.sum(jnp.where((pass_expert[None, :] == eids[:, None]) & (pass_rows[None, :] > 0), 1, 0), axis=1)
    has = n_pass_e > 0
    later = (eids[None, :] > eids[:, None]) & has[None, :]
    next_used = jnp.min(jnp.where(later, eids[None, :], n_e), axis=1)
    next_used = jnp.where(next_used == n_e, -1, next_used).astype(I32)
    slot_e = ((jnp.cumsum(has.astype(I32)) - 1) % 2).astype(I32)
    pass_next = _take(next_used, pass_expert)
    pass_slot = jnp.maximum(_take(slot_e, pass_expert), 0)
    any_spec = pl.BlockSpec(memory_space=pl.ANY)
    grid_spec = pltpu.PrefetchScalarGridSpec(
        num_scalar_prefetch=6,
        grid=(pass_tile.shape[0],),
        in_specs=[pl.BlockSpec((tile, d), lambda i, pt, *_: (pt[i], 0)), any_spec, any_spec, any_spec],
        out_specs=pl.BlockSpec((tile, d), lambda i, pt, ph, *_: (pt[i], ph[i])),
        scratch_shapes=[pltpu.VMEM((2, d, f), F32), pltpu.VMEM((2, d, f), F32), pltpu.VMEM((2, f, d), F32),
                        pltpu.VMEM((d, f), BF16), pltpu.VMEM((d, f), BF16), pltpu.VMEM((f, d), BF16),
                        pltpu.SemaphoreType.DMA((2,))],
    )
    return pl.pallas_call(
        functools.partial(_expert_body, layer=layer),
        grid_spec=grid_spec,
        out_shape=jax.ShapeDtypeStruct((n_rows, 2 * d), F32),
        compiler_params=_cparams("arbitrary"),
        name="moe_experts",
    )(pass_tile, pass_hi, pass_expert, pass_rows, pass_next, pass_slot, hs, w_gate, w_up, w_down)


def _combine_body(pos_ref, x_ref, mod_ref, wts_ref, lng_ref, lnb_ref, ys_ref, *rest,
                  tm, alpha, next_mod):
    if next_mod:
        nmod_ref, x2_ref, h_ref, buf, sem = rest
    else:
        x2_ref, buf, sem = rest
    step = pl.program_id(0) * pl.num_programs(1) + pl.program_id(1)
    n_steps = pl.num_programs(0) * pl.num_programs(1)
    d = x_ref.shape[2]

    def gather(s, slot):
        def issue(blk, carry):
            for u in range(SUBLANES):
                p = pos_ref[s * tm + blk * SUBLANES + u]
                pltpu.make_async_copy(ys_ref.at[p >> 3, pl.ds(p & (SUBLANES - 1), 1)],
                                      buf.at[slot, blk, pl.ds(u, 1)], sem.at[slot]).start(priority=u % 2)
            return carry

        lax.fori_loop(0, tm // SUBLANES, issue, 0, unroll=2)

    @pl.when(step == 0)
    def _():
        gather(0, 0)

    @pl.when(step + 1 < n_steps)
    def _():
        gather(step + 1, (step + 1) % 2)

    slot = step % 2
    pltpu.make_async_copy(ys_ref.at[pl.ds(0, tm // SUBLANES)], buf.at[slot], sem.at[slot]).wait()

    w = wts_ref[...]
    rows = buf[slot].reshape(tm, 2 * d)
    moe = w[:, 0:1] * rows[:, :d] + w[:, 1:2] * rows[:, d:]
    g2 = mod_ref[0, 5:6, :]
    x2 = _layer_norm(alpha * x_ref[0] + g2 * moe, lng_ref[...], lnb_ref[...])
    x2_ref[0] = x2
    if next_mod:
        h_ref[0] = x2 * (1.0 + nmod_ref[0, 1:2, :]) + nmod_ref[0, 0:1, :]


def _combine(ys, pos, wts, x, mods, ln_g, ln_b, alpha, next_mods=None):
    b, l, d = x.shape
    tm = min(l, 512)
    nt = l // tm
    tok = lambda i, j, *_: (i, j, 0)
    bat = lambda i, j, *_: (i, 0, 0)
    const2 = lambda i, j, *_: (0, 0)
    in_specs = [pl.BlockSpec((1, tm, d), tok),
                pl.BlockSpec((1, 6, d), bat),
                pl.BlockSpec((tm, 2), lambda i, j, *_: (i * nt + j, 0)),
                pl.BlockSpec((1, d), const2),
                pl.BlockSpec((1, d), const2),
                pl.BlockSpec(memory_space=pl.ANY)]
    ys_tiles = ys.reshape(ys.shape[0] // SUBLANES, SUBLANES, ys.shape[1])
    args = [x, mods, wts, ln_g.reshape(1, d), ln_b.reshape(1, d), ys_tiles]
    out_shape = [jax.ShapeDtypeStruct((b, l, d), F32)]
    out_specs = [pl.BlockSpec((1, tm, d), tok)]
    if next_mods is not None:
        in_specs.append(pl.BlockSpec((1, 6, d), bat))
        args.append(next_mods)
        out_shape.append(jax.ShapeDtypeStruct((b, l, d), F32))
        out_specs.append(pl.BlockSpec((1, tm, d), tok))
    grid_spec = pltpu.PrefetchScalarGridSpec(
        num_scalar_prefetch=1,
        grid=(b, nt),
        in_specs=in_specs,
        out_specs=out_specs,
        scratch_shapes=[pltpu.VMEM((2, tm // SUBLANES, SUBLANES, 2 * d), F32),
                        pltpu.SemaphoreType.DMA((2,))],
    )
    return pl.pallas_call(
        functools.partial(_combine_body, tm=tm, alpha=alpha, next_mod=next_mods is not None),
        grid_spec=grid_spec,
        out_shape=out_shape,
        compiler_params=_cparams("arbitrary", "arbitrary"),
        name="moe_combine",
    )(pos, *args)


MOE_TILE = 512


def _moe(h_rows, route, counts, w_gate, w_up, w_down, layer):
    n = h_rows.shape[0]
    tile = min(n, MOE_TILE)
    (pos, pass_tile, pass_hi, pass_expert, pass_rows, last_tile, tiles, n_used_tiles,
     n_row_tiles) = _dispatch_plan(route, counts, tile)
    hs = _dispatch(h_rows, pos, last_tile, tiles, n_used_tiles, n_row_tiles * tile, tile)
    ys = _expert_ffn(hs, pass_tile, pass_hi, pass_expert, pass_rows, w_gate, w_up, w_down, layer, tile)
    return ys, pos, jnp.transpose(route[2:4])


def _pool_group(h_ref, w_ref, sc_ref, o_ref, col_ref, k, n_rows):
    n = n_rows * GRID_W
    c = h_ref.shape[2]
    blk = 4 * GRID_W
    half = k // 2
    pad = half * GRID_W
    ti = lax.broadcasted_iota(I32, (blk, blk), 0)
    si = lax.broadcasted_iota(I32, (blk, blk), 1)
    shift = GRID_W.bit_length() - 1
    same_row = (ti >> shift) == (si >> shift)
    band = jnp.where(same_row & (si - ti >= -half) & (si - ti <= half - 1), 1.0, 0.0).astype(BF16)
    col_ref[0:pad, :] = jnp.zeros((pad, c), F32)
    col_ref[pad + n:pad + n + pad, :] = jnp.zeros((pad, c), F32)
    for b0 in range(0, n, blk):
        hb = h_ref[0, b0:b0 + blk, :]
        head = hb.astype(BF16)
        rest = (hb - head.astype(F32)).astype(BF16)
        col_ref[pad + b0:pad + b0 + blk, :] = (jnp.dot(band, head, preferred_element_type=F32)
                                               + jnp.dot(band, rest, preferred_element_type=F32))
    acc = col_ref[0:n, :]
    for j in range(1, k):
        acc = acc + col_ref[j * GRID_W:j * GRID_W + n, :]
    t = lax.broadcasted_iota(I32, (n, 1), 0)
    wc = t & (GRID_W - 1)
    wr = t >> shift
    cnt_c = jnp.minimum(wc + half - 1, GRID_W - 1) - jnp.maximum(wc - half, 0) + 1
    cnt_r = jnp.minimum(wr + half - 1, n_rows - 1) - jnp.maximum(wr - half, 0) + 1
    mean = acc / (cnt_c * cnt_r).astype(F32)
    pooled = (mean - h_ref[0]).astype(BF16)
    o_ref[0] = jnp.dot(pooled, w_ref[0].astype(BF16), preferred_element_type=F32) * sc_ref[...]


def _pool_body(h_ref, w_ref, sc_ref, o_ref, col_ref, *, n_rows):
    g = pl.program_id(1)
    for gi, k in enumerate(POOL_WINDOWS):
        @pl.when(g == gi)
        def _(k=k):
            _pool_group(h_ref, w_ref, sc_ref, o_ref, col_ref, k, n_rows)


def _pool_mix(h, w_grp, scale):
    b, n, d = h.shape
    n_g, c, _ = w_grp.shape
    n_rows = n // GRID_W
    pad = (max(POOL_WINDOWS) // 2) * GRID_W
    return pl.pallas_call(
        functools.partial(_pool_body, n_rows=n_rows),
        grid=(b, n_g),
        in_specs=[pl.BlockSpec((1, n, c), lambda i, j: (i, 0, j)),
                  pl.BlockSpec((1, c, c), lambda i, j: (j, 0, 0)),
                  pl.BlockSpec((1, c), lambda i, j: (0, j))],
        out_specs=pl.BlockSpec((1, n, c), lambda i, j: (i, 0, j)),
        out_shape=jax.ShapeDtypeStruct((b, n, d), F32),
        scratch_shapes=[pltpu.VMEM((n + 2 * pad, c), F32)],
        compiler_params=_cparams("arbitrary", "arbitrary"),
        name="pool_mix",
    )(h, w_grp, scale.reshape(1, d))


def kernel(x, c, ctx, c_ctx, mod_w, mod_b, ln_g, ln_b, s5_lam_re, s5_lam_im, s5_log_dt, s5_b_re, s5_b_im,
           s5_c_re, s5_c_im, s5_d, s5_w_val, s5_w_gate, pool_w, pool_scale, router_w, router_b,
           moe_w_gate, moe_w_up, moe_w_down):
    b, l, d = x.shape
    depth = mod_w.shape[0]
    assert depth == 2 and b + 1 <= SUBLANES and d % LANES == 0 and GRID_W & (GRID_W - 1) == 0
    alpha = (2 * depth) ** 0.25

    cond = jnp.zeros((SUBLANES, d), F32).at[:b].set(c).at[b].set(c_ctx)
    mods = _modulation(cond, mod_w, mod_b).reshape(depth, SUBLANES, 6, d)

    toep, w1, w2t, a_r, a_i = _s5_weights(s5_lam_re[0], s5_lam_im[0], s5_log_dt[0], s5_b_re[0], s5_b_im[0],
                                          s5_c_re[0], s5_c_im[0], s5_d[0])
    y = _s5_mix(x, ctx, mods[0, :b], mods[0, b:b + 1], toep, w1, w2t, a_r, a_i)
    x1, h_rows, route, counts = _post_mixer(y, x, mods[0, :b], ln_g[0, 0], ln_b[0, 0], router_w, router_b,
                                            alpha, glu_w=(s5_w_val[0], s5_w_gate[0]))
    ys, pos, wts = _moe(h_rows, route, counts, moe_w_gate, moe_w_up, moe_w_down, 0)
    x2, h = _combine(ys, pos, wts, x1, mods[0, :b], ln_g[0, 1], ln_b[0, 1], alpha, next_mods=mods[1, :b])

    m = _pool_mix(h, pool_w[0], pool_scale[0])
    x3, h_rows, route, counts = _post_mixer(m, x2, mods[1, :b], ln_g[1, 0], ln_b[1, 0], router_w, router_b,
                                            alpha)
    ys, pos, wts = _moe(h_rows, route, counts, moe_w_gate, moe_w_up, moe_w_down, 1)
    (out,) = _combine(ys, pos, wts, x3, mods[1, :b], ln_g[1, 1], ln_b[1, 1], alpha)
    return out
```

```python
import functools

import jax
import jax.numpy as jnp
from jax import lax
from jax.experimental import pallas as pl
from jax.experimental.pallas import tpu as pltpu

F32 = jnp.float32
BF16 = jnp.bfloat16
I32 = jnp.int32

GRID_W = 64
S5_H = 16
S5_T = 16
POOL_WINDOWS = (2, 4, 8, 16)
N_EXPERT_GROUPS = 4
LN_EPS = 1e-5
LANES = 128
SUBLANES = 8
VMEM_LIMIT = 52 * 1024 * 1024

EXPERTS_PER_GROUP = 4
PAIRS_PER_GROUP = EXPERTS_PER_GROUP * (EXPERTS_PER_GROUP - 1) // 2
N_PAIR_CLASSES = N_EXPERT_GROUPS * PAIRS_PER_GROUP
ROUTE_ROWS = 8


def _cparams(*sem):
    return pltpu.CompilerParams(dimension_semantics=sem, vmem_limit_bytes=VMEM_LIMIT)


def _dot3(a, b, dims):
    a_head = a.astype(BF16)
    a_rest = (a - a_head.astype(F32)).astype(BF16)
    b_head = b.astype(BF16)
    b_rest = (b - b_head.astype(F32)).astype(BF16)
    dot = functools.partial(lax.dot_general, dimension_numbers=dims, preferred_element_type=F32)
    return dot(a_head, b_head) + dot(a_head, b_rest) + dot(a_rest, b_head)


MATMUL_NN = (((1,), (0,)), ((), ()))
MATMUL_NT = (((1,), (1,)), ((), ()))


def _mod_body(c_ref, w_ref, b_ref, o_ref):
    c = c_ref[...]
    s = c * jax.nn.sigmoid(c)
    o_ref[0] = _dot3(s, w_ref[0], MATMUL_NN) + b_ref[0]


def _modulation(cond, mod_w, mod_b):
    depth, d, n6 = mod_w.shape
    tn = min(n6, 1536)
    return pl.pallas_call(
        _mod_body,
        grid=(depth, n6 // tn),
        in_specs=[pl.BlockSpec((SUBLANES, d), lambda i, j: (0, 0)),
                  pl.BlockSpec((1, d, tn), lambda i, j: (i, 0, j)),
                  pl.BlockSpec((1, 1, tn), lambda i, j: (i, 0, j))],
        out_specs=pl.BlockSpec((1, SUBLANES, tn), lambda i, j: (i, 0, j)),
        out_shape=jax.ShapeDtypeStruct((depth, SUBLANES, n6), F32),
        compiler_params=_cparams("arbitrary", "arbitrary"),
        name="modulation",
    )(cond, mod_w, mod_b.reshape(depth, 1, n6))


def _s5_direction_terms(lam_re, lam_im, log_dt, b_re, b_im):
    lr = lam_re.astype(F32)
    li = lam_im.astype(F32)
    dt = jnp.exp(log_dt.astype(F32))[:, None]
    mag = jnp.exp(lr * dt)
    ar = mag * jnp.cos(li * dt)
    ai = mag * jnp.sin(li * dt)
    den = lr * lr + li * li
    nr = ar - 1.0
    fr = (nr * lr + ai * li) / den
    fi = (ai * lr - nr * li) / den
    br_, bi_ = b_re.astype(F32), b_im.astype(F32)
    bbr = fr[..., None] * br_ - fi[..., None] * bi_
    bbi = fr[..., None] * bi_ + fi[..., None] * br_
    k = jnp.arange(S5_T + 1, dtype=F32)[:, None, None]
    pm = jnp.exp(k * (lr * dt))
    pr = pm * jnp.cos(k * (li * dt))
    pi = pm * jnp.sin(k * (li * dt))
    return pr, pi, bbr, bbi


def _s5_weights(lam_re, lam_im, log_dt, b_re, b_im, c_re, c_im, d_skip):
    t = S5_T
    g, p = lam_re.shape[1:]
    h = b_re.shape[-1]
    terms = [_s5_direction_terms(lam_re[d], lam_im[d], log_dt[d], b_re[d], b_im[d]) for d in (0, 1)]
    pw = jnp.stack([jnp.transpose(terms[d][k], (1, 0, 2)) for d in (0, 1) for k in (0, 1)], axis=1)
    bt = jnp.stack([jnp.transpose(terms[d][k], (0, 2, 1)) for d in (0, 1) for k in (2, 3)], axis=1)
    cc = jnp.stack([c[d].astype(F32) for d in (0, 1) for c in (c_re, c_im)], axis=1)
    gp = 8
    spec4 = lambda rows: pl.BlockSpec((gp, 4, rows, p), lambda i: (i, 0, 0, 0))
    wide = pl.BlockSpec((gp, t * h, t * h), lambda i: (i, 0, 0))
    toep, w1, w2t = pl.pallas_call(
        functools.partial(_s5_prep_body, gp=gp),
        grid=(g // gp,),
        in_specs=[spec4(t + 1), spec4(h), spec4(h), pl.BlockSpec((gp, 1, h), lambda i: (i, 0, 0))],
        out_specs=[wide, wide, wide],
        out_shape=[jax.ShapeDtypeStruct((g, t * h, t * h), BF16) for _ in range(3)],
        compiler_params=_cparams("arbitrary"),
        name="s5_prep",
    )(pw, bt, cc, d_skip.astype(F32).reshape(g, 1, h))
    a_r = jnp.concatenate([pw[:, 0, t], pw[:, 2, t]], axis=-1)[:, None, :]
    a_i = jnp.concatenate([pw[:, 1, t], pw[:, 3, t]], axis=-1)[:, None, :]
    return toep, w1, w2t, a_r, a_i


def _cmul(ar, ai, br, bi):
    return ar * br - ai * bi, ar * bi + ai * br


def _s5_prep_body(pw_ref, bt_ref, cc_ref, d_ref, toep_ref, w1_ref, w2t_ref, *, gp):
    t = pw_ref.shape[2] - 1
    h = bt_ref.shape[2]
    eye =(lax.broadcasted_iota(I32, (h, h), 0) == lax.broadcasted_iota(I32, (h, h), 1))
    for g in range(gp):
        prf, pif, prb, pib = (pw_ref[g, k] for k in range(4))
        btf = (bt_ref[g, 0], bt_ref[g, 1])
        btb = (bt_ref[g, 2], bt_ref[g, 3])
        ccf = (cc_ref[g, 0], cc_ref[g, 1])
        ccb = (cc_ref[g, 2], cc_ref[g, 3])
        caf, cab = [], []
        for j in range(t):
            f_r, f_i = _cmul(*btf, prf[t - 1 - j:t - j], pif[t - 1 - j:t - j])
            b_r, b_i = _cmul(*btb, prb[j:j + 1], pib[j:j + 1])
            w1_ref[g, j * h:(j + 1) * h, :] = jnp.concatenate([f_r, b_r, f_i, b_i], axis=1).astype(BF16)
            mf_r, mf_i = _cmul(*ccf, prf[j + 1:j + 2], pif[j + 1:j + 2])
            mb_r, mb_i = _cmul(*ccb, prb[t - j:t - j + 1], pib[t - j:t - j + 1])
            w2t_ref[g, j * h:(j + 1) * h, :] = jnp.concatenate([mf_r, mb_r, -mf_i, -mb_i],
                                                               axis=1).astype(BF16)
            caf.append(_cmul(*ccf, prf[j:j + 1], pif[j:j + 1]))
            cab.append(_cmul(*ccb, prb[t - 1 - j:t - j], pib[t - 1 - j:t - j]))

        def lag_kernels(btx, ca):
            car = jnp.concatenate([c[0] for c in ca], axis=0)
            cai = jnp.concatenate([c[1] for c in ca], axis=0)
            return _dot3(btx[0], car, MATMUL_NT) - _dot3(btx[1], cai, MATMUL_NT)

        kf = lag_kernels(btf, caf)
        kb = lag_kernels(btb, cab)
        skip = jnp.where(eye, jnp.broadcast_to(d_ref[g], (h, h)), 0.0)
        mid = kb[:, (t - 1) * h:] + kf[:, :h] + skip
        kwide = jnp.concatenate([kb[:, :(t - 1) * h], mid, kf[:, h:], jnp.zeros((h, h), F32)], axis=1)
        for i in range(t):
            off = (t - 1 - i) * h
            toep_ref[g, i * h:(i + 1) * h, :] = kwide[:, off:off + t * h].astype(BF16)


def _granule_transpose(v):
    n = len(v)
    gran = lax.broadcasted_iota(I32, v[0].shape, 1) >> 4
    at = [gran == q for q in range(n)]
    rot = []
    for d in range(n):
        m = v[d]
        for q in range(1, n):
            m = jnp.where(at[q], v[(q + d) % n], m)
        rot.append(pltpu.roll(m, d * S5_H, 1) if d else m)
    out = []
    for q in range(n):
        w = rot[(-q) % n]
        for j in range(1, n):
            w = jnp.where(at[j], rot[(j - q) % n], w)
        out.append(w)
    return out


def _s5_body(x_ref, ctx_ref, mod_ref, cmod_ref, tt_ref, w1_ref, w2t_ref, ar_ref, ai_ref, y_ref,
             u_ref, s_ref, *, n_ctx_chunks, n_lat_chunks, gb):
    n_b = x_ref.shape[0]
    n_g = u_ref.shape[0]
    n_blocks = s_ref.shape[1] // SUBLANES
    half = SUBLANES // 2
    tok_blk = SUBLANES * S5_T
    lat_lo = n_ctx_chunks * n_b
    lat_rows = n_lat_chunks * n_b

    def chunk_rows(ref, b, tok0, shift, scale1):
        vs = [ref[b, pl.ds(tok0 + j, SUBLANES, stride=S5_T), :] * scale1 + shift for j in range(S5_T)]
        lo = _granule_transpose(vs[:SUBLANES])
        hi = _granule_transpose(vs[SUBLANES:])
        return lo, hi

    def put_rows(rows, chunk0, b):
        for q in range(n_g):
            for jh in range(2):
                u_ref[q, jh, pl.ds(chunk0 * n_b + b, SUBLANES, stride=n_b), :] = rows[jh][q]

    def u_rows(g, lo, n):
        return jnp.concatenate([u_ref[g, 0, lo:lo + n, :], u_ref[g, 1, lo:lo + n, :]], axis=1)

    shift = jnp.broadcast_to(cmod_ref[0, 0:1, :], (SUBLANES, LANES))
    scale1 = 1.0 + jnp.broadcast_to(cmod_ref[0, 1:2, :], (SUBLANES, LANES))
    for b in range(n_b):
        for cb in range(n_ctx_chunks // SUBLANES):
            rows = chunk_rows(ctx_ref, b, cb * tok_blk, shift, scale1)
            put_rows(rows, cb * SUBLANES, b)
            put_rows(rows, n_ctx_chunks + n_lat_chunks + cb * SUBLANES, b)

    def fill(cb, carry):
        for b in range(n_b):
            shift = jnp.broadcast_to(mod_ref[b, 0:1, :], (SUBLANES, LANES))
            scale1 = 1.0 + jnp.broadcast_to(mod_ref[b, 1:2, :], (SUBLANES, LANES))
            put_rows(chunk_rows(x_ref, b, cb * tok_blk, shift, scale1), n_ctx_chunks + cb * SUBLANES, b)
        return carry

    lax.fori_loop(0, n_lat_chunks // SUBLANES, fill, 0)

    for g in range(n_g):
        s_ref[g] = jnp.dot(u_rows(g, 0, n_blocks * SUBLANES).astype(BF16), w1_ref[g],
                           preferred_element_type=F32)

    lane = lax.broadcasted_iota(I32, (SUBLANES, LANES), 1)
    row = lax.broadcasted_iota(I32, (SUBLANES, LANES), 0)
    is_fwd = lane < LANES // 2
    is_fwd2 = jnp.concatenate([is_fwd, is_fwd], axis=1)
    top = row < half
    zero = jnp.zeros((SUBLANES, LANES), F32)
    for g0 in range(0, n_g, gb):
        ars = [jnp.broadcast_to(ar_ref[g0 + g], (SUBLANES, LANES)) for g in range(gb)]
        ais = [jnp.broadcast_to(ai_ref[g0 + g], (SUBLANES, LANES)) for g in range(gb)]

        def step(k, carry, g0=g0, ars=ars, ais=ais):
            fo = pl.multiple_of(k * SUBLANES, SUBLANES)
            bo = pl.multiple_of((n_blocks - 1 - k) * SUBLANES, SUBLANES)
            new = []
            loaded = [(s_ref[g0 + g, pl.ds(fo, SUBLANES), :], s_ref[g0 + g, pl.ds(bo, SUBLANES), :])
                      for g in range(gb)]
            stores = []
            for g in range(gb):
                xr, xi = carry[2 * g], carry[2 * g + 1]
                vf, vb_raw = loaded[g]
                vb = pltpu.roll(vb_raw, half, 0)
                vr = jnp.where(is_fwd, vf[:, :LANES], vb[:, :LANES])
                vi = jnp.where(is_fwd, vf[:, LANES:], vb[:, LANES:])
                ar, ai = ars[g], ais[g]
                yr = ar * xr - ai * xi + vr
                yi = ar * xi + ai * xr + vi
                yrr = pltpu.roll(yr, half, 0)
                yir = pltpu.roll(yi, half, 0)
                zr = ar * yrr - ai * yir + vr
                zi = ar * yir + ai * yrr + vi
                inc = jnp.concatenate([jnp.where(top, xr, yrr), jnp.where(top, xi, yir)], axis=1)
                stores.append((jnp.where(is_fwd2, inc, vf),
                               jnp.where(is_fwd2, vb_raw, pltpu.roll(inc, half, 0))))
                new.append(jnp.where(top, pltpu.roll(zr, half, 0), zr))
                new.append(jnp.where(top, pltpu.roll(zi, half, 0), zi))
            for g in range(gb):
                s_ref[g0 + g, pl.ds(fo, SUBLANES), :] = stores[g][0]
                s_ref[g0 + g, pl.ds(bo, SUBLANES), :] = stores[g][1]
            return tuple(new)

        lax.fori_loop(0, (n_ctx_chunks + n_lat_chunks) // 2, step, tuple(zero for _ in range(2 * gb)))

    for g in range(n_g):
        y = (jnp.dot(u_rows(g, lat_lo, lat_rows).astype(BF16), tt_ref[g], preferred_element_type=F32)
             + lax.dot_general(s_ref[g, lat_lo:lat_lo + lat_rows, :].astype(BF16), w2t_ref[g],
                               (((1,), (1,)), ((), ())), preferred_element_type=F32))
        for jh in range(2):
            u_ref[g, jh, lat_lo:lat_lo + lat_rows, :] = y[:, jh * LANES:(jh + 1) * LANES]

    def emit(cb, carry):
        for b in range(n_b):
            for jh in range(S5_T // SUBLANES):
                w = [u_ref[q, jh, pl.ds((n_ctx_chunks + cb * SUBLANES) * n_b + b, SUBLANES, stride=n_b), :]
                     for q in range(n_g)]
                v = _granule_transpose(w)
                for j in range(SUBLANES):
                    y_ref[b, pl.ds(cb * tok_blk + jh * SUBLANES + j, SUBLANES, stride=S5_T), :] = v[j]
        return carry

    lax.fori_loop(0, n_lat_chunks // SUBLANES, emit, 0)


def _s5_mix(x, ctx, mods, cmods, toep, w1, w2t, a_r, a_i):
    b, l, d = x.shape
    n_ctx = ctx.shape[1]
    n_g = LANES // S5_H
    n_ctx_chunks, n_lat_chunks = n_ctx // S5_T, l // S5_T
    assert b * 2 == SUBLANES and n_ctx_chunks % SUBLANES == 0 and n_lat_chunks % SUBLANES == 0
    rows = (2 * n_ctx_chunks + n_lat_chunks) * b
    w = S5_T * S5_H
    body = functools.partial(_s5_body, n_ctx_chunks=n_ctx_chunks, n_lat_chunks=n_lat_chunks, gb=8)
    lane_tile = lambda i: (0, 0, i)
    wspec = pl.BlockSpec((n_g, w, w), lambda i: (i, 0, 0))
    aspec = pl.BlockSpec((n_g, 1, w // 2), lambda i: (i, 0, 0))
    return pl.pallas_call(
        body,
        grid=(d // LANES,),
        in_specs=[pl.BlockSpec((b, l, LANES), lane_tile),
                  pl.BlockSpec((b, n_ctx, LANES), lane_tile),
                  pl.BlockSpec((b, 6, LANES), lane_tile),
                  pl.BlockSpec((1, 6, LANES), lane_tile),
                  wspec, wspec, wspec, aspec, aspec],
        out_specs=pl.BlockSpec((b, l, LANES), lane_tile, pipeline_mode=pl.Buffered(1)),
        out_shape=jax.ShapeDtypeStruct((b, l, d), F32),
        scratch_shapes=[pltpu.VMEM((n_g, w // LANES, rows, LANES), F32), pltpu.VMEM((n_g, rows, w), F32)],
        compiler_params=_cparams("arbitrary"),
        name="s5_mix",
    )(x, ctx, mods, cmods, toep, w1, w2t, a_r, a_i)


def _layer_norm(r, g, b):
    mu = jnp.mean(r, axis=-1, keepdims=True)
    xc = r - mu
    var = jnp.mean(xc * xc, axis=-1, keepdims=True)
    return xc * lax.rsqrt(var + LN_EPS) * g + b


def _max2_of4(a, b, c, d):
    h1, l1 = jnp.maximum(a, b), jnp.minimum(a, b)
    h2, l2 = jnp.maximum(c, d), jnp.minimum(c, d)
    return jnp.maximum(h1, h2) + jnp.maximum(jnp.minimum(h1, h2), jnp.maximum(l1, l2))


def _argmax_first(vals):
    idx = jnp.zeros(vals[0].shape, I32)
    best = vals[0]
    for j in range(1, len(vals)):
        upd = vals[j] > best
        idx = jnp.where(upd, j, idx)
        best = jnp.where(upd, vals[j], best)
    return idx, best


def _route(logits_t, count_ref, route_ref):
    n_e, tm = logits_t.shape
    per = n_e // N_EXPERT_GROUPS
    mx = jnp.max(logits_t, axis=0, keepdims=True)
    ex = jnp.exp(logits_t - mx)
    sc = ex / jnp.sum(ex, axis=0, keepdims=True)
    rows = [sc[e:e + 1, :] for e in range(n_e)]
    gscore = [_max2_of4(*rows[per * g:per * (g + 1)]) for g in range(N_EXPERT_GROUPS)]
    best, _ = _argmax_first(gscore)
    vals = []
    for j in range(per):
        v = rows[per * (N_EXPERT_GROUPS - 1) + j]
        for g in range(N_EXPERT_GROUPS - 2, -1, -1):
            v = jnp.where(best == g, rows[per * g + j], v)
        vals.append(v)
    i1, m1 = _argmax_first(vals)
    i2, m2 = _argmax_first([jnp.where(i1 == j, -1.0, vals[j]) for j in range(per)])
    den = m1 + m2
    first_lo = i1 < i2
    lo = jnp.minimum(i1, i2)
    hi = jnp.maximum(i1, i2)
    pair = jnp.where(lo == 0, 0, jnp.where(lo == 1, per - 1, 2 * per - 3)) + hi - lo - 1
    cls = best * PAIRS_PER_GROUP + pair
    w_lo = jnp.where(first_lo, m1, m2) / den
    w_hi = jnp.where(first_lo, m2, m1) / den

    n_cls = count_ref.shape[0]
    hit = lax.broadcasted_iota(I32, (n_cls, tm), 0) == cls
    onehot = jnp.where(hit, 1.0, 0.0)
    src = lax.broadcasted_iota(I32, (tm, tm), 0)
    dst = lax.broadcasted_iota(I32, (tm, tm), 1)
    tri = jnp.where(src <= dst, 1.0, 0.0).astype(BF16)
    cum = jnp.dot(onehot.astype(BF16), tri, preferred_element_type=F32)
    excl = cum - onehot + count_ref[:, 0:1]
    rank = jnp.sum(jnp.where(hit, excl, 0.0), axis=0, keepdims=True)
    count_ref[...] = count_ref[...] + jnp.sum(onehot, axis=1, keepdims=True)

    zero = jnp.zeros((1, tm), F32)
    route_ref[...] = jnp.concatenate([cls.astype(F32), rank, w_lo, w_hi, zero, zero, zero, zero], axis=0)


def _post_mixer_body(m_ref, x_ref, mod_ref, lng_ref, lnb_ref, rwt_ref, rb_ref, *rest, glu, alpha):
    if glu:
        wv_ref, wg_ref, x1_ref, h_ref, route_ref, cnt_out_ref, cnt_ref, wvb, wgb = rest
    else:
        x1_ref, h_ref, route_ref, cnt_out_ref, cnt_ref = rest

    @pl.when((pl.program_id(0) == 0) & (pl.program_id(1) == 0))
    def _():
        cnt_ref[...] = jnp.zeros_like(cnt_ref)
        if glu:
            wvb[...] = wv_ref[...].astype(BF16)
            wgb[...] = wg_ref[...].astype(BF16)

    if glu:
        a = jax.nn.gelu(m_ref[0], approximate=True).astype(BF16)
        val = jnp.dot(a, wvb[...], preferred_element_type=F32)
        gate = jnp.dot(a, wgb[...], preferred_element_type=F32)
        m = val * jax.nn.sigmoid(gate)
    else:
        m = m_ref[0]

    g1 = mod_ref[0, 2:3, :]
    sh2 = mod_ref[0, 3:4, :]
    sc2 = mod_ref[0, 4:5, :]
    x1 = _layer_norm(alpha * x_ref[0] + g1 * m, lng_ref[...], lnb_ref[...])
    x1_ref[0] = x1
    h = x1 * (1.0 + sc2) + sh2
    h_ref[...] = h
    logits_t = _dot3(rwt_ref[...], h, MATMUL_NT) + rb_ref[...]
    _route(logits_t, cnt_ref, route_ref)
    cnt_out_ref[...] = cnt_ref[...]


def _post_mixer(m, x, mods, ln_g, ln_b, router_w, router_b, alpha, glu_w=None):
    b, l, d = x.shape
    n_e = router_w.shape[1]
    assert n_e == N_EXPERT_GROUPS * EXPERTS_PER_GROUP
    tm = min(l, 512)
    nt = l // tm
    tok = lambda i, j: (i, j, 0)
    const2 = lambda i, j: (0, 0)
    in_specs = [pl.BlockSpec((1, tm, d), tok),
                pl.BlockSpec((1, tm, d), tok),
                pl.BlockSpec((1, 6, d), lambda i, j: (i, 0, 0)),
                pl.BlockSpec((1, d), const2),
                pl.BlockSpec((1, d), const2),
                pl.BlockSpec((n_e, d), const2),
                pl.BlockSpec((n_e, 1), const2)]
    args = [m, x, mods, ln_g.reshape(1, d), ln_b.reshape(1, d), router_w.T, router_b.reshape(n_e, 1)]
    scratch = [pltpu.VMEM((N_PAIR_CLASSES, LANES), F32)]
    if glu_w is not None:
        resident = pl.BlockSpec((d, d), const2, pipeline_mode=pl.Buffered(1))
        in_specs += [resident, resident]
        args += [glu_w[0], glu_w[1]]
        scratch += [pltpu.VMEM((d, d), BF16), pltpu.VMEM((d, d), BF16)]
    out_shape = [jax.ShapeDtypeStruct((b, l, d), F32),
                 jax.ShapeDtypeStruct((b * l, d), F32),
                 jax.ShapeDtypeStruct((ROUTE_ROWS, b * l), F32),
                 jax.ShapeDtypeStruct((N_PAIR_CLASSES, LANES), F32)]
    out_specs = [pl.BlockSpec((1, tm, d), tok),
                 pl.BlockSpec((tm, d), lambda i, j: (i * nt + j, 0)),
                 pl.BlockSpec((ROUTE_ROWS, tm), lambda i, j: (0, i * nt + j)),
                 pl.BlockSpec((N_PAIR_CLASSES, LANES), const2)]
    return pl.pallas_call(
        functools.partial(_post_mixer_body, glu=glu_w is not None, alpha=alpha),
        grid=(b, nt),
        in_specs=in_specs,
        out_specs=out_specs,
        out_shape=out_shape,
        scratch_shapes=scratch,
        compiler_params=_cparams("arbitrary", "arbitrary"),
        name="post_mixer_glu" if glu_w is not None else "post_mixer",
    )(*args)


def _take(table, idx):
    ids = jnp.arange(table.shape[0], dtype=I32)
    return jnp.sum(jnp.where(idx[:, None] == ids[None, :], table[None, :], 0), axis=1)


def _pass_segments():
    seg_cls, seg_hi = [], []
    pairs = [(a, b) for a in range(EXPERTS_PER_GROUP) for b in range(a + 1, EXPERTS_PER_GROUP)]
    for g in range(N_EXPERT_GROUPS):
        for m in range(EXPERTS_PER_GROUP):
            for idx, (a, b) in enumerate(pairs):
                if m in (a, b):
                    seg_cls.append(g * PAIRS_PER_GROUP + idx)
                    seg_hi.append(int(m == b))
    return seg_cls, seg_hi


def _dispatch_plan(route, counts, tile):
    n_cls = counts.shape[0]
    n_tok = route.shape[1]
    cnt = counts[:, 0].astype(I32)
    tiles = (cnt + tile - 1) // tile
    tile_end = jnp.cumsum(tiles)
    tile_off = tile_end - tiles
    cids = jnp.arange(n_cls, dtype=I32)[:, None]
    cls = route[0].astype(I32)
    pos = jnp.sum(jnp.where(cls[None, :] == cids, (tile_off * tile)[:, None], 0), axis=0) + route[1].astype(I32)

    n_row_tiles = n_tok // tile + n_cls
    seg_cls, seg_hi = _pass_segments()
    seg_cls = jnp.asarray(seg_cls, I32)
    seg_hi = jnp.asarray(seg_hi, I32)
    seg_per_expert = EXPERTS_PER_GROUP - 1
    seg_tiles = _take(tiles, seg_cls)
    seg_end = jnp.cumsum(seg_tiles)
    n_used = seg_end[-1]
    p = jnp.arange(2 * n_row_tiles, dtype=I32)
    seg = jnp.minimum(jnp.sum((seg_end[None, :] <= p[:, None]).astype(I32), axis=1), seg_cls.shape[0] - 1)
    within = p - _take(seg_end - seg_tiles, seg)
    spare = p - n_used
    used = p < n_used
    pass_tile = jnp.where(used, _take(tile_off, _take(seg_cls, seg)) + within, tile_end[-1] + spare // 2).astype(I32)
    pass_hi = jnp.where(used, _take(seg_hi, seg), spare % 2).astype(I32)
    pass_expert = jnp.where(used, seg // seg_per_expert, N_EXPERT_GROUPS * EXPERTS_PER_GROUP - 1).astype(I32)
    pass_rows = jnp.where(used, jnp.clip(_take(_take(cnt, seg_cls), seg) - within * tile, 0, tile), 0)
    last_tile = (tile_end - 1).astype(I32)
    return (pos.astype(I32), pass_tile, pass_hi, pass_expert, pass_rows.astype(I32),
            last_tile, tiles.astype(I32), tile_end[-1:].astype(I32), n_row_tiles)


def _zero_tiles(last_ref, tiles_ref, nu_ref, zero_ref, hs_ref, sem, n_e, tile, wait):
    t8 = tile // SUBLANES
    for e in range(n_e):
        @pl.when(tiles_ref[e] > 0)
        def _(e=e):
            cp = pltpu.make_async_copy(zero_ref, hs_ref.at[pl.ds(last_ref[e] * t8, t8)], sem)
            if wait:
                cp.wait()
            else:
                cp.start()

    def body(j, carry):
        cp = pltpu.make_async_copy(zero_ref, hs_ref.at[pl.ds(j * t8, t8)], sem)
        if wait:
            cp.wait()
        else:
            cp.start()
        return carry

    lax.fori_loop(nu_ref[0], hs_ref.shape[0] // t8, body, 0)


def _dispatch_body(pos_ref, last_ref, tiles_ref, nu_ref, h_ref, hs_ref, zero_ref, sem_z, sem, *,
                   tm, tile, n_e):
    i = pl.program_id(0)

    @pl.when(i == 0)
    def _():
        zero_ref[...] = jnp.zeros_like(zero_ref)
        _zero_tiles(last_ref, tiles_ref, nu_ref, zero_ref, hs_ref, sem_z, n_e, tile, False)
        _zero_tiles(last_ref, tiles_ref, nu_ref, zero_ref, hs_ref, sem_z, n_e, tile, True)

    base = i * tm

    def issue(blk, carry):
        for u in range(SUBLANES):
            p = pos_ref[base + blk * SUBLANES + u]
            pltpu.make_async_copy(h_ref.at[blk, pl.ds(u, 1)],
                                  hs_ref.at[p >> 3, pl.ds(p & (SUBLANES - 1), 1)], sem).start(priority=u % 2)
        return carry

    lax.fori_loop(0, tm // SUBLANES, issue, 0, unroll=2)
    pltpu.make_async_copy(h_ref, hs_ref.at[pl.ds(0, tm // SUBLANES)], sem).wait()


def _dispatch(h_rows, pos, last_tile, tiles, n_used, n_rows, tile):
    n, d = h_rows.shape
    n_e = tiles.shape[0]
    tm = min(n, 2048)
    grid_spec = pltpu.PrefetchScalarGridSpec(
        num_scalar_prefetch=4,
        grid=(n // tm,),
        in_specs=[pl.BlockSpec((tm // SUBLANES, SUBLANES, d), lambda i, *_: (i, 0, 0))],
        out_specs=pl.BlockSpec(memory_space=pl.ANY),
        scratch_shapes=[pltpu.VMEM((tile // SUBLANES, SUBLANES, d), F32),
                        pltpu.SemaphoreType.DMA(()), pltpu.SemaphoreType.DMA(())],
    )
    hs = pl.pallas_call(
        functools.partial(_dispatch_body, tm=tm, tile=tile, n_e=n_e),
        grid_spec=grid_spec,
        out_shape=jax.ShapeDtypeStruct((n_rows // SUBLANES, SUBLANES, d), F32),
        compiler_params=_cparams("arbitrary"),
        name="moe_dispatch",
    )(pos, last_tile, tiles, n_used, h_rows.reshape(n // SUBLANES, SUBLANES, d))
    return hs.reshape(n_rows, d)


def _expert_body(pt_ref, ph_ref, te_ref, pr_ref, nx_ref, sl_ref, hs_ref, wg_ref, wu_ref, wd_ref, ys_ref,
                 wgf, wuf, wdf, wgb, wub, wdb, sem, *, layer):
    i = pl.program_id(0)
    rows = pr_ref[i]
    first = jnp.logical_or(i == 0, te_ref[i] != te_ref[jnp.maximum(i - 1, 0)])

    def weight_copies(expert, slot):
        return [pltpu.make_async_copy(w_ref.at[layer, expert], buf.at[slot], sem.at[slot])
                for w_ref, buf in ((wg_ref, wgf), (wu_ref, wuf), (wd_ref, wdf))]

    @pl.when(jnp.logical_and(rows > 0, first))
    def _():
        slot = sl_ref[i]

        @pl.when(i == 0)
        def _():
            for cp in weight_copies(te_ref[i], slot):
                cp.start()

        for cp in weight_copies(te_ref[i], slot):
            cp.wait()

        @pl.when(nx_ref[i] >= 0)
        def _():
            for cp in weight_copies(nx_ref[i], 1 - slot):
                cp.start()

        wgb[...] = wgf[slot].astype(BF16)
        wub[...] = wuf[slot].astype(BF16)
        wdb[...] = wdf[slot].astype(BF16)

    def ffn(n):
        x = hs_ref[0:n, :].astype(BF16)
        gate = jnp.dot(x, wgb[...], preferred_element_type=F32)
        up = jnp.dot(x, wub[...], preferred_element_type=F32)
        a = (gate * jax.nn.sigmoid(gate) * up).astype(BF16)
        ys_ref[0:n, :] = jnp.dot(a, wdb[...], preferred_element_type=F32)

    tile = hs_ref.shape[0]
    sizes = (tile, tile // 2, tile // 4)
    for k, n in enumerate(sizes):
        lo = sizes[k + 1] if k + 1 < len(sizes) else 0

        @pl.when(jnp.logical_and(rows > lo, rows <= n))
        def _(n=n):
            ffn(n)
            if n < tile:
                ys_ref[n:, :] = jnp.zeros((tile - n, ys_ref.shape[1]), F32)

    @pl.when(rows == 0)
    def _():
        ys_ref[...] = jnp.zeros_like(ys_ref)


def _expert_ffn(hs, pass_tile, pass_hi, pass_expert, pass_rows, w_gate, w_up, w_down, layer, tile):
    _, n_e, d, f = w_gate.shape
    n_rows = hs.shape[0]
    eids = jnp.arange(n_e, dtype=I32)
    n_pass_e = jnp.sum(jnp.where((pass_expert[None, :] == eids[:, None]) & (pass_rows[None, :] > 0), 1, 0), axis=1)
    has = n_pass_e > 0
    later = (eids[None, :] > eids[:, None]) & has[None, :]
    next_used = jnp.min(jnp.where(later, eids[None, :], n_e), axis=1)
    next_used = jnp.where(next_used == n_e, -1, next_used).astype(I32)
    slot_e = ((jnp.cumsum(has.astype(I32)) - 1) % 2).astype(I32)
    pass_next = _take(next_used, pass_expert)
    pass_slot = jnp.maximum(_take(slot_e, pass_expert), 0)
    any_spec = pl.BlockSpec(memory_space=pl.ANY)
    grid_spec = pltpu.PrefetchScalarGridSpec(
        num_scalar_prefetch=6,
        grid=(pass_tile.shape[0],),
        in_specs=[pl.BlockSpec((tile, d), lambda i, pt, ph, te, pr, *_: (pt[jnp.where(pr[i] > 0, i, 0)], 0)),
                  any_spec, any_spec, any_spec],
        out_specs=pl.BlockSpec((tile, d), lambda i, pt, ph, *_: (pt[i], ph[i])),
        scratch_shapes=[pltpu.VMEM((2, d, f), F32), pltpu.VMEM((2, d, f), F32), pltpu.VMEM((2, f, d), F32),
                        pltpu.VMEM((d, f), BF16), pltpu.VMEM((d, f), BF16), pltpu.VMEM((f, d), BF16),
                        pltpu.SemaphoreType.DMA((2,))],
    )
    return pl.pallas_call(
        functools.partial(_expert_body, layer=layer),
        grid_spec=grid_spec,
        out_shape=jax.ShapeDtypeStruct((n_rows, 2 * d), F32),
        compiler_params=_cparams("arbitrary"),
        name="moe_experts",
    )(pass_tile, pass_hi, pass_expert, pass_rows, pass_next, pass_slot, hs, w_gate, w_up, w_down)


def _combine_body(pos_ref, x_ref, mod_ref, wts_ref, lng_ref, lnb_ref, ys_ref, *rest,
                  tm, alpha, next_mod):
    if next_mod:
        nmod_ref, x2_ref, h_ref, buf, sem = rest
    else:
        x2_ref, buf, sem = rest
    step = pl.program_id(0) * pl.num_programs(1) + pl.program_id(1)
    n_steps = pl.num_programs(0) * pl.num_programs(1)
    d = x_ref.shape[2]

    def gather(s, slot):
        def issue(blk, carry):
            for u in range(SUBLANES):
                p = pos_ref[s * tm + blk * SUBLANES + u]
                pltpu.make_async_copy(ys_ref.at[p >> 3, pl.ds(p & (SUBLANES - 1), 1)],
                                      buf.at[slot, blk, pl.ds(u, 1)], sem.at[slot]).start(priority=u % 2)
            return carry

        lax.fori_loop(0, tm // SUBLANES, issue, 0, unroll=2)

    @pl.when(step == 0)
    def _():
        gather(0, 0)

    @pl.when(step + 1 < n_steps)
    def _():
        gather(step + 1, (step + 1) % 2)

    slot = step % 2
    pltpu.make_async_copy(ys_ref.at[pl.ds(0, tm // SUBLANES)], buf.at[slot], sem.at[slot]).wait()

    w = wts_ref[...]
    rows = buf[slot].reshape(tm, 2 * d)
    moe = w[:, 0:1] * rows[:, :d] + w[:, 1:2] * rows[:, d:]
    g2 = mod_ref[0, 5:6, :]
    x2 = _layer_norm(alpha * x_ref[0] + g2 * moe, lng_ref[...], lnb_ref[...])
    x2_ref[0] = x2
    if next_mod:
        h_ref[0] = x2 * (1.0 + nmod_ref[0, 1:2, :]) + nmod_ref[0, 0:1, :]


def _combine(ys, pos, wts, x, mods, ln_g, ln_b, alpha, next_mods=None):
    b, l, d = x.shape
    tm = min(l, 512)
    nt = l // tm
    tok = lambda i, j, *_: (i, j, 0)
    bat = lambda i, j, *_: (i, 0, 0)
    const2 = lambda i, j, *_: (0, 0)
    in_specs = [pl.BlockSpec((1, tm, d), tok),
                pl.BlockSpec((1, 6, d), bat),
                pl.BlockSpec((tm, 2), lambda i, j, *_: (i * nt + j, 0)),
                pl.BlockSpec((1, d), const2),
                pl.BlockSpec((1, d), const2),
                pl.BlockSpec(memory_space=pl.ANY)]
    ys_tiles = ys.reshape(ys.shape[0] // SUBLANES, SUBLANES, ys.shape[1])
    args = [x, mods, wts, ln_g.reshape(1, d), ln_b.reshape(1, d), ys_tiles]
    out_shape = [jax.ShapeDtypeStruct((b, l, d), F32)]
    out_specs = [pl.BlockSpec((1, tm, d), tok)]
    if next_mods is not None:
        in_specs.append(pl.BlockSpec((1, 6, d), bat))
        args.append(next_mods)
        out_shape.append(jax.ShapeDtypeStruct((b, l, d), F32))
        out_specs.append(pl.BlockSpec((1, tm, d), tok))
    grid_spec = pltpu.PrefetchScalarGridSpec(
        num_scalar_prefetch=1,
        grid=(b, nt),
        in_specs=in_specs,
        out_specs=out_specs,
        scratch_shapes=[pltpu.VMEM((2, tm // SUBLANES, SUBLANES, 2 * d), F32),
                        pltpu.SemaphoreType.DMA((2,))],
    )
    return pl.pallas_call(
        functools.partial(_combine_body, tm=tm, alpha=alpha, next_mod=next_mods is not None),
        grid_spec=grid_spec,
        out_shape=out_shape,
        compiler_params=_cparams("arbitrary", "arbitrary"),
        name="moe_combine",
    )(pos, *args)


MOE_TILE = 256


def _moe(h_rows, route, counts, w_gate, w_up, w_down, layer):
    n = h_rows.shape[0]
    tile = min(n, MOE_TILE)
    (pos, pass_tile, pass_hi, pass_expert, pass_rows, last_tile, tiles, n_used_tiles,
     n_row_tiles) = _dispatch_plan(route, counts, tile)
    hs = _dispatch(h_rows, pos, last_tile, tiles, n_used_tiles, n_row_tiles * tile, tile)
    ys = _expert_ffn(hs, pass_tile, pass_hi, pass_expert, pass_rows, w_gate, w_up, w_down, layer, tile)
    return ys, pos, jnp.transpose(route[2:4])


def _pool_group(h_ref, w_ref, sc_ref, o_ref, col_ref, k, n_rows):
    n = n_rows * GRID_W
    c = h_ref.shape[2]
    blk = 4 * GRID_W
    half = k // 2
    pad = half * GRID_W
    ti = lax.broadcasted_iota(I32, (blk, blk), 0)
    si = lax.broadcasted_iota(I32, (blk, blk), 1)
    shift = GRID_W.bit_length() - 1
    same_row = (ti >> shift) == (si >> shift)
    band = jnp.where(same_row & (si - ti >= -half) & (si - ti <= half - 1), 1.0, 0.0).astype(BF16)
    col_ref[0:pad, :] = jnp.zeros((pad, c), F32)
    col_ref[pad + n:pad + n + pad, :] = jnp.zeros((pad, c), F32)
    for b0 in range(0, n, blk):
        hb = h_ref[0, b0:b0 + blk, :]
        head = hb.astype(BF16)
        rest = (hb - head.astype(F32)).astype(BF16)
        col_ref[pad + b0:pad + b0 + blk, :] = (jnp.dot(band, head, preferred_element_type=F32)
                                               + jnp.dot(band, rest, preferred_element_type=F32))
    acc = col_ref[0:n, :]
    for j in range(1, k):
        acc = acc + col_ref[j * GRID_W:j * GRID_W + n, :]
    t = lax.broadcasted_iota(I32, (n, 1), 0)
    wc = t & (GRID_W - 1)
    wr = t >> shift
    cnt_c = jnp.minimum(wc + half - 1, GRID_W - 1) - jnp.maximum(wc - half, 0) + 1
    cnt_r = jnp.minimum(wr + half - 1, n_rows - 1) - jnp.maximum(wr - half, 0) + 1
    mean = acc / (cnt_c * cnt_r).astype(F32)
    pooled = (mean - h_ref[0]).astype(BF16)
    o_ref[0] = jnp.dot(pooled, w_ref[0].astype(BF16), preferred_element_type=F32) * sc_ref[...]


def _pool_body(h_ref, w_ref, sc_ref, o_ref, col_ref, *, n_rows):
    g = pl.program_id(1)
    for gi, k in enumerate(POOL_WINDOWS):
        @pl.when(g == gi)
        def _(k=k):
            _pool_group(h_ref, w_ref, sc_ref, o_ref, col_ref, k, n_rows)


def _pool_mix(h, w_grp, scale):
    b, n, d = h.shape
    n_g, c, _ = w_grp.shape
    n_rows = n // GRID_W
    pad = (max(POOL_WINDOWS) // 2) * GRID_W
    return pl.pallas_call(
        functools.partial(_pool_body, n_rows=n_rows),
        grid=(b, n_g),
        in_specs=[pl.BlockSpec((1, n, c), lambda i, j: (i, 0, j)),
                  pl.BlockSpec((1, c, c), lambda i, j: (j, 0, 0)),
                  pl.BlockSpec((1, c), lambda i, j: (0, j))],
        out_specs=pl.BlockSpec((1, n, c), lambda i, j: (i, 0, j)),
        out_shape=jax.ShapeDtypeStruct((b, n, d), F32),
        scratch_shapes=[pltpu.VMEM((n + 2 * pad, c), F32)],
        compiler_params=_cparams("arbitrary", "arbitrary"),
        name="pool_mix",
    )(h, w_grp, scale.reshape(1, d))


def kernel(x, c, ctx, c_ctx, mod_w, mod_b, ln_g, ln_b, s5_lam_re, s5_lam_im, s5_log_dt, s5_b_re, s5_b_im,
           s5_c_re, s5_c_im, s5_d, s5_w_val, s5_w_gate, pool_w, pool_scale, router_w, router_b,
           moe_w_gate, moe_w_up, moe_w_down):
    b, l, d = x.shape
    depth = mod_w.shape[0]
    assert depth == 2 and b + 1 <= SUBLANES and d % LANES == 0 and GRID_W & (GRID_W - 1) == 0
    alpha = (2 * depth) ** 0.25

    cond = jnp.zeros((SUBLANES, d), F32).at[:b].set(c).at[b].set(c_ctx)
    mods = _modulation(cond, mod_w, mod_b).reshape(depth, SUBLANES, 6, d)

    toep, w1, w2t, a_r, a_i = _s5_weights(s5_lam_re[0], s5_lam_im[0], s5_log_dt[0], s5_b_re[0], s5_b_im[0],
                                          s5_c_re[0], s5_c_im[0], s5_d[0])
    y = _s5_mix(x, ctx, mods[0, :b], mods[0, b:b + 1], toep, w1, w2t, a_r, a_i)
    x1, h_rows, route, counts = _post_mixer(y, x, mods[0, :b], ln_g[0, 0], ln_b[0, 0], router_w, router_b,
                                            alpha, glu_w=(s5_w_val[0], s5_w_gate[0]))
    ys, pos, wts = _moe(h_rows, route, counts, moe_w_gate, moe_w_up, moe_w_down, 0)
    x2, h = _combine(ys, pos, wts, x1, mods[0, :b], ln_g[0, 1], ln_b[0, 1], alpha, next_mods=mods[1, :b])

    m = _pool_mix(h, pool_w[0], pool_scale[0])
    x3, h_rows, route, counts = _post_mixer(m, x2, mods[1, :b], ln_g[1, 0], ln_b[1, 0], router_w, router_b,
                                            alpha)
    ys, pos, wts = _moe(h_rows, route, counts, moe_w_gate, moe_w_up, moe_w_down, 1)
    (out,) = _combine(ys, pos, wts, x3, mods[1, :b], ln_g[1, 1], ln_b[1, 1], alpha)
    return out
```

```python
import functools

import jax
import jax.numpy as jnp
from jax import lax
from jax.experimental import pallas as pl
from jax.experimental.pallas import tpu as pltpu

F32 = jnp.float32
BF16 = jnp.bfloat16
I32 = jnp.int32

GRID_W = 64
S5_H = 16
S5_T = 16
POOL_WINDOWS = (2, 4, 8, 16)
N_EXPERT_GROUPS = 4
LN_EPS = 1e-5
LANES = 128
SUBLANES = 8
VMEM_LIMIT = 52 * 1024 * 1024

EXPERTS_PER_GROUP = 4
PAIRS_PER_GROUP = EXPERTS_PER_GROUP * (EXPERTS_PER_GROUP - 1) // 2
N_PAIR_CLASSES = N_EXPERT_GROUPS * PAIRS_PER_GROUP
ROUTE_ROWS = 8


def _cparams(*sem):
    return pltpu.CompilerParams(dimension_semantics=sem, vmem_limit_bytes=VMEM_LIMIT)


def _dot3(a, b, dims):
    a_head = a.astype(BF16)
    a_rest = (a - a_head.astype(F32)).astype(BF16)
    b_head = b.astype(BF16)
    b_rest = (b - b_head.astype(F32)).astype(BF16)
    dot = functools.partial(lax.dot_general, dimension_numbers=dims, preferred_element_type=F32)
    return dot(a_head, b_head) + dot(a_head, b_rest) + dot(a_rest, b_head)


MATMUL_NN = (((1,), (0,)), ((), ()))
MATMUL_NT = (((1,), (1,)), ((), ()))


def _mod_body(c_ref, w_ref, b_ref, o_ref):
    c = c_ref[...]
    s = c * jax.nn.sigmoid(c)
    o_ref[0] = _dot3(s, w_ref[0], MATMUL_NN) + b_ref[0]


def _modulation(cond, mod_w, mod_b):
    depth, d, n6 = mod_w.shape
    tn = min(n6, 1536)
    return pl.pallas_call(
        _mod_body,
        grid=(depth, n6 // tn),
        in_specs=[pl.BlockSpec((SUBLANES, d), lambda i, j: (0, 0)),
                  pl.BlockSpec((1, d, tn), lambda i, j: (i, 0, j)),
                  pl.BlockSpec((1, 1, tn), lambda i, j: (i, 0, j))],
        out_specs=pl.BlockSpec((1, SUBLANES, tn), lambda i, j: (i, 0, j)),
        out_shape=jax.ShapeDtypeStruct((depth, SUBLANES, n6), F32),
        compiler_params=_cparams("arbitrary", "arbitrary"),
        name="modulation",
    )(cond, mod_w, mod_b.reshape(depth, 1, n6))


def _s5_direction_terms(lam_re, lam_im, log_dt, b_re, b_im):
    lr = lam_re.astype(F32)
    li = lam_im.astype(F32)
    dt = jnp.exp(log_dt.astype(F32))[:, None]
    mag = jnp.exp(lr * dt)
    ar = mag * jnp.cos(li * dt)
    ai = mag * jnp.sin(li * dt)
    den = lr * lr + li * li
    nr = ar - 1.0
    fr = (nr * lr + ai * li) / den
    fi = (ai * lr - nr * li) / den
    br_, bi_ = b_re.astype(F32), b_im.astype(F32)
    bbr = fr[..., None] * br_ - fi[..., None] * bi_
    bbi = fr[..., None] * bi_ + fi[..., None] * br_
    k = jnp.arange(S5_T + 1, dtype=F32)[:, None, None]
    pm = jnp.exp(k * (lr * dt))
    pr = pm * jnp.cos(k * (li * dt))
    pi = pm * jnp.sin(k * (li * dt))
    return pr, pi, bbr, bbi


def _s5_weights(lam_re, lam_im, log_dt, b_re, b_im, c_re, c_im, d_skip):
    t = S5_T
    g, p = lam_re.shape[1:]
    h = b_re.shape[-1]
    terms = [_s5_direction_terms(lam_re[d], lam_im[d], log_dt[d], b_re[d], b_im[d]) for d in (0, 1)]
    pw = jnp.stack([jnp.transpose(terms[d][k], (1, 0, 2)) for d in (0, 1) for k in (0, 1)], axis=1)
    bt = jnp.stack([jnp.transpose(terms[d][k], (0, 2, 1)) for d in (0, 1) for k in (2, 3)], axis=1)
    cc = jnp.stack([c[d].astype(F32) for d in (0, 1) for c in (c_re, c_im)], axis=1)
    gp = 8
    spec4 = lambda rows: pl.BlockSpec((gp, 4, rows, p), lambda i: (i, 0, 0, 0))
    wide = pl.BlockSpec((gp, t * h, t * h), lambda i: (i, 0, 0))
    toep, w1, w2t = pl.pallas_call(
        functools.partial(_s5_prep_body, gp=gp),
        grid=(g // gp,),
        in_specs=[spec4(t + 1), spec4(h), spec4(h), pl.BlockSpec((gp, 1, h), lambda i: (i, 0, 0))],
        out_specs=[wide, wide, wide],
        out_shape=[jax.ShapeDtypeStruct((g, t * h, t * h), BF16) for _ in range(3)],
        compiler_params=_cparams("arbitrary"),
        name="s5_prep",
    )(pw, bt, cc, d_skip.astype(F32).reshape(g, 1, h))
    a_r = jnp.concatenate([pw[:, 0, t], pw[:, 2, t]], axis=-1)[:, None, :]
    a_i = jnp.concatenate([pw[:, 1, t], pw[:, 3, t]], axis=-1)[:, None, :]
    return toep, w1, w2t, a_r, a_i


def _cmul(ar, ai, br, bi):
    return ar * br - ai * bi, ar * bi + ai * br


def _s5_prep_body(pw_ref, bt_ref, cc_ref, d_ref, toep_ref, w1_ref, w2t_ref, *, gp):
    t = pw_ref.shape[2] - 1
    h = bt_ref.shape[2]
    eye =(lax.broadcasted_iota(I32, (h, h), 0) == lax.broadcasted_iota(I32, (h, h), 1))
    for g in range(gp):
        prf, pif, prb, pib = (pw_ref[g, k] for k in range(4))
        btf = (bt_ref[g, 0], bt_ref[g, 1])
        btb = (bt_ref[g, 2], bt_ref[g, 3])
        ccf = (cc_ref[g, 0], cc_ref[g, 1])
        ccb = (cc_ref[g, 2], cc_ref[g, 3])
        caf, cab = [], []
        for j in range(t):
            f_r, f_i = _cmul(*btf, prf[t - 1 - j:t - j], pif[t - 1 - j:t - j])
            b_r, b_i = _cmul(*btb, prb[j:j + 1], pib[j:j + 1])
            w1_ref[g, j * h:(j + 1) * h, :] = jnp.concatenate([f_r, b_r, f_i, b_i], axis=1).astype(BF16)
            mf_r, mf_i = _cmul(*ccf, prf[j + 1:j + 2], pif[j + 1:j + 2])
            mb_r, mb_i = _cmul(*ccb, prb[t - j:t - j + 1], pib[t - j:t - j + 1])
            w2t_ref[g, j * h:(j + 1) * h, :] = jnp.concatenate([mf_r, mb_r, -mf_i, -mb_i],
                                                               axis=1).astype(BF16)
            caf.append(_cmul(*ccf, prf[j:j + 1], pif[j:j + 1]))
            cab.append(_cmul(*ccb, prb[t - 1 - j:t - j], pib[t - 1 - j:t - j]))

        def lag_kernels(btx, ca):
            car = jnp.concatenate([c[0] for c in ca], axis=0)
            cai = jnp.concatenate([c[1] for c in ca], axis=0)
            return _dot3(btx[0], car, MATMUL_NT) - _dot3(btx[1], cai, MATMUL_NT)

        kf = lag_kernels(btf, caf)
        kb = lag_kernels(btb, cab)
        skip = jnp.where(eye, jnp.broadcast_to(d_ref[g], (h, h)), 0.0)
        mid = kb[:, (t - 1) * h:] + kf[:, :h] + skip
        kwide = jnp.concatenate([kb[:, :(t - 1) * h], mid, kf[:, h:], jnp.zeros((h, h), F32)], axis=1)
        for i in range(t):
            off = (t - 1 - i) * h
            toep_ref[g, i * h:(i + 1) * h, :] = kwide[:, off:off + t * h].astype(BF16)


def _granule_transpose(v):
    n = len(v)
    gran = lax.broadcasted_iota(I32, v[0].shape, 1) >> 4
    at = [gran == q for q in range(n)]
    rot = []
    for d in range(n):
        m = v[d]
        for q in range(1, n):
            m = jnp.where(at[q], v[(q + d) % n], m)
        rot.append(pltpu.roll(m, d * S5_H, 1) if d else m)
    out = []
    for q in range(n):
        w = rot[(-q) % n]
        for j in range(1, n):
            w = jnp.where(at[j], rot[(j - q) % n], w)
        out.append(w)
    return out


def _s5_body(x_ref, ctx_ref, mod_ref, cmod_ref, tt_ref, w1_ref, w2t_ref, ar_ref, ai_ref, y_ref,
             u_ref, s_ref, *, n_ctx_chunks, n_lat_chunks, gb):
    n_b = x_ref.shape[0]
    n_g = u_ref.shape[0]
    n_blocks = s_ref.shape[1] // SUBLANES
    half = SUBLANES // 2
    tok_blk = SUBLANES * S5_T
    lat_lo = n_ctx_chunks * n_b
    lat_rows = n_lat_chunks * n_b

    def chunk_rows(ref, b, tok0, shift, scale1):
        vs = [ref[b, pl.ds(tok0 + j, SUBLANES, stride=S5_T), :] * scale1 + shift for j in range(S5_T)]
        lo = _granule_transpose(vs[:SUBLANES])
        hi = _granule_transpose(vs[SUBLANES:])
        return lo, hi

    def put_rows(rows, chunk0, b):
        for q in range(n_g):
            for jh in range(2):
                u_ref[q, jh, pl.ds(chunk0 * n_b + b, SUBLANES, stride=n_b), :] = rows[jh][q]

    def u_rows(g, lo, n):
        return jnp.concatenate([u_ref[g, 0, lo:lo + n, :], u_ref[g, 1, lo:lo + n, :]], axis=1)

    shift = jnp.broadcast_to(cmod_ref[0, 0:1, :], (SUBLANES, LANES))
    scale1 = 1.0 + jnp.broadcast_to(cmod_ref[0, 1:2, :], (SUBLANES, LANES))
    for b in range(n_b):
        for cb in range(n_ctx_chunks // SUBLANES):
            rows = chunk_rows(ctx_ref, b, cb * tok_blk, shift, scale1)
            put_rows(rows, cb * SUBLANES, b)
            put_rows(rows, n_ctx_chunks + n_lat_chunks + cb * SUBLANES, b)

    def fill(cb, carry):
        for b in range(n_b):
            shift = jnp.broadcast_to(mod_ref[b, 0:1, :], (SUBLANES, LANES))
            scale1 = 1.0 + jnp.broadcast_to(mod_ref[b, 1:2, :], (SUBLANES, LANES))
            put_rows(chunk_rows(x_ref, b, cb * tok_blk, shift, scale1), n_ctx_chunks + cb * SUBLANES, b)
        return carry

    lax.fori_loop(0, n_lat_chunks // SUBLANES, fill, 0)

    for g in range(n_g):
        s_ref[g] = jnp.dot(u_rows(g, 0, n_blocks * SUBLANES).astype(BF16), w1_ref[g],
                           preferred_element_type=F32)

    lane = lax.broadcasted_iota(I32, (SUBLANES, LANES), 1)
    row = lax.broadcasted_iota(I32, (SUBLANES, LANES), 0)
    is_fwd = lane < LANES // 2
    is_fwd2 = jnp.concatenate([is_fwd, is_fwd], axis=1)
    first = (lane >> ((LANES // 2).bit_length() - 1)) == (row >> (half.bit_length() - 1))
    zero = jnp.zeros((SUBLANES, LANES), F32)
    for g0 in range(0, n_g, gb):
        ars = [jnp.broadcast_to(ar_ref[g0 + g], (SUBLANES, LANES)) for g in range(gb)]
        ais = [jnp.broadcast_to(ai_ref[g0 + g], (SUBLANES, LANES)) for g in range(gb)]

        def step(k, carry, g0=g0, ars=ars, ais=ais):
            fo = pl.multiple_of(k * SUBLANES, SUBLANES)
            bo = pl.multiple_of((n_blocks - 1 - k) * SUBLANES, SUBLANES)
            new = []
            loaded = [(s_ref[g0 + g, pl.ds(fo, SUBLANES), :], s_ref[g0 + g, pl.ds(bo, SUBLANES), :])
                      for g in range(gb)]
            stores = []
            for g in range(gb):
                xr, xi = carry[2 * g], carry[2 * g + 1]
                vf, vb = loaded[g]
                vr = jnp.where(is_fwd, vf[:, :LANES], vb[:, :LANES])
                vi = jnp.where(is_fwd, vf[:, LANES:], vb[:, LANES:])
                ar, ai = ars[g], ais[g]
                yr = ar * xr - ai * xi + vr
                yi = ar * xi + ai * xr + vi
                yrr = pltpu.roll(yr, half, 0)
                yir = pltpu.roll(yi, half, 0)
                zr = ar * yrr - ai * yir + vr
                zi = ar * yir + ai * yrr + vi
                inc = jnp.concatenate([jnp.where(first, xr, yrr), jnp.where(first, xi, yir)], axis=1)
                stores.append((jnp.where(is_fwd2, inc, vf), jnp.where(is_fwd2, vb, inc)))
                new.append(jnp.where(first, pltpu.roll(zr, half, 0), zr))
                new.append(jnp.where(first, pltpu.roll(zi, half, 0), zi))
            for g in range(gb):
                s_ref[g0 + g, pl.ds(fo, SUBLANES), :] = stores[g][0]
                s_ref[g0 + g, pl.ds(bo, SUBLANES), :] = stores[g][1]
            return tuple(new)

        lax.fori_loop(0, (n_ctx_chunks + n_lat_chunks) // 2, step, tuple(zero for _ in range(2 * gb)))

    for g in range(n_g):
        y = (jnp.dot(u_rows(g, lat_lo, lat_rows).astype(BF16), tt_ref[g], preferred_element_type=F32)
             + lax.dot_general(s_ref[g, lat_lo:lat_lo + lat_rows, :].astype(BF16), w2t_ref[g],
                               (((1,), (1,)), ((), ())), preferred_element_type=F32))
        for jh in range(2):
            u_ref[g, jh, lat_lo:lat_lo + lat_rows, :] = y[:, jh * LANES:(jh + 1) * LANES]

    def emit(cb, carry):
        for b in range(n_b):
            for jh in range(S5_T // SUBLANES):
                w = [u_ref[q, jh, pl.ds((n_ctx_chunks + cb * SUBLANES) * n_b + b, SUBLANES, stride=n_b), :]
                     for q in range(n_g)]
                v = _granule_transpose(w)
                for j in range(SUBLANES):
                    y_ref[b, pl.ds(cb * tok_blk + jh * SUBLANES + j, SUBLANES, stride=S5_T), :] = v[j]
        return carry

    lax.fori_loop(0, n_lat_chunks // SUBLANES, emit, 0)


def _s5_mix(x, ctx, mods, cmods, toep, w1, w2t, a_r, a_i):
    b, l, d = x.shape
    n_ctx = ctx.shape[1]
    n_g = LANES // S5_H
    n_ctx_chunks, n_lat_chunks = n_ctx // S5_T, l // S5_T
    assert b * 2 == SUBLANES and n_ctx_chunks % SUBLANES == 0 and n_lat_chunks % SUBLANES == 0
    rows = (2 * n_ctx_chunks + n_lat_chunks) * b
    w = S5_T * S5_H
    body = functools.partial(_s5_body, n_ctx_chunks=n_ctx_chunks, n_lat_chunks=n_lat_chunks, gb=8)
    lane_tile = lambda i: (0, 0, i)
    wspec = pl.BlockSpec((n_g, w, w), lambda i: (i, 0, 0))
    aspec = pl.BlockSpec((n_g, 1, w // 2), lambda i: (i, 0, 0))
    return pl.pallas_call(
        body,
        grid=(d // LANES,),
        in_specs=[pl.BlockSpec((b, l, LANES), lane_tile),
                  pl.BlockSpec((b, n_ctx, LANES), lane_tile),
                  pl.BlockSpec((b, 6, LANES), lane_tile),
                  pl.BlockSpec((1, 6, LANES), lane_tile),
                  wspec, wspec, wspec, aspec, aspec],
        out_specs=pl.BlockSpec((b, l, LANES), lane_tile, pipeline_mode=pl.Buffered(1)),
        out_shape=jax.ShapeDtypeStruct((b, l, d), F32),
        scratch_shapes=[pltpu.VMEM((n_g, w // LANES, rows, LANES), F32), pltpu.VMEM((n_g, rows, w), F32)],
        compiler_params=_cparams("arbitrary"),
        name="s5_mix",
    )(x, ctx, mods, cmods, toep, w1, w2t, a_r, a_i)


def _layer_norm(r, g, b):
    mu = jnp.mean(r, axis=-1, keepdims=True)
    xc = r - mu
    var = jnp.mean(xc * xc, axis=-1, keepdims=True)
    return xc * lax.rsqrt(var + LN_EPS) * g + b


def _max2_of4(a, b, c, d):
    h1, l1 = jnp.maximum(a, b), jnp.minimum(a, b)
    h2, l2 = jnp.maximum(c, d), jnp.minimum(c, d)
    return jnp.maximum(h1, h2) + jnp.maximum(jnp.minimum(h1, h2), jnp.maximum(l1, l2))


def _argmax_first(vals):
    idx = jnp.zeros(vals[0].shape, I32)
    best = vals[0]
    for j in range(1, len(vals)):
        upd = vals[j] > best
        idx = jnp.where(upd, j, idx)
        best = jnp.where(upd, vals[j], best)
    return idx, best


def _route(logits_t, count_ref, route_ref):
    n_e, tm = logits_t.shape
    per = n_e // N_EXPERT_GROUPS
    mx = jnp.max(logits_t, axis=0, keepdims=True)
    ex = jnp.exp(logits_t - mx)
    sc = ex / jnp.sum(ex, axis=0, keepdims=True)
    rows = [sc[e:e + 1, :] for e in range(n_e)]
    gscore = [_max2_of4(*rows[per * g:per * (g + 1)]) for g in range(N_EXPERT_GROUPS)]
    best, _ = _argmax_first(gscore)
    vals = []
    for j in range(per):
        v = rows[per * (N_EXPERT_GROUPS - 1) + j]
        for g in range(N_EXPERT_GROUPS - 2, -1, -1):
            v = jnp.where(best == g, rows[per * g + j], v)
        vals.append(v)
    i1, m1 = _argmax_first(vals)
    i2, m2 = _argmax_first([jnp.where(i1 == j, -1.0, vals[j]) for j in range(per)])
    den = m1 + m2
    first_lo = i1 < i2
    lo = jnp.minimum(i1, i2)
    hi = jnp.maximum(i1, i2)
    pair = jnp.where(lo == 0, 0, jnp.where(lo == 1, per - 1, 2 * per - 3)) + hi - lo - 1
    cls = best * PAIRS_PER_GROUP + pair
    w_lo = jnp.where(first_lo, m1, m2) / den
    w_hi = jnp.where(first_lo, m2, m1) / den

    n_cls = count_ref.shape[0]
    hit = lax.broadcasted_iota(I32, (n_cls, tm), 0) == cls
    onehot = jnp.where(hit, 1.0, 0.0)
    src = lax.broadcasted_iota(I32, (tm, tm), 0)
    dst = lax.broadcasted_iota(I32, (tm, tm), 1)
    tri = jnp.where(src <= dst, 1.0, 0.0).astype(BF16)
    cum = jnp.dot(onehot.astype(BF16), tri, preferred_element_type=F32)
    excl = cum - onehot + count_ref[:, 0:1]
    rank = jnp.sum(jnp.where(hit, excl, 0.0), axis=0, keepdims=True)
    count_ref[...] = count_ref[...] + jnp.sum(onehot, axis=1, keepdims=True)

    zero = jnp.zeros((1, tm), F32)
    route_ref[...] = jnp.concatenate([cls.astype(F32), rank, w_lo, w_hi, zero, zero, zero, zero], axis=0)


def _post_mixer_body(m_ref, x_ref, mod_ref, lng_ref, lnb_ref, rwt_ref, rb_ref, *rest, glu, alpha):
    if glu:
        wv_ref, wg_ref, x1_ref, h_ref, route_ref, cnt_out_ref, cnt_ref, wvb, wgb = rest
    else:
        x1_ref, h_ref, route_ref, cnt_out_ref, cnt_ref = rest

    @pl.when((pl.program_id(0) == 0) & (pl.program_id(1) == 0))
    def _():
        cnt_ref[...] = jnp.zeros_like(cnt_ref)
        if glu:
            wvb[...] = wv_ref[...].astype(BF16)
            wgb[...] = wg_ref[...].astype(BF16)

    if glu:
        a = jax.nn.gelu(m_ref[0], approximate=True).astype(BF16)
        val = jnp.dot(a, wvb[...], preferred_element_type=F32)
        gate = jnp.dot(a, wgb[...], preferred_element_type=F32)
        m = val * jax.nn.sigmoid(gate)
    else:
        m = m_ref[0]

    g1 = mod_ref[0, 2:3, :]
    sh2 = mod_ref[0, 3:4, :]
    sc2 = mod_ref[0, 4:5, :]
    x1 = _layer_norm(alpha * x_ref[0] + g1 * m, lng_ref[...], lnb_ref[...])
    x1_ref[0] = x1
    h = x1 * (1.0 + sc2) + sh2
    h_ref[...] = h
    logits_t = _dot3(rwt_ref[...], h, MATMUL_NT) + rb_ref[...]
    _route(logits_t, cnt_ref, route_ref)
    cnt_out_ref[...] = cnt_ref[...]


def _post_mixer(m, x, mods, ln_g, ln_b, router_w, router_b, alpha, glu_w=None):
    b, l, d = x.shape
    n_e = router_w.shape[1]
    assert n_e == N_EXPERT_GROUPS * EXPERTS_PER_GROUP
    tm = min(l, 512)
    nt = l // tm
    tok = lambda i, j: (i, j, 0)
    const2 = lambda i, j: (0, 0)
    in_specs = [pl.BlockSpec((1, tm, d), tok),
                pl.BlockSpec((1, tm, d), tok),
                pl.BlockSpec((1, 6, d), lambda i, j: (i, 0, 0)),
                pl.BlockSpec((1, d), const2),
                pl.BlockSpec((1, d), const2),
                pl.BlockSpec((n_e, d), const2),
                pl.BlockSpec((n_e, 1), const2)]
    args = [m, x, mods, ln_g.reshape(1, d), ln_b.reshape(1, d), router_w.T, router_b.reshape(n_e, 1)]
    scratch = [pltpu.VMEM((N_PAIR_CLASSES, LANES), F32)]
    if glu_w is not None:
        resident = pl.BlockSpec((d, d), const2, pipeline_mode=pl.Buffered(1))
        in_specs += [resident, resident]
        args += [glu_w[0], glu_w[1]]
        scratch += [pltpu.VMEM((d, d), BF16), pltpu.VMEM((d, d), BF16)]
    out_shape = [jax.ShapeDtypeStruct((b, l, d), F32),
                 jax.ShapeDtypeStruct((b * l, d), F32),
                 jax.ShapeDtypeStruct((ROUTE_ROWS, b * l), F32),
                 jax.ShapeDtypeStruct((N_PAIR_CLASSES, LANES), F32)]
    out_specs = [pl.BlockSpec((1, tm, d), tok),
                 pl.BlockSpec((tm, d), lambda i, j: (i * nt + j, 0)),
                 pl.BlockSpec((ROUTE_ROWS, tm), lambda i, j: (0, i * nt + j)),
                 pl.BlockSpec((N_PAIR_CLASSES, LANES), const2)]
    return pl.pallas_call(
        functools.partial(_post_mixer_body, glu=glu_w is not None, alpha=alpha),
        grid=(b, nt),
        in_specs=in_specs,
        out_specs=out_specs,
        out_shape=out_shape,
        scratch_shapes=scratch,
        compiler_params=_cparams("arbitrary", "arbitrary"),
        name="post_mixer_glu" if glu_w is not None else "post_mixer",
    )(*args)


def _take(table, idx):
    ids = jnp.arange(table.shape[0], dtype=I32)
    return jnp.sum(jnp.where(idx[:, None] == ids[None, :], table[None, :], 0), axis=1)


def _pass_segments():
    seg_cls, seg_hi = [], []
    pairs = [(a, b) for a in range(EXPERTS_PER_GROUP) for b in range(a + 1, EXPERTS_PER_GROUP)]
    for g in range(N_EXPERT_GROUPS):
        for m in range(EXPERTS_PER_GROUP):
            for idx, (a, b) in enumerate(pairs):
                if m in (a, b):
                    seg_cls.append(g * PAIRS_PER_GROUP + idx)
                    seg_hi.append(int(m == b))
    return seg_cls, seg_hi


def _dispatch_plan(route, counts, tile):
    n_cls = counts.shape[0]
    n_tok = route.shape[1]
    cnt = counts[:, 0].astype(I32)
    tiles = (cnt + tile - 1) // tile
    tile_end = jnp.cumsum(tiles)
    tile_off = tile_end - tiles
    cids = jnp.arange(n_cls, dtype=I32)[:, None]
    cls = route[0].astype(I32)
    pos = jnp.sum(jnp.where(cls[None, :] == cids, (tile_off * tile)[:, None], 0), axis=0) + route[1].astype(I32)

    n_row_tiles = n_tok // tile + n_cls
    seg_cls, seg_hi = _pass_segments()
    seg_cls = jnp.asarray(seg_cls, I32)
    seg_hi = jnp.asarray(seg_hi, I32)
    seg_per_expert = EXPERTS_PER_GROUP - 1
    seg_tiles = _take(tiles, seg_cls)
    seg_end = jnp.cumsum(seg_tiles)
    n_used = seg_end[-1]
    p = jnp.arange(2 * n_row_tiles, dtype=I32)
    seg = jnp.minimum(jnp.sum((seg_end[None, :] <= p[:, None]).astype(I32), axis=1), seg_cls.shape[0] - 1)
    within = p - _take(seg_end - seg_tiles, seg)
    spare = p - n_used
    used = p < n_used
    pass_tile = jnp.where(used, _take(tile_off, _take(seg_cls, seg)) + within, tile_end[-1] + spare // 2).astype(I32)
    pass_hi = jnp.where(used, _take(seg_hi, seg), spare % 2).astype(I32)
    pass_expert = jnp.where(used, seg // seg_per_expert, N_EXPERT_GROUPS * EXPERTS_PER_GROUP - 1).astype(I32)
    pass_rows = jnp.where(used, jnp.clip(_take(_take(cnt, seg_cls), seg) - within * tile, 0, tile), 0)
    last_tile = (tile_end - 1).astype(I32)
    return (pos.astype(I32), pass_tile, pass_hi, pass_expert, pass_rows.astype(I32),
            last_tile, tiles.astype(I32), tile_end[-1:].astype(I32), n_row_tiles)


def _zero_tiles(last_ref, tiles_ref, nu_ref, zero_ref, hs_ref, sem, n_e, tile, wait):
    t8 = tile // SUBLANES
    for e in range(n_e):
        @pl.when(tiles_ref[e] > 0)
        def _(e=e):
            cp = pltpu.make_async_copy(zero_ref, hs_ref.at[pl.ds(last_ref[e] * t8, t8)], sem)
            if wait:
                cp.wait()
            else:
                cp.start()

    def body(j, carry):
        cp = pltpu.make_async_copy(zero_ref, hs_ref.at[pl.ds(j * t8, t8)], sem)
        if wait:
            cp.wait()
        else:
            cp.start()
        return carry

    lax.fori_loop(nu_ref[0], hs_ref.shape[0] // t8, body, 0)


def _dispatch_body(pos_ref, last_ref, tiles_ref, nu_ref, h_ref, hs_ref, zero_ref, sem_z, sem, *,
                   tm, tile, n_e):
    i = pl.program_id(0)

    @pl.when(i == 0)
    def _():
        zero_ref[...] = jnp.zeros_like(zero_ref)
        _zero_tiles(last_ref, tiles_ref, nu_ref, zero_ref, hs_ref, sem_z, n_e, tile, False)
        _zero_tiles(last_ref, tiles_ref, nu_ref, zero_ref, hs_ref, sem_z, n_e, tile, True)

    base = i * tm

    def issue(blk, carry):
        for u in range(SUBLANES):
            p = pos_ref[base + blk * SUBLANES + u]
            pltpu.make_async_copy(h_ref.at[blk, pl.ds(u, 1)],
                                  hs_ref.at[p >> 3, pl.ds(p & (SUBLANES - 1), 1)], sem).start(priority=u % 2)
        return carry

    lax.fori_loop(0, tm // SUBLANES, issue, 0, unroll=2)
    pltpu.make_async_copy(h_ref, hs_ref.at[pl.ds(0, tm // SUBLANES)], sem).wait()


def _dispatch(h_rows, pos, last_tile, tiles, n_used, n_rows, tile):
    n, d = h_rows.shape
    n_e = tiles.shape[0]
    tm = min(n, 2048)
    grid_spec = pltpu.PrefetchScalarGridSpec(
        num_scalar_prefetch=4,
        grid=(n // tm,),
        in_specs=[pl.BlockSpec((tm // SUBLANES, SUBLANES, d), lambda i, *_: (i, 0, 0))],
        out_specs=pl.BlockSpec(memory_space=pl.ANY),
        scratch_shapes=[pltpu.VMEM((tile // SUBLANES, SUBLANES, d), F32),
                        pltpu.SemaphoreType.DMA(()), pltpu.SemaphoreType.DMA(())],
    )
    hs = pl.pallas_call(
        functools.partial(_dispatch_body, tm=tm, tile=tile, n_e=n_e),
        grid_spec=grid_spec,
        out_shape=jax.ShapeDtypeStruct((n_rows // SUBLANES, SUBLANES, d), F32),
        compiler_params=_cparams("arbitrary"),
        name="moe_dispatch",
    )(pos, last_tile, tiles, n_used, h_rows.reshape(n // SUBLANES, SUBLANES, d))
    return hs.reshape(n_rows, d)


def _expert_body(pt_ref, ph_ref, te_ref, pr_ref, nx_ref, sl_ref, hs_ref, wg_ref, wu_ref, wd_ref, ys_ref,
                 wgf, wuf, wdf, wgb, wub, wdb, sem, *, layer):
    i = pl.program_id(0)
    rows = pr_ref[i]
    first = jnp.logical_or(i == 0, te_ref[i] != te_ref[jnp.maximum(i - 1, 0)])

    def weight_copies(expert, slot):
        return [pltpu.make_async_copy(w_ref.at[layer, expert], buf.at[slot], sem.at[slot])
                for w_ref, buf in ((wg_ref, wgf), (wu_ref, wuf), (wd_ref, wdf))]

    @pl.when(jnp.logical_and(rows > 0, first))
    def _():
        slot = sl_ref[i]

        @pl.when(i == 0)
        def _():
            for cp in weight_copies(te_ref[i], slot):
                cp.start()

        for cp in weight_copies(te_ref[i], slot):
            cp.wait()

        @pl.when(nx_ref[i] >= 0)
        def _():
            for cp in weight_copies(nx_ref[i], 1 - slot):
                cp.start()

        wgb[...] = wgf[slot].astype(BF16)
        wub[...] = wuf[slot].astype(BF16)
        wdb[...] = wdf[slot].astype(BF16)

    def ffn(n):
        x = hs_ref[0:n, :].astype(BF16)
        gate = jnp.dot(x, wgb[...], preferred_element_type=F32)
        up = jnp.dot(x, wub[...], preferred_element_type=F32)
        a = (gate * jax.nn.sigmoid(gate) * up).astype(BF16)
        ys_ref[0:n, :] = jnp.dot(a, wdb[...], preferred_element_type=F32)

    tile = hs_ref.shape[0]
    sizes = (tile, tile // 2, tile // 4)
    for k, n in enumerate(sizes):
        lo = sizes[k + 1] if k + 1 < len(sizes) else 0

        @pl.when(jnp.logical_and(rows > lo, rows <= n))
        def _(n=n):
            ffn(n)
            if n < tile:
                ys_ref[n:, :] = jnp.zeros((tile - n, ys_ref.shape[1]), F32)

    @pl.when(rows == 0)
    def _():
        ys_ref[...] = jnp.zeros_like(ys_ref)


def _expert_ffn(hs, pass_tile, pass_hi, pass_expert, pass_rows, w_gate, w_up, w_down, layer, tile):
    _, n_e, d, f = w_gate.shape
    n_rows = hs.shape[0]
    eids = jnp.arange(n_e, dtype=I32)
    n_pass_e = jnp.sum(jnp.where((pass_expert[None, :] == eids[:, None]) & (pass_rows[None, :] > 0), 1, 0), axis=1)
    has = n_pass_e > 0
    later = (eids[None, :] > eids[:, None]) & has[None, :]
    next_used = jnp.min(jnp.where(later, eids[None, :], n_e), axis=1)
    next_used = jnp.where(next_used == n_e, -1, next_used).astype(I32)
    slot_e = ((jnp.cumsum(has.astype(I32)) - 1) % 2).astype(I32)
    pass_next = _take(next_used, pass_expert)
    pass_slot = jnp.maximum(_take(slot_e, pass_expert), 0)
    any_spec = pl.BlockSpec(memory_space=pl.ANY)
    grid_spec = pltpu.PrefetchScalarGridSpec(
        num_scalar_prefetch=6,
        grid=(pass_tile.shape[0],),
        in_specs=[pl.BlockSpec((tile, d), lambda i, pt, ph, te, pr, *_: (pt[jnp.where(pr[i] > 0, i, 0)], 0)),
                  any_spec, any_spec, any_spec],
        out_specs=pl.BlockSpec((tile, d), lambda i, pt, ph, *_: (pt[i], ph[i])),
        scratch_shapes=[pltpu.VMEM((2, d, f), F32), pltpu.VMEM((2, d, f), F32), pltpu.VMEM((2, f, d), F32),
                        pltpu.VMEM((d, f), BF16), pltpu.VMEM((d, f), BF16), pltpu.VMEM((f, d), BF16),
                        pltpu.SemaphoreType.DMA((2,))],
    )
    return pl.pallas_call(
        functools.partial(_expert_body, layer=layer),
        grid_spec=grid_spec,
        out_shape=jax.ShapeDtypeStruct((n_rows, 2 * d), F32),
        compiler_params=_cparams("arbitrary"),
        name="moe_experts",
    )(pass_tile, pass_hi, pass_expert, pass_rows, pass_next, pass_slot, hs, w_gate, w_up, w_down)


def _combine_body(pos_ref, x_ref, mod_ref, wts_ref, lng_ref, lnb_ref, ys_ref, *rest,
                  tm, alpha, next_mod):
    if next_mod:
        nmod_ref, x2_ref, h_ref, buf, sem = rest
    else:
        x2_ref, buf, sem = rest
    step = pl.program_id(0) * pl.num_programs(1) + pl.program_id(1)
    n_steps = pl.num_programs(0) * pl.num_programs(1)
    d = x_ref.shape[2]

    def gather(s, slot):
        def issue(blk, carry):
            for u in range(SUBLANES):
                p = pos_ref[s * tm + blk * SUBLANES + u]
                pltpu.make_async_copy(ys_ref.at[p >> 3, pl.ds(p & (SUBLANES - 1), 1)],
                                      buf.at[slot, blk, pl.ds(u, 1)], sem.at[slot]).start(priority=u % 2)
            return carry

        lax.fori_loop(0, tm // SUBLANES, issue, 0, unroll=2)

    @pl.when(step == 0)
    def _():
        gather(0, 0)

    @pl.when(step + 1 < n_steps)
    def _():
        gather(step + 1, (step + 1) % 2)

    slot = step % 2
    pltpu.make_async_copy(ys_ref.at[pl.ds(0, tm // SUBLANES)], buf.at[slot], sem.at[slot]).wait()

    w = wts_ref[...]
    rows = buf[slot].reshape(tm, 2 * d)
    moe = w[:, 0:1] * rows[:, :d] + w[:, 1:2] * rows[:, d:]
    g2 = mod_ref[0, 5:6, :]
    x2 = _layer_norm(alpha * x_ref[0] + g2 * moe, lng_ref[...], lnb_ref[...])
    x2_ref[0] = x2
    if next_mod:
        h_ref[0] = x2 * (1.0 + nmod_ref[0, 1:2, :]) + nmod_ref[0, 0:1, :]


def _combine(ys, pos, wts, x, mods, ln_g, ln_b, alpha, next_mods=None):
    b, l, d = x.shape
    tm = min(l, 1024)
    nt = l // tm
    tok = lambda i, j, *_: (i, j, 0)
    bat = lambda i, j, *_: (i, 0, 0)
    const2 = lambda i, j, *_: (0, 0)
    in_specs = [pl.BlockSpec((1, tm, d), tok),
                pl.BlockSpec((1, 6, d), bat),
                pl.BlockSpec((tm, 2), lambda i, j, *_: (i * nt + j, 0)),
                pl.BlockSpec((1, d), const2),
                pl.BlockSpec((1, d), const2),
                pl.BlockSpec(memory_space=pl.ANY)]
    ys_tiles = ys.reshape(ys.shape[0] // SUBLANES, SUBLANES, ys.shape[1])
    args = [x, mods, wts, ln_g.reshape(1, d), ln_b.reshape(1, d), ys_tiles]
    out_shape = [jax.ShapeDtypeStruct((b, l, d), F32)]
    out_specs = [pl.BlockSpec((1, tm, d), tok)]
    if next_mods is not None:
        in_specs.append(pl.BlockSpec((1, 6, d), bat))
        args.append(next_mods)
        out_shape.append(jax.ShapeDtypeStruct((b, l, d), F32))
        out_specs.append(pl.BlockSpec((1, tm, d), tok))
    grid_spec = pltpu.PrefetchScalarGridSpec(
        num_scalar_prefetch=1,
        grid=(b, nt),
        in_specs=in_specs,
        out_specs=out_specs,
        scratch_shapes=[pltpu.VMEM((2, tm // SUBLANES, SUBLANES, 2 * d), F32),
                        pltpu.SemaphoreType.DMA((2,))],
    )
    return pl.pallas_call(
        functools.partial(_combine_body, tm=tm, alpha=alpha, next_mod=next_mods is not None),
        grid_spec=grid_spec,
        out_shape=out_shape,
        compiler_params=_cparams("arbitrary", "arbitrary"),
        name="moe_combine",
    )(pos, *args)


MOE_TILE = 256


def _moe(h_rows, route, counts, w_gate, w_up, w_down, layer):
    n = h_rows.shape[0]
    tile = min(n, MOE_TILE)
    (pos, pass_tile, pass_hi, pass_expert, pass_rows, last_tile, tiles, n_used_tiles,
     n_row_tiles) = _dispatch_plan(route, counts, tile)
    hs = _dispatch(h_rows, pos, last_tile, tiles, n_used_tiles, n_row_tiles * tile, tile)
    ys = _expert_ffn(hs, pass_tile, pass_hi, pass_expert, pass_rows, w_gate, w_up, w_down, layer, tile)
    return ys, pos, jnp.transpose(route[2:4])


def _pool_group(h_ref, w_ref, sc_ref, o_ref, col_ref, k, n_rows):
    n = n_rows * GRID_W
    c = h_ref.shape[2]
    blk = 4 * GRID_W
    half = k // 2
    pad = half * GRID_W
    ti = lax.broadcasted_iota(I32, (blk, blk), 0)
    si = lax.broadcasted_iota(I32, (blk, blk), 1)
    shift = GRID_W.bit_length() - 1
    same_row = (ti >> shift) == (si >> shift)
    band = jnp.where(same_row & (si - ti >= -half) & (si - ti <= half - 1), 1.0, 0.0).astype(BF16)
    col_ref[0:pad, :] = jnp.zeros((pad, c), F32)
    col_ref[pad + n:pad + n + pad, :] = jnp.zeros((pad, c), F32)
    for b0 in range(0, n, blk):
        hb = h_ref[0, b0:b0 + blk, :]
        head = hb.astype(BF16)
        rest = (hb - head.astype(F32)).astype(BF16)
        col_ref[pad + b0:pad + b0 + blk, :] = (jnp.dot(band, head, preferred_element_type=F32)
                                               + jnp.dot(band, rest, preferred_element_type=F32))
    acc = col_ref[0:n, :]
    for j in range(1, k):
        acc = acc + col_ref[j * GRID_W:j * GRID_W + n, :]
    t = lax.broadcasted_iota(I32, (n, 1), 0)
    wc = t & (GRID_W - 1)
    wr = t >> shift
    cnt_c = jnp.minimum(wc + half - 1, GRID_W - 1) - jnp.maximum(wc - half, 0) + 1
    cnt_r = jnp.minimum(wr + half - 1, n_rows - 1) - jnp.maximum(wr - half, 0) + 1
    mean = acc / (cnt_c * cnt_r).astype(F32)
    pooled = (mean - h_ref[0]).astype(BF16)
    o_ref[0] = jnp.dot(pooled, w_ref[0].astype(BF16), preferred_element_type=F32) * sc_ref[...]


def _pool_body(h_ref, w_ref, sc_ref, o_ref, col_ref, *, n_rows):
    g = pl.program_id(1)
    for gi, k in enumerate(POOL_WINDOWS):
        @pl.when(g == gi)
        def _(k=k):
            _pool_group(h_ref, w_ref, sc_ref, o_ref, col_ref, k, n_rows)


def _pool_mix(h, w_grp, scale):
    b, n, d = h.shape
    n_g, c, _ = w_grp.shape
    n_rows = n // GRID_W
    pad = (max(POOL_WINDOWS) // 2) * GRID_W
    return pl.pallas_call(
        functools.partial(_pool_body, n_rows=n_rows),
        grid=(b, n_g),
        in_specs=[pl.BlockSpec((1, n, c), lambda i, j: (i, 0, j)),
                  pl.BlockSpec((1, c, c), lambda i, j: (j, 0, 0)),
                  pl.BlockSpec((1, c), lambda i, j: (0, j))],
        out_specs=pl.BlockSpec((1, n, c), lambda i, j: (i, 0, j)),
        out_shape=jax.ShapeDtypeStruct((b, n, d), F32),
        scratch_shapes=[pltpu.VMEM((n + 2 * pad, c), F32)],
        compiler_params=_cparams("arbitrary", "arbitrary"),
        name="pool_mix",
    )(h, w_grp, scale.reshape(1, d))


def kernel(x, c, ctx, c_ctx, mod_w, mod_b, ln_g, ln_b, s5_lam_re, s5_lam_im, s5_log_dt, s5_b_re, s5_b_im,
           s5_c_re, s5_c_im, s5_d, s5_w_val, s5_w_gate, pool_w, pool_scale, router_w, router_b,
           moe_w_gate, moe_w_up, moe_w_down):
    b, l, d = x.shape
    depth = mod_w.shape[0]
    assert depth == 2 and b + 1 <= SUBLANES and d % LANES == 0 and GRID_W & (GRID_W - 1) == 0
    alpha = (2 * depth) ** 0.25

    cond = jnp.zeros((SUBLANES, d), F32).at[:b].set(c).at[b].set(c_ctx)
    mods = _modulation(cond, mod_w, mod_b).reshape(depth, SUBLANES, 6, d)

    toep, w1, w2t, a_r, a_i = _s5_weights(s5_lam_re[0], s5_lam_im[0], s5_log_dt[0], s5_b_re[0], s5_b_im[0],
                                          s5_c_re[0], s5_c_im[0], s5_d[0])
    y = _s5_mix(x, ctx, mods[0, :b], mods[0, b:b + 1], toep, w1, w2t, a_r, a_i)
    x1, h_rows, route, counts = _post_mixer(y, x, mods[0, :b], ln_g[0, 0], ln_b[0, 0], router_w, router_b,
                                            alpha, glu_w=(s5_w_val[0], s5_w_gate[0]))
    ys, pos, wts = _moe(h_rows, route, counts, moe_w_gate, moe_w_up, moe_w_down, 0)
    x2, h = _combine(ys, pos, wts, x1, mods[0, :b], ln_g[0, 1], ln_b[0, 1], alpha, next_mods=mods[1, :b])

    m = _pool_mix(h, pool_w[0], pool_scale[0])
    x3, h_rows, route, counts = _post_mixer(m, x2, mods[1, :b], ln_g[1, 0], ln_b[1, 0], router_w, router_b,
                                            alpha)
    ys, pos, wts = _moe(h_rows, route, counts, moe_w_gate, moe_w_up, moe_w_down, 1)
    (out,) = _combine(ys, pos, wts, x3, mods[1, :b], ln_g[1, 1], ln_b[1, 1], alpha)
    return out
```

```python
import functools

import jax
import jax.numpy as jnp
from jax import lax
from jax.experimental import pallas as pl
from jax.experimental.pallas import tpu as pltpu

F32 = jnp.float32
BF16 = jnp.bfloat16
I32 = jnp.int32

GRID_W = 64
S5_H = 16
S5_T = 16
POOL_WINDOWS = (2, 4, 8, 16)
N_EXPERT_GROUPS = 4
LN_EPS = 1e-5
LANES = 128
SUBLANES = 8
VMEM_LIMIT = 52 * 1024 * 1024

EXPERTS_PER_GROUP = 4
PAIRS_PER_GROUP = EXPERTS_PER_GROUP * (EXPERTS_PER_GROUP - 1) // 2
N_PAIR_CLASSES = N_EXPERT_GROUPS * PAIRS_PER_GROUP
ROUTE_ROWS = 8


def _cparams(*sem):
    return pltpu.CompilerParams(dimension_semantics=sem, vmem_limit_bytes=VMEM_LIMIT)


def _dot3(a, b, dims):
    a_head = a.astype(BF16)
    a_rest = (a - a_head.astype(F32)).astype(BF16)
    b_head = b.astype(BF16)
    b_rest = (b - b_head.astype(F32)).astype(BF16)
    dot = functools.partial(lax.dot_general, dimension_numbers=dims, preferred_element_type=F32)
    return dot(a_head, b_head) + dot(a_head, b_rest) + dot(a_rest, b_head)


MATMUL_NN = (((1,), (0,)), ((), ()))
MATMUL_NT = (((1,), (1,)), ((), ()))


def _mod_body(c_ref, w_ref, b_ref, o_ref):
    c = c_ref[...]
    s = c * jax.nn.sigmoid(c)
    o_ref[0] = _dot3(s, w_ref[0], MATMUL_NN) + b_ref[0]


def _modulation(cond, mod_w, mod_b):
    depth, d, n6 = mod_w.shape
    tn = min(n6, 1536)
    return pl.pallas_call(
        _mod_body,
        grid=(depth, n6 // tn),
        in_specs=[pl.BlockSpec((SUBLANES, d), lambda i, j: (0, 0)),
                  pl.BlockSpec((1, d, tn), lambda i, j: (i, 0, j)),
                  pl.BlockSpec((1, 1, tn), lambda i, j: (i, 0, j))],
        out_specs=pl.BlockSpec((1, SUBLANES, tn), lambda i, j: (i, 0, j)),
        out_shape=jax.ShapeDtypeStruct((depth, SUBLANES, n6), F32),
        compiler_params=_cparams("arbitrary", "arbitrary"),
        name="modulation",
    )(cond, mod_w, mod_b.reshape(depth, 1, n6))


def _s5_direction_terms(lam_re, lam_im, log_dt, b_re, b_im):
    lr = lam_re.astype(F32)
    li = lam_im.astype(F32)
    dt = jnp.exp(log_dt.astype(F32))[:, None]
    mag = jnp.exp(lr * dt)
    ar = mag * jnp.cos(li * dt)
    ai = mag * jnp.sin(li * dt)
    den = lr * lr + li * li
    nr = ar - 1.0
    fr = (nr * lr + ai * li) / den
    fi = (ai * lr - nr * li) / den
    br_, bi_ = b_re.astype(F32), b_im.astype(F32)
    bbr = fr[..., None] * br_ - fi[..., None] * bi_
    bbi = fr[..., None] * bi_ + fi[..., None] * br_
    k = jnp.arange(S5_T + 1, dtype=F32)[:, None, None]
    pm = jnp.exp(k * (lr * dt))
    pr = pm * jnp.cos(k * (li * dt))
    pi = pm * jnp.sin(k * (li * dt))
    return pr, pi, bbr, bbi


def _s5_weights(lam_re, lam_im, log_dt, b_re, b_im, c_re, c_im, d_skip):
    t = S5_T
    g, p = lam_re.shape[1:]
    h = b_re.shape[-1]
    terms = [_s5_direction_terms(lam_re[d], lam_im[d], log_dt[d], b_re[d], b_im[d]) for d in (0, 1)]
    pw = jnp.stack([jnp.transpose(terms[d][k], (1, 0, 2)) for d in (0, 1) for k in (0, 1)], axis=1)
    bt = jnp.stack([jnp.transpose(terms[d][k], (0, 2, 1)) for d in (0, 1) for k in (2, 3)], axis=1)
    cc = jnp.stack([c[d].astype(F32) for d in (0, 1) for c in (c_re, c_im)], axis=1)
    gp = 8
    spec4 = lambda rows: pl.BlockSpec((gp, 4, rows, p), lambda i: (i, 0, 0, 0))
    wide = pl.BlockSpec((gp, t * h, t * h), lambda i: (i, 0, 0))
    toep, w1, w2t = pl.pallas_call(
        functools.partial(_s5_prep_body, gp=gp),
        grid=(g // gp,),
        in_specs=[spec4(t + 1), spec4(h), spec4(h), pl.BlockSpec((gp, 1, h), lambda i: (i, 0, 0))],
        out_specs=[wide, wide, wide],
        out_shape=[jax.ShapeDtypeStruct((g, t * h, t * h), BF16) for _ in range(3)],
        compiler_params=_cparams("arbitrary"),
        name="s5_prep",
    )(pw, bt, cc, d_skip.astype(F32).reshape(g, 1, h))
    a_r = jnp.concatenate([pw[:, 0, t], pw[:, 2, t]], axis=-1)[:, None, :]
    a_i = jnp.concatenate([pw[:, 1, t], pw[:, 3, t]], axis=-1)[:, None, :]
    return toep, w1, w2t, a_r, a_i


def _cmul(ar, ai, br, bi):
    return ar * br - ai * bi, ar * bi + ai * br


def _s5_prep_body(pw_ref, bt_ref, cc_ref, d_ref, toep_ref, w1_ref, w2t_ref, *, gp):
    t = pw_ref.shape[2] - 1
    h = bt_ref.shape[2]
    eye =(lax.broadcasted_iota(I32, (h, h), 0) == lax.broadcasted_iota(I32, (h, h), 1))
    for g in range(gp):
        prf, pif, prb, pib = (pw_ref[g, k] for k in range(4))
        btf = (bt_ref[g, 0], bt_ref[g, 1])
        btb = (bt_ref[g, 2], bt_ref[g, 3])
        ccf = (cc_ref[g, 0], cc_ref[g, 1])
        ccb = (cc_ref[g, 2], cc_ref[g, 3])
        caf, cab = [], []
        for j in range(t):
            f_r, f_i = _cmul(*btf, prf[t - 1 - j:t - j], pif[t - 1 - j:t - j])
            b_r, b_i = _cmul(*btb, prb[j:j + 1], pib[j:j + 1])
            w1_ref[g, j * h:(j + 1) * h, :] = jnp.concatenate([f_r, b_r, f_i, b_i], axis=1).astype(BF16)
            mf_r, mf_i = _cmul(*ccf, prf[j + 1:j + 2], pif[j + 1:j + 2])
            mb_r, mb_i = _cmul(*ccb, prb[t - j:t - j + 1], pib[t - j:t - j + 1])
            w2t_ref[g, j * h:(j + 1) * h, :] = jnp.concatenate([mf_r, mb_r, -mf_i, -mb_i],
                                                               axis=1).astype(BF16)
            caf.append(_cmul(*ccf, prf[j:j + 1], pif[j:j + 1]))
            cab.append(_cmul(*ccb, prb[t - 1 - j:t - j], pib[t - 1 - j:t - j]))

        def lag_kernels(btx, ca):
            car = jnp.concatenate([c[0] for c in ca], axis=0)
            cai = jnp.concatenate([c[1] for c in ca], axis=0)
            return _dot3(btx[0], car, MATMUL_NT) - _dot3(btx[1], cai, MATMUL_NT)

        kf = lag_kernels(btf, caf)
        kb = lag_kernels(btb, cab)
        skip = jnp.where(eye, jnp.broadcast_to(d_ref[g], (h, h)), 0.0)
        mid = kb[:, (t - 1) * h:] + kf[:, :h] + skip
        kwide = jnp.concatenate([kb[:, :(t - 1) * h], mid, kf[:, h:], jnp.zeros((h, h), F32)], axis=1)
        for i in range(t):
            off = (t - 1 - i) * h
            toep_ref[g, i * h:(i + 1) * h, :] = kwide[:, off:off + t * h].astype(BF16)


def _granule_transpose(v):
    n = len(v)
    gran = lax.broadcasted_iota(I32, v[0].shape, 1) >> 4
    at = [gran == q for q in range(n)]
    rot = []
    for d in range(n):
        m = v[d]
        for q in range(1, n):
            m = jnp.where(at[q], v[(q + d) % n], m)
        rot.append(pltpu.roll(m, d * S5_H, 1) if d else m)
    out = []
    for q in range(n):
        w = rot[(-q) % n]
        for j in range(1, n):
            w = jnp.where(at[j], rot[(j - q) % n], w)
        out.append(w)
    return out


def _s5_body(x_ref, ctx_ref, mod_ref, cmod_ref, tt_ref, w1_ref, w2t_ref, ar_ref, ai_ref, y_ref,
             u_ref, s_ref, *, n_ctx_chunks, n_lat_chunks, gb):
    n_b = x_ref.shape[0]
    n_g = u_ref.shape[0]
    n_blocks = s_ref.shape[1] // SUBLANES
    half = SUBLANES // 2
    tok_blk = SUBLANES * S5_T
    lat_lo = n_ctx_chunks * n_b
    lat_rows = n_lat_chunks * n_b

    def chunk_rows(ref, b, tok0, shift, scale1):
        vs = [ref[b, pl.ds(tok0 + j, SUBLANES, stride=S5_T), :] * scale1 + shift for j in range(S5_T)]
        lo = _granule_transpose(vs[:SUBLANES])
        hi = _granule_transpose(vs[SUBLANES:])
        return lo, hi

    def put_rows(rows, chunk0, b):
        for q in range(n_g):
            for jh in range(2):
                u_ref[q, jh, pl.ds(chunk0 * n_b + b, SUBLANES, stride=n_b), :] = rows[jh][q]

    def u_rows(g, lo, n):
        return jnp.concatenate([u_ref[g, 0, lo:lo + n, :], u_ref[g, 1, lo:lo + n, :]], axis=1)

    shift = jnp.broadcast_to(cmod_ref[0, 0:1, :], (SUBLANES, LANES))
    scale1 = 1.0 + jnp.broadcast_to(cmod_ref[0, 1:2, :], (SUBLANES, LANES))
    for b in range(n_b):
        for cb in range(n_ctx_chunks // SUBLANES):
            rows = chunk_rows(ctx_ref, b, cb * tok_blk, shift, scale1)
            put_rows(rows, cb * SUBLANES, b)
            put_rows(rows, n_ctx_chunks + n_lat_chunks + cb * SUBLANES, b)

    def fill(cb, carry):
        for b in range(n_b):
            shift = jnp.broadcast_to(mod_ref[b, 0:1, :], (SUBLANES, LANES))
            scale1 = 1.0 + jnp.broadcast_to(mod_ref[b, 1:2, :], (SUBLANES, LANES))
            put_rows(chunk_rows(x_ref, b, cb * tok_blk, shift, scale1), n_ctx_chunks + cb * SUBLANES, b)
        return carry

    lax.fori_loop(0, n_lat_chunks // SUBLANES, fill, 0)

    for g in range(n_g):
        s_ref[g] = jnp.dot(u_rows(g, 0, n_blocks * SUBLANES).astype(BF16), w1_ref[g],
                           preferred_element_type=F32)

    lane = lax.broadcasted_iota(I32, (SUBLANES, LANES), 1)
    row = lax.broadcasted_iota(I32, (SUBLANES, LANES), 0)
    is_fwd = lane < LANES // 2
    is_fwd2 = jnp.concatenate([is_fwd, is_fwd], axis=1)
    first = (lane >> ((LANES // 2).bit_length() - 1)) == (row >> (half.bit_length() - 1))
    zero = jnp.zeros((SUBLANES, LANES), F32)
    for g0 in range(0, n_g, gb):
        ars = [jnp.broadcast_to(ar_ref[g0 + g], (SUBLANES, LANES)) for g in range(gb)]
        ais = [jnp.broadcast_to(ai_ref[g0 + g], (SUBLANES, LANES)) for g in range(gb)]

        def step(k, carry, g0=g0, ars=ars, ais=ais):
            fo = pl.multiple_of(k * SUBLANES, SUBLANES)
            bo = pl.multiple_of((n_blocks - 1 - k) * SUBLANES, SUBLANES)
            new = []
            loaded = [(s_ref[g0 + g, pl.ds(fo, SUBLANES), :], s_ref[g0 + g, pl.ds(bo, SUBLANES), :])
                      for g in range(gb)]
            stores = []
            for g in range(gb):
                xr, xi = carry[2 * g], carry[2 * g + 1]
                vf, vb = loaded[g]
                vr = jnp.where(is_fwd, vf[:, :LANES], vb[:, :LANES])
                vi = jnp.where(is_fwd, vf[:, LANES:], vb[:, LANES:])
                ar, ai = ars[g], ais[g]
                yr = ar * xr - ai * xi + vr
                yi = ar * xi + ai * xr + vi
                yrr = pltpu.roll(yr, half, 0)
                yir = pltpu.roll(yi, half, 0)
                zr = ar * yrr - ai * yir + vr
                zi = ar * yir + ai * yrr + vi
                inc = jnp.concatenate([jnp.where(first, xr, yrr), jnp.where(first, xi, yir)], axis=1)
                stores.append((jnp.where(is_fwd2, inc, vf), jnp.where(is_fwd2, vb, inc)))
                new.append(jnp.where(first, pltpu.roll(zr, half, 0), zr))
                new.append(jnp.where(first, pltpu.roll(zi, half, 0), zi))
            for g in range(gb):
                s_ref[g0 + g, pl.ds(fo, SUBLANES), :] = stores[g][0]
                s_ref[g0 + g, pl.ds(bo, SUBLANES), :] = stores[g][1]
            return tuple(new)

        lax.fori_loop(0, (n_ctx_chunks + n_lat_chunks) // 2, step, tuple(zero for _ in range(2 * gb)))

    for g in range(n_g):
        y = (jnp.dot(u_rows(g, lat_lo, lat_rows).astype(BF16), tt_ref[g], preferred_element_type=F32)
             + lax.dot_general(s_ref[g, lat_lo:lat_lo + lat_rows, :].astype(BF16), w2t_ref[g],
                               (((1,), (1,)), ((), ())), preferred_element_type=F32))
        for jh in range(2):
            u_ref[g, jh, lat_lo:lat_lo + lat_rows, :] = y[:, jh * LANES:(jh + 1) * LANES]

    def emit(cb, carry):
        for b in range(n_b):
            for jh in range(S5_T // SUBLANES):
                w = [u_ref[q, jh, pl.ds((n_ctx_chunks + cb * SUBLANES) * n_b + b, SUBLANES, stride=n_b), :]
                     for q in range(n_g)]
                v = _granule_transpose(w)
                for j in range(SUBLANES):
                    y_ref[b, pl.ds(cb * tok_blk + jh * SUBLANES + j, SUBLANES, stride=S5_T), :] = v[j]
        return carry

    lax.fori_loop(0, n_lat_chunks // SUBLANES, emit, 0)


def _s5_mix(x, ctx, mods, cmods, toep, w1, w2t, a_r, a_i):
    b, l, d = x.shape
    n_ctx = ctx.shape[1]
    n_g = LANES // S5_H
    n_ctx_chunks, n_lat_chunks = n_ctx // S5_T, l // S5_T
    assert b * 2 == SUBLANES and n_ctx_chunks % SUBLANES == 0 and n_lat_chunks % SUBLANES == 0
    rows = (2 * n_ctx_chunks + n_lat_chunks) * b
    w = S5_T * S5_H
    body = functools.partial(_s5_body, n_ctx_chunks=n_ctx_chunks, n_lat_chunks=n_lat_chunks, gb=8)
    lane_tile = lambda i: (0, 0, i)
    wspec = pl.BlockSpec((n_g, w, w), lambda i: (i, 0, 0))
    aspec = pl.BlockSpec((n_g, 1, w // 2), lambda i: (i, 0, 0))
    return pl.pallas_call(
        body,
        grid=(d // LANES,),
        in_specs=[pl.BlockSpec((b, l, LANES), lane_tile),
                  pl.BlockSpec((b, n_ctx, LANES), lane_tile),
                  pl.BlockSpec((b, 6, LANES), lane_tile),
                  pl.BlockSpec((1, 6, LANES), lane_tile),
                  wspec, wspec, wspec, aspec, aspec],
        out_specs=pl.BlockSpec((b, l, LANES), lane_tile, pipeline_mode=pl.Buffered(1)),
        out_shape=jax.ShapeDtypeStruct((b, l, d), F32),
        scratch_shapes=[pltpu.VMEM((n_g, w // LANES, rows, LANES), F32), pltpu.VMEM((n_g, rows, w), F32)],
        compiler_params=_cparams("arbitrary"),
        name="s5_mix",
    )(x, ctx, mods, cmods, toep, w1, w2t, a_r, a_i)


def _layer_norm(r, g, b):
    mu = jnp.mean(r, axis=-1, keepdims=True)
    xc = r - mu
    var = jnp.mean(xc * xc, axis=-1, keepdims=True)
    return xc * lax.rsqrt(var + LN_EPS) * g + b


def _max2_of4(a, b, c, d):
    h1, l1 = jnp.maximum(a, b), jnp.minimum(a, b)
    h2, l2 = jnp.maximum(c, d), jnp.minimum(c, d)
    return jnp.maximum(h1, h2) + jnp.maximum(jnp.minimum(h1, h2), jnp.maximum(l1, l2))


def _argmax_first(vals):
    idx = jnp.zeros(vals[0].shape, I32)
    best = vals[0]
    for j in range(1, len(vals)):
        upd = vals[j] > best
        idx = jnp.where(upd, j, idx)
        best = jnp.where(upd, vals[j], best)
    return idx, best


def _route(logits_t, count_ref, route_ref, tri_ref):
    n_e, tm = logits_t.shape
    per = n_e // N_EXPERT_GROUPS
    mx = jnp.max(logits_t, axis=0, keepdims=True)
    ex = jnp.exp(logits_t - mx)
    sc = ex / jnp.sum(ex, axis=0, keepdims=True)
    rows = [sc[e:e + 1, :] for e in range(n_e)]
    gscore = [_max2_of4(*rows[per * g:per * (g + 1)]) for g in range(N_EXPERT_GROUPS)]
    best, _ = _argmax_first(gscore)
    vals = []
    for j in range(per):
        v = rows[per * (N_EXPERT_GROUPS - 1) + j]
        for g in range(N_EXPERT_GROUPS - 2, -1, -1):
            v = jnp.where(best == g, rows[per * g + j], v)
        vals.append(v)
    i1, m1 = _argmax_first(vals)
    i2, m2 = _argmax_first([jnp.where(i1 == j, -1.0, vals[j]) for j in range(per)])
    den = m1 + m2
    first_lo = i1 < i2
    lo = jnp.minimum(i1, i2)
    hi = jnp.maximum(i1, i2)
    pair = jnp.where(lo == 0, 0, jnp.where(lo == 1, per - 1, 2 * per - 3)) + hi - lo - 1
    cls = best * PAIRS_PER_GROUP + pair
    w_lo = jnp.where(first_lo, m1, m2) / den
    w_hi = jnp.where(first_lo, m2, m1) / den

    n_cls = count_ref.shape[0]
    hit = lax.broadcasted_iota(I32, (n_cls, tm), 0) == cls
    onehot = jnp.where(hit, 1.0, 0.0)
    cum = jnp.dot(onehot.astype(BF16), tri_ref[...], preferred_element_type=F32)
    excl = cum - onehot + count_ref[:, 0:1]
    rank = jnp.sum(jnp.where(hit, excl, 0.0), axis=0, keepdims=True)
    count_ref[...] = count_ref[...] + jnp.sum(onehot, axis=1, keepdims=True)

    zero = jnp.zeros((1, tm), F32)
    route_ref[...] = jnp.concatenate([cls.astype(F32), rank, w_lo, w_hi, zero, zero, zero, zero], axis=0)


def _post_mixer_body(m_ref, x_ref, mod_ref, lng_ref, lnb_ref, rwt_ref, rb_ref, *rest, glu, alpha):
    if glu:
        wv_ref, wg_ref, x1_ref, h_ref, route_ref, cnt_out_ref, cnt_ref, tri_ref, wvb, wgb = rest
    else:
        x1_ref, h_ref, route_ref, cnt_out_ref, cnt_ref, tri_ref = rest

    @pl.when((pl.program_id(0) == 0) & (pl.program_id(1) == 0))
    def _():
        cnt_ref[...] = jnp.zeros_like(cnt_ref)
        src = lax.broadcasted_iota(I32, tri_ref.shape, 0)
        dst = lax.broadcasted_iota(I32, tri_ref.shape, 1)
        tri_ref[...] = jnp.where(src <= dst, 1.0, 0.0).astype(BF16)
        if glu:
            wvb[...] = wv_ref[...].astype(BF16)
            wgb[...] = wg_ref[...].astype(BF16)

    if glu:
        a = jax.nn.gelu(m_ref[0], approximate=True).astype(BF16)
        val = jnp.dot(a, wvb[...], preferred_element_type=F32)
        gate = jnp.dot(a, wgb[...], preferred_element_type=F32)
        m = val * jax.nn.sigmoid(gate)
    else:
        m = m_ref[0]

    g1 = mod_ref[0, 2:3, :]
    sh2 = mod_ref[0, 3:4, :]
    sc2 = mod_ref[0, 4:5, :]
    x1 = _layer_norm(alpha * x_ref[0] + g1 * m, lng_ref[...], lnb_ref[...])
    x1_ref[0] = x1
    h = x1 * (1.0 + sc2) + sh2
    h_ref[...] = h
    logits_t = _dot3(rwt_ref[...], h, MATMUL_NT) + rb_ref[...]
    _route(logits_t, cnt_ref, route_ref, tri_ref)
    cnt_out_ref[...] = cnt_ref[...]


def _post_mixer(m, x, mods, ln_g, ln_b, router_w, router_b, alpha, glu_w=None):
    b, l, d = x.shape
    n_e = router_w.shape[1]
    assert n_e == N_EXPERT_GROUPS * EXPERTS_PER_GROUP
    tm = min(l, 512)
    nt = l // tm
    tok = lambda i, j: (i, j, 0)
    const2 = lambda i, j: (0, 0)
    in_specs = [pl.BlockSpec((1, tm, d), tok),
                pl.BlockSpec((1, tm, d), tok),
                pl.BlockSpec((1, 6, d), lambda i, j: (i, 0, 0)),
                pl.BlockSpec((1, d), const2),
                pl.BlockSpec((1, d), const2),
                pl.BlockSpec((n_e, d), const2),
                pl.BlockSpec((n_e, 1), const2)]
    args = [m, x, mods, ln_g.reshape(1, d), ln_b.reshape(1, d), router_w.T, router_b.reshape(n_e, 1)]
    scratch = [pltpu.VMEM((N_PAIR_CLASSES, LANES), F32), pltpu.VMEM((tm, tm), BF16)]
    if glu_w is not None:
        resident = pl.BlockSpec((d, d), const2, pipeline_mode=pl.Buffered(1))
        in_specs += [resident, resident]
        args += [glu_w[0], glu_w[1]]
        scratch += [pltpu.VMEM((d, d), BF16), pltpu.VMEM((d, d), BF16)]
    out_shape = [jax.ShapeDtypeStruct((b, l, d), F32),
                 jax.ShapeDtypeStruct((b * l, d), F32),
                 jax.ShapeDtypeStruct((ROUTE_ROWS, b * l), F32),
                 jax.ShapeDtypeStruct((N_PAIR_CLASSES, LANES), F32)]
    out_specs = [pl.BlockSpec((1, tm, d), tok),
                 pl.BlockSpec((tm, d), lambda i, j: (i * nt + j, 0)),
                 pl.BlockSpec((ROUTE_ROWS, tm), lambda i, j: (0, i * nt + j)),
                 pl.BlockSpec((N_PAIR_CLASSES, LANES), const2)]
    return pl.pallas_call(
        functools.partial(_post_mixer_body, glu=glu_w is not None, alpha=alpha),
        grid=(b, nt),
        in_specs=in_specs,
        out_specs=out_specs,
        out_shape=out_shape,
        scratch_shapes=scratch,
        compiler_params=_cparams("arbitrary", "arbitrary"),
        name="post_mixer_glu" if glu_w is not None else "post_mixer",
    )(*args)


def _take(table, idx):
    ids = jnp.arange(table.shape[0], dtype=I32)
    return jnp.sum(jnp.where(idx[:, None] == ids[None, :], table[None, :], 0), axis=1)


def _pass_segments():
    seg_cls, seg_hi = [], []
    pairs = [(a, b) for a in range(EXPERTS_PER_GROUP) for b in range(a + 1, EXPERTS_PER_GROUP)]
    for g in range(N_EXPERT_GROUPS):
        for m in range(EXPERTS_PER_GROUP):
            for idx, (a, b) in enumerate(pairs):
                if m in (a, b):
                    seg_cls.append(g * PAIRS_PER_GROUP + idx)
                    seg_hi.append(int(m == b))
    return seg_cls, seg_hi


def _dispatch_plan(route, counts, tile):
    n_cls = counts.shape[0]
    n_tok = route.shape[1]
    cnt = counts[:, 0].astype(I32)
    tiles = (cnt + tile - 1) // tile
    tile_end = jnp.cumsum(tiles)
    tile_off = tile_end - tiles
    cids = jnp.arange(n_cls, dtype=I32)[:, None]
    cls = route[0].astype(I32)
    pos = jnp.sum(jnp.where(cls[None, :] == cids, (tile_off * tile)[:, None], 0), axis=0) + route[1].astype(I32)

    n_row_tiles = n_tok // tile + n_cls
    seg_cls, seg_hi = _pass_segments()
    seg_cls = jnp.asarray(seg_cls, I32)
    seg_hi = jnp.asarray(seg_hi, I32)
    seg_per_expert = EXPERTS_PER_GROUP - 1
    seg_tiles = _take(tiles, seg_cls)
    seg_end = jnp.cumsum(seg_tiles)
    n_used = seg_end[-1]
    p = jnp.arange(2 * n_row_tiles, dtype=I32)
    seg = jnp.minimum(jnp.sum((seg_end[None, :] <= p[:, None]).astype(I32), axis=1), seg_cls.shape[0] - 1)
    within = p - _take(seg_end - seg_tiles, seg)
    spare = p - n_used
    used = p < n_used
    pass_tile = jnp.where(used, _take(tile_off, _take(seg_cls, seg)) + within, tile_end[-1] + spare // 2).astype(I32)
    pass_hi = jnp.where(used, _take(seg_hi, seg), spare % 2).astype(I32)
    pass_expert = jnp.where(used, seg // seg_per_expert, N_EXPERT_GROUPS * EXPERTS_PER_GROUP - 1).astype(I32)
    pass_rows = jnp.where(used, jnp.clip(_take(_take(cnt, seg_cls), seg) - within * tile, 0, tile), 0)
    last_tile = (tile_end - 1).astype(I32)
    return (pos.astype(I32), pass_tile, pass_hi, pass_expert, pass_rows.astype(I32),
            last_tile, tiles.astype(I32), tile_end[-1:].astype(I32), n_row_tiles)


def _zero_tiles(last_ref, tiles_ref, nu_ref, zero_ref, hs_ref, sem, n_e, tile, wait):
    t8 = tile // SUBLANES
    for e in range(n_e):
        @pl.when(tiles_ref[e] > 0)
        def _(e=e):
            cp = pltpu.make_async_copy(zero_ref, hs_ref.at[pl.ds(last_ref[e] * t8, t8)], sem)
            if wait:
                cp.wait()
            else:
                cp.start()

    def body(j, carry):
        cp = pltpu.make_async_copy(zero_ref, hs_ref.at[pl.ds(j * t8, t8)], sem)
        if wait:
            cp.wait()
        else:
            cp.start()
        return carry

    lax.fori_loop(nu_ref[0], hs_ref.shape[0] // t8, body, 0)


def _dispatch_body(pos_ref, last_ref, tiles_ref, nu_ref, h_ref, hs_ref, zero_ref, sem_z, sem, *,
                   tm, tile, n_e):
    i = pl.program_id(0)

    @pl.when(i == 0)
    def _():
        zero_ref[...] = jnp.zeros_like(zero_ref)
        _zero_tiles(last_ref, tiles_ref, nu_ref, zero_ref, hs_ref, sem_z, n_e, tile, False)
        _zero_tiles(last_ref, tiles_ref, nu_ref, zero_ref, hs_ref, sem_z, n_e, tile, True)

    base = i * tm

    def issue(blk, carry):
        for u in range(SUBLANES):
            p = pos_ref[base + blk * SUBLANES + u]
            pltpu.make_async_copy(h_ref.at[blk, pl.ds(u, 1)],
                                  hs_ref.at[p >> 3, pl.ds(p & (SUBLANES - 1), 1)], sem).start(priority=u % 2)
        return carry

    lax.fori_loop(0, tm // SUBLANES, issue, 0, unroll=2)
    pltpu.make_async_copy(h_ref, hs_ref.at[pl.ds(0, tm // SUBLANES)], sem).wait()


def _dispatch(h_rows, pos, last_tile, tiles, n_used, n_rows, tile):
    n, d = h_rows.shape
    n_e = tiles.shape[0]
    tm = min(n, 4096)
    grid_spec = pltpu.PrefetchScalarGridSpec(
        num_scalar_prefetch=4,
        grid=(n // tm,),
        in_specs=[pl.BlockSpec((tm // SUBLANES, SUBLANES, d), lambda i, *_: (i, 0, 0))],
        out_specs=pl.BlockSpec(memory_space=pl.ANY),
        scratch_shapes=[pltpu.VMEM((tile // SUBLANES, SUBLANES, d), F32),
                        pltpu.SemaphoreType.DMA(()), pltpu.SemaphoreType.DMA(())],
    )
    hs = pl.pallas_call(
        functools.partial(_dispatch_body, tm=tm, tile=tile, n_e=n_e),
        grid_spec=grid_spec,
        out_shape=jax.ShapeDtypeStruct((n_rows // SUBLANES, SUBLANES, d), F32),
        compiler_params=_cparams("arbitrary"),
        name="moe_dispatch",
    )(pos, last_tile, tiles, n_used, h_rows.reshape(n // SUBLANES, SUBLANES, d))
    return hs.reshape(n_rows, d)


def _expert_body(pt_ref, ph_ref, te_ref, pr_ref, nx_ref, sl_ref, hs_ref, wg_ref, wu_ref, wd_ref, ys_ref,
                 wgf, wuf, wdf, wgb, wub, wdb, sem, *, layer):
    i = pl.program_id(0)
    rows = pr_ref[i]
    first = jnp.logical_or(i == 0, te_ref[i] != te_ref[jnp.maximum(i - 1, 0)])

    def weight_copies(expert, slot):
        return [pltpu.make_async_copy(w_ref.at[layer, expert], buf.at[slot], sem.at[slot])
                for w_ref, buf in ((wg_ref, wgf), (wu_ref, wuf), (wd_ref, wdf))]

    @pl.when(jnp.logical_and(rows > 0, first))
    def _():
        slot = sl_ref[i]

        @pl.when(i == 0)
        def _():
            for cp in weight_copies(te_ref[i], slot):
                cp.start()

        for cp in weight_copies(te_ref[i], slot):
            cp.wait()

        @pl.when(nx_ref[i] >= 0)
        def _():
            for cp in weight_copies(nx_ref[i], 1 - slot):
                cp.start()

        wgb[...] = wgf[slot].astype(BF16)
        wub[...] = wuf[slot].astype(BF16)
        wdb[...] = wdf[slot].astype(BF16)

    def ffn(n):
        x = hs_ref[0:n, :].astype(BF16)
        gate = jnp.dot(x, wgb[...], preferred_element_type=F32)
        up = jnp.dot(x, wub[...], preferred_element_type=F32)
        a = (gate * jax.nn.sigmoid(gate) * up).astype(BF16)
        ys_ref[0:n, :] = jnp.dot(a, wdb[...], preferred_element_type=F32)

    tile = hs_ref.shape[0]
    sizes = (tile, tile // 2, tile // 4)
    for k, n in enumerate(sizes):
        lo = sizes[k + 1] if k + 1 < len(sizes) else 0

        @pl.when(jnp.logical_and(rows > lo, rows <= n))
        def _(n=n):
            ffn(n)
            if n < tile:
                ys_ref[n:, :] = jnp.zeros((tile - n, ys_ref.shape[1]), F32)

    @pl.when(rows == 0)
    def _():
        ys_ref[...] = jnp.zeros_like(ys_ref)


def _expert_ffn(hs, pass_tile, pass_hi, pass_expert, pass_rows, w_gate, w_up, w_down, layer, tile):
    _, n_e, d, f = w_gate.shape
    n_rows = hs.shape[0]
    eids = jnp.arange(n_e, dtype=I32)
    n_pass_e = jnp.sum(jnp.where((pass_expert[None, :] == eids[:, None]) & (pass_rows[None, :] > 0), 1, 0), axis=1)
    has = n_pass_e > 0
    later = (eids[None, :] > eids[:, None]) & has[None, :]
    next_used = jnp.min(jnp.where(later, eids[None, :], n_e), axis=1)
    next_used = jnp.where(next_used == n_e, -1, next_used).astype(I32)
    slot_e = ((jnp.cumsum(has.astype(I32)) - 1) % 2).astype(I32)
    pass_next = _take(next_used, pass_expert)
    pass_slot = jnp.maximum(_take(slot_e, pass_expert), 0)
    any_spec = pl.BlockSpec(memory_space=pl.ANY)
    grid_spec = pltpu.PrefetchScalarGridSpec(
        num_scalar_prefetch=6,
        grid=(pass_tile.shape[0],),
        in_specs=[pl.BlockSpec((tile, d), lambda i, pt, ph, te, pr, *_: (pt[jnp.where(pr[i] > 0, i, 0)], 0)),
                  any_spec, any_spec, any_spec],
        out_specs=pl.BlockSpec((tile, d), lambda i, pt, ph, *_: (pt[i], ph[i])),
        scratch_shapes=[pltpu.VMEM((2, d, f), F32), pltpu.VMEM((2, d, f), F32), pltpu.VMEM((2, f, d), F32),
                        pltpu.VMEM((d, f), BF16), pltpu.VMEM((d, f), BF16), pltpu.VMEM((f, d), BF16),
                        pltpu.SemaphoreType.DMA((2,))],
    )
    return pl.pallas_call(
        functools.partial(_expert_body, layer=layer),
        grid_spec=grid_spec,
        out_shape=jax.ShapeDtypeStruct((n_rows, 2 * d), F32),
        compiler_params=_cparams("arbitrary"),
        name="moe_experts",
    )(pass_tile, pass_hi, pass_expert, pass_rows, pass_next, pass_slot, hs, w_gate, w_up, w_down)


def _combine_body(pos_ref, x_ref, mod_ref, wts_ref, lng_ref, lnb_ref, ys_ref, *rest,
                  tm, alpha, next_mod):
    if next_mod:
        nmod_ref, x2_ref, h_ref, buf, sem = rest
    else:
        x2_ref, buf, sem = rest
    step = pl.program_id(0) * pl.num_programs(1) + pl.program_id(1)
    n_steps = pl.num_programs(0) * pl.num_programs(1)
    d = x_ref.shape[2]

    def gather(s, slot):
        def issue(blk, carry):
            for u in range(SUBLANES):
                p = pos_ref[s * tm + blk * SUBLANES + u]
                pltpu.make_async_copy(ys_ref.at[p >> 3, pl.ds(p & (SUBLANES - 1), 1)],
                                      buf.at[slot, blk, pl.ds(u, 1)], sem.at[slot]).start(priority=u % 2)
            return carry

        lax.fori_loop(0, tm // SUBLANES, issue, 0, unroll=2)

    @pl.when(step == 0)
    def _():
        gather(0, 0)

    @pl.when(step + 1 < n_steps)
    def _():
        gather(step + 1, (step + 1) % 2)

    slot = step % 2
    pltpu.make_async_copy(ys_ref.at[pl.ds(0, tm // SUBLANES)], buf.at[slot], sem.at[slot]).wait()

    w = wts_ref[...]
    rows = buf[slot].reshape(tm, 2 * d)
    moe = w[:, 0:1] * rows[:, :d] + w[:, 1:2] * rows[:, d:]
    g2 = mod_ref[0, 5:6, :]
    x2 = _layer_norm(alpha * x_ref[0] + g2 * moe, lng_ref[...], lnb_ref[...])
    x2_ref[0] = x2
    if next_mod:
        h_ref[0] = x2 * (1.0 + nmod_ref[0, 1:2, :]) + nmod_ref[0, 0:1, :]


def _combine(ys, pos, wts, x, mods, ln_g, ln_b, alpha, next_mods=None):
    b, l, d = x.shape
    tm = min(l, 512)
    nt = l // tm
    tok = lambda i, j, *_: (i, j, 0)
    bat = lambda i, j, *_: (i, 0, 0)
    const2 = lambda i, j, *_: (0, 0)
    in_specs = [pl.BlockSpec((1, tm, d), tok),
                pl.BlockSpec((1, 6, d), bat),
                pl.BlockSpec((tm, 2), lambda i, j, *_: (i * nt + j, 0)),
                pl.BlockSpec((1, d), const2),
                pl.BlockSpec((1, d), const2),
                pl.BlockSpec(memory_space=pl.ANY)]
    ys_tiles = ys.reshape(ys.shape[0] // SUBLANES, SUBLANES, ys.shape[1])
    args = [x, mods, wts, ln_g.reshape(1, d), ln_b.reshape(1, d), ys_tiles]
    out_shape = [jax.ShapeDtypeStruct((b, l, d), F32)]
    out_specs = [pl.BlockSpec((1, tm, d), tok)]
    if next_mods is not None:
        in_specs.append(pl.BlockSpec((1, 6, d), bat))
        args.append(next_mods)
        out_shape.append(jax.ShapeDtypeStruct((b, l, d), F32))
        out_specs.append(pl.BlockSpec((1, tm, d), tok))
    grid_spec = pltpu.PrefetchScalarGridSpec(
        num_scalar_prefetch=1,
        grid=(b, nt),
        in_specs=in_specs,
        out_specs=out_specs,
        scratch_shapes=[pltpu.VMEM((2, tm // SUBLANES, SUBLANES, 2 * d), F32),
                        pltpu.SemaphoreType.DMA((2,))],
    )
    return pl.pallas_call(
        functools.partial(_combine_body, tm=tm, alpha=alpha, next_mod=next_mods is not None),
        grid_spec=grid_spec,
        out_shape=out_shape,
        compiler_params=_cparams("arbitrary", "arbitrary"),
        name="moe_combine",
    )(pos, *args)


MOE_TILE = 256


def _moe(h_rows, route, counts, w_gate, w_up, w_down, layer):
    n = h_rows.shape[0]
    tile = min(n, MOE_TILE)
    (pos, pass_tile, pass_hi, pass_expert, pass_rows, last_tile, tiles, n_used_tiles,
     n_row_tiles) = _dispatch_plan(route, counts, tile)
    hs = _dispatch(h_rows, pos, last_tile, tiles, n_used_tiles, n_row_tiles * tile, tile)
    ys = _expert_ffn(hs, pass_tile, pass_hi, pass_expert, pass_rows, w_gate, w_up, w_down, layer, tile)
    return ys, pos, jnp.transpose(route[2:4])


def _pool_group(h_ref, w_ref, sc_ref, o_ref, col_ref, k, n_rows):
    n = n_rows * GRID_W
    c = h_ref.shape[2]
    blk = 4 * GRID_W
    half = k // 2
    pad = half * GRID_W
    ti = lax.broadcasted_iota(I32, (blk, blk), 0)
    si = lax.broadcasted_iota(I32, (blk, blk), 1)
    shift = GRID_W.bit_length() - 1
    same_row = (ti >> shift) == (si >> shift)
    band = jnp.where(same_row & (si - ti >= -half) & (si - ti <= half - 1), 1.0, 0.0).astype(BF16)
    col_ref[0:pad, :] = jnp.zeros((pad, c), F32)
    col_ref[pad + n:pad + n + pad, :] = jnp.zeros((pad, c), F32)
    for b0 in range(0, n, blk):
        hb = h_ref[0, b0:b0 + blk, :]
        head = hb.astype(BF16)
        rest = (hb - head.astype(F32)).astype(BF16)
        col_ref[pad + b0:pad + b0 + blk, :] = (jnp.dot(band, head, preferred_element_type=F32)
                                               + jnp.dot(band, rest, preferred_element_type=F32))
    acc = col_ref[0:n, :]
    for j in range(1, k):
        acc = acc + col_ref[j * GRID_W:j * GRID_W + n, :]
    t = lax.broadcasted_iota(I32, (n, 1), 0)
    wc = t & (GRID_W - 1)
    wr = t >> shift
    cnt_c = jnp.minimum(wc + half - 1, GRID_W - 1) - jnp.maximum(wc - half, 0) + 1
    cnt_r = jnp.minimum(wr + half - 1, n_rows - 1) - jnp.maximum(wr - half, 0) + 1
    mean = acc / (cnt_c * cnt_r).astype(F32)
    pooled = (mean - h_ref[0]).astype(BF16)
    o_ref[0] = jnp.dot(pooled, w_ref[0].astype(BF16), preferred_element_type=F32) * sc_ref[...]


def _pool_body(h_ref, w_ref, sc_ref, o_ref, col_ref, *, n_rows):
    g = pl.program_id(1)
    for gi, k in enumerate(POOL_WINDOWS):
        @pl.when(g == gi)
        def _(k=k):
            _pool_group(h_ref, w_ref, sc_ref, o_ref, col_ref, k, n_rows)


def _pool_mix(h, w_grp, scale):
    b, n, d = h.shape
    n_g, c, _ = w_grp.shape
    n_rows = n // GRID_W
    pad = (max(POOL_WINDOWS) // 2) * GRID_W
    return pl.pallas_call(
        functools.partial(_pool_body, n_rows=n_rows),
        grid=(b, n_g),
        in_specs=[pl.BlockSpec((1, n, c), lambda i, j: (i, 0, j)),
                  pl.BlockSpec((1, c, c), lambda i, j: (j, 0, 0)),
                  pl.BlockSpec((1, c), lambda i, j: (0, j))],
        out_specs=pl.BlockSpec((1, n, c), lambda i, j: (i, 0, j)),
        out_shape=jax.ShapeDtypeStruct((b, n, d), F32),
        scratch_shapes=[pltpu.VMEM((n + 2 * pad, c), F32)],
        compiler_params=_cparams("arbitrary", "arbitrary"),
        name="pool_mix",
    )(h, w_grp, scale.reshape(1, d))


def kernel(x, c, ctx, c_ctx, mod_w, mod_b, ln_g, ln_b, s5_lam_re, s5_lam_im, s5_log_dt, s5_b_re, s5_b_im,
           s5_c_re, s5_c_im, s5_d, s5_w_val, s5_w_gate, pool_w, pool_scale, router_w, router_b,
           moe_w_gate, moe_w_up, moe_w_down):
    b, l, d = x.shape
    depth = mod_w.shape[0]
    assert depth == 2 and b + 1 <= SUBLANES and d % LANES == 0 and GRID_W & (GRID_W - 1) == 0
    alpha = (2 * depth) ** 0.25

    cond = jnp.zeros((SUBLANES, d), F32).at[:b].set(c).at[b].set(c_ctx)
    mods = _modulation(cond, mod_w, mod_b).reshape(depth, SUBLANES, 6, d)

    toep, w1, w2t, a_r, a_i = _s5_weights(s5_lam_re[0], s5_lam_im[0], s5_log_dt[0], s5_b_re[0], s5_b_im[0],
                                          s5_c_re[0], s5_c_im[0], s5_d[0])
    y = _s5_mix(x, ctx, mods[0, :b], mods[0, b:b + 1], toep, w1, w2t, a_r, a_i)
    x1, h_rows, route, counts = _post_mixer(y, x, mods[0, :b], ln_g[0, 0], ln_b[0, 0], router_w, router_b,
                                            alpha, glu_w=(s5_w_val[0], s5_w_gate[0]))
    ys, pos, wts = _moe(h_rows, route, counts, moe_w_gate, moe_w_up, moe_w_down, 0)
    x2, h = _combine(ys, pos, wts, x1, mods[0, :b], ln_g[0, 1], ln_b[0, 1], alpha, next_mods=mods[1, :b])

    m = _pool_mix(h, pool_w[0], pool_scale[0])
    x3, h_rows, route, counts = _post_mixer(m, x2, mods[1, :b], ln_g[1, 0], ln_b[1, 0], router_w, router_b,
                                            alpha)
    ys, pos, wts = _moe(h_rows, route, counts, moe_w_gate, moe_w_up, moe_w_down, 1)
    (out,) = _combine(ys, pos, wts, x3, mods[1, :b], ln_g[1, 1], ln_b[1, 1], alpha)
    return out
```

```python
import functools

import jax
import jax.numpy as jnp
from jax import lax
from jax.experimental import pallas as pl
from jax.experimental.pallas import tpu as pltpu

F32 = jnp.float32
BF16 = jnp.bfloat16
I32 = jnp.int32

GRID_W = 64
S5_H = 16
S5_T = 16
POOL_WINDOWS = (2, 4, 8, 16)
N_EXPERT_GROUPS = 4
LN_EPS = 1e-5
LANES = 128
SUBLANES = 8
VMEM_LIMIT = 52 * 1024 * 1024

EXPERTS_PER_GROUP = 4
PAIRS_PER_GROUP = EXPERTS_PER_GROUP * (EXPERTS_PER_GROUP - 1) // 2
N_PAIR_CLASSES = N_EXPERT_GROUPS * PAIRS_PER_GROUP
ROUTE_ROWS = 8


def _cparams(*sem):
    return pltpu.CompilerParams(dimension_semantics=sem, vmem_limit_bytes=VMEM_LIMIT)


def _dot3(a, b, dims):
    a_head = a.astype(BF16)
    a_rest = (a - a_head.astype(F32)).astype(BF16)
    b_head = b.astype(BF16)
    b_rest = (b - b_head.astype(F32)).astype(BF16)
    dot = functools.partial(lax.dot_general, dimension_numbers=dims, preferred_element_type=F32)
    return dot(a_head, b_head) + dot(a_head, b_rest) + dot(a_rest, b_head)


MATMUL_NN = (((1,), (0,)), ((), ()))
MATMUL_NT = (((1,), (1,)), ((), ()))


def _mod_body(c_ref, w_ref, b_ref, o_ref):
    c = c_ref[...]
    s = c * jax.nn.sigmoid(c)
    o_ref[0] = _dot3(s, w_ref[0], MATMUL_NN) + b_ref[0]


def _modulation(cond, mod_w, mod_b):
    depth, d, n6 = mod_w.shape
    tn = min(n6, 1536)
    return pl.pallas_call(
        _mod_body,
        grid=(depth, n6 // tn),
        in_specs=[pl.BlockSpec((SUBLANES, d), lambda i, j: (0, 0)),
                  pl.BlockSpec((1, d, tn), lambda i, j: (i, 0, j)),
                  pl.BlockSpec((1, 1, tn), lambda i, j: (i, 0, j))],
        out_specs=pl.BlockSpec((1, SUBLANES, tn), lambda i, j: (i, 0, j)),
        out_shape=jax.ShapeDtypeStruct((depth, SUBLANES, n6), F32),
        compiler_params=_cparams("arbitrary", "arbitrary"),
        name="modulation",
    )(cond, mod_w, mod_b.reshape(depth, 1, n6))


def _s5_direction_terms(lam_re, lam_im, log_dt, b_re, b_im):
    lr = lam_re.astype(F32)
    li = lam_im.astype(F32)
    dt = jnp.exp(log_dt.astype(F32))[:, None]
    mag = jnp.exp(lr * dt)
    ar = mag * jnp.cos(li * dt)
    ai = mag * jnp.sin(li * dt)
    den = lr * lr + li * li
    nr = ar - 1.0
    fr = (nr * lr + ai * li) / den
    fi = (ai * lr - nr * li) / den
    br_, bi_ = b_re.astype(F32), b_im.astype(F32)
    bbr = fr[..., None] * br_ - fi[..., None] * bi_
    bbi = fr[..., None] * bi_ + fi[..., None] * br_
    k = jnp.arange(S5_T + 1, dtype=F32)[:, None, None]
    pm = jnp.exp(k * (lr * dt))
    pr = pm * jnp.cos(k * (li * dt))
    pi = pm * jnp.sin(k * (li * dt))
    return pr, pi, bbr, bbi


def _s5_weights(lam_re, lam_im, log_dt, b_re, b_im, c_re, c_im, d_skip):
    t = S5_T
    g, p = lam_re.shape[1:]
    h = b_re.shape[-1]
    terms = [_s5_direction_terms(lam_re[d], lam_im[d], log_dt[d], b_re[d], b_im[d]) for d in (0, 1)]
    pw = jnp.stack([jnp.transpose(terms[d][k], (1, 0, 2)) for d in (0, 1) for k in (0, 1)], axis=1)
    bt = jnp.stack([jnp.transpose(terms[d][k], (0, 2, 1)) for d in (0, 1) for k in (2, 3)], axis=1)
    cc = jnp.stack([c[d].astype(F32) for d in (0, 1) for c in (c_re, c_im)], axis=1)
    gp = 8
    spec4 = lambda rows: pl.BlockSpec((gp, 4, rows, p), lambda i: (i, 0, 0, 0))
    wide = pl.BlockSpec((gp, t * h, t * h), lambda i: (i, 0, 0))
    toep, w1, w2t = pl.pallas_call(
        functools.partial(_s5_prep_body, gp=gp),
        grid=(g // gp,),
        in_specs=[spec4(t + 1), spec4(h), spec4(h), pl.BlockSpec((gp, 1, h), lambda i: (i, 0, 0))],
        out_specs=[wide, wide, wide],
        out_shape=[jax.ShapeDtypeStruct((g, t * h, t * h), BF16) for _ in range(3)],
        compiler_params=_cparams("arbitrary"),
        name="s5_prep",
    )(pw, bt, cc, d_skip.astype(F32).reshape(g, 1, h))
    a_r = jnp.concatenate([pw[:, 0, t], pw[:, 2, t]], axis=-1)[:, None, :]
    a_i = jnp.concatenate([pw[:, 1, t], pw[:, 3, t]], axis=-1)[:, None, :]
    return toep, w1, w2t, a_r, a_i


def _cmul(ar, ai, br, bi):
    return ar * br - ai * bi, ar * bi + ai * br


def _s5_prep_body(pw_ref, bt_ref, cc_ref, d_ref, toep_ref, w1_ref, w2t_ref, *, gp):
    t = pw_ref.shape[2] - 1
    h = bt_ref.shape[2]
    eye =(lax.broadcasted_iota(I32, (h, h), 0) == lax.broadcasted_iota(I32, (h, h), 1))
    for g in range(gp):
        prf, pif, prb, pib = (pw_ref[g, k] for k in range(4))
        btf = (bt_ref[g, 0], bt_ref[g, 1])
        btb = (bt_ref[g, 2], bt_ref[g, 3])
        ccf = (cc_ref[g, 0], cc_ref[g, 1])
        ccb = (cc_ref[g, 2], cc_ref[g, 3])
        caf, cab = [], []
        for j in range(t):
            f_r, f_i = _cmul(*btf, prf[t - 1 - j:t - j], pif[t - 1 - j:t - j])
            b_r, b_i = _cmul(*btb, prb[j:j + 1], pib[j:j + 1])
            w1_ref[g, j * h:(j + 1) * h, :] = jnp.concatenate([f_r, b_r, f_i, b_i], axis=1).astype(BF16)
            mf_r, mf_i = _cmul(*ccf, prf[j + 1:j + 2], pif[j + 1:j + 2])
            mb_r, mb_i = _cmul(*ccb, prb[t - j:t - j + 1], pib[t - j:t - j + 1])
            w2t_ref[g, j * h:(j + 1) * h, :] = jnp.concatenate([mf_r, mb_r, -mf_i, -mb_i],
                                                               axis=1).astype(BF16)
            caf.append(_cmul(*ccf, prf[j:j + 1], pif[j:j + 1]))
            cab.append(_cmul(*ccb, prb[t - 1 - j:t - j], pib[t - 1 - j:t - j]))

        def lag_kernels(btx, ca):
            car = jnp.concatenate([c[0] for c in ca], axis=0)
            cai = jnp.concatenate([c[1] for c in ca], axis=0)
            return _dot3(btx[0], car, MATMUL_NT) - _dot3(btx[1], cai, MATMUL_NT)

        kf = lag_kernels(btf, caf)
        kb = lag_kernels(btb, cab)
        skip = jnp.where(eye, jnp.broadcast_to(d_ref[g], (h, h)), 0.0)
        mid = kb[:, (t - 1) * h:] + kf[:, :h] + skip
        kwide = jnp.concatenate([kb[:, :(t - 1) * h], mid, kf[:, h:], jnp.zeros((h, h), F32)], axis=1)
        for i in range(t):
            off = (t - 1 - i) * h
            toep_ref[g, i * h:(i + 1) * h, :] = kwide[:, off:off + t * h].astype(BF16)


def _granule_transpose(v):
    n = len(v)
    gran = lax.broadcasted_iota(I32, v[0].shape, 1) >> 4
    at = [gran == q for q in range(n)]
    rot = []
    for d in range(n):
        m = v[d]
        for q in range(1, n):
            m = jnp.where(at[q], v[(q + d) % n], m)
        rot.append(pltpu.roll(m, d * S5_H, 1) if d else m)
    out = []
    for q in range(n):
        w = rot[(-q) % n]
        for j in range(1, n):
            w = jnp.where(at[j], rot[(j - q) % n], w)
        out.append(w)
    return out


def _s5_body(x_ref, ctx_ref, mod_ref, cmod_ref, tt_ref, w1_ref, w2t_ref, ar_ref, ai_ref, y_ref,
             u_ref, s_ref, *, n_ctx_chunks, n_lat_chunks, gb):
    n_b = x_ref.shape[0]
    n_g = u_ref.shape[0]
    n_blocks = s_ref.shape[1] // SUBLANES
    half = SUBLANES // 2
    tok_blk = SUBLANES * S5_T
    lat_lo = n_ctx_chunks * n_b
    lat_rows = n_lat_chunks * n_b

    def chunk_rows(ref, b, tok0, shift, scale1):
        vs = [ref[b, pl.ds(tok0 + j, SUBLANES, stride=S5_T), :] * scale1 + shift for j in range(S5_T)]
        lo = _granule_transpose(vs[:SUBLANES])
        hi = _granule_transpose(vs[SUBLANES:])
        return lo, hi

    def put_rows(rows, chunk0, b):
        for q in range(n_g):
            for jh in range(2):
                u_ref[q, jh, pl.ds(chunk0 * n_b + b, SUBLANES, stride=n_b), :] = rows[jh][q]

    def u_rows(g, lo, n):
        return jnp.concatenate([u_ref[g, 0, lo:lo + n, :], u_ref[g, 1, lo:lo + n, :]], axis=1)

    shift = jnp.broadcast_to(cmod_ref[0, 0:1, :], (SUBLANES, LANES))
    scale1 = 1.0 + jnp.broadcast_to(cmod_ref[0, 1:2, :], (SUBLANES, LANES))
    for b in range(n_b):
        for cb in range(n_ctx_chunks // SUBLANES):
            rows = chunk_rows(ctx_ref, b, cb * tok_blk, shift, scale1)
            put_rows(rows, cb * SUBLANES, b)
            put_rows(rows, n_ctx_chunks + n_lat_chunks + cb * SUBLANES, b)

    def fill(cb, carry):
        for b in range(n_b):
            shift = jnp.broadcast_to(mod_ref[b, 0:1, :], (SUBLANES, LANES))
            scale1 = 1.0 + jnp.broadcast_to(mod_ref[b, 1:2, :], (SUBLANES, LANES))
            put_rows(chunk_rows(x_ref, b, cb * tok_blk, shift, scale1), n_ctx_chunks + cb * SUBLANES, b)
        return carry

    lax.fori_loop(0, n_lat_chunks // SUBLANES, fill, 0)

    for g in range(n_g):
        s_ref[g] = jnp.dot(u_rows(g, 0, n_blocks * SUBLANES).astype(BF16), w1_ref[g],
                           preferred_element_type=F32)

    lane = lax.broadcasted_iota(I32, (SUBLANES, LANES), 1)
    row = lax.broadcasted_iota(I32, (SUBLANES, LANES), 0)
    is_fwd = lane < LANES // 2
    is_fwd2 = jnp.concatenate([is_fwd, is_fwd], axis=1)
    first = (lane >> ((LANES // 2).bit_length() - 1)) == (row >> (half.bit_length() - 1))
    zero = jnp.zeros((SUBLANES, LANES), F32)
    for g0 in range(0, n_g, gb):
        ars = [jnp.broadcast_to(ar_ref[g0 + g], (SUBLANES, LANES)) for g in range(gb)]
        ais = [jnp.broadcast_to(ai_ref[g0 + g], (SUBLANES, LANES)) for g in range(gb)]

        def step(k, carry, g0=g0, ars=ars, ais=ais):
            fo = pl.multiple_of(k * SUBLANES, SUBLANES)
            bo = pl.multiple_of((n_blocks - 1 - k) * SUBLANES, SUBLANES)
            new = []
            loaded = [(s_ref[g0 + g, pl.ds(fo, SUBLANES), :], s_ref[g0 + g, pl.ds(bo, SUBLANES), :])
                      for g in range(gb)]
            stores = []
            for g in range(gb):
                xr, xi = carry[2 * g], carry[2 * g + 1]
                vf, vb = loaded[g]
                vr = jnp.where(is_fwd, vf[:, :LANES], vb[:, :LANES])
                vi = jnp.where(is_fwd, vf[:, LANES:], vb[:, LANES:])
                ar, ai = ars[g], ais[g]
                yr = ar * xr - ai * xi + vr
                yi = ar * xi + ai * xr + vi
                yrr = pltpu.roll(yr, half, 0)
                yir = pltpu.roll(yi, half, 0)
                zr = ar * yrr - ai * yir + vr
                zi = ar * yir + ai * yrr + vi
                inc = jnp.concatenate([jnp.where(first, xr, yrr), jnp.where(first, xi, yir)], axis=1)
                stores.append((jnp.where(is_fwd2, inc, vf), jnp.where(is_fwd2, vb, inc)))
                new.append(jnp.where(first, pltpu.roll(zr, half, 0), zr))
                new.append(jnp.where(first, pltpu.roll(zi, half, 0), zi))
            for g in range(gb):
                s_ref[g0 + g, pl.ds(fo, SUBLANES), :] = stores[g][0]
                s_ref[g0 + g, pl.ds(bo, SUBLANES), :] = stores[g][1]
            return tuple(new)

        lax.fori_loop(0, (n_ctx_chunks + n_lat_chunks) // 2, step, tuple(zero for _ in range(2 * gb)))

    for g in range(n_g):
        y = (jnp.dot(u_rows(g, lat_lo, lat_rows).astype(BF16), tt_ref[g], preferred_element_type=F32)
             + lax.dot_general(s_ref[g, lat_lo:lat_lo + lat_rows, :].astype(BF16), w2t_ref[g],
                               (((1,), (1,)), ((), ())), preferred_element_type=F32))
        for jh in range(2):
            u_ref[g, jh, lat_lo:lat_lo + lat_rows, :] = y[:, jh * LANES:(jh + 1) * LANES]

    def emit(cb, carry):
        for b in range(n_b):
            for jh in range(S5_T // SUBLANES):
                w = [u_ref[q, jh, pl.ds((n_ctx_chunks + cb * SUBLANES) * n_b + b, SUBLANES, stride=n_b), :]
                     for q in range(n_g)]
                v = _granule_transpose(w)
                for j in range(SUBLANES):
                    y_ref[b, pl.ds(cb * tok_blk + jh * SUBLANES + j, SUBLANES, stride=S5_T), :] = v[j]
        return carry

    lax.fori_loop(0, n_lat_chunks // SUBLANES, emit, 0)


def _s5_mix(x, ctx, mods, cmods, toep, w1, w2t, a_r, a_i):
    b, l, d = x.shape
    n_ctx = ctx.shape[1]
    n_g = LANES // S5_H
    n_ctx_chunks, n_lat_chunks = n_ctx // S5_T, l // S5_T
    assert b * 2 == SUBLANES and n_ctx_chunks % SUBLANES == 0 and n_lat_chunks % SUBLANES == 0
    rows = (2 * n_ctx_chunks + n_lat_chunks) * b
    w = S5_T * S5_H
    body = functools.partial(_s5_body, n_ctx_chunks=n_ctx_chunks, n_lat_chunks=n_lat_chunks, gb=8)
    lane_tile = lambda i: (0, 0, i)
    wspec = pl.BlockSpec((n_g, w, w), lambda i: (i, 0, 0))
    aspec = pl.BlockSpec((n_g, 1, w // 2), lambda i: (i, 0, 0))
    return pl.pallas_call(
        body,
        grid=(d // LANES,),
        in_specs=[pl.BlockSpec((b, l, LANES), lane_tile),
                  pl.BlockSpec((b, n_ctx, LANES), lane_tile),
                  pl.BlockSpec((b, 6, LANES), lane_tile),
                  pl.BlockSpec((1, 6, LANES), lane_tile),
                  wspec, wspec, wspec, aspec, aspec],
        out_specs=pl.BlockSpec((b, l, LANES), lane_tile, pipeline_mode=pl.Buffered(1)),
        out_shape=jax.ShapeDtypeStruct((b, l, d), F32),
        scratch_shapes=[pltpu.VMEM((n_g, w // LANES, rows, LANES), F32), pltpu.VMEM((n_g, rows, w), F32)],
        compiler_params=_cparams("arbitrary"),
        name="s5_mix",
    )(x, ctx, mods, cmods, toep, w1, w2t, a_r, a_i)


def _layer_norm(r, g, b):
    mu = jnp.mean(r, axis=-1, keepdims=True)
    xc = r - mu
    var = jnp.mean(xc * xc, axis=-1, keepdims=True)
    return xc * lax.rsqrt(var + LN_EPS) * g + b


def _max2_of4(a, b, c, d):
    h1, l1 = jnp.maximum(a, b), jnp.minimum(a, b)
    h2, l2 = jnp.maximum(c, d), jnp.minimum(c, d)
    return jnp.maximum(h1, h2) + jnp.maximum(jnp.minimum(h1, h2), jnp.maximum(l1, l2))


def _argmax_first(vals):
    idx = jnp.zeros(vals[0].shape, I32)
    best = vals[0]
    for j in range(1, len(vals)):
        upd = vals[j] > best
        idx = jnp.where(upd, j, idx)
        best = jnp.where(upd, vals[j], best)
    return idx, best


def _route(logits_t, count_ref, route_ref, tri_ref):
    n_e, tm = logits_t.shape
    per = n_e // N_EXPERT_GROUPS
    mx = jnp.max(logits_t, axis=0, keepdims=True)
    ex = jnp.exp(logits_t - mx)
    sc = ex / jnp.sum(ex, axis=0, keepdims=True)
    rows = [sc[e:e + 1, :] for e in range(n_e)]
    gscore = [_max2_of4(*rows[per * g:per * (g + 1)]) for g in range(N_EXPERT_GROUPS)]
    best, _ = _argmax_first(gscore)
    vals = []
    for j in range(per):
        v = rows[per * (N_EXPERT_GROUPS - 1) + j]
        for g in range(N_EXPERT_GROUPS - 2, -1, -1):
            v = jnp.where(best == g, rows[per * g + j], v)
        vals.append(v)
    i1, m1 = _argmax_first(vals)
    i2, m2 = _argmax_first([jnp.where(i1 == j, -1.0, vals[j]) for j in range(per)])
    den = m1 + m2
    first_lo = i1 < i2
    lo = jnp.minimum(i1, i2)
    hi = jnp.maximum(i1, i2)
    pair = jnp.where(lo == 0, 0, jnp.where(lo == 1, per - 1, 2 * per - 3)) + hi - lo - 1
    cls = best * PAIRS_PER_GROUP + pair
    w_lo = jnp.where(first_lo, m1, m2) / den
    w_hi = jnp.where(first_lo, m2, m1) / den

    n_cls = count_ref.shape[0]
    hit = lax.broadcasted_iota(I32, (n_cls, tm), 0) == cls
    onehot = jnp.where(hit, 1.0, 0.0)
    cum = jnp.dot(onehot.astype(BF16), tri_ref[...], preferred_element_type=F32)
    excl = cum - onehot + count_ref[:, 0:1]
    rank = jnp.sum(jnp.where(hit, excl, 0.0), axis=0, keepdims=True)
    count_ref[...] = count_ref[...] + jnp.sum(onehot, axis=1, keepdims=True)

    zero = jnp.zeros((1, tm), F32)
    route_ref[...] = jnp.concatenate([cls.astype(F32), rank, w_lo, w_hi, zero, zero, zero, zero], axis=0)


def _post_mixer_body(m_ref, x_ref, mod_ref, lng_ref, lnb_ref, rwt_ref, rb_ref, *rest, glu, alpha):
    if glu:
        wv_ref, wg_ref, x1_ref, h_ref, route_ref, cnt_out_ref, cnt_ref, tri_ref, wvb, wgb = rest
    else:
        x1_ref, h_ref, route_ref, cnt_out_ref, cnt_ref, tri_ref = rest

    @pl.when((pl.program_id(0) == 0) & (pl.program_id(1) == 0))
    def _():
        cnt_ref[...] = jnp.zeros_like(cnt_ref)
        src = lax.broadcasted_iota(I32, tri_ref.shape, 0)
        dst = lax.broadcasted_iota(I32, tri_ref.shape, 1)
        tri_ref[...] = jnp.where(src <= dst, 1.0, 0.0).astype(BF16)
        if glu:
            wvb[...] = wv_ref[...].astype(BF16)
            wgb[...] = wg_ref[...].astype(BF16)

    if glu:
        a = jax.nn.gelu(m_ref[0], approximate=True).astype(BF16)
        val = jnp.dot(a, wvb[...], preferred_element_type=F32)
        gate = jnp.dot(a, wgb[...], preferred_element_type=F32)
        m = val * jax.nn.sigmoid(gate)
    else:
        m = m_ref[0]

    g1 = mod_ref[0, 2:3, :]
    sh2 = mod_ref[0, 3:4, :]
    sc2 = mod_ref[0, 4:5, :]
    x1 = _layer_norm(alpha * x_ref[0] + g1 * m, lng_ref[...], lnb_ref[...])
    x1_ref[0] = x1
    h = x1 * (1.0 + sc2) + sh2
    h_ref[...] = h
    logits_t = _dot3(rwt_ref[...], h, MATMUL_NT) + rb_ref[...]
    _route(logits_t, cnt_ref, route_ref, tri_ref)
    cnt_out_ref[...] = cnt_ref[...]


def _post_mixer(m, x, mods, ln_g, ln_b, router_w, router_b, alpha, glu_w=None):
    b, l, d = x.shape
    n_e = router_w.shape[1]
    assert n_e == N_EXPERT_GROUPS * EXPERTS_PER_GROUP
    tm = min(l, 512)
    nt = l // tm
    tok = lambda i, j: (i, j, 0)
    const2 = lambda i, j: (0, 0)
    in_specs = [pl.BlockSpec((1, tm, d), tok),
                pl.BlockSpec((1, tm, d), tok),
                pl.BlockSpec((1, 6, d), lambda i, j: (i, 0, 0)),
                pl.BlockSpec((1, d), const2),
                pl.BlockSpec((1, d), const2),
                pl.BlockSpec((n_e, d), const2),
                pl.BlockSpec((n_e, 1), const2)]
    args = [m, x, mods, ln_g.reshape(1, d), ln_b.reshape(1, d), router_w.T, router_b.reshape(n_e, 1)]
    scratch = [pltpu.VMEM((N_PAIR_CLASSES, LANES), F32), pltpu.VMEM((tm, tm), BF16)]
    if glu_w is not None:
        resident = pl.BlockSpec((d, d), const2, pipeline_mode=pl.Buffered(1))
        in_specs += [resident, resident]
        args += [glu_w[0], glu_w[1]]
        scratch += [pltpu.VMEM((d, d), BF16), pltpu.VMEM((d, d), BF16)]
    out_shape = [jax.ShapeDtypeStruct((b, l, d), F32),
                 jax.ShapeDtypeStruct((b * l, d), F32),
                 jax.ShapeDtypeStruct((ROUTE_ROWS, b * l), F32),
                 jax.ShapeDtypeStruct((N_PAIR_CLASSES, LANES), F32)]
    out_specs = [pl.BlockSpec((1, tm, d), tok),
                 pl.BlockSpec((tm, d), lambda i, j: (i * nt + j, 0)),
                 pl.BlockSpec((ROUTE_ROWS, tm), lambda i, j: (0, i * nt + j)),
                 pl.BlockSpec((N_PAIR_CLASSES, LANES), const2)]
    return pl.pallas_call(
        functools.partial(_post_mixer_body, glu=glu_w is not None, alpha=alpha),
        grid=(b, nt),
        in_specs=in_specs,
        out_specs=out_specs,
        out_shape=out_shape,
        scratch_shapes=scratch,
        compiler_params=_cparams("arbitrary", "arbitrary"),
        name="post_mixer_glu" if glu_w is not None else "post_mixer",
    )(*args)


def _take(table, idx):
    ids = jnp.arange(table.shape[0], dtype=I32)
    return jnp.sum(jnp.where(idx[:, None] == ids[None, :], table[None, :], 0), axis=1)


def _pass_segments():
    seg_cls, seg_hi = [], []
    pairs = [(a, b) for a in range(EXPERTS_PER_GROUP) for b in range(a + 1, EXPERTS_PER_GROUP)]
    for g in range(N_EXPERT_GROUPS):
        for m in range(EXPERTS_PER_GROUP):
            for idx, (a, b) in enumerate(pairs):
                if m in (a, b):
                    seg_cls.append(g * PAIRS_PER_GROUP + idx)
                    seg_hi.append(int(m == b))
    return seg_cls, seg_hi


def _dispatch_plan(route, counts, tile):
    n_cls = counts.shape[0]
    n_tok = route.shape[1]
    cnt = counts[:, 0].astype(I32)
    tiles = (cnt + tile - 1) // tile
    tile_end = jnp.cumsum(tiles)
    tile_off = tile_end - tiles
    cids = jnp.arange(n_cls, dtype=I32)[:, None]
    cls = route[0].astype(I32)
    pos = jnp.sum(jnp.where(cls[None, :] == cids, (tile_off * tile)[:, None], 0), axis=0) + route[1].astype(I32)

    n_row_tiles = n_tok // tile + n_cls
    seg_cls, seg_hi = _pass_segments()
    seg_cls = jnp.asarray(seg_cls, I32)
    seg_hi = jnp.asarray(seg_hi, I32)
    seg_per_expert = EXPERTS_PER_GROUP - 1
    seg_tiles = _take(tiles, seg_cls)
    seg_end = jnp.cumsum(seg_tiles)
    n_used = seg_end[-1]
    p = jnp.arange(2 * n_row_tiles, dtype=I32)
    seg = jnp.minimum(jnp.sum((seg_end[None, :] <= p[:, None]).astype(I32), axis=1), seg_cls.shape[0] - 1)
    within = p - _take(seg_end - seg_tiles, seg)
    spare = p - n_used
    used = p < n_used
    pass_tile = jnp.where(used, _take(tile_off, _take(seg_cls, seg)) + within, tile_end[-1] + spare // 2).astype(I32)
    pass_hi = jnp.where(used, _take(seg_hi, seg), spare % 2).astype(I32)
    pass_expert = jnp.where(used, seg // seg_per_expert, N_EXPERT_GROUPS * EXPERTS_PER_GROUP - 1).astype(I32)
    pass_rows = jnp.where(used, jnp.clip(_take(_take(cnt, seg_cls), seg) - within * tile, 0, tile), 0)
    last_tile = (tile_end - 1).astype(I32)
    return (pos.astype(I32), pass_tile, pass_hi, pass_expert, pass_rows.astype(I32),
            last_tile, tiles.astype(I32), tile_end[-1:].astype(I32), n_row_tiles)


def _zero_tiles(last_ref, tiles_ref, nu_ref, zero_ref, hs_ref, sem, n_e, tile, wait):
    t8 = tile // SUBLANES
    for e in range(n_e):
        @pl.when(tiles_ref[e] > 0)
        def _(e=e):
            cp = pltpu.make_async_copy(zero_ref, hs_ref.at[pl.ds(last_ref[e] * t8, t8)], sem)
            if wait:
                cp.wait()
            else:
                cp.start()

    def body(j, carry):
        cp = pltpu.make_async_copy(zero_ref, hs_ref.at[pl.ds(j * t8, t8)], sem)
        if wait:
            cp.wait()
        else:
            cp.start()
        return carry

    lax.fori_loop(nu_ref[0], hs_ref.shape[0] // t8, body, 0)


def _dispatch_body(pos_ref, last_ref, tiles_ref, nu_ref, h_ref, hs_ref, zero_ref, sem_z, sem, *,
                   tm, tile, n_e):
    i = pl.program_id(0)

    @pl.when(i == 0)
    def _():
        zero_ref[...] = jnp.zeros_like(zero_ref)
        _zero_tiles(last_ref, tiles_ref, nu_ref, zero_ref, hs_ref, sem_z, n_e, tile, False)
        _zero_tiles(last_ref, tiles_ref, nu_ref, zero_ref, hs_ref, sem_z, n_e, tile, True)

    base = i * tm

    def issue(blk, carry):
        for u in range(SUBLANES):
            p = pos_ref[base + blk * SUBLANES + u]
            pltpu.make_async_copy(h_ref.at[blk, pl.ds(u, 1)],
                                  hs_ref.at[p >> 3, pl.ds(p & (SUBLANES - 1), 1)], sem).start(priority=u % 2)
        return carry

    lax.fori_loop(0, tm // SUBLANES, issue, 0, unroll=2)
    pltpu.make_async_copy(h_ref, hs_ref.at[pl.ds(0, tm // SUBLANES)], sem).wait()


def _dispatch(h_rows, pos, last_tile, tiles, n_used, n_rows, tile):
    n, d = h_rows.shape
    n_e = tiles.shape[0]
    tm = min(n, 2048)
    grid_spec = pltpu.PrefetchScalarGridSpec(
        num_scalar_prefetch=4,
        grid=(n // tm,),
        in_specs=[pl.BlockSpec((tm // SUBLANES, SUBLANES, d), lambda i, *_: (i, 0, 0))],
        out_specs=pl.BlockSpec(memory_space=pl.ANY),
        scratch_shapes=[pltpu.VMEM((tile // SUBLANES, SUBLANES, d), F32),
                        pltpu.SemaphoreType.DMA(()), pltpu.SemaphoreType.DMA(())],
    )
    hs = pl.pallas_call(
        functools.partial(_dispatch_body, tm=tm, tile=tile, n_e=n_e),
        grid_spec=grid_spec,
        out_shape=jax.ShapeDtypeStruct((n_rows // SUBLANES, SUBLANES, d), F32),
        compiler_params=_cparams("arbitrary"),
        name="moe_dispatch",
    )(pos, last_tile, tiles, n_used, h_rows.reshape(n // SUBLANES, SUBLANES, d))
    return hs.reshape(n_rows, d)


def _expert_body(pt_ref, ph_ref, te_ref, pr_ref, nx_ref, sl_ref, hs_ref, wg_ref, wu_ref, wd_ref, ys_ref,
                 wgf, wuf, wdf, wgb, wub, wdb, sem, *, layer):
    i = pl.program_id(0)
    rows = pr_ref[i]
    first = jnp.logical_or(i == 0, te_ref[i] != te_ref[jnp.maximum(i - 1, 0)])

    def weight_copies(expert, slot):
        return [pltpu.make_async_copy(w_ref.at[layer, expert], buf.at[slot], sem.at[slot])
                for w_ref, buf in ((wg_ref, wgf), (wu_ref, wuf), (wd_ref, wdf))]

    @pl.when(jnp.logical_and(rows > 0, first))
    def _():
        slot = sl_ref[i]

        @pl.when(i == 0)
        def _():
            for cp in weight_copies(te_ref[i], slot):
                cp.start()

        for cp in weight_copies(te_ref[i], slot):
            cp.wait()

        @pl.when(nx_ref[i] >= 0)
        def _():
            for cp in weight_copies(nx_ref[i], 1 - slot):
                cp.start()

        wgb[...] = wgf[slot].astype(BF16)
        wub[...] = wuf[slot].astype(BF16)
        wdb[...] = wdf[slot].astype(BF16)

    def ffn(n):
        x = hs_ref[0:n, :].astype(BF16)
        gate = jnp.dot(x, wgb[...], preferred_element_type=F32)
        up = jnp.dot(x, wub[...], preferred_element_type=F32)
        a = (gate * jax.nn.sigmoid(gate) * up).astype(BF16)
        ys_ref[0:n, :] = jnp.dot(a, wdb[...], preferred_element_type=F32)

    tile = hs_ref.shape[0]
    sizes = (tile, tile // 2, tile // 4)
    for k, n in enumerate(sizes):
        lo = sizes[k + 1] if k + 1 < len(sizes) else 0

        @pl.when(jnp.logical_and(rows > lo, rows <= n))
        def _(n=n):
            ffn(n)
            if n < tile:
                ys_ref[n:, :] = jnp.zeros((tile - n, ys_ref.shape[1]), F32)

    @pl.when(rows == 0)
    def _():
        ys_ref[...] = jnp.zeros_like(ys_ref)


def _expert_ffn(hs, pass_tile, pass_hi, pass_expert, pass_rows, w_gate, w_up, w_down, layer, tile):
    _, n_e, d, f = w_gate.shape
    n_rows = hs.shape[0]
    eids = jnp.arange(n_e, dtype=I32)
    n_pass_e = jnp.sum(jnp.where((pass_expert[None, :] == eids[:, None]) & (pass_rows[None, :] > 0), 1, 0), axis=1)
    has = n_pass_e > 0
    later = (eids[None, :] > eids[:, None]) & has[None, :]
    next_used = jnp.min(jnp.where(later, eids[None, :], n_e), axis=1)
    next_used = jnp.where(next_used == n_e, -1, next_used).astype(I32)
    slot_e = ((jnp.cumsum(has.astype(I32)) - 1) % 2).astype(I32)
    pass_next = _take(next_used, pass_expert)
    pass_slot = jnp.maximum(_take(slot_e, pass_expert), 0)
    any_spec = pl.BlockSpec(memory_space=pl.ANY)
    grid_spec = pltpu.PrefetchScalarGridSpec(
        num_scalar_prefetch=6,
        grid=(pass_tile.shape[0],),
        in_specs=[pl.BlockSpec((tile, d), lambda i, pt, ph, te, pr, *_: (pt[jnp.where(pr[i] > 0, i, 0)], 0)),
                  any_spec, any_spec, any_spec],
        out_specs=pl.BlockSpec((tile, d), lambda i, pt, ph, *_: (pt[i], ph[i])),
        scratch_shapes=[pltpu.VMEM((2, d, f), F32), pltpu.VMEM((2, d, f), F32), pltpu.VMEM((2, f, d), F32),
                        pltpu.VMEM((d, f), BF16), pltpu.VMEM((d, f), BF16), pltpu.VMEM((f, d), BF16),
                        pltpu.SemaphoreType.DMA((2,))],
    )
    return pl.pallas_call(
        functools.partial(_expert_body, layer=layer),
        grid_spec=grid_spec,
        out_shape=jax.ShapeDtypeStruct((n_rows, 2 * d), F32),
        compiler_params=_cparams("arbitrary"),
        name="moe_experts",
    )(pass_tile, pass_hi, pass_expert, pass_rows, pass_next, pass_slot, hs, w_gate, w_up, w_down)


def _combine_body(pos_ref, x_ref, mod_ref, wts_ref, lng_ref, lnb_ref, ys_ref, *rest,
                  tm, alpha, next_mod):
    if next_mod:
        nmod_ref, x2_ref, h_ref, buf, sem = rest
    else:
        x2_ref, buf, sem = rest
    step = pl.program_id(0) * pl.num_programs(1) + pl.program_id(1)
    n_steps = pl.num_programs(0) * pl.num_programs(1)
    d = x_ref.shape[2]

    def gather(s, slot):
        def issue(blk, carry):
            for u in range(SUBLANES):
                p = pos_ref[s * tm + blk * SUBLANES + u]
                pltpu.make_async_copy(ys_ref.at[p >> 3, pl.ds(p & (SUBLANES - 1), 1)],
                                      buf.at[slot, blk, pl.ds(u, 1)], sem.at[slot]).start(priority=u % 2)
            return carry

        lax.fori_loop(0, tm // SUBLANES, issue, 0, unroll=2)

    @pl.when(step == 0)
    def _():
        gather(0, 0)

    @pl.when(step + 1 < n_steps)
    def _():
        gather(step + 1, (step + 1) % 2)

    slot = step % 2
    pltpu.make_async_copy(ys_ref.at[pl.ds(0, tm // SUBLANES)], buf.at[slot], sem.at[slot]).wait()

    w = wts_ref[...]
    rows = buf[slot].reshape(tm, 2 * d)
    moe = w[:, 0:1] * rows[:, :d] + w[:, 1:2] * rows[:, d:]
    g2 = mod_ref[0, 5:6, :]
    x2 = _layer_norm(alpha * x_ref[0] + g2 * moe, lng_ref[...], lnb_ref[...])
    x2_ref[0] = x2
    if next_mod:
        h_ref[0] = x2 * (1.0 + nmod_ref[0, 1:2, :]) + nmod_ref[0, 0:1, :]


def _combine(ys, pos, wts, x, mods, ln_g, ln_b, alpha, next_mods=None):
    b, l, d = x.shape
    tm = min(l, 512)
    nt = l // tm
    tok = lambda i, j, *_: (i, j, 0)
    bat = lambda i, j, *_: (i, 0, 0)
    const2 = lambda i, j, *_: (0, 0)
    in_specs = [pl.BlockSpec((1, tm, d), tok),
                pl.BlockSpec((1, 6, d), bat),
                pl.BlockSpec((tm, 2), lambda i, j, *_: (i * nt + j, 0)),
                pl.BlockSpec((1, d), const2),
                pl.BlockSpec((1, d), const2),
                pl.BlockSpec(memory_space=pl.ANY)]
    ys_tiles = ys.reshape(ys.shape[0] // SUBLANES, SUBLANES, ys.shape[1])
    args = [x, mods, wts, ln_g.reshape(1, d), ln_b.reshape(1, d), ys_tiles]
    out_shape = [jax.ShapeDtypeStruct((b, l, d), F32)]
    out_specs = [pl.BlockSpec((1, tm, d), tok)]
    if next_mods is not None:
        in_specs.append(pl.BlockSpec((1, 6, d), bat))
        args.append(next_mods)
        out_shape.append(jax.ShapeDtypeStruct((b, l, d), F32))
        out_specs.append(pl.BlockSpec((1, tm, d), tok))
    grid_spec = pltpu.PrefetchScalarGridSpec(
        num_scalar_prefetch=1,
        grid=(b, nt),
        in_specs=in_specs,
        out_specs=out_specs,
        scratch_shapes=[pltpu.VMEM((2, tm // SUBLANES, SUBLANES, 2 * d), F32),
                        pltpu.SemaphoreType.DMA((2,))],
    )
    return pl.pallas_call(
        functools.partial(_combine_body, tm=tm, alpha=alpha, next_mod=next_mods is not None),
        grid_spec=grid_spec,
        out_shape=out_shape,
        compiler_params=_cparams("arbitrary", "arbitrary"),
        name="moe_combine",
    )(pos, *args)


MOE_TILE = 256


def _moe(h_rows, route, counts, w_gate, w_up, w_down, layer):
    n = h_rows.shape[0]
    tile = min(n, MOE_TILE)
    (pos, pass_tile, pass_hi, pass_expert, pass_rows, last_tile, tiles, n_used_tiles,
     n_row_tiles) = _dispatch_plan(route, counts, tile)
    hs = _dispatch(h_rows, pos, last_tile, tiles, n_used_tiles, n_row_tiles * tile, tile)
    ys = _expert_ffn(hs, pass_tile, pass_hi, pass_expert, pass_rows, w_gate, w_up, w_down, layer, tile)
    return ys, pos, jnp.transpose(route[2:4])


def _pool_group(h_ref, w_ref, sc_ref, o_ref, col_ref, k, n_rows):
    n = n_rows * GRID_W
    c = h_ref.shape[2]
    blk = 4 * GRID_W
    half = k // 2
    pad = half * GRID_W
    ti = lax.broadcasted_iota(I32, (blk, blk), 0)
    si = lax.broadcasted_iota(I32, (blk, blk), 1)
    shift = GRID_W.bit_length() - 1
    same_row = (ti >> shift) == (si >> shift)
    band = jnp.where(same_row & (si - ti >= -half) & (si - ti <= half - 1), 1.0, 0.0).astype(BF16)
    col_ref[0:pad, :] = jnp.zeros((pad, c), F32)
    col_ref[pad + n:pad + n + pad, :] = jnp.zeros((pad, c), F32)
    for b0 in range(0, n, blk):
        hb = h_ref[0, b0:b0 + blk, :]
        head = hb.astype(BF16)
        rest = (hb - head.astype(F32)).astype(BF16)
        col_ref[pad + b0:pad + b0 + blk, :] = (jnp.dot(band, head, preferred_element_type=F32)
                                               + jnp.dot(band, rest, preferred_element_type=F32))
    w_bf = w_ref[0].astype(BF16)
    chunk = 2 * GRID_W

    def rows_chunk(ci, carry):
        c0 = pl.multiple_of(ci * chunk, chunk)
        acc = col_ref[pl.ds(c0, chunk), :]
        for j in range(1, k):
            acc = acc + col_ref[pl.ds(c0 + j * GRID_W, chunk), :]
        t = c0 + lax.broadcasted_iota(I32, (chunk, 1), 0)
        wc = t & (GRID_W - 1)
        wr = t >> shift
        cnt_c = jnp.minimum(wc + half - 1, GRID_W - 1) - jnp.maximum(wc - half, 0) + 1
        cnt_r = jnp.minimum(wr + half - 1, n_rows - 1) - jnp.maximum(wr - half, 0) + 1
        mean = acc / (cnt_c * cnt_r).astype(F32)
        pooled = (mean - h_ref[0, pl.ds(c0, chunk), :]).astype(BF16)
        o_ref[0, pl.ds(c0, chunk), :] = jnp.dot(pooled, w_bf, preferred_element_type=F32) * sc_ref[...]
        return carry

    lax.fori_loop(0, n // chunk, rows_chunk, 0)


def _pool_body(h_ref, w_ref, sc_ref, o_ref, col_ref, *, n_rows):
    g = pl.program_id(1)
    for gi, k in enumerate(POOL_WINDOWS):
        @pl.when(g == gi)
        def _(k=k):
            _pool_group(h_ref, w_ref, sc_ref, o_ref, col_ref, k, n_rows)


def _pool_mix(h, w_grp, scale):
    b, n, d = h.shape
    n_g, c, _ = w_grp.shape
    n_rows = n // GRID_W
    pad = (max(POOL_WINDOWS) // 2) * GRID_W
    return pl.pallas_call(
        functools.partial(_pool_body, n_rows=n_rows),
        grid=(b, n_g),
        in_specs=[pl.BlockSpec((1, n, c), lambda i, j: (i, 0, j)),
                  pl.BlockSpec((1, c, c), lambda i, j: (j, 0, 0)),
                  pl.BlockSpec((1, c), lambda i, j: (0, j))],
        out_specs=pl.BlockSpec((1, n, c), lambda i, j: (i, 0, j)),
        out_shape=jax.ShapeDtypeStruct((b, n, d), F32),
        scratch_shapes=[pltpu.VMEM((n + 2 * pad, c), F32)],
        compiler_params=_cparams("arbitrary", "arbitrary"),
        name="pool_mix",
    )(h, w_grp, scale.reshape(1, d))


def kernel(x, c, ctx, c_ctx, mod_w, mod_b, ln_g, ln_b, s5_lam_re, s5_lam_im, s5_log_dt, s5_b_re, s5_b_im,
           s5_c_re, s5_c_im, s5_d, s5_w_val, s5_w_gate, pool_w, pool_scale, router_w, router_b,
           moe_w_gate, moe_w_up, moe_w_down):
    b, l, d = x.shape
    depth = mod_w.shape[0]
    assert depth == 2 and b + 1 <= SUBLANES and d % LANES == 0 and GRID_W & (GRID_W - 1) == 0
    alpha = (2 * depth) ** 0.25

    cond = jnp.zeros((SUBLANES, d), F32).at[:b].set(c).at[b].set(c_ctx)
    mods = _modulation(cond, mod_w, mod_b).reshape(depth, SUBLANES, 6, d)

    toep, w1, w2t, a_r, a_i = _s5_weights(s5_lam_re[0], s5_lam_im[0], s5_log_dt[0], s5_b_re[0], s5_b_im[0],
                                          s5_c_re[0], s5_c_im[0], s5_d[0])
    y = _s5_mix(x, ctx, mods[0, :b], mods[0, b:b + 1], toep, w1, w2t, a_r, a_i)
    x1, h_rows, route, counts = _post_mixer(y, x, mods[0, :b], ln_g[0, 0], ln_b[0, 0], router_w, router_b,
                                            alpha, glu_w=(s5_w_val[0], s5_w_gate[0]))
    ys, pos, wts = _moe(h_rows, route, counts, moe_w_gate, moe_w_up, moe_w_down, 0)
    x2, h = _combine(ys, pos, wts, x1, mods[0, :b], ln_g[0, 1], ln_b[0, 1], alpha, next_mods=mods[1, :b])

    m = _pool_mix(h, pool_w[0], pool_scale[0])
    x3, h_rows, route, counts = _post_mixer(m, x2, mods[1, :b], ln_g[1, 0], ln_b[1, 0], router_w, router_b,
                                            alpha)
    ys, pos, wts = _moe(h_rows, route, counts, moe_w_gate, moe_w_up, moe_w_down, 1)
    (out,) = _combine(ys, pos, wts, x3, mods[1, :b], ln_g[1, 1], ln_b[1, 1], alpha)
    return out
```

```python
import functools

import jax
import jax.numpy as jnp
from jax import lax
from jax.experimental import pallas as pl
from jax.experimental.pallas import tpu as pltpu

F32 = jnp.float32
BF16 = jnp.bfloat16
I32 = jnp.int32

GRID_W = 64
S5_H = 16
S5_T = 16
POOL_WINDOWS = (2, 4, 8, 16)
N_EXPERT_GROUPS = 4
LN_EPS = 1e-5
LANES = 128
SUBLANES = 8
VMEM_LIMIT = 52 * 1024 * 1024

EXPERTS_PER_GROUP = 4
PAIRS_PER_GROUP = EXPERTS_PER_GROUP * (EXPERTS_PER_GROUP - 1) // 2
N_PAIR_CLASSES = N_EXPERT_GROUPS * PAIRS_PER_GROUP
ROUTE_ROWS = 8


def _cparams(*sem):
    return pltpu.CompilerParams(dimension_semantics=sem, vmem_limit_bytes=VMEM_LIMIT)


def _dot3(a, b, dims):
    a_head = a.astype(BF16)
    a_rest = (a - a_head.astype(F32)).astype(BF16)
    b_head = b.astype(BF16)
    b_rest = (b - b_head.astype(F32)).astype(BF16)
    dot = functools.partial(lax.dot_general, dimension_numbers=dims, preferred_element_type=F32)
    return dot(a_head, b_head) + dot(a_head, b_rest) + dot(a_rest, b_head)


MATMUL_NN = (((1,), (0,)), ((), ()))
MATMUL_NT = (((1,), (1,)), ((), ()))


def _mod_body(c_ref, w_ref, b_ref, o_ref):
    c = c_ref[...]
    s = c * jax.nn.sigmoid(c)
    o_ref[0] = _dot3(s, w_ref[0], MATMUL_NN) + b_ref[0]


def _modulation(cond, mod_w, mod_b):
    depth, d, n6 = mod_w.shape
    tn = min(n6, 1536)
    return pl.pallas_call(
        _mod_body,
        grid=(depth, n6 // tn),
        in_specs=[pl.BlockSpec((SUBLANES, d), lambda i, j: (0, 0)),
                  pl.BlockSpec((1, d, tn), lambda i, j: (i, 0, j)),
                  pl.BlockSpec((1, 1, tn), lambda i, j: (i, 0, j))],
        out_specs=pl.BlockSpec((1, SUBLANES, tn), lambda i, j: (i, 0, j)),
        out_shape=jax.ShapeDtypeStruct((depth, SUBLANES, n6), F32),
        compiler_params=_cparams("arbitrary", "arbitrary"),
        name="modulation",
    )(cond, mod_w, mod_b.reshape(depth, 1, n6))


def _s5_direction_terms(lam_re, lam_im, log_dt, b_re, b_im):
    lr = lam_re.astype(F32)
    li = lam_im.astype(F32)
    dt = jnp.exp(log_dt.astype(F32))[:, None]
    mag = jnp.exp(lr * dt)
    ar = mag * jnp.cos(li * dt)
    ai = mag * jnp.sin(li * dt)
    den = lr * lr + li * li
    nr = ar - 1.0
    fr = (nr * lr + ai * li) / den
    fi = (ai * lr - nr * li) / den
    br_, bi_ = b_re.astype(F32), b_im.astype(F32)
    bbr = fr[..., None] * br_ - fi[..., None] * bi_
    bbi = fr[..., None] * bi_ + fi[..., None] * br_
    k = jnp.arange(S5_T + 1, dtype=F32)[:, None, None]
    pm = jnp.exp(k * (lr * dt))
    pr = pm * jnp.cos(k * (li * dt))
    pi = pm * jnp.sin(k * (li * dt))
    return pr, pi, bbr, bbi


def _s5_weights(lam_re, lam_im, log_dt, b_re, b_im, c_re, c_im, d_skip):
    t = S5_T
    g, p = lam_re.shape[1:]
    h = b_re.shape[-1]
    terms = [_s5_direction_terms(lam_re[d], lam_im[d], log_dt[d], b_re[d], b_im[d]) for d in (0, 1)]
    pw = jnp.stack([jnp.transpose(terms[d][k], (1, 0, 2)) for d in (0, 1) for k in (0, 1)], axis=1)
    bt = jnp.stack([jnp.transpose(terms[d][k], (0, 2, 1)) for d in (0, 1) for k in (2, 3)], axis=1)
    cc = jnp.stack([c[d].astype(F32) for d in (0, 1) for c in (c_re, c_im)], axis=1)
    gp = 8
    spec4 = lambda rows: pl.BlockSpec((gp, 4, rows, p), lambda i: (i, 0, 0, 0))
    wide = pl.BlockSpec((gp, t * h, t * h), lambda i: (i, 0, 0))
    toep, w1, w2t = pl.pallas_call(
        functools.partial(_s5_prep_body, gp=gp),
        grid=(g // gp,),
        in_specs=[spec4(t + 1), spec4(h), spec4(h), pl.BlockSpec((gp, 1, h), lambda i: (i, 0, 0))],
        out_specs=[wide, wide, wide],
        out_shape=[jax.ShapeDtypeStruct((g, t * h, t * h), BF16) for _ in range(3)],
        compiler_params=_cparams("arbitrary"),
        name="s5_prep",
    )(pw, bt, cc, d_skip.astype(F32).reshape(g, 1, h))
    a_r = jnp.concatenate([pw[:, 0, t], pw[:, 2, t]], axis=-1)[:, None, :]
    a_i = jnp.concatenate([pw[:, 1, t], pw[:, 3, t]], axis=-1)[:, None, :]
    return toep, w1, w2t, a_r, a_i


def _cmul(ar, ai, br, bi):
    return ar * br - ai * bi, ar * bi + ai * br


def _s5_prep_body(pw_ref, bt_ref, cc_ref, d_ref, toep_ref, w1_ref, w2t_ref, *, gp):
    t = pw_ref.shape[2] - 1
    h = bt_ref.shape[2]
    eye =(lax.broadcasted_iota(I32, (h, h), 0) == lax.broadcasted_iota(I32, (h, h), 1))
    for g in range(gp):
        prf, pif, prb, pib = (pw_ref[g, k] for k in range(4))
        btf = (bt_ref[g, 0], bt_ref[g, 1])
        btb = (bt_ref[g, 2], bt_ref[g, 3])
        ccf = (cc_ref[g, 0], cc_ref[g, 1])
        ccb = (cc_ref[g, 2], cc_ref[g, 3])
        caf, cab = [], []
        for j in range(t):
            f_r, f_i = _cmul(*btf, prf[t - 1 - j:t - j], pif[t - 1 - j:t - j])
            b_r, b_i = _cmul(*btb, prb[j:j + 1], pib[j:j + 1])
            w1_ref[g, j * h:(j + 1) * h, :] = jnp.concatenate([f_r, b_r, f_i, b_i], axis=1).astype(BF16)
            mf_r, mf_i = _cmul(*ccf, prf[j + 1:j + 2], pif[j + 1:j + 2])
            mb_r, mb_i = _cmul(*ccb, prb[t - j:t - j + 1], pib[t - j:t - j + 1])
            w2t_ref[g, j * h:(j + 1) * h, :] = jnp.concatenate([mf_r, mb_r, -mf_i, -mb_i],
                                                               axis=1).astype(BF16)
            caf.append(_cmul(*ccf, prf[j:j + 1], pif[j:j + 1]))
            cab.append(_cmul(*ccb, prb[t - 1 - j:t - j], pib[t - 1 - j:t - j]))

        def lag_kernels(btx, ca):
            car = jnp.concatenate([c[0] for c in ca], axis=0)
            cai = jnp.concatenate([c[1] for c in ca], axis=0)
            return _dot3(btx[0], car, MATMUL_NT) - _dot3(btx[1], cai, MATMUL_NT)

        kf = lag_kernels(btf, caf)
        kb = lag_kernels(btb, cab)
        skip = jnp.where(eye, jnp.broadcast_to(d_ref[g], (h, h)), 0.0)
        mid = kb[:, (t - 1) * h:] + kf[:, :h] + skip
        kwide = jnp.concatenate([kb[:, :(t - 1) * h], mid, kf[:, h:], jnp.zeros((h, h), F32)], axis=1)
        for i in range(t):
            off = (t - 1 - i) * h
            toep_ref[g, i * h:(i + 1) * h, :] = kwide[:, off:off + t * h].astype(BF16)


def _granule_transpose(v):
    n = len(v)
    gran = lax.broadcasted_iota(I32, v[0].shape, 1) >> 4
    at = [gran == q for q in range(n)]
    rot = []
    for d in range(n):
        m = v[d]
        for q in range(1, n):
            m = jnp.where(at[q], v[(q + d) % n], m)
        rot.append(pltpu.roll(m, d * S5_H, 1) if d else m)
    out = []
    for q in range(n):
        w = rot[(-q) % n]
        for j in range(1, n):
            w = jnp.where(at[j], rot[(j - q) % n], w)
        out.append(w)
    return out


def _s5_body(x_ref, ctx_ref, mod_ref, cmod_ref, tt_ref, w1_ref, w2t_ref, ar_ref, ai_ref, y_ref,
             u_ref, s_ref, *, n_ctx_chunks, n_lat_chunks, gb):
    n_b = x_ref.shape[0]
    n_g = u_ref.shape[0]
    n_blocks = s_ref.shape[1] // SUBLANES
    half = SUBLANES // 2
    tok_blk = SUBLANES * S5_T
    lat_lo = n_ctx_chunks * n_b
    lat_rows = n_lat_chunks * n_b

    def chunk_rows(ref, b, tok0, shift, scale1):
        vs = [ref[b, pl.ds(tok0 + j, SUBLANES, stride=S5_T), :] * scale1 + shift for j in range(S5_T)]
        lo = _granule_transpose(vs[:SUBLANES])
        hi = _granule_transpose(vs[SUBLANES:])
        return lo, hi

    def put_rows(rows, chunk0, b):
        for q in range(n_g):
            for jh in range(2):
                u_ref[q, jh, pl.ds(chunk0 * n_b + b, SUBLANES, stride=n_b), :] = rows[jh][q]

    def u_rows(g, lo, n):
        return jnp.concatenate([u_ref[g, 0, lo:lo + n, :], u_ref[g, 1, lo:lo + n, :]], axis=1)

    shift = jnp.broadcast_to(cmod_ref[0, 0:1, :], (SUBLANES, LANES))
    scale1 = 1.0 + jnp.broadcast_to(cmod_ref[0, 1:2, :], (SUBLANES, LANES))
    for b in range(n_b):
        for cb in range(n_ctx_chunks // SUBLANES):
            rows = chunk_rows(ctx_ref, b, cb * tok_blk, shift, scale1)
            put_rows(rows, cb * SUBLANES, b)
            put_rows(rows, n_ctx_chunks + n_lat_chunks + cb * SUBLANES, b)

    def fill(cb, carry):
        for b in range(n_b):
            shift = jnp.broadcast_to(mod_ref[b, 0:1, :], (SUBLANES, LANES))
            scale1 = 1.0 + jnp.broadcast_to(mod_ref[b, 1:2, :], (SUBLANES, LANES))
            put_rows(chunk_rows(x_ref, b, cb * tok_blk, shift, scale1), n_ctx_chunks + cb * SUBLANES, b)
        return carry

    lax.fori_loop(0, n_lat_chunks // SUBLANES, fill, 0)

    for g in range(n_g):
        s_ref[g] = jnp.dot(u_rows(g, 0, n_blocks * SUBLANES).astype(BF16), w1_ref[g],
                           preferred_element_type=F32)

    lane = lax.broadcasted_iota(I32, (SUBLANES, LANES), 1)
    row = lax.broadcasted_iota(I32, (SUBLANES, LANES), 0)
    is_fwd = lane < LANES // 2
    is_fwd2 = jnp.concatenate([is_fwd, is_fwd], axis=1)
    first = (lane >> ((LANES // 2).bit_length() - 1)) == (row >> (half.bit_length() - 1))
    zero = jnp.zeros((SUBLANES, LANES), F32)
    for g0 in range(0, n_g, gb):
        ars = [jnp.broadcast_to(ar_ref[g0 + g], (SUBLANES, LANES)) for g in range(gb)]
        ais = [jnp.broadcast_to(ai_ref[g0 + g], (SUBLANES, LANES)) for g in range(gb)]

        def step(k, carry, g0=g0, ars=ars, ais=ais):
            fo = pl.multiple_of(k * SUBLANES, SUBLANES)
            bo = pl.multiple_of((n_blocks - 1 - k) * SUBLANES, SUBLANES)
            new = []
            loaded = [(s_ref[g0 + g, pl.ds(fo, SUBLANES), :], s_ref[g0 + g, pl.ds(bo, SUBLANES), :])
                      for g in range(gb)]
            stores = []
            for g in range(gb):
                xr, xi = carry[2 * g], carry[2 * g + 1]
                vf, vb = loaded[g]
                vr = jnp.where(is_fwd, vf[:, :LANES], vb[:, :LANES])
                vi = jnp.where(is_fwd, vf[:, LANES:], vb[:, LANES:])
                ar, ai = ars[g], ais[g]
                yr = ar * xr - ai * xi + vr
                yi = ar * xi + ai * xr + vi
                yrr = pltpu.roll(yr, half, 0)
                yir = pltpu.roll(yi, half, 0)
                zr = ar * yrr - ai * yir + vr
                zi = ar * yir + ai * yrr + vi
                inc = jnp.concatenate([jnp.where(first, xr, yrr), jnp.where(first, xi, yir)], axis=1)
                stores.append((jnp.where(is_fwd2, inc, vf), jnp.where(is_fwd2, vb, inc)))
                new.append(jnp.where(first, pltpu.roll(zr, half, 0), zr))
                new.append(jnp.where(first, pltpu.roll(zi, half, 0), zi))
            for g in range(gb):
                s_ref[g0 + g, pl.ds(fo, SUBLANES), :] = stores[g][0]
                s_ref[g0 + g, pl.ds(bo, SUBLANES), :] = stores[g][1]
            return tuple(new)

        lax.fori_loop(0, (n_ctx_chunks + n_lat_chunks) // 2, step, tuple(zero for _ in range(2 * gb)))

    for g in range(n_g):
        y = (jnp.dot(u_rows(g, lat_lo, lat_rows).astype(BF16), tt_ref[g], preferred_element_type=F32)
             + lax.dot_general(s_ref[g, lat_lo:lat_lo + lat_rows, :].astype(BF16), w2t_ref[g],
                               (((1,), (1,)), ((), ())), preferred_element_type=F32))
        for jh in range(2):
            u_ref[g, jh, lat_lo:lat_lo + lat_rows, :] = y[:, jh * LANES:(jh + 1) * LANES]

    def emit(cb, carry):
        for b in range(n_b):
            for jh in range(S5_T // SUBLANES):
                w = [u_ref[q, jh, pl.ds((n_ctx_chunks + cb * SUBLANES) * n_b + b, SUBLANES, stride=n_b), :]
                     for q in range(n_g)]
                v = _granule_transpose(w)
                for j in range(SUBLANES):
                    y_ref[b, pl.ds(cb * tok_blk + jh * SUBLANES + j, SUBLANES, stride=S5_T), :] = v[j]
        return carry

    lax.fori_loop(0, n_lat_chunks // SUBLANES, emit, 0)


def _s5_mix(x, ctx, mods, cmods, toep, w1, w2t, a_r, a_i):
    b, l, d = x.shape
    n_ctx = ctx.shape[1]
    n_g = LANES // S5_H
    n_ctx_chunks, n_lat_chunks = n_ctx // S5_T, l // S5_T
    assert b * 2 == SUBLANES and n_ctx_chunks % SUBLANES == 0 and n_lat_chunks % SUBLANES == 0
    rows = (2 * n_ctx_chunks + n_lat_chunks) * b
    w = S5_T * S5_H
    body = functools.partial(_s5_body, n_ctx_chunks=n_ctx_chunks, n_lat_chunks=n_lat_chunks, gb=8)
    lane_tile = lambda i: (0, 0, i)
    wspec = pl.BlockSpec((n_g, w, w), lambda i: (i, 0, 0))
    aspec = pl.BlockSpec((n_g, 1, w // 2), lambda i: (i, 0, 0))
    return pl.pallas_call(
        body,
        grid=(d // LANES,),
        in_specs=[pl.BlockSpec((b, l, LANES), lane_tile),
                  pl.BlockSpec((b, n_ctx, LANES), lane_tile),
                  pl.BlockSpec((b, 6, LANES), lane_tile),
                  pl.BlockSpec((1, 6, LANES), lane_tile),
                  wspec, wspec, wspec, aspec, aspec],
        out_specs=pl.BlockSpec((b, l, LANES), lane_tile, pipeline_mode=pl.Buffered(1)),
        out_shape=jax.ShapeDtypeStruct((b, l, d), F32),
        scratch_shapes=[pltpu.VMEM((n_g, w // LANES, rows, LANES), F32), pltpu.VMEM((n_g, rows, w), F32)],
        compiler_params=_cparams("arbitrary"),
        name="s5_mix",
    )(x, ctx, mods, cmods, toep, w1, w2t, a_r, a_i)


def _layer_norm(r, g, b):
    mu = jnp.mean(r, axis=-1, keepdims=True)
    xc = r - mu
    var = jnp.mean(xc * xc, axis=-1, keepdims=True)
    return xc * lax.rsqrt(var + LN_EPS) * g + b


def _max2_of4(a, b, c, d):
    h1, l1 = jnp.maximum(a, b), jnp.minimum(a, b)
    h2, l2 = jnp.maximum(c, d), jnp.minimum(c, d)
    return jnp.maximum(h1, h2) + jnp.maximum(jnp.minimum(h1, h2), jnp.maximum(l1, l2))


def _argmax_first(vals):
    idx = jnp.zeros(vals[0].shape, I32)
    best = vals[0]
    for j in range(1, len(vals)):
        upd = vals[j] > best
        idx = jnp.where(upd, j, idx)
        best = jnp.where(upd, vals[j], best)
    return idx, best


def _route(logits_t, count_ref, route_ref, tri_ref):
    n_e, tm = logits_t.shape
    per = n_e // N_EXPERT_GROUPS
    mx = jnp.max(logits_t, axis=0, keepdims=True)
    ex = jnp.exp(logits_t - mx)
    sc = ex / jnp.sum(ex, axis=0, keepdims=True)
    rows = [sc[e:e + 1, :] for e in range(n_e)]
    gscore = [_max2_of4(*rows[per * g:per * (g + 1)]) for g in range(N_EXPERT_GROUPS)]
    best, _ = _argmax_first(gscore)
    vals = []
    for j in range(per):
        v = rows[per * (N_EXPERT_GROUPS - 1) + j]
        for g in range(N_EXPERT_GROUPS - 2, -1, -1):
            v = jnp.where(best == g, rows[per * g + j], v)
        vals.append(v)
    i1, m1 = _argmax_first(vals)
    i2, m2 = _argmax_first([jnp.where(i1 == j, -1.0, vals[j]) for j in range(per)])
    den = m1 + m2
    first_lo = i1 < i2
    lo = jnp.minimum(i1, i2)
    hi = jnp.maximum(i1, i2)
    pair = jnp.where(lo == 0, 0, jnp.where(lo == 1, per - 1, 2 * per - 3)) + hi - lo - 1
    cls = best * PAIRS_PER_GROUP + pair
    w_lo = jnp.where(first_lo, m1, m2) / den
    w_hi = jnp.where(first_lo, m2, m1) / den

    n_cls = count_ref.shape[0]
    hit = lax.broadcasted_iota(I32, (n_cls, tm), 0) == cls
    onehot = jnp.where(hit, 1.0, 0.0)
    cum = jnp.dot(onehot.astype(BF16), tri_ref[...], preferred_element_type=F32)
    excl = cum - onehot + count_ref[:, 0:1]
    rank = jnp.sum(jnp.where(hit, excl, 0.0), axis=0, keepdims=True)
    count_ref[...] = count_ref[...] + jnp.sum(onehot, axis=1, keepdims=True)

    zero = jnp.zeros((1, tm), F32)
    route_ref[...] = jnp.concatenate([cls.astype(F32), rank, w_lo, w_hi, zero, zero, zero, zero], axis=0)


def _post_mixer_body(m_ref, x_ref, mod_ref, lng_ref, lnb_ref, rwt_ref, rb_ref, *rest, glu, alpha):
    if glu:
        wv_ref, wg_ref, x1_ref, h_ref, route_ref, cnt_out_ref, cnt_ref, tri_ref, wvb, wgb = rest
    else:
        x1_ref, h_ref, route_ref, cnt_out_ref, cnt_ref, tri_ref = rest

    @pl.when((pl.program_id(0) == 0) & (pl.program_id(1) == 0))
    def _():
        cnt_ref[...] = jnp.zeros_like(cnt_ref)
        src = lax.broadcasted_iota(I32, tri_ref.shape, 0)
        dst = lax.broadcasted_iota(I32, tri_ref.shape, 1)
        tri_ref[...] = jnp.where(src <= dst, 1.0, 0.0).astype(BF16)
        if glu:
            wvb[...] = wv_ref[...].astype(BF16)
            wgb[...] = wg_ref[...].astype(BF16)

    if glu:
        a = jax.nn.gelu(m_ref[0], approximate=True).astype(BF16)
        val = jnp.dot(a, wvb[...], preferred_element_type=F32)
        gate = jnp.dot(a, wgb[...], preferred_element_type=F32)
        m = val * jax.nn.sigmoid(gate)
    else:
        m = m_ref[0]

    g1 = mod_ref[0, 2:3, :]
    sh2 = mod_ref[0, 3:4, :]
    sc2 = mod_ref[0, 4:5, :]
    x1 = _layer_norm(alpha * x_ref[0] + g1 * m, lng_ref[...], lnb_ref[...])
    x1_ref[0] = x1
    h = x1 * (1.0 + sc2) + sh2
    h_ref[...] = h
    logits_t = _dot3(rwt_ref[...], h, MATMUL_NT) + rb_ref[...]
    _route(logits_t, cnt_ref, route_ref, tri_ref)
    cnt_out_ref[...] = cnt_ref[...]


def _post_mixer(m, x, mods, ln_g, ln_b, router_w, router_b, alpha, glu_w=None):
    b, l, d = x.shape
    n_e = router_w.shape[1]
    assert n_e == N_EXPERT_GROUPS * EXPERTS_PER_GROUP
    tm = min(l, 512)
    nt = l // tm
    tok = lambda i, j: (i, j, 0)
    const2 = lambda i, j: (0, 0)
    in_specs = [pl.BlockSpec((1, tm, d), tok),
                pl.BlockSpec((1, tm, d), tok),
                pl.BlockSpec((1, 6, d), lambda i, j: (i, 0, 0)),
                pl.BlockSpec((1, d), const2),
                pl.BlockSpec((1, d), const2),
                pl.BlockSpec((n_e, d), const2),
                pl.BlockSpec((n_e, 1), const2)]
    args = [m, x, mods, ln_g.reshape(1, d), ln_b.reshape(1, d), router_w.T, router_b.reshape(n_e, 1)]
    scratch = [pltpu.VMEM((N_PAIR_CLASSES, LANES), F32), pltpu.VMEM((tm, tm), BF16)]
    if glu_w is not None:
        resident = pl.BlockSpec((d, d), const2, pipeline_mode=pl.Buffered(1))
        in_specs += [resident, resident]
        args += [glu_w[0], glu_w[1]]
        scratch += [pltpu.VMEM((d, d), BF16), pltpu.VMEM((d, d), BF16)]
    out_shape = [jax.ShapeDtypeStruct((b, l, d), F32),
                 jax.ShapeDtypeStruct((b * l, d), F32),
                 jax.ShapeDtypeStruct((ROUTE_ROWS, b * l), F32),
                 jax.ShapeDtypeStruct((N_PAIR_CLASSES, LANES), F32)]
    out_specs = [pl.BlockSpec((1, tm, d), tok),
                 pl.BlockSpec((tm, d), lambda i, j: (i * nt + j, 0)),
                 pl.BlockSpec((ROUTE_ROWS, tm), lambda i, j: (0, i * nt + j)),
                 pl.BlockSpec((N_PAIR_CLASSES, LANES), const2)]
    return pl.pallas_call(
        functools.partial(_post_mixer_body, glu=glu_w is not None, alpha=alpha),
        grid=(b, nt),
        in_specs=in_specs,
        out_specs=out_specs,
        out_shape=out_shape,
        scratch_shapes=scratch,
        compiler_params=_cparams("arbitrary", "arbitrary"),
        name="post_mixer_glu" if glu_w is not None else "post_mixer",
    )(*args)


def _take(table, idx):
    ids = jnp.arange(table.shape[0], dtype=I32)
    return jnp.sum(jnp.where(idx[:, None] == ids[None, :], table[None, :], 0), axis=1)


def _pass_segments():
    seg_cls, seg_hi = [], []
    pairs = [(a, b) for a in range(EXPERTS_PER_GROUP) for b in range(a + 1, EXPERTS_PER_GROUP)]
    for g in range(N_EXPERT_GROUPS):
        for m in range(EXPERTS_PER_GROUP):
            for idx, (a, b) in enumerate(pairs):
                if m in (a, b):
                    seg_cls.append(g * PAIRS_PER_GROUP + idx)
                    seg_hi.append(int(m == b))
    return seg_cls, seg_hi


def _dispatch_plan(route, counts, tile):
    n_cls = counts.shape[0]
    n_tok = route.shape[1]
    cnt = counts[:, 0].astype(I32)
    tiles = (cnt + tile - 1) // tile
    tile_end = jnp.cumsum(tiles)
    tile_off = tile_end - tiles
    cids = jnp.arange(n_cls, dtype=I32)[:, None]
    cls = route[0].astype(I32)
    pos = jnp.sum(jnp.where(cls[None, :] == cids, (tile_off * tile)[:, None], 0), axis=0) + route[1].astype(I32)

    n_row_tiles = n_tok // tile + n_cls
    seg_cls, seg_hi = _pass_segments()
    seg_cls = jnp.asarray(seg_cls, I32)
    seg_hi = jnp.asarray(seg_hi, I32)
    seg_per_expert = EXPERTS_PER_GROUP - 1
    seg_tiles = _take(tiles, seg_cls)
    seg_end = jnp.cumsum(seg_tiles)
    n_used = seg_end[-1]
    p = jnp.arange(2 * n_row_tiles, dtype=I32)
    seg = jnp.minimum(jnp.sum((seg_end[None, :] <= p[:, None]).astype(I32), axis=1), seg_cls.shape[0] - 1)
    within = p - _take(seg_end - seg_tiles, seg)
    spare = p - n_used
    used = p < n_used
    pass_tile = jnp.where(used, _take(tile_off, _take(seg_cls, seg)) + within, tile_end[-1] + spare // 2).astype(I32)
    pass_hi = jnp.where(used, _take(seg_hi, seg), spare % 2).astype(I32)
    pass_expert = jnp.where(used, seg // seg_per_expert, N_EXPERT_GROUPS * EXPERTS_PER_GROUP - 1).astype(I32)
    pass_rows = jnp.where(used, jnp.clip(_take(_take(cnt, seg_cls), seg) - within * tile, 0, tile), 0)
    last_tile = (tile_end - 1).astype(I32)
    return (pos.astype(I32), pass_tile, pass_hi, pass_expert, pass_rows.astype(I32),
            last_tile, tiles.astype(I32), tile_end[-1:].astype(I32), n_row_tiles)


def _zero_tiles(last_ref, tiles_ref, nu_ref, zero_ref, hs_ref, sem, n_e, tile, wait):
    t8 = tile // SUBLANES
    for e in range(n_e):
        @pl.when(tiles_ref[e] > 0)
        def _(e=e):
            cp = pltpu.make_async_copy(zero_ref, hs_ref.at[pl.ds(last_ref[e] * t8, t8)], sem)
            if wait:
                cp.wait()
            else:
                cp.start()

    def body(j, carry):
        cp = pltpu.make_async_copy(zero_ref, hs_ref.at[pl.ds(j * t8, t8)], sem)
        if wait:
            cp.wait()
        else:
            cp.start()
        return carry

    lax.fori_loop(nu_ref[0], hs_ref.shape[0] // t8, body, 0)


def _dispatch_body(pos_ref, last_ref, tiles_ref, nu_ref, h_ref, hs_ref, zero_ref, sem_z, sem, *,
                   tm, tile, n_e):
    i = pl.program_id(0)

    @pl.when(i == 0)
    def _():
        zero_ref[...] = jnp.zeros_like(zero_ref)
        _zero_tiles(last_ref, tiles_ref, nu_ref, zero_ref, hs_ref, sem_z, n_e, tile, False)
        _zero_tiles(last_ref, tiles_ref, nu_ref, zero_ref, hs_ref, sem_z, n_e, tile, True)

    base = i * tm

    def issue(blk, carry):
        for u in range(SUBLANES):
            p = pos_ref[base + blk * SUBLANES + u]
            pltpu.make_async_copy(h_ref.at[blk, pl.ds(u, 1)],
                                  hs_ref.at[p >> 3, pl.ds(p & (SUBLANES - 1), 1)], sem).start(priority=u % 2)
        return carry

    lax.fori_loop(0, tm // SUBLANES, issue, 0, unroll=2)
    pltpu.make_async_copy(h_ref, hs_ref.at[pl.ds(0, tm // SUBLANES)], sem).wait()


def _dispatch(h_rows, pos, last_tile, tiles, n_used, n_rows, tile):
    n, d = h_rows.shape
    n_e = tiles.shape[0]
    tm = min(n, 2048)
    grid_spec = pltpu.PrefetchScalarGridSpec(
        num_scalar_prefetch=4,
        grid=(n // tm,),
        in_specs=[pl.BlockSpec((tm // SUBLANES, SUBLANES, d), lambda i, *_: (i, 0, 0))],
        out_specs=pl.BlockSpec(memory_space=pl.ANY),
        scratch_shapes=[pltpu.VMEM((tile // SUBLANES, SUBLANES, d), F32),
                        pltpu.SemaphoreType.DMA(()), pltpu.SemaphoreType.DMA(())],
    )
    hs = pl.pallas_call(
        functools.partial(_dispatch_body, tm=tm, tile=tile, n_e=n_e),
        grid_spec=grid_spec,
        out_shape=jax.ShapeDtypeStruct((n_rows // SUBLANES, SUBLANES, d), F32),
        compiler_params=_cparams("arbitrary"),
        name="moe_dispatch",
    )(pos, last_tile, tiles, n_used, h_rows.reshape(n // SUBLANES, SUBLANES, d))
    return hs.reshape(n_rows, d)


def _expert_body(pt_ref, ph_ref, te_ref, pr_ref, nx_ref, sl_ref, hs_ref, wg_ref, wu_ref, wd_ref, ys_ref,
                 wgf, wuf, wdf, wgb, wub, wdb, sem, *, layer):
    i = pl.program_id(0)
    rows = pr_ref[i]
    first = jnp.logical_or(i == 0, te_ref[i] != te_ref[jnp.maximum(i - 1, 0)])

    def weight_copies(expert, slot):
        return [pltpu.make_async_copy(w_ref.at[layer, expert], buf.at[slot], sem.at[slot])
                for w_ref, buf in ((wg_ref, wgf), (wu_ref, wuf), (wd_ref, wdf))]

    @pl.when(jnp.logical_and(rows > 0, first))
    def _():
        slot = sl_ref[i]

        @pl.when(i == 0)
        def _():
            for cp in weight_copies(te_ref[i], slot):
                cp.start()

        for cp in weight_copies(te_ref[i], slot):
            cp.wait()

        @pl.when(nx_ref[i] >= 0)
        def _():
            for cp in weight_copies(nx_ref[i], 1 - slot):
                cp.start()

        wgb[...] = wgf[slot].astype(BF16)
        wub[...] = wuf[slot].astype(BF16)
        wdb[...] = wdf[slot].astype(BF16)

    def ffn(n):
        x = hs_ref[0:n, :].astype(BF16)
        gate = jnp.dot(x, wgb[...], preferred_element_type=F32)
        up = jnp.dot(x, wub[...], preferred_element_type=F32)
        a = (gate * jax.nn.sigmoid(gate) * up).astype(BF16)
        ys_ref[0:n, :] = jnp.dot(a, wdb[...], preferred_element_type=F32)

    tile = hs_ref.shape[0]
    sizes = (tile, tile // 2, tile // 4)
    for k, n in enumerate(sizes):
        lo = sizes[k + 1] if k + 1 < len(sizes) else 0

        @pl.when(jnp.logical_and(rows > lo, rows <= n))
        def _(n=n):
            ffn(n)
            if n < tile:
                ys_ref[n:, :] = jnp.zeros((tile - n, ys_ref.shape[1]), F32)

    @pl.when(rows == 0)
    def _():
        ys_ref[...] = jnp.zeros_like(ys_ref)


def _expert_ffn(hs, pass_tile, pass_hi, pass_expert, pass_rows, w_gate, w_up, w_down, layer, tile):
    _, n_e, d, f = w_gate.shape
    n_rows = hs.shape[0]
    eids = jnp.arange(n_e, dtype=I32)
    n_pass_e = jnp.sum(jnp.where((pass_expert[None, :] == eids[:, None]) & (pass_rows[None, :] > 0), 1, 0), axis=1)
    has = n_pass_e > 0
    later = (eids[None, :] > eids[:, None]) & has[None, :]
    next_used = jnp.min(jnp.where(later, eids[None, :], n_e), axis=1)
    next_used = jnp.where(next_used == n_e, -1, next_used).astype(I32)
    slot_e = ((jnp.cumsum(has.astype(I32)) - 1) % 2).astype(I32)
    pass_next = _take(next_used, pass_expert)
    pass_slot = jnp.maximum(_take(slot_e, pass_expert), 0)
    any_spec = pl.BlockSpec(memory_space=pl.ANY)
    grid_spec = pltpu.PrefetchScalarGridSpec(
        num_scalar_prefetch=6,
        grid=(pass_tile.shape[0],),
        in_specs=[pl.BlockSpec((tile, d), lambda i, pt, ph, te, pr, *_: (pt[jnp.where(pr[i] > 0, i, 0)], 0)),
                  any_spec, any_spec, any_spec],
        out_specs=pl.BlockSpec((tile, d), lambda i, pt, ph, *_: (pt[i], ph[i])),
        scratch_shapes=[pltpu.VMEM((2, d, f), F32), pltpu.VMEM((2, d, f), F32), pltpu.VMEM((2, f, d), F32),
                        pltpu.VMEM((d, f), BF16), pltpu.VMEM((d, f), BF16), pltpu.VMEM((f, d), BF16),
                        pltpu.SemaphoreType.DMA((2,))],
    )
    return pl.pallas_call(
        functools.partial(_expert_body, layer=layer),
        grid_spec=grid_spec,
        out_shape=jax.ShapeDtypeStruct((n_rows, 2 * d), F32),
        compiler_params=_cparams("arbitrary"),
        name="moe_experts",
    )(pass_tile, pass_hi, pass_expert, pass_rows, pass_next, pass_slot, hs, w_gate, w_up, w_down)


def _combine_body(pos_ref, x_ref, mod_ref, wts_ref, lng_ref, lnb_ref, ys_ref, *rest,
                  tm, alpha, next_mod):
    if next_mod:
        nmod_ref, x2_ref, h_ref, buf, sem = rest
    else:
        x2_ref, buf, sem = rest
    step = pl.program_id(0) * pl.num_programs(1) + pl.program_id(1)
    n_steps = pl.num_programs(0) * pl.num_programs(1)
    d = x_ref.shape[2]

    def gather(s, slot):
        def issue(blk, carry):
            for u in range(SUBLANES):
                p = pos_ref[s * tm + blk * SUBLANES + u]
                pltpu.make_async_copy(ys_ref.at[p >> 3, pl.ds(p & (SUBLANES - 1), 1)],
                                      buf.at[slot, blk, pl.ds(u, 1)], sem.at[slot]).start(priority=u % 2)
            return carry

        lax.fori_loop(0, tm // SUBLANES, issue, 0, unroll=2)

    @pl.when(step == 0)
    def _():
        gather(0, 0)

    @pl.when(step + 1 < n_steps)
    def _():
        gather(step + 1, (step + 1) % 2)

    slot = step % 2
    pltpu.make_async_copy(ys_ref.at[pl.ds(0, tm // SUBLANES)], buf.at[slot], sem.at[slot]).wait()

    w = wts_ref[...]
    rows = buf[slot].reshape(tm, 2 * d)
    moe = w[:, 0:1] * rows[:, :d] + w[:, 1:2] * rows[:, d:]
    g2 = mod_ref[0, 5:6, :]
    x2 = _layer_norm(alpha * x_ref[0] + g2 * moe, lng_ref[...], lnb_ref[...])
    x2_ref[0] = x2
    if next_mod:
        h_ref[0] = x2 * (1.0 + nmod_ref[0, 1:2, :]) + nmod_ref[0, 0:1, :]


def _combine(ys, pos, wts, x, mods, ln_g, ln_b, alpha, next_mods=None):
    b, l, d = x.shape
    tm = min(l, 512)
    nt = l // tm
    tok = lambda i, j, *_: (i, j, 0)
    bat = lambda i, j, *_: (i, 0, 0)
    const2 = lambda i, j, *_: (0, 0)
    in_specs = [pl.BlockSpec((1, tm, d), tok),
                pl.BlockSpec((1, 6, d), bat),
                pl.BlockSpec((tm, 2), lambda i, j, *_: (i * nt + j, 0)),
                pl.BlockSpec((1, d), const2),
                pl.BlockSpec((1, d), const2),
                pl.BlockSpec(memory_space=pl.ANY)]
    ys_tiles = ys.reshape(ys.shape[0] // SUBLANES, SUBLANES, ys.shape[1])
    args = [x, mods, wts, ln_g.reshape(1, d), ln_b.reshape(1, d), ys_tiles]
    out_shape = [jax.ShapeDtypeStruct((b, l, d), F32)]
    out_specs = [pl.BlockSpec((1, tm, d), tok)]
    if next_mods is not None:
        in_specs.append(pl.BlockSpec((1, 6, d), bat))
        args.append(next_mods)
        out_shape.append(jax.ShapeDtypeStruct((b, l, d), F32))
        out_specs.append(pl.BlockSpec((1, tm, d), tok))
    grid_spec = pltpu.PrefetchScalarGridSpec(
        num_scalar_prefetch=1,
        grid=(b, nt),
        in_specs=in_specs,
        out_specs=out_specs,
        scratch_shapes=[pltpu.VMEM((2, tm // SUBLANES, SUBLANES, 2 * d), F32),
                        pltpu.SemaphoreType.DMA((2,))],
    )
    return pl.pallas_call(
        functools.partial(_combine_body, tm=tm, alpha=alpha, next_mod=next_mods is not None),
        grid_spec=grid_spec,
        out_shape=out_shape,
        compiler_params=_cparams("arbitrary", "arbitrary"),
        name="moe_combine",
    )(pos, *args)


MOE_TILE = 256


def _moe(h_rows, route, counts, w_gate, w_up, w_down, layer):
    n = h_rows.shape[0]
    tile = min(n, MOE_TILE)
    (pos, pass_tile, pass_hi, pass_expert, pass_rows, last_tile, tiles, n_used_tiles,
     n_row_tiles) = _dispatch_plan(route, counts, tile)
    hs = _dispatch(h_rows, pos, last_tile, tiles, n_used_tiles, n_row_tiles * tile, tile)
    ys = _expert_ffn(hs, pass_tile, pass_hi, pass_expert, pass_rows, w_gate, w_up, w_down, layer, tile)
    return ys, pos, jnp.transpose(route[2:4])


def _pool_group(h_ref, w_ref, sc_ref, o_ref, col_ref, k, n_rows):
    n = n_rows * GRID_W
    c = h_ref.shape[2]
    blk = 4 * GRID_W
    half = k // 2
    pad = half * GRID_W
    ti = lax.broadcasted_iota(I32, (blk, blk), 0)
    si = lax.broadcasted_iota(I32, (blk, blk), 1)
    shift = GRID_W.bit_length() - 1
    same_row = (ti >> shift) == (si >> shift)
    band = jnp.where(same_row & (si - ti >= -half) & (si - ti <= half - 1), 1.0, 0.0).astype(BF16)
    col_ref[0:pad, :] = jnp.zeros((pad, c), F32)
    col_ref[pad + n:pad + n + pad, :] = jnp.zeros((pad, c), F32)
    for b0 in range(0, n, blk):
        hb = h_ref[0, b0:b0 + blk, :]
        head = hb.astype(BF16)
        rest = (hb - head.astype(F32)).astype(BF16)
        col_ref[pad + b0:pad + b0 + blk, :] = (jnp.dot(band, head, preferred_element_type=F32)
                                               + jnp.dot(band, rest, preferred_element_type=F32))
    acc = col_ref[0:n, :]
    for j in range(1, k):
        acc = acc + col_ref[j * GRID_W:j * GRID_W + n, :]
    t = lax.broadcasted_iota(I32, (n, 1), 0)
    wc = t & (GRID_W - 1)
    wr = t >> shift
    cnt_c = jnp.minimum(wc + half - 1, GRID_W - 1) - jnp.maximum(wc - half, 0) + 1
    cnt_r = jnp.minimum(wr + half - 1, n_rows - 1) - jnp.maximum(wr - half, 0) + 1
    mean = acc / (cnt_c * cnt_r).astype(F32)
    pooled = (mean - h_ref[0]).astype(BF16)
    o_ref[0] = jnp.dot(pooled, w_ref[0].astype(BF16), preferred_element_type=F32) * sc_ref[...]


def _pool_body(h_ref, w_ref, sc_ref, o_ref, col_ref, *, n_rows):
    g = pl.program_id(1)
    for gi, k in enumerate(POOL_WINDOWS):
        @pl.when(g == gi)
        def _(k=k):
            _pool_group(h_ref, w_ref, sc_ref, o_ref, col_ref, k, n_rows)


def _pool_mix(h, w_grp, scale):
    b, n, d = h.shape
    n_g, c, _ = w_grp.shape
    n_rows = n // GRID_W
    pad = (max(POOL_WINDOWS) // 2) * GRID_W
    return pl.pallas_call(
        functools.partial(_pool_body, n_rows=n_rows),
        grid=(b, n_g),
        in_specs=[pl.BlockSpec((1, n, c), lambda i, j: (i, 0, j)),
                  pl.BlockSpec((1, c, c), lambda i, j: (j, 0, 0)),
                  pl.BlockSpec((1, c), lambda i, j: (0, j))],
        out_specs=pl.BlockSpec((1, n, c), lambda i, j: (i, 0, j)),
        out_shape=jax.ShapeDtypeStruct((b, n, d), F32),
        scratch_shapes=[pltpu.VMEM((n + 2 * pad, c), F32)],
        compiler_params=_cparams("arbitrary", "arbitrary"),
        name="pool_mix",
    )(h, w_grp, scale.reshape(1, d))


def kernel(x, c, ctx, c_ctx, mod_w, mod_b, ln_g, ln_b, s5_lam_re, s5_lam_im, s5_log_dt, s5_b_re, s5_b_im,
           s5_c_re, s5_c_im, s5_d, s5_w_val, s5_w_gate, pool_w, pool_scale, router_w, router_b,
           moe_w_gate, moe_w_up, moe_w_down):
    b, l, d = x.shape
    depth = mod_w.shape[0]
    assert depth == 2 and b + 1 <= SUBLANES and d % LANES == 0 and GRID_W & (GRID_W - 1) == 0
    alpha = (2 * depth) ** 0.25

    cond = jnp.zeros((SUBLANES, d), F32).at[:b].set(c).at[b].set(c_ctx)
    mods = _modulation(cond, mod_w, mod_b).reshape(depth, SUBLANES, 6, d)

    toep, w1, w2t, a_r, a_i = _s5_weights(s5_lam_re[0], s5_lam_im[0], s5_log_dt[0], s5_b_re[0], s5_b_im[0],
                                          s5_c_re[0], s5_c_im[0], s5_d[0])
    y = _s5_mix(x, ctx, mods[0, :b], mods[0, b:b + 1], toep, w1, w2t, a_r, a_i)
    x1, h_rows, route, counts = _post_mixer(y, x, mods[0, :b], ln_g[0, 0], ln_b[0, 0], router_w, router_b,
                                            alpha, glu_w=(s5_w_val[0], s5_w_gate[0]))
    ys, pos, wts = _moe(h_rows, route, counts, moe_w_gate, moe_w_up, moe_w_down, 0)
    x2, h = _combine(ys, pos, wts, x1, mods[0, :b], ln_g[0, 1], ln_b[0, 1], alpha, next_mods=mods[1, :b])

    m = _pool_mix(h, pool_w[0], pool_scale[0])
    x3, h_rows, route, counts = _post_mixer(m, x2, mods[1, :b], ln_g[1, 0], ln_b[1, 0], router_w, router_b,
                                            alpha)
    ys, pos, wts = _moe(h_rows, route, counts, moe_w_gate, moe_w_up, moe_w_down, 1)
    (out,) = _combine(ys, pos, wts, x3, mods[1, :b], ln_g[1, 1], ln_b[1, 1], alpha)
    return out
```

```python
import functools

import jax
import jax.numpy as jnp
from jax import lax
from jax.experimental import pallas as pl
from jax.experimental.pallas import tpu as pltpu

F32 = jnp.float32
BF16 = jnp.bfloat16
I32 = jnp.int32

GRID_W = 64
S5_H = 16
S5_T = 16
POOL_WINDOWS = (2, 4, 8, 16)
N_EXPERT_GROUPS = 4
LN_EPS = 1e-5
LANES = 128
SUBLANES = 8
VMEM_LIMIT = 52 * 1024 * 1024

EXPERTS_PER_GROUP = 4
PAIRS_PER_GROUP = EXPERTS_PER_GROUP * (EXPERTS_PER_GROUP - 1) // 2
N_PAIR_CLASSES = N_EXPERT_GROUPS * PAIRS_PER_GROUP
ROUTE_ROWS = 8


def _cparams(*sem):
    return pltpu.CompilerParams(dimension_semantics=sem, vmem_limit_bytes=VMEM_LIMIT)


def _dot3(a, b, dims):
    a_head = a.astype(BF16)
    a_rest = (a - a_head.astype(F32)).astype(BF16)
    b_head = b.astype(BF16)
    b_rest = (b - b_head.astype(F32)).astype(BF16)
    dot = functools.partial(lax.dot_general, dimension_numbers=dims, preferred_element_type=F32)
    return dot(a_head, b_head) + dot(a_head, b_rest) + dot(a_rest, b_head)


MATMUL_NN = (((1,), (0,)), ((), ()))
MATMUL_NT = (((1,), (1,)), ((), ()))


def _mod_body(c_ref, w_ref, b_ref, o_ref):
    c = c_ref[...]
    s = c * jax.nn.sigmoid(c)
    o_ref[0] = _dot3(s, w_ref[0], MATMUL_NN) + b_ref[0]


def _modulation(cond, mod_w, mod_b):
    depth, d, n6 = mod_w.shape
    tn = min(n6, 3072)
    return pl.pallas_call(
        _mod_body,
        grid=(depth, n6 // tn),
        in_specs=[pl.BlockSpec((SUBLANES, d), lambda i, j: (0, 0)),
                  pl.BlockSpec((1, d, tn), lambda i, j: (i, 0, j)),
                  pl.BlockSpec((1, 1, tn), lambda i, j: (i, 0, j))],
        out_specs=pl.BlockSpec((1, SUBLANES, tn), lambda i, j: (i, 0, j)),
        out_shape=jax.ShapeDtypeStruct((depth, SUBLANES, n6), F32),
        compiler_params=_cparams("arbitrary", "arbitrary"),
        name="modulation",
    )(cond, mod_w, mod_b.reshape(depth, 1, n6))


def _s5_direction_terms(lam_re, lam_im, log_dt, b_re, b_im):
    lr = lam_re.astype(F32)
    li = lam_im.astype(F32)
    dt = jnp.exp(log_dt.astype(F32))[:, None]
    mag = jnp.exp(lr * dt)
    ar = mag * jnp.cos(li * dt)
    ai = mag * jnp.sin(li * dt)
    den = lr * lr + li * li
    nr = ar - 1.0
    fr = (nr * lr + ai * li) / den
    fi = (ai * lr - nr * li) / den
    br_, bi_ = b_re.astype(F32), b_im.astype(F32)
    bbr = fr[..., None] * br_ - fi[..., None] * bi_
    bbi = fr[..., None] * bi_ + fi[..., None] * br_
    k = jnp.arange(S5_T + 1, dtype=F32)[:, None, None]
    pm = jnp.exp(k * (lr * dt))
    pr = pm * jnp.cos(k * (li * dt))
    pi = pm * jnp.sin(k * (li * dt))
    return pr, pi, bbr, bbi


def _s5_weights(lam_re, lam_im, log_dt, b_re, b_im, c_re, c_im, d_skip):
    t = S5_T
    g, p = lam_re.shape[1:]
    h = b_re.shape[-1]
    terms = [_s5_direction_terms(lam_re[d], lam_im[d], log_dt[d], b_re[d], b_im[d]) for d in (0, 1)]
    pw = jnp.stack([jnp.transpose(terms[d][k], (1, 0, 2)) for d in (0, 1) for k in (0, 1)], axis=1)
    bt = jnp.stack([jnp.transpose(terms[d][k], (0, 2, 1)) for d in (0, 1) for k in (2, 3)], axis=1)
    cc = jnp.stack([c[d].astype(F32) for d in (0, 1) for c in (c_re, c_im)], axis=1)
    gp = 8
    spec4 = lambda rows: pl.BlockSpec((gp, 4, rows, p), lambda i: (i, 0, 0, 0))
    wide = pl.BlockSpec((gp, t * h, t * h), lambda i: (i, 0, 0))
    toep, w1, w2t = pl.pallas_call(
        functools.partial(_s5_prep_body, gp=gp),
        grid=(g // gp,),
        in_specs=[spec4(t + 1), spec4(h), spec4(h), pl.BlockSpec((gp, 1, h), lambda i: (i, 0, 0))],
        out_specs=[wide, wide, wide],
        out_shape=[jax.ShapeDtypeStruct((g, t * h, t * h), BF16) for _ in range(3)],
        compiler_params=_cparams("arbitrary"),
        name="s5_prep",
    )(pw, bt, cc, d_skip.astype(F32).reshape(g, 1, h))
    a_r = jnp.concatenate([pw[:, 0, t], pw[:, 2, t]], axis=-1)[:, None, :]
    a_i = jnp.concatenate([pw[:, 1, t], pw[:, 3, t]], axis=-1)[:, None, :]
    return toep, w1, w2t, a_r, a_i


def _cmul(ar, ai, br, bi):
    return ar * br - ai * bi, ar * bi + ai * br


def _s5_prep_body(pw_ref, bt_ref, cc_ref, d_ref, toep_ref, w1_ref, w2t_ref, *, gp):
    t = pw_ref.shape[2] - 1
    h = bt_ref.shape[2]
    eye =(lax.broadcasted_iota(I32, (h, h), 0) == lax.broadcasted_iota(I32, (h, h), 1))
    for g in range(gp):
        prf, pif, prb, pib = (pw_ref[g, k] for k in range(4))
        btf = (bt_ref[g, 0], bt_ref[g, 1])
        btb = (bt_ref[g, 2], bt_ref[g, 3])
        ccf = (cc_ref[g, 0], cc_ref[g, 1])
        ccb = (cc_ref[g, 2], cc_ref[g, 3])
        caf, cab = [], []
        for j in range(t):
            f_r, f_i = _cmul(*btf, prf[t - 1 - j:t - j], pif[t - 1 - j:t - j])
            b_r, b_i = _cmul(*btb, prb[j:j + 1], pib[j:j + 1])
            w1_ref[g, j * h:(j + 1) * h, :] = jnp.concatenate([f_r, b_r, f_i, b_i], axis=1).astype(BF16)
            mf_r, mf_i = _cmul(*ccf, prf[j + 1:j + 2], pif[j + 1:j + 2])
            mb_r, mb_i = _cmul(*ccb, prb[t - j:t - j + 1], pib[t - j:t - j + 1])
            w2t_ref[g, j * h:(j + 1) * h, :] = jnp.concatenate([mf_r, mb_r, -mf_i, -mb_i],
                                                               axis=1).astype(BF16)
            caf.append(_cmul(*ccf, prf[j:j + 1], pif[j:j + 1]))
            cab.append(_cmul(*ccb, prb[t - 1 - j:t - j], pib[t - 1 - j:t - j]))

        def lag_kernels(btx, ca):
            car = jnp.concatenate([c[0] for c in ca], axis=0)
            cai = jnp.concatenate([c[1] for c in ca], axis=0)
            return _dot3(btx[0], car, MATMUL_NT) - _dot3(btx[1], cai, MATMUL_NT)

        kf = lag_kernels(btf, caf)
        kb = lag_kernels(btb, cab)
        skip = jnp.where(eye, jnp.broadcast_to(d_ref[g], (h, h)), 0.0)
        mid = kb[:, (t - 1) * h:] + kf[:, :h] + skip
        kwide = jnp.concatenate([kb[:, :(t - 1) * h], mid, kf[:, h:], jnp.zeros((h, h), F32)], axis=1)
        for i in range(t):
            off = (t - 1 - i) * h
            toep_ref[g, i * h:(i + 1) * h, :] = kwide[:, off:off + t * h].astype(BF16)


def _granule_transpose(v):
    n = len(v)
    gran = lax.broadcasted_iota(I32, v[0].shape, 1) >> 4
    at = [gran == q for q in range(n)]
    rot = []
    for d in range(n):
        m = v[d]
        for q in range(1, n):
            m = jnp.where(at[q], v[(q + d) % n], m)
        rot.append(pltpu.roll(m, d * S5_H, 1) if d else m)
    out = []
    for q in range(n):
        w = rot[(-q) % n]
        for j in range(1, n):
            w = jnp.where(at[j], rot[(j - q) % n], w)
        out.append(w)
    return out


def _s5_body(x_ref, ctx_ref, mod_ref, cmod_ref, tt_ref, w1_ref, w2t_ref, ar_ref, ai_ref, y_ref,
             u_ref, s_ref, *, n_ctx_chunks, n_lat_chunks, gb):
    n_b = x_ref.shape[0]
    n_g = u_ref.shape[0]
    n_blocks = s_ref.shape[1] // SUBLANES
    half = SUBLANES // 2
    tok_blk = SUBLANES * S5_T
    lat_lo = n_ctx_chunks * n_b
    lat_rows = n_lat_chunks * n_b

    def chunk_rows(ref, b, tok0, shift, scale1):
        vs = [ref[b, pl.ds(tok0 + j, SUBLANES, stride=S5_T), :] * scale1 + shift for j in range(S5_T)]
        lo = _granule_transpose(vs[:SUBLANES])
        hi = _granule_transpose(vs[SUBLANES:])
        return lo, hi

    def put_rows(rows, chunk0, b):
        for q in range(n_g):
            for jh in range(2):
                u_ref[q, jh, pl.ds(chunk0 * n_b + b, SUBLANES, stride=n_b), :] = rows[jh][q]

    def u_rows(g, lo, n):
        return jnp.concatenate([u_ref[g, 0, lo:lo + n, :], u_ref[g, 1, lo:lo + n, :]], axis=1)

    shift = jnp.broadcast_to(cmod_ref[0, 0:1, :], (SUBLANES, LANES))
    scale1 = 1.0 + jnp.broadcast_to(cmod_ref[0, 1:2, :], (SUBLANES, LANES))
    for b in range(n_b):
        for cb in range(n_ctx_chunks // SUBLANES):
            rows = chunk_rows(ctx_ref, b, cb * tok_blk, shift, scale1)
            put_rows(rows, cb * SUBLANES, b)
            put_rows(rows, n_ctx_chunks + n_lat_chunks + cb * SUBLANES, b)

    def fill(cb, carry):
        for b in range(n_b):
            shift = jnp.broadcast_to(mod_ref[b, 0:1, :], (SUBLANES, LANES))
            scale1 = 1.0 + jnp.broadcast_to(mod_ref[b, 1:2, :], (SUBLANES, LANES))
            put_rows(chunk_rows(x_ref, b, cb * tok_blk, shift, scale1), n_ctx_chunks + cb * SUBLANES, b)
        return carry

    lax.fori_loop(0, n_lat_chunks // SUBLANES, fill, 0)

    for g in range(n_g):
        s_ref[g] = jnp.dot(u_rows(g, 0, n_blocks * SUBLANES).astype(BF16), w1_ref[g],
                           preferred_element_type=F32)

    lane = lax.broadcasted_iota(I32, (SUBLANES, LANES), 1)
    row = lax.broadcasted_iota(I32, (SUBLANES, LANES), 0)
    is_fwd = lane < LANES // 2
    is_fwd2 = jnp.concatenate([is_fwd, is_fwd], axis=1)
    first = (lane >> ((LANES // 2).bit_length() - 1)) == (row >> (half.bit_length() - 1))
    zero = jnp.zeros((SUBLANES, LANES), F32)
    for g0 in range(0, n_g, gb):
        ars = [jnp.broadcast_to(ar_ref[g0 + g], (SUBLANES, LANES)) for g in range(gb)]
        ais = [jnp.broadcast_to(ai_ref[g0 + g], (SUBLANES, LANES)) for g in range(gb)]

        def step(k, carry, g0=g0, ars=ars, ais=ais):
            fo = pl.multiple_of(k * SUBLANES, SUBLANES)
            bo = pl.multiple_of((n_blocks - 1 - k) * SUBLANES, SUBLANES)
            new = []
            loaded = [(s_ref[g0 + g, pl.ds(fo, SUBLANES), :], s_ref[g0 + g, pl.ds(bo, SUBLANES), :])
                      for g in range(gb)]
            stores = []
            for g in range(gb):
                xr, xi = carry[2 * g], carry[2 * g + 1]
                vf, vb = loaded[g]
                vr = jnp.where(is_fwd, vf[:, :LANES], vb[:, :LANES])
                vi = jnp.where(is_fwd, vf[:, LANES:], vb[:, LANES:])
                ar, ai = ars[g], ais[g]
                yr = ar * xr - ai * xi + vr
                yi = ar * xi + ai * xr + vi
                yrr = pltpu.roll(yr, half, 0)
                yir = pltpu.roll(yi, half, 0)
                zr = ar * yrr - ai * yir + vr
                zi = ar * yir + ai * yrr + vi
                inc = jnp.concatenate([jnp.where(first, xr, yrr), jnp.where(first, xi, yir)], axis=1)
                stores.append((jnp.where(is_fwd2, inc, vf), jnp.where(is_fwd2, vb, inc)))
                new.append(jnp.where(first, pltpu.roll(zr, half, 0), zr))
                new.append(jnp.where(first, pltpu.roll(zi, half, 0), zi))
            for g in range(gb):
                s_ref[g0 + g, pl.ds(fo, SUBLANES), :] = stores[g][0]
                s_ref[g0 + g, pl.ds(bo, SUBLANES), :] = stores[g][1]
            return tuple(new)

        lax.fori_loop(0, (n_ctx_chunks + n_lat_chunks) // 2, step, tuple(zero for _ in range(2 * gb)))

    for g in range(n_g):
        y = (jnp.dot(u_rows(g, lat_lo, lat_rows).astype(BF16), tt_ref[g], preferred_element_type=F32)
             + lax.dot_general(s_ref[g, lat_lo:lat_lo + lat_rows, :].astype(BF16), w2t_ref[g],
                               (((1,), (1,)), ((), ())), preferred_element_type=F32))
        for jh in range(2):
            u_ref[g, jh, lat_lo:lat_lo + lat_rows, :] = y[:, jh * LANES:(jh + 1) * LANES]

    def emit(cb, carry):
        for b in range(n_b):
            for jh in range(S5_T // SUBLANES):
                w = [u_ref[q, jh, pl.ds((n_ctx_chunks + cb * SUBLANES) * n_b + b, SUBLANES, stride=n_b), :]
                     for q in range(n_g)]
                v = _granule_transpose(w)
                for j in range(SUBLANES):
                    y_ref[b, pl.ds(cb * tok_blk + jh * SUBLANES + j, SUBLANES, stride=S5_T), :] = v[j]
        return carry

    lax.fori_loop(0, n_lat_chunks // SUBLANES, emit, 0)


def _s5_mix(x, ctx, mods, cmods, toep, w1, w2t, a_r, a_i):
    b, l, d = x.shape
    n_ctx = ctx.shape[1]
    n_g = LANES // S5_H
    n_ctx_chunks, n_lat_chunks = n_ctx // S5_T, l // S5_T
    assert b * 2 == SUBLANES and n_ctx_chunks % SUBLANES == 0 and n_lat_chunks % SUBLANES == 0
    rows = (2 * n_ctx_chunks + n_lat_chunks) * b
    w = S5_T * S5_H
    body = functools.partial(_s5_body, n_ctx_chunks=n_ctx_chunks, n_lat_chunks=n_lat_chunks, gb=8)
    lane_tile = lambda i: (0, 0, i)
    wspec = pl.BlockSpec((n_g, w, w), lambda i: (i, 0, 0))
    aspec = pl.BlockSpec((n_g, 1, w // 2), lambda i: (i, 0, 0))
    return pl.pallas_call(
        body,
        grid=(d // LANES,),
        in_specs=[pl.BlockSpec((b, l, LANES), lane_tile),
                  pl.BlockSpec((b, n_ctx, LANES), lane_tile),
                  pl.BlockSpec((b, 6, LANES), lane_tile),
                  pl.BlockSpec((1, 6, LANES), lane_tile),
                  wspec, wspec, wspec, aspec, aspec],
        out_specs=pl.BlockSpec((b, l, LANES), lane_tile, pipeline_mode=pl.Buffered(1)),
        out_shape=jax.ShapeDtypeStruct((b, l, d), F32),
        scratch_shapes=[pltpu.VMEM((n_g, w // LANES, rows, LANES), F32), pltpu.VMEM((n_g, rows, w), F32)],
        compiler_params=_cparams("arbitrary"),
        name="s5_mix",
    )(x, ctx, mods, cmods, toep, w1, w2t, a_r, a_i)


def _layer_norm(r, g, b):
    mu = jnp.mean(r, axis=-1, keepdims=True)
    xc = r - mu
    var = jnp.mean(xc * xc, axis=-1, keepdims=True)
    return xc * lax.rsqrt(var + LN_EPS) * g + b


def _max2_of4(a, b, c, d):
    h1, l1 = jnp.maximum(a, b), jnp.minimum(a, b)
    h2, l2 = jnp.maximum(c, d), jnp.minimum(c, d)
    return jnp.maximum(h1, h2) + jnp.maximum(jnp.minimum(h1, h2), jnp.maximum(l1, l2))


def _argmax_first(vals):
    idx = jnp.zeros(vals[0].shape, I32)
    best = vals[0]
    for j in range(1, len(vals)):
        upd = vals[j] > best
        idx = jnp.where(upd, j, idx)
        best = jnp.where(upd, vals[j], best)
    return idx, best


def _route(logits_t, count_ref, route_ref):
    n_e, tm = logits_t.shape
    per = n_e // N_EXPERT_GROUPS
    mx = jnp.max(logits_t, axis=0, keepdims=True)
    ex = jnp.exp(logits_t - mx)
    sc = ex / jnp.sum(ex, axis=0, keepdims=True)
    rows = [sc[e:e + 1, :] for e in range(n_e)]
    gscore = [_max2_of4(*rows[per * g:per * (g + 1)]) for g in range(N_EXPERT_GROUPS)]
    best, _ = _argmax_first(gscore)
    vals = []
    for j in range(per):
        v = rows[per * (N_EXPERT_GROUPS - 1) + j]
        for g in range(N_EXPERT_GROUPS - 2, -1, -1):
            v = jnp.where(best == g, rows[per * g + j], v)
        vals.append(v)
    i1, m1 = _argmax_first(vals)
    i2, m2 = _argmax_first([jnp.where(i1 == j, -1.0, vals[j]) for j in range(per)])
    den = m1 + m2
    first_lo = i1 < i2
    lo = jnp.minimum(i1, i2)
    hi = jnp.maximum(i1, i2)
    pair = jnp.where(lo == 0, 0, jnp.where(lo == 1, per - 1, 2 * per - 3)) + hi - lo - 1
    cls = best * PAIRS_PER_GROUP + pair
    w_lo = jnp.where(first_lo, m1, m2) / den
    w_hi = jnp.where(first_lo, m2, m1) / den

    n_cls = count_ref.shape[0]
    hit = lax.broadcasted_iota(I32, (n_cls, tm), 0) == cls
    onehot = jnp.where(hit, 1.0, 0.0)
    src = lax.broadcasted_iota(I32, (tm, tm), 0)
    dst = lax.broadcasted_iota(I32, (tm, tm), 1)
    tri = jnp.where(src <= dst, 1.0, 0.0).astype(BF16)
    cum = jnp.dot(onehot.astype(BF16), tri, preferred_element_type=F32)
    excl = cum - onehot + count_ref[:, 0:1]
    rank = jnp.sum(jnp.where(hit, excl, 0.0), axis=0, keepdims=True)
    count_ref[...] = count_ref[...] + jnp.sum(onehot, axis=1, keepdims=True)

    zero = jnp.zeros((1, tm), F32)
    route_ref[...] = jnp.concatenate([cls.astype(F32), rank, w_lo, w_hi, zero, zero, zero, zero], axis=0)


def _post_mixer_body(m_ref, x_ref, mod_ref, lng_ref, lnb_ref, rwt_ref, rb_ref, *rest, glu, alpha):
    if glu:
        wv_ref, wg_ref, x1_ref, h_ref, route_ref, cnt_out_ref, cnt_ref, wvb, wgb = rest
    else:
        x1_ref, h_ref, route_ref, cnt_out_ref, cnt_ref = rest

    @pl.when((pl.program_id(0) == 0) & (pl.program_id(1) == 0))
    def _():
        cnt_ref[...] = jnp.zeros_like(cnt_ref)
        if glu:
            wvb[...] = wv_ref[...].astype(BF16)
            wgb[...] = wg_ref[...].astype(BF16)

    if glu:
        a = jax.nn.gelu(m_ref[0], approximate=True).astype(BF16)
        val = jnp.dot(a, wvb[...], preferred_element_type=F32)
        gate = jnp.dot(a, wgb[...], preferred_element_type=F32)
        m = val * jax.nn.sigmoid(gate)
    else:
        m = m_ref[0]

    g1 = mod_ref[0, 2:3, :]
    sh2 = mod_ref[0, 3:4, :]
    sc2 = mod_ref[0, 4:5, :]
    x1 = _layer_norm(alpha * x_ref[0] + g1 * m, lng_ref[...], lnb_ref[...])
    x1_ref[0] = x1
    h = x1 * (1.0 + sc2) + sh2
    h_ref[...] = h
    logits_t = _dot3(rwt_ref[...], h, MATMUL_NT) + rb_ref[...]
    _route(logits_t, cnt_ref, route_ref)
    cnt_out_ref[...] = cnt_ref[...]


def _post_mixer(m, x, mods, ln_g, ln_b, router_w, router_b, alpha, glu_w=None):
    b, l, d = x.shape
    n_e = router_w.shape[1]
    assert n_e == N_EXPERT_GROUPS * EXPERTS_PER_GROUP
    tm = min(l, 512)
    nt = l // tm
    tok = lambda i, j: (i, j, 0)
    const2 = lambda i, j: (0, 0)
    in_specs = [pl.BlockSpec((1, tm, d), tok),
                pl.BlockSpec((1, tm, d), tok),
                pl.BlockSpec((1, 6, d), lambda i, j: (i, 0, 0)),
                pl.BlockSpec((1, d), const2),
                pl.BlockSpec((1, d), const2),
                pl.BlockSpec((n_e, d), const2),
                pl.BlockSpec((n_e, 1), const2)]
    args = [m, x, mods, ln_g.reshape(1, d), ln_b.reshape(1, d), router_w.T, router_b.reshape(n_e, 1)]
    scratch = [pltpu.VMEM((N_PAIR_CLASSES, LANES), F32)]
    if glu_w is not None:
        resident = pl.BlockSpec((d, d), const2, pipeline_mode=pl.Buffered(1))
        in_specs += [resident, resident]
        args += [glu_w[0], glu_w[1]]
        scratch += [pltpu.VMEM((d, d), BF16), pltpu.VMEM((d, d), BF16)]
    out_shape = [jax.ShapeDtypeStruct((b, l, d), F32),
                 jax.ShapeDtypeStruct((b * l, d), F32),
                 jax.ShapeDtypeStruct((ROUTE_ROWS, b * l), F32),
                 jax.ShapeDtypeStruct((N_PAIR_CLASSES, LANES), F32)]
    out_specs = [pl.BlockSpec((1, tm, d), tok),
                 pl.BlockSpec((tm, d), lambda i, j: (i * nt + j, 0)),
                 pl.BlockSpec((ROUTE_ROWS, tm), lambda i, j: (0, i * nt + j)),
                 pl.BlockSpec((N_PAIR_CLASSES, LANES), const2)]
    return pl.pallas_call(
        functools.partial(_post_mixer_body, glu=glu_w is not None, alpha=alpha),
        grid=(b, nt),
        in_specs=in_specs,
        out_specs=out_specs,
        out_shape=out_shape,
        scratch_shapes=scratch,
        compiler_params=_cparams("arbitrary", "arbitrary"),
        name="post_mixer_glu" if glu_w is not None else "post_mixer",
    )(*args)


def _take(table, idx):
    ids = jnp.arange(table.shape[0], dtype=I32)
    return jnp.sum(jnp.where(idx[:, None] == ids[None, :], table[None, :], 0), axis=1)


def _pass_segments():
    seg_cls, seg_hi = [], []
    pairs = [(a, b) for a in range(EXPERTS_PER_GROUP) for b in range(a + 1, EXPERTS_PER_GROUP)]
    for g in range(N_EXPERT_GROUPS):
        for m in range(EXPERTS_PER_GROUP):
            for idx, (a, b) in enumerate(pairs):
                if m in (a, b):
                    seg_cls.append(g * PAIRS_PER_GROUP + idx)
                    seg_hi.append(int(m == b))
    return seg_cls, seg_hi


def _dispatch_plan(route, counts, tile):
    n_cls = counts.shape[0]
    n_tok = route.shape[1]
    cnt = counts[:, 0].astype(I32)
    tiles = (cnt + tile - 1) // tile
    tile_end = jnp.cumsum(tiles)
    tile_off = tile_end - tiles
    cids = jnp.arange(n_cls, dtype=I32)[:, None]
    cls = route[0].astype(I32)
    pos = jnp.sum(jnp.where(cls[None, :] == cids, (tile_off * tile)[:, None], 0), axis=0) + route[1].astype(I32)

    n_row_tiles = n_tok // tile + n_cls
    seg_cls, seg_hi = _pass_segments()
    seg_cls = jnp.asarray(seg_cls, I32)
    seg_hi = jnp.asarray(seg_hi, I32)
    seg_per_expert = EXPERTS_PER_GROUP - 1
    seg_tiles = _take(tiles, seg_cls)
    seg_end = jnp.cumsum(seg_tiles)
    n_used = seg_end[-1]
    p = jnp.arange(2 * n_row_tiles, dtype=I32)
    seg = jnp.minimum(jnp.sum((seg_end[None, :] <= p[:, None]).astype(I32), axis=1), seg_cls.shape[0] - 1)
    within = p - _take(seg_end - seg_tiles, seg)
    spare = p - n_used
    used = p < n_used
    pass_tile = jnp.where(used, _take(tile_off, _take(seg_cls, seg)) + within, tile_end[-1] + spare // 2).astype(I32)
    pass_hi = jnp.where(used, _take(seg_hi, seg), spare % 2).astype(I32)
    pass_expert = jnp.where(used, seg // seg_per_expert, N_EXPERT_GROUPS * EXPERTS_PER_GROUP - 1).astype(I32)
    pass_rows = jnp.where(used, jnp.clip(_take(_take(cnt, seg_cls), seg) - within * tile, 0, tile), 0)
    last_tile = (tile_end - 1).astype(I32)
    return (pos.astype(I32), pass_tile, pass_hi, pass_expert, pass_rows.astype(I32),
            last_tile, tiles.astype(I32), tile_end[-1:].astype(I32), n_row_tiles)


def _zero_tiles(last_ref, tiles_ref, nu_ref, zero_ref, hs_ref, sem, n_e, tile, wait):
    t8 = tile // SUBLANES
    for e in range(n_e):
        @pl.when(tiles_ref[e] > 0)
        def _(e=e):
            cp = pltpu.make_async_copy(zero_ref, hs_ref.at[pl.ds(last_ref[e] * t8, t8)], sem)
            if wait:
                cp.wait()
            else:
                cp.start()

    def body(j, carry):
        cp = pltpu.make_async_copy(zero_ref, hs_ref.at[pl.ds(j * t8, t8)], sem)
        if wait:
            cp.wait()
        else:
            cp.start()
        return carry

    lax.fori_loop(nu_ref[0], hs_ref.shape[0] // t8, body, 0)


def _dispatch_body(pos_ref, last_ref, tiles_ref, nu_ref, h_ref, hs_ref, zero_ref, sem_z, sem, *,
                   tm, tile, n_e):
    i = pl.program_id(0)

    @pl.when(i == 0)
    def _():
        zero_ref[...] = jnp.zeros_like(zero_ref)
        _zero_tiles(last_ref, tiles_ref, nu_ref, zero_ref, hs_ref, sem_z, n_e, tile, False)
        _zero_tiles(last_ref, tiles_ref, nu_ref, zero_ref, hs_ref, sem_z, n_e, tile, True)

    base = i * tm

    def issue(blk, carry):
        for u in range(SUBLANES):
            p = pos_ref[base + blk * SUBLANES + u]
            pltpu.make_async_copy(h_ref.at[blk, pl.ds(u, 1)],
                                  hs_ref.at[p >> 3, pl.ds(p & (SUBLANES - 1), 1)], sem).start(priority=u % 2)
        return carry

    lax.fori_loop(0, tm // SUBLANES, issue, 0, unroll=4)
    pltpu.make_async_copy(h_ref, hs_ref.at[pl.ds(0, tm // SUBLANES)], sem).wait()


def _dispatch(h_rows, pos, last_tile, tiles, n_used, n_rows, tile):
    n, d = h_rows.shape
    n_e = tiles.shape[0]
    tm = min(n, 2048)
    grid_spec = pltpu.PrefetchScalarGridSpec(
        num_scalar_prefetch=4,
        grid=(n // tm,),
        in_specs=[pl.BlockSpec((tm // SUBLANES, SUBLANES, d), lambda i, *_: (i, 0, 0))],
        out_specs=pl.BlockSpec(memory_space=pl.ANY),
        scratch_shapes=[pltpu.VMEM((tile // SUBLANES, SUBLANES, d), F32),
                        pltpu.SemaphoreType.DMA(()), pltpu.SemaphoreType.DMA(())],
    )
    hs = pl.pallas_call(
        functools.partial(_dispatch_body, tm=tm, tile=tile, n_e=n_e),
        grid_spec=grid_spec,
        out_shape=jax.ShapeDtypeStruct((n_rows // SUBLANES, SUBLANES, d), F32),
        compiler_params=_cparams("arbitrary"),
        name="moe_dispatch",
    )(pos, last_tile, tiles, n_used, h_rows.reshape(n // SUBLANES, SUBLANES, d))
    return hs.reshape(n_rows, d)


def _expert_body(pt_ref, ph_ref, te_ref, pr_ref, nx_ref, sl_ref, hs_ref, wg_ref, wu_ref, wd_ref, ys_ref,
                 wgf, wuf, wdf, wgb, wub, wdb, sem, *, layer):
    i = pl.program_id(0)
    rows = pr_ref[i]
    first = jnp.logical_or(i == 0, te_ref[i] != te_ref[jnp.maximum(i - 1, 0)])

    def weight_copies(expert, slot):
        return [pltpu.make_async_copy(w_ref.at[layer, expert], buf.at[slot], sem.at[slot])
                for w_ref, buf in ((wg_ref, wgf), (wu_ref, wuf), (wd_ref, wdf))]

    @pl.when(jnp.logical_and(rows > 0, first))
    def _():
        slot = sl_ref[i]

        @pl.when(i == 0)
        def _():
            for cp in weight_copies(te_ref[i], slot):
                cp.start()

        for cp in weight_copies(te_ref[i], slot):
            cp.wait()

        @pl.when(nx_ref[i] >= 0)
        def _():
            for cp in weight_copies(nx_ref[i], 1 - slot):
                cp.start()

        wgb[...] = wgf[slot].astype(BF16)
        wub[...] = wuf[slot].astype(BF16)
        wdb[...] = wdf[slot].astype(BF16)

    def ffn(n):
        x = hs_ref[0:n, :].astype(BF16)
        gate = jnp.dot(x, wgb[...], preferred_element_type=F32)
        up = jnp.dot(x, wub[...], preferred_element_type=F32)
        a = (gate * jax.nn.sigmoid(gate) * up).astype(BF16)
        ys_ref[0:n, :] = jnp.dot(a, wdb[...], preferred_element_type=F32)

    tile = hs_ref.shape[0]
    sizes = (tile, tile // 2, tile // 4)
    for k, n in enumerate(sizes):
        lo = sizes[k + 1] if k + 1 < len(sizes) else 0

        @pl.when(jnp.logical_and(rows > lo, rows <= n))
        def _(n=n):
            ffn(n)
            if n < tile:
                ys_ref[n:, :] = jnp.zeros((tile - n, ys_ref.shape[1]), F32)

    @pl.when(rows == 0)
    def _():
        ys_ref[...] = jnp.zeros_like(ys_ref)


def _expert_ffn(hs, pass_tile, pass_hi, pass_expert, pass_rows, w_gate, w_up, w_down, layer, tile):
    _, n_e, d, f = w_gate.shape
    n_rows = hs.shape[0]
    eids = jnp.arange(n_e, dtype=I32)
    n_pass_e = jnp.sum(jnp.where((pass_expert[None, :] == eids[:, None]) & (pass_rows[None, :] > 0), 1, 0), axis=1)
    has = n_pass_e > 0
    later = (eids[None, :] > eids[:, None]) & has[None, :]
    next_used = jnp.min(jnp.where(later, eids[None, :], n_e), axis=1)
    next_used = jnp.where(next_used == n_e, -1, next_used).astype(I32)
    slot_e = ((jnp.cumsum(has.astype(I32)) - 1) % 2).astype(I32)
    pass_next = _take(next_used, pass_expert)
    pass_slot = jnp.maximum(_take(slot_e, pass_expert), 0)
    any_spec = pl.BlockSpec(memory_space=pl.ANY)
    grid_spec = pltpu.PrefetchScalarGridSpec(
        num_scalar_prefetch=6,
        grid=(pass_tile.shape[0],),
        in_specs=[pl.BlockSpec((tile, d), lambda i, pt, ph, te, pr, *_: (pt[jnp.where(pr[i] > 0, i, 0)], 0)),
                  any_spec, any_spec, any_spec],
        out_specs=pl.BlockSpec((tile, d), lambda i, pt, ph, *_: (pt[i], ph[i])),
        scratch_shapes=[pltpu.VMEM((2, d, f), F32), pltpu.VMEM((2, d, f), F32), pltpu.VMEM((2, f, d), F32),
                        pltpu.VMEM((d, f), BF16), pltpu.VMEM((d, f), BF16), pltpu.VMEM((f, d), BF16),
                        pltpu.SemaphoreType.DMA((2,))],
    )
    return pl.pallas_call(
        functools.partial(_expert_body, layer=layer),
        grid_spec=grid_spec,
        out_shape=jax.ShapeDtypeStruct((n_rows, 2 * d), F32),
        compiler_params=_cparams("arbitrary"),
        name="moe_experts",
    )(pass_tile, pass_hi, pass_expert, pass_rows, pass_next, pass_slot, hs, w_gate, w_up, w_down)


def _combine_body(pos_ref, x_ref, mod_ref, wts_ref, lng_ref, lnb_ref, ys_ref, *rest,
                  tm, alpha, next_mod):
    if next_mod:
        nmod_ref, x2_ref, h_ref, buf, sem = rest
    else:
        x2_ref, buf, sem = rest
    step = pl.program_id(0) * pl.num_programs(1) + pl.program_id(1)
    n_steps = pl.num_programs(0) * pl.num_programs(1)
    d = x_ref.shape[2]

    def gather(s, slot):
        def issue(blk, carry):
            for u in range(SUBLANES):
                p = pos_ref[s * tm + blk * SUBLANES + u]
                pltpu.make_async_copy(ys_ref.at[p >> 3, pl.ds(p & (SUBLANES - 1), 1)],
                                      buf.at[slot, blk, pl.ds(u, 1)], sem.at[slot]).start(priority=u % 2)
            return carry

        lax.fori_loop(0, tm // SUBLANES, issue, 0, unroll=4)

    @pl.when(step == 0)
    def _():
        gather(0, 0)

    @pl.when(step + 1 < n_steps)
    def _():
        gather(step + 1, (step + 1) % 2)

    slot = step % 2
    pltpu.make_async_copy(ys_ref.at[pl.ds(0, tm // SUBLANES)], buf.at[slot], sem.at[slot]).wait()

    w = wts_ref[...]
    rows = buf[slot].reshape(tm, 2 * d)
    moe = w[:, 0:1] * rows[:, :d] + w[:, 1:2] * rows[:, d:]
    g2 = mod_ref[0, 5:6, :]
    x2 = _layer_norm(alpha * x_ref[0] + g2 * moe, lng_ref[...], lnb_ref[...])
    x2_ref[0] = x2
    if next_mod:
        h_ref[0] = x2 * (1.0 + nmod_ref[0, 1:2, :]) + nmod_ref[0, 0:1, :]


def _combine(ys, pos, wts, x, mods, ln_g, ln_b, alpha, next_mods=None):
    b, l, d = x.shape
    tm = min(l, 512)
    nt = l // tm
    tok = lambda i, j, *_: (i, j, 0)
    bat = lambda i, j, *_: (i, 0, 0)
    const2 = lambda i, j, *_: (0, 0)
    in_specs = [pl.BlockSpec((1, tm, d), tok),
                pl.BlockSpec((1, 6, d), bat),
                pl.BlockSpec((tm, 2), lambda i, j, *_: (i * nt + j, 0)),
                pl.BlockSpec((1, d), const2),
                pl.BlockSpec((1, d), const2),
                pl.BlockSpec(memory_space=pl.ANY)]
    ys_tiles = ys.reshape(ys.shape[0] // SUBLANES, SUBLANES, ys.shape[1])
    args = [x, mods, wts, ln_g.reshape(1, d), ln_b.reshape(1, d), ys_tiles]
    out_shape = [jax.ShapeDtypeStruct((b, l, d), F32)]
    out_specs = [pl.BlockSpec((1, tm, d), tok)]
    if next_mods is not None:
        in_specs.append(pl.BlockSpec((1, 6, d), bat))
        args.append(next_mods)
        out_shape.append(jax.ShapeDtypeStruct((b, l, d), F32))
        out_specs.append(pl.BlockSpec((1, tm, d), tok))
    grid_spec = pltpu.PrefetchScalarGridSpec(
        num_scalar_prefetch=1,
        grid=(b, nt),
        in_specs=in_specs,
        out_specs=out_specs,
        scratch_shapes=[pltpu.VMEM((2, tm // SUBLANES, SUBLANES, 2 * d), F32),
                        pltpu.SemaphoreType.DMA((2,))],
    )
    return pl.pallas_call(
        functools.partial(_combine_body, tm=tm, alpha=alpha, next_mod=next_mods is not None),
        grid_spec=grid_spec,
        out_shape=out_shape,
        compiler_params=_cparams("arbitrary", "arbitrary"),
        name="moe_combine",
    )(pos, *args)


MOE_TILE = 256


def _moe(h_rows, route, counts, w_gate, w_up, w_down, layer):
    n = h_rows.shape[0]
    tile = min(n, MOE_TILE)
    (pos, pass_tile, pass_hi, pass_expert, pass_rows, last_tile, tiles, n_used_tiles,
     n_row_tiles) = _dispatch_plan(route, counts, tile)
    hs = _dispatch(h_rows, pos, last_tile, tiles, n_used_tiles, n_row_tiles * tile, tile)
    ys = _expert_ffn(hs, pass_tile, pass_hi, pass_expert, pass_rows, w_gate, w_up, w_down, layer, tile)
    return ys, pos, jnp.transpose(route[2:4])


def _pool_group(h_ref, w_ref, sc_ref, o_ref, col_ref, k, n_rows):
    n = n_rows * GRID_W
    c = h_ref.shape[2]
    blk = 4 * GRID_W
    half = k // 2
    pad = half * GRID_W
    ti = lax.broadcasted_iota(I32, (blk, blk), 0)
    si = lax.broadcasted_iota(I32, (blk, blk), 1)
    shift = GRID_W.bit_length() - 1
    same_row = (ti >> shift) == (si >> shift)
    band = jnp.where(same_row & (si - ti >= -half) & (si - ti <= half - 1), 1.0, 0.0).astype(BF16)
    col_ref[0:pad, :] = jnp.zeros((pad, c), F32)
    col_ref[pad + n:pad + n + pad, :] = jnp.zeros((pad, c), F32)
    for b0 in range(0, n, blk):
        hb = h_ref[0, b0:b0 + blk, :]
        head = hb.astype(BF16)
        rest = (hb - head.astype(F32)).astype(BF16)
        col_ref[pad + b0:pad + b0 + blk, :] = (jnp.dot(band, head, preferred_element_type=F32)
                                               + jnp.dot(band, rest, preferred_element_type=F32))
    acc = col_ref[0:n, :]
    for j in range(1, k):
        acc = acc + col_ref[j * GRID_W:j * GRID_W + n, :]
    t = lax.broadcasted_iota(I32, (n, 1), 0)
    wc = t & (GRID_W - 1)
    wr = t >> shift
    cnt_c = jnp.minimum(wc + half - 1, GRID_W - 1) - jnp.maximum(wc - half, 0) + 1
    cnt_r = jnp.minimum(wr + half - 1, n_rows - 1) - jnp.maximum(wr - half, 0) + 1
    mean = acc / (cnt_c * cnt_r).astype(F32)
    pooled = (mean - h_ref[0]).astype(BF16)
    o_ref[0] = jnp.dot(pooled, w_ref[0].astype(BF16), preferred_element_type=F32) * sc_ref[...]


def _pool_body(h_ref, w_ref, sc_ref, o_ref, col_ref, *, n_rows):
    g = pl.program_id(1)
    for gi, k in enumerate(POOL_WINDOWS):
        @pl.when(g == gi)
        def _(k=k):
            _pool_group(h_ref, w_ref, sc_ref, o_ref, col_ref, k, n_rows)


def _pool_mix(h, w_grp, scale):
    b, n, d = h.shape
    n_g, c, _ = w_grp.shape
    n_rows = n // GRID_W
    pad = (max(POOL_WINDOWS) // 2) * GRID_W
    return pl.pallas_call(
        functools.partial(_pool_body, n_rows=n_rows),
        grid=(b, n_g),
        in_specs=[pl.BlockSpec((1, n, c), lambda i, j: (i, 0, j)),
                  pl.BlockSpec((1, c, c), lambda i, j: (j, 0, 0)),
                  pl.BlockSpec((1, c), lambda i, j: (0, j))],
        out_specs=pl.BlockSpec((1, n, c), lambda i, j: (i, 0, j)),
        out_shape=jax.ShapeDtypeStruct((b, n, d), F32),
        scratch_shapes=[pltpu.VMEM((n + 2 * pad, c), F32)],
        compiler_params=_cparams("arbitrary", "arbitrary"),
        name="pool_mix",
    )(h, w_grp, scale.reshape(1, d))


def kernel(x, c, ctx, c_ctx, mod_w, mod_b, ln_g, ln_b, s5_lam_re, s5_lam_im, s5_log_dt, s5_b_re, s5_b_im,
           s5_c_re, s5_c_im, s5_d, s5_w_val, s5_w_gate, pool_w, pool_scale, router_w, router_b,
           moe_w_gate, moe_w_up, moe_w_down):
    b, l, d = x.shape
    depth = mod_w.shape[0]
    assert depth == 2 and b + 1 <= SUBLANES and d % LANES == 0 and GRID_W & (GRID_W - 1) == 0
    alpha = (2 * depth) ** 0.25

    cond = jnp.zeros((SUBLANES, d), F32).at[:b].set(c).at[b].set(c_ctx)
    mods = _modulation(cond, mod_w, mod_b).reshape(depth, SUBLANES, 6, d)

    toep, w1, w2t, a_r, a_i = _s5_weights(s5_lam_re[0], s5_lam_im[0], s5_log_dt[0], s5_b_re[0], s5_b_im[0],
                                          s5_c_re[0], s5_c_im[0], s5_d[0])
    y = _s5_mix(x, ctx, mods[0, :b], mods[0, b:b + 1], toep, w1, w2t, a_r, a_i)
    x1, h_rows, route, counts = _post_mixer(y, x, mods[0, :b], ln_g[0, 0], ln_b[0, 0], router_w, router_b,
                                            alpha, glu_w=(s5_w_val[0], s5_w_gate[0]))
    ys, pos, wts = _moe(h_rows, route, counts, moe_w_gate, moe_w_up, moe_w_down, 0)
    x2, h = _combine(ys, pos, wts, x1, mods[0, :b], ln_g[0, 1], ln_b[0, 1], alpha, next_mods=mods[1, :b])

    m = _pool_mix(h, pool_w[0], pool_scale[0])
    x3, h_rows, route, counts = _post_mixer(m, x2, mods[1, :b], ln_g[1, 0], ln_b[1, 0], router_w, router_b,
                                            alpha)
    ys, pos, wts = _moe(h_rows, route, counts, moe_w_gate, moe_w_up, moe_w_down, 1)
    (out,) = _combine(ys, pos, wts, x3, mods[1, :b], ln_g[1, 1], ln_b[1, 1], alpha)
    return out
```
